```python
import jax, jax.numpy as jnp
from jax import lax
import numpy as np

D_MODEL = 4096
BATCH = 1
SEQ = 8192
DEPTH = 1
DEC_BATCH = 32
DEC_SEQ = 8
PAST_LEN = 8192
PAGE_SIZE = 128

CHUNK = 128
D_A = D_MODEL // 2
GROUP_DIM_A = 128
N_GROUPS_A = D_A // GROUP_DIM_A
HEAD_DIM = 128
HEADS_PER_GROUP = D_MODEL // 512
DILATION_GROUPS = ((128, 1), (512, 4), (2048, 16))
N_HEADS_B = HEADS_PER_GROUP * len(DILATION_GROUPS)
D_QKV_B = N_HEADS_B * HEAD_DIM
D_B = HEADS_PER_GROUP * HEAD_DIM
STEPS = 128
N_BUCKETS = 32
MAX_EXACT = 16
MAX_DISTANCE = 2048
D_FF = 11008
CONV_W = 3
EPS = 1e-6
NEG = -1e30
ATTN_SCALE = HEAD_DIM ** -0.5
D_IN = 2 * D_A + 3 * D_QKV_B + 2 * D_MODEL

kernel_name = 'hybrid_gmlp_dilated_attn_convffn_step'


def rmsnorm(x, g):
    xf = x.astype(jnp.float32)
    y = xf * lax.rsqrt(jnp.mean(xf * xf, axis=-1, keepdims=True) + EPS)
    return (y * g).astype(x.dtype)


def layernorm(x, g, b):
    xf = x.astype(jnp.float32)
    mu = jnp.mean(xf, axis=-1, keepdims=True)
    var = jnp.mean(jnp.square(xf - mu), axis=-1, keepdims=True)
    return ((xf - mu) * lax.rsqrt(var + EPS) * g + b).astype(x.dtype)


def t5_bucket(dist):
    n = np.asarray(dist, np.int32)
    safe = np.maximum(n, 1).astype(np.float32)
    large = MAX_EXACT + (np.log(safe / MAX_EXACT) / np.log(np.float32(MAX_DISTANCE / MAX_EXACT))
                         * (N_BUCKETS - MAX_EXACT)).astype(np.int32)
    large = np.minimum(large, N_BUCKETS - 1)
    return np.where(n < MAX_EXACT, n, large).astype(np.int32)


def spatial_gating(za, ln_g, ln_b, w_s, b_s):
    n, t, _ = za.shape
    za = jax.nn.gelu(za)
    u, v = za[..., :D_A], layernorm(za[..., D_A:], ln_g, ln_b)
    c = min(t, CHUNK)
    w = jnp.where(np.tril(np.ones((c, c), bool)), w_s[:, :c, :c], 0)
    vc = v.reshape(n, t // c, c, N_GROUPS_A, GROUP_DIM_A)
    mixed = jnp.einsum('gpq,nkqgc->nkpgc', w, vc) + b_s[:, :c].T[None, None, :, :, None]
    return u * mixed.reshape(n, t, D_A), v


def dilated_attn_prompt(q, k, v, bias_tab, dil):
    n, t, h, hd = q.shape
    L = t // dil
    nb = -(-L // STEPS)
    pad = nb * STEPS - L

    def to_blocks(a):
        a = a.reshape(n, L, dil, h, hd).transpose(0, 2, 1, 3, 4).reshape(n * dil, L, h, hd)
        a = jnp.pad(a, ((0, 0), (0, pad), (0, 0), (0, 0)))
        return a.reshape(n * dil, nb, STEPS, h, hd)

    def with_prev(a):
        prev = jnp.pad(a[:, :-1], ((0, 0), (1, 0), (0, 0), (0, 0), (0, 0)))
        return jnp.concatenate([prev, a], axis=2)

    def from_blocks(a):
        rest = a.shape[3:]
        a = a.reshape((n, dil, nb * STEPS) + rest)[:, :, :L]
        return a.swapaxes(1, 2).reshape((n, t) + rest)

    qb, kb, vb = to_blocks(q), to_blocks(k), to_blocks(v)
    kk, vv = with_prev(kb), with_prev(vb)
    p_idx = np.arange(STEPS)[:, None]
    c_idx = np.arange(2 * STEPS)[None, :]
    steps = p_idx + STEPS - c_idx
    band = (steps >= 0) & (steps <= STEPS)
    valid = band[None] & ((np.arange(nb)[:, None, None] > 0) | (c_idx >= STEPS)[None])
    bias = bias_tab[t5_bucket(np.clip(steps, 0, STEPS) * dil)].transpose(2, 0, 1)
    s = jnp.einsum('nbqhd,nbkhd->nbhqk', qb.astype(jnp.float32), kk.astype(jnp.float32)) * ATTN_SCALE
    s = jnp.where(valid[None, :, None], s + bias[None, None], NEG)
    m = jnp.max(s, axis=-1, keepdims=True)
    e = jnp.exp(s - m)
    den = jnp.sum(e, axis=-1)
    o = jnp.einsum('nbhqk,nbkhd->nbqhd', e, vv.astype(jnp.float32)) / den.transpose(0, 1, 3, 2)[..., None]
    lse = (m[..., 0] + jnp.log(den)).transpose(0, 1, 3, 2)
    return from_blocks(o), from_blocks(lse)


def dilated_attn_sample(q, k_cache, v_cache, k_new, v_new, bias_tab, dil):
    s_len = q.shape[1]
    lc = k_cache.shape[1]
    kc = jnp.concatenate([k_cache.astype(k_new.dtype), k_new], axis=1)
    vc = jnp.concatenate([v_cache.astype(v_new.dtype), v_new], axis=1)
    j = np.arange(STEPS + 1)
    idx = lc + np.arange(s_len)[:, None] - j[None, :] * dil
    valid = idx >= 0
    idx = np.maximum(idx, 0)
    kg, vg = kc[:, idx], vc[:, idx]
    bias = bias_tab[t5_bucket(j * dil)].T
    sc = jnp.einsum('nshd,nsjhd->nshj', q.astype(jnp.float32), kg.astype(jnp.float32)) * ATTN_SCALE
    sc = jnp.where(valid[None, :, None, :], sc + bias[None, None], NEG)
    m = jnp.max(sc, axis=-1, keepdims=True)
    e = jnp.exp(sc - m)
    den = jnp.sum(e, axis=-1)
    o = jnp.einsum('nshj,nsjhd->nshd', e, vg.astype(jnp.float32)) / den[..., None]
    return o, m[..., 0] + jnp.log(den)


def merge_by_denominator(outs, lses):
    w = jax.nn.softmax(jnp.stack(lses, axis=0), axis=0)
    return jnp.einsum('gnth,gnthd->nthd', w, jnp.stack(outs, axis=0))


def conv_ffn(h, hist, w_gate, w_up, conv_w, conv_b, w_down):
    t = h.shape[1]
    a = h @ w_gate
    u = h @ w_up
    ah = jnp.concatenate([hist.astype(a.dtype), a], axis=1)
    c = conv_b + sum(conv_w[i] * ah[:, i:i + t] for i in range(CONV_W))
    return (jax.nn.gelu(c) * u) @ w_down, ah[:, t:]


def hybrid_layer(x, p, kv_cache, conv_hist):
    n, t, _ = x.shape
    h = rmsnorm(x, p['g_pre_mix'])
    z = h @ p['w_in']
    o1 = 2 * D_A
    o2 = o1 + 3 * D_QKV_B
    o_a, v_rows = spatial_gating(z[..., :o1], p['sgu_ln_g'], p['sgu_ln_b'], p['w_spatial'], p['b_spatial'])
    q = z[..., o1:o1 + D_QKV_B].reshape(n, t, N_HEADS_B, HEAD_DIM)
    k = z[..., o1 + D_QKV_B:o1 + 2 * D_QKV_B].reshape(n, t, N_HEADS_B, HEAD_DIM)
    v = z[..., o1 + 2 * D_QKV_B:o2].reshape(n, t, N_HEADS_B, HEAD_DIM)
    gate_a = jax.nn.sigmoid(z[..., o2:o2 + D_MODEL])
    gate_b = jax.nn.sigmoid(z[..., o2 + D_MODEL:])
    outs, lses, new_kv = [], [], []
    for gi, (win, dil) in enumerate(DILATION_GROUPS):
        hs = slice(gi * HEADS_PER_GROUP, (gi + 1) * HEADS_PER_GROUP)
        qg, kg, vg = q[:, :, hs], k[:, :, hs], v[:, :, hs]
        bias_tab = p['rel_bias'][:, hs]
        if kv_cache is None:
            o, lse = dilated_attn_prompt(qg, kg, vg, bias_tab, dil)
            keep = min(win, t)
            new_kv += [kg[:, t - keep:], vg[:, t - keep:]]
        else:
            o, lse = dilated_attn_sample(qg, kv_cache[2 * gi], kv_cache[2 * gi + 1], kg, vg, bias_tab, dil)
            new_kv += [kg, vg]
        outs.append(o)
        lses.append(lse)
    o_b = merge_by_denominator(outs, lses).reshape(n, t, D_B).astype(x.dtype)
    merged = gate_a * (o_a @ p['w_proj_a']) + gate_b * (o_b @ p['w_proj_b'])
    x = x + rmsnorm(merged @ p['w_out'], p['g_post_mix'])
    if conv_hist is None:
        conv_hist = jnp.zeros((n, CONV_W - 1, D_FF), x.dtype)
    f, new_conv = conv_ffn(rmsnorm(x, p['g_pre_ffn']), conv_hist, p['w_gate'], p['w_up'],
                           p['conv_w'], p['conv_b'], p['w_down'])
    x = x + rmsnorm(f, p['g_post_ffn'])
    sgu_state = v_rows[:, ((t - 1) // CHUNK) * CHUNK:]
    return x, (new_kv[0], new_kv[1], new_kv[2], new_kv[3], new_kv[4], new_kv[5], sgu_state, new_conv)


def setup_inputs(seed: int = 0) -> dict:
    key = jax.random.key(seed)
    ks = jax.random.split(key, 32)

    def nrm(k, shape, scale):
        return jax.random.normal(k, shape, jnp.float32) * scale

    L1, L2, L3 = (min(w, PAST_LEN) for w, _ in DILATION_GROUPS)

    def cshape(L):
        return (DEPTH, DEC_BATCH, L, HEADS_PER_GROUP, HEAD_DIM)

    return {
        'x_prompt': nrm(ks[0], (BATCH, SEQ, D_MODEL), 1.0),
        'x_sample': nrm(ks[1], (DEC_BATCH, DEC_SEQ, D_MODEL), 1.0),
        'cache_k_g1': nrm(ks[2], cshape(L1), 1.0),
        'cache_v_g1': nrm(ks[3], cshape(L1), 1.0),
        'cache_k_g2': nrm(ks[4], cshape(L2), 1.0),
        'cache_v_g2': nrm(ks[5], cshape(L2), 1.0),
        'cache_k_g3': nrm(ks[6], cshape(L3), 1.0),
        'cache_v_g3': nrm(ks[7], cshape(L3), 1.0),
        'state_conv': nrm(ks[8], (DEPTH, DEC_BATCH, CONV_W - 1, D_FF), 1.0),
        'g_pre_mix': 1.0 + nrm(ks[9], (DEPTH, D_MODEL), 0.1),
        'w_in': nrm(ks[10], (DEPTH, D_MODEL, D_IN), D_MODEL ** -0.5),
        'sgu_ln_g': 1.0 + nrm(ks[11], (DEPTH, D_A), 0.1),
        'sgu_ln_b': nrm(ks[12], (DEPTH, D_A), 0.02),
        'w_spatial': nrm(ks[13], (DEPTH, N_GROUPS_A, CHUNK, CHUNK), CHUNK ** -0.5),
        'b_spatial': 1.0 + nrm(ks[14], (DEPTH, N_GROUPS_A, CHUNK), 0.1),
        'rel_bias': nrm(ks[15], (N_BUCKETS, N_HEADS_B), 0.5),
        'w_proj_a': nrm(ks[16], (DEPTH, D_A, D_MODEL), D_A ** -0.5),
        'w_proj_b': nrm(ks[17], (DEPTH, D_B, D_MODEL), D_B ** -0.5),
        'w_out': nrm(ks[18], (DEPTH, D_MODEL, D_MODEL), D_MODEL ** -0.5),
        'g_post_mix': 1.0 + nrm(ks[19], (DEPTH, D_MODEL), 0.1),
        'g_pre_ffn': 1.0 + nrm(ks[20], (DEPTH, D_MODEL), 0.1),
        'w_gate': nrm(ks[21], (DEPTH, D_MODEL, D_FF), D_MODEL ** -0.5),
        'w_up': nrm(ks[22], (DEPTH, D_MODEL, D_FF), D_MODEL ** -0.5),
        'conv_w': nrm(ks[23], (DEPTH, CONV_W, D_FF), 0.5),
        'conv_b': nrm(ks[24], (DEPTH, D_FF), 0.02),
        'w_down': nrm(ks[25], (DEPTH, D_FF, D_MODEL), D_FF ** -0.5),
        'g_post_ffn': 1.0 + nrm(ks[26], (DEPTH, D_MODEL), 0.1),
    }


def reference(x_prompt, x_sample, cache_k_g1, cache_v_g1, cache_k_g2, cache_v_g2, cache_k_g3, cache_v_g3,
              state_conv, g_pre_mix, w_in, sgu_ln_g, sgu_ln_b, w_spatial, b_spatial, rel_bias,
              w_proj_a, w_proj_b, w_out, g_post_mix, g_pre_ffn, w_gate, w_up, conv_w, conv_b, w_down,
              g_post_ffn):
    yp, ys = x_prompt, x_sample
    prompt_states, sample_states = [], []
    for l in range(DEPTH):
        p = dict(g_pre_mix=g_pre_mix[l], w_in=w_in[l], sgu_ln_g=sgu_ln_g[l], sgu_ln_b=sgu_ln_b[l],
                 w_spatial=w_spatial[l], b_spatial=b_spatial[l], rel_bias=rel_bias,
                 w_proj_a=w_proj_a[l], w_proj_b=w_proj_b[l], w_out=w_out[l], g_post_mix=g_post_mix[l],
                 g_pre_ffn=g_pre_ffn[l], w_gate=w_gate[l], w_up=w_up[l], conv_w=conv_w[l],
                 conv_b=conv_b[l], w_down=w_down[l], g_post_ffn=g_post_ffn[l])
        yp, st_p = hybrid_layer(yp, p, None, None)
        kv = (cache_k_g1[l], cache_v_g1[l], cache_k_g2[l], cache_v_g2[l], cache_k_g3[l], cache_v_g3[l])
        ys, st_s = hybrid_layer(ys, p, kv, state_conv[l])
        prompt_states.append(st_p)
        sample_states.append(st_s)
    pk1, pv1, pk2, pv2, pk3, pv3, p_sgu, p_conv = [jnp.stack([s[i] for s in prompt_states]) for i in range(8)]
    sk1, sv1, sk2, sv2, sk3, sv3, s_sgu, s_conv = [jnp.stack([s[i] for s in sample_states]) for i in range(8)]
    return (yp, ys, pk1, pv1, pk2, pv2, pk3, pv3, p_sgu, p_conv, sk1, sv1, sk2, sv2, sk3, sv3, s_sgu, s_conv)
```

```python
import functools
import math

import numpy as np
import jax
import jax.numpy as jnp
from jax import lax
from jax.experimental import pallas as pl
from jax.experimental.pallas import tpu as pltpu

F32 = jnp.float32
BF16 = jnp.bfloat16

HEAD_DIM = 128
STEPS = 128
CHUNK = 128
GROUP_DIM_A = 128
DILATION_GROUPS = ((128, 1), (512, 4), (2048, 16))
MAX_EXACT = 16
MAX_DISTANCE = 2048
EPS = 1e-6
NEG = -1e30
ATTN_SCALE = HEAD_DIM ** -0.5

LANE = 128
SUBLANE = 8
VMEM_LIMIT_BYTES = 56 * 1024 * 1024
MATMUL_TILE = 1024


def _tile(dim, target, align):
    best = None
    t = align
    while t <= min(dim, target):
        if dim % t == 0:
            best = t
        t += align
    return best if best is not None else dim


def _params(semantics):
    return pltpu.CompilerParams(dimension_semantics=semantics, vmem_limit_bytes=VMEM_LIMIT_BYTES)


def _gelu(x):
    return 0.5 * x * (1.0 + jnp.tanh(math.sqrt(2.0 / math.pi) * (x + 0.044715 * (x * x * x))))


def _sigmoid(x):
    return 1.0 / (1.0 + jnp.exp(-x))


def _rms_scale(x):
    return lax.rsqrt(jnp.mean(x * x, axis=-1, keepdims=True) + EPS)


def _rmsnorm_cast_kernel(x_ref, g_ref, o_ref):
    x = x_ref[...]
    o_ref[...] = ((x * _rms_scale(x)) * g_ref[...]).astype(o_ref.dtype)


def _rmsnorm_cast(x, g):
    t, d = x.shape
    bt = _tile(t, 256, SUBLANE)
    return pl.pallas_call(
        _rmsnorm_cast_kernel,
        grid=(t // bt,),
        in_specs=[pl.BlockSpec((bt, d), lambda i: (i, 0)), pl.BlockSpec((1, d), lambda i: (0, 0))],
        out_specs=pl.BlockSpec((bt, d), lambda i: (i, 0)),
        out_shape=jax.ShapeDtypeStruct((t, d), BF16),
        compiler_params=_params(("arbitrary",)),
        name="rmsnorm_cast",
    )(x, g.reshape(1, d))


def _in_proj_kernel(x_ref, w_ref, o_ref, *, n_gelu, n_plain):
    acc = jnp.dot(x_ref[...], w_ref[...], preferred_element_type=F32)
    j = pl.program_id(1)

    @pl.when(j < n_gelu)
    def _():
        o_ref[...] = _gelu(acc)

    @pl.when((j >= n_gelu) & (j < n_gelu + n_plain))
    def _():
        o_ref[...] = acc

    @pl.when(j >= n_gelu + n_plain)
    def _():
        o_ref[...] = _sigmoid(acc)


def _in_proj(h, w, bn, n_gelu_cols, n_plain_cols):
    t, d = h.shape
    n = w.shape[1]
    bm = _tile(t, MATMUL_TILE, SUBLANE)
    kern = functools.partial(_in_proj_kernel, n_gelu=n_gelu_cols // bn, n_plain=n_plain_cols // bn)
    return pl.pallas_call(
        kern,
        grid=(t // bm, n // bn),
        in_specs=[pl.BlockSpec((bm, d), lambda i, j: (i, 0)), pl.BlockSpec((d, bn), lambda i, j: (0, j))],
        out_specs=pl.BlockSpec((bm, bn), lambda i, j: (i, j)),
        out_shape=jax.ShapeDtypeStruct((t, n), F32),
        compiler_params=_params(("arbitrary", "arbitrary")),
        name="in_proj",
    )(h, w)


def _sgu_kernel(u_ref, v_ref, lg_ref, lb_ref, w_ref, b_ref, o_ref, vs_ref, *, n_groups):
    vp = v_ref[...]
    mu = jnp.mean(vp, axis=-1, keepdims=True)
    vc = vp - mu
    var = jnp.mean(vc * vc, axis=-1, keepdims=True)
    v = vc * lax.rsqrt(var + EPS) * lg_ref[...] + lb_ref[...]
    vs_ref[...] = v
    vb = v.astype(BF16)
    for g in range(n_groups):
        sl = slice(g * GROUP_DIM_A, (g + 1) * GROUP_DIM_A)
        mixed = jnp.dot(w_ref[g], vb[:, sl], preferred_element_type=F32) + b_ref[:, sl]
        o_ref[:, sl] = (u_ref[:, sl] * mixed).astype(o_ref.dtype)


def _sgu(z, d_a, ln_g, ln_b, w_mix, b_full):
    t = z.shape[0]
    n_groups, c, _ = w_mix.shape
    kern = functools.partial(_sgu_kernel, n_groups=n_groups)
    return pl.pallas_call(
        kern,
        grid=(t // c,),
        in_specs=[
            pl.BlockSpec((c, d_a), lambda i: (i, 0)),
            pl.BlockSpec((c, d_a), lambda i: (i, 1)),
            pl.BlockSpec((1, d_a), lambda i: (0, 0)),
            pl.BlockSpec((1, d_a), lambda i: (0, 0)),
            pl.BlockSpec((n_groups, c, c), lambda i: (0, 0, 0)),
            pl.BlockSpec((c, d_a), lambda i: (0, 0)),
        ],
        out_specs=[pl.BlockSpec((c, d_a), lambda i: (i, 0)), pl.BlockSpec((c, d_a), lambda i: (0, 0))],
        out_shape=[jax.ShapeDtypeStruct((t, d_a), BF16), jax.ShapeDtypeStruct((c, d_a), F32)],
        compiler_params=_params(("arbitrary",)),
        name="sgu",
    )(z, z, ln_g.reshape(1, d_a), ln_b.reshape(1, d_a), w_mix, b_full)


def _t5_bucket(dist, n_buckets):
    n = np.asarray(dist, np.int32)
    safe = np.maximum(n, 1).astype(np.float32)
    large = MAX_EXACT + (np.log(safe / MAX_EXACT) / np.log(np.float32(MAX_DISTANCE / MAX_EXACT))
                         * (n_buckets - MAX_EXACT)).astype(np.int32)
    large = np.minimum(large, n_buckets - 1)
    return np.where(n < MAX_EXACT, n, large).astype(np.int32)


def _softmax_parts(s):
    m = jnp.max(s, axis=-1, keepdims=True)
    e = jnp.exp(s - m)
    return m, e, jnp.sum(e, axis=-1, keepdims=True)


def _attn_prompt_kernel(q_ref, kp_ref, kc_ref, vp_ref, vc_ref, bias_ref, band_ref, o_ref, lse_ref, *, hpg):
    b = pl.program_id(1)
    col = lax.broadcasted_iota(jnp.int32, (STEPS, 2 * STEPS), 1)
    valid = (band_ref[...] > 0.5) & ((b > 0) | (col >= STEPS))
    for h in range(hpg):
        sl = slice(h * HEAD_DIM, (h + 1) * HEAD_DIM)
        q = q_ref[:, sl].astype(BF16)
        kk = jnp.concatenate([kp_ref[:, sl], kc_ref[:, sl]], axis=0).astype(BF16)
        vv = jnp.concatenate([vp_ref[:, sl], vc_ref[:, sl]], axis=0).astype(BF16)
        s = lax.dot_general(q, kk, (((1,), (1,)), ((), ())), preferred_element_type=F32) * ATTN_SCALE
        s = jnp.where(valid, s + bias_ref[h], NEG)
        m, e, den = _softmax_parts(s)
        o_ref[:, sl] = jnp.dot(e.astype(BF16), vv, preferred_element_type=F32) / den
        lse_ref[:, sl] = jnp.broadcast_to(m + jnp.log(den), (STEPS, HEAD_DIM))


def _attn_prompt(z, gw, q_blk, k_blk, v_blk, bias_tab, dil):
    t, d_in = z.shape
    hpg = gw // HEAD_DIM
    sub_len = t // dil
    assert sub_len % STEPS == 0
    nb = sub_len // STEPS
    zc = d_in // gw
    zr = z.reshape(sub_len, dil * d_in)

    p_idx = np.arange(STEPS)[:, None]
    c_idx = np.arange(2 * STEPS)[None, :]
    steps = p_idx + STEPS - c_idx
    band = ((steps >= 0) & (steps <= STEPS)).astype(np.float32)
    bucket = _t5_bucket(np.clip(steps, 0, STEPS) * dil, bias_tab.shape[0])
    bias = jnp.transpose(bias_tab[bucket], (2, 0, 1))

    def cur(blk):
        return pl.BlockSpec((STEPS, gw), lambda r, b: (b, r * zc + blk))

    def prev(blk):
        return pl.BlockSpec((STEPS, gw), lambda r, b: (jnp.maximum(b - 1, 0), r * zc + blk))

    o, lse = pl.pallas_call(
        functools.partial(_attn_prompt_kernel, hpg=hpg),
        grid=(dil, nb),
        in_specs=[cur(q_blk), prev(k_blk), cur(k_blk), prev(v_blk), cur(v_blk),
                  pl.BlockSpec((hpg, STEPS, 2 * STEPS), lambda r, b: (0, 0, 0)),
                  pl.BlockSpec((STEPS, 2 * STEPS), lambda r, b: (0, 0))],
        out_specs=[pl.BlockSpec((STEPS, gw), lambda r, b: (b, r)),
                   pl.BlockSpec((STEPS, gw), lambda r, b: (b, r))],
        out_shape=[jax.ShapeDtypeStruct((sub_len, dil * gw), F32)] * 2,
        compiler_params=_params(("arbitrary", "arbitrary")),
        name=f"attn_prompt_d{dil}",
    )(zr, zr, zr, zr, zr, bias, jnp.asarray(band))
    return o.reshape(t, gw), lse.reshape(t, gw)


def _attn_sample_kernel(q_ref, kn_ref, vn_ref, ck_ref, cv_ref, bc_ref, mc_ref, bnew_ref, mnew_ref,
                        o_ref, lse_ref, *, hpg, s_len):
    b = pl.program_id(0)
    n_new = kn_ref.shape[0]
    col = lax.broadcasted_iota(jnp.int32, (s_len, n_new), 1)
    own = (col >= b * s_len) & (col < (b + 1) * s_len)
    valid_new = (mnew_ref[...] > 0.5) & own
    valid_c = mc_ref[...] > 0.5
    for h in range(hpg):
        sl = slice(h * HEAD_DIM, (h + 1) * HEAD_DIM)
        q = q_ref[:, sl].astype(BF16)
        sc = lax.dot_general(q, ck_ref[0, :, sl].astype(BF16), (((1,), (1,)), ((), ())),
                             preferred_element_type=F32) * ATTN_SCALE
        sc = jnp.where(valid_c, sc + bc_ref[h], NEG)
        sn = lax.dot_general(q, kn_ref[:, sl].astype(BF16), (((1,), (1,)), ((), ())),
                             preferred_element_type=F32) * ATTN_SCALE
        sn = jnp.where(valid_new, sn + bnew_ref[h], NEG)
        m = jnp.maximum(jnp.max(sc, axis=-1, keepdims=True), jnp.max(sn, axis=-1, keepdims=True))
        ec = jnp.exp(sc - m)
        en = jnp.exp(sn - m)
        den = jnp.sum(ec, axis=-1, keepdims=True) + jnp.sum(en, axis=-1, keepdims=True)
        acc = jnp.dot(ec.astype(BF16), cv_ref[0, :, sl].astype(BF16), preferred_element_type=F32)
        acc = acc + jnp.dot(en.astype(BF16), vn_ref[:, sl].astype(BF16), preferred_element_type=F32)
        o_ref[:, sl] = acc / den
        lse_ref[:, sl] = jnp.broadcast_to(m + jnp.log(den), (s_len, HEAD_DIM))


def _attn_sample(z, s_len, gw, q_blk, k_blk, v_blk, cache_k, cache_v, bias_tab, dil):
    t, d_in = z.shape
    n_seq = t // s_len
    hpg = gw // HEAD_DIM
    lc = cache_k.shape[1]
    ck = cache_k.reshape(n_seq, lc, gw)
    cv = cache_v.reshape(n_seq, lc, gw)

    j = np.arange(STEPS + 1)
    idx = lc + np.arange(s_len)[:, None] - j[None, :] * dil
    assert idx.min() >= 0
    bucket = _t5_bucket(j * dil, bias_tab.shape[0])
    mask = np.zeros((s_len, lc + s_len), np.float32)
    bsel = np.zeros((s_len, lc + s_len), np.int32)
    for s in range(s_len):
        mask[s, idx[s]] = 1.0
        bsel[s, idx[s]] = bucket
    bias_all = jnp.transpose(bias_tab[bsel], (2, 0, 1))
    bias_c = bias_all[:, :, :lc]
    bias_new = jnp.tile(bias_all[:, :, lc:], (1, 1, n_seq))
    mask_c = jnp.asarray(mask[:, :lc])
    mask_new = jnp.asarray(np.tile(mask[:, lc:], (1, n_seq)))

    o, lse = pl.pallas_call(
        functools.partial(_attn_sample_kernel, hpg=hpg, s_len=s_len),
        grid=(n_seq,),
        in_specs=[pl.BlockSpec((s_len, gw), lambda b: (b, q_blk)),
                  pl.BlockSpec((t, gw), lambda b: (0, k_blk)),
                  pl.BlockSpec((t, gw), lambda b: (0, v_blk)),
                  pl.BlockSpec((1, lc, gw), lambda b: (b, 0, 0)),
                  pl.BlockSpec((1, lc, gw), lambda b: (b, 0, 0)),
                  pl.BlockSpec((hpg, s_len, lc), lambda b: (0, 0, 0)),
                  pl.BlockSpec((s_len, lc), lambda b: (0, 0)),
                  pl.BlockSpec((hpg, s_len, t), lambda b: (0, 0, 0)),
                  pl.BlockSpec((s_len, t), lambda b: (0, 0))],
        out_specs=[pl.BlockSpec((s_len, gw), lambda b: (b, 0)),
                   pl.BlockSpec((s_len, gw), lambda b: (b, 0))],
        out_shape=[jax.ShapeDtypeStruct((t, gw), F32)] * 2,
        compiler_params=_params(("arbitrary",)),
        name=f"attn_sample_d{dil}",
    )(z, z, z, ck, cv, bias_c, mask_c, bias_new, mask_new)
    return o, lse


def _merge_kernel(o1, o2, o3, l1, l2, l3, out_ref):
    a, b, c = l1[...], l2[...], l3[...]
    m = jnp.maximum(jnp.maximum(a, b), c)
    ea, eb, ec = jnp.exp(a - m), jnp.exp(b - m), jnp.exp(c - m)
    tot = ea + eb + ec
    out_ref[...] = ((ea * o1[...] + eb * o2[...] + ec * o3[...]) / tot).astype(out_ref.dtype)


def _merge(outs, lses):
    t, gw = outs[0].shape
    bt = _tile(t, 512, SUBLANE)
    spec = pl.BlockSpec((bt, gw), lambda i: (i, 0))
    return pl.pallas_call(
        _merge_kernel,
        grid=(t // bt,),
        in_specs=[spec] * 6,
        out_specs=spec,
        out_shape=jax.ShapeDtypeStruct((t, gw), BF16),
        compiler_params=_params(("arbitrary",)),
        name="merge_groups",
    )(*outs, *lses)


def _gated_proj_kernel(a_ref, b_ref, wa_ref, wb_ref, ga_ref, gb_ref, o_ref):
    pa = jnp.dot(a_ref[...], wa_ref[...], preferred_element_type=F32)
    pb = jnp.dot(b_ref[...], wb_ref[...], preferred_element_type=F32)
    o_ref[...] = (ga_ref[...] * pa + gb_ref[...] * pb).astype(o_ref.dtype)


def _gated_proj(o_a, o_b, w_a, w_b, z, bn, gate_a_blk, gate_b_blk):
    t, d_a = o_a.shape
    d_b = o_b.shape[1]
    d = w_a.shape[1]
    bm = _tile(t, MATMUL_TILE // 2, SUBLANE)
    return pl.pallas_call(
        _gated_proj_kernel,
        grid=(t // bm, d // bn),
        in_specs=[pl.BlockSpec((bm, d_a), lambda i, j: (i, 0)),
                  pl.BlockSpec((bm, d_b), lambda i, j: (i, 0)),
                  pl.BlockSpec((d_a, bn), lambda i, j: (0, j)),
                  pl.BlockSpec((d_b, bn), lambda i, j: (0, j)),
                  pl.BlockSpec((bm, bn), lambda i, j: (i, gate_a_blk + j)),
                  pl.BlockSpec((bm, bn), lambda i, j: (i, gate_b_blk + j))],
        out_specs=pl.BlockSpec((bm, bn), lambda i, j: (i, j)),
        out_shape=jax.ShapeDtypeStruct((t, d), BF16),
        compiler_params=_params(("arbitrary", "arbitrary")),
        name="gated_proj",
    )(o_a, o_b, w_a, w_b, z, z)


def _matmul_kernel(x_ref, w_ref, o_ref, acc_ref, *, nk):
    k = pl.program_id(2)
    part = jnp.dot(x_ref[...], w_ref[...], preferred_element_type=F32)
    if nk == 1:
        o_ref[...] = part
    else:
        @pl.when(k == 0)
        def _():
            acc_ref[...] = part

        @pl.when((k > 0) & (k < nk - 1))
        def _():
            acc_ref[...] += part

        @pl.when(k == nk - 1)
        def _():
            o_ref[...] = acc_ref[...] + part


def _matmul(x, w, name, k_tile=None):
    t, kd = x.shape
    n = w.shape[1]
    bm = _tile(t, MATMUL_TILE, SUBLANE)
    bn = _tile(n, MATMUL_TILE, LANE)
    bk = kd if k_tile is None else k_tile
    nk = kd // bk
    return pl.pallas_call(
        functools.partial(_matmul_kernel, nk=nk),
        grid=(t // bm, n // bn, nk),
        in_specs=[pl.BlockSpec((bm, bk), lambda i, j, k: (i, k)),
                  pl.BlockSpec((bk, bn), lambda i, j, k: (k, j))],
        out_specs=pl.BlockSpec((bm, bn), lambda i, j, k: (i, j)),
        out_shape=jax.ShapeDtypeStruct((t, n), F32),
        scratch_shapes=[pltpu.VMEM((bm, bn) if nk > 1 else (SUBLANE, LANE), F32)],
        compiler_params=_params(("arbitrary", "arbitrary", "arbitrary")),
        name=name,
    )(x, w)


def _post_mix_kernel(x_ref, y_ref, g1_ref, g2_ref, x1_ref, h2_ref):
    y = y_ref[...]
    x1 = x_ref[...] + (y * _rms_scale(y)) * g1_ref[...]
    x1_ref[...] = x1
    h2_ref[...] = ((x1 * _rms_scale(x1)) * g2_ref[...]).astype(h2_ref.dtype)


def _post_mix(x, y, g_post, g_pre):
    t, d = x.shape
    bt = _tile(t, 256, SUBLANE)
    row = pl.BlockSpec((bt, d), lambda i: (i, 0))
    vec = pl.BlockSpec((1, d), lambda i: (0, 0))
    return pl.pallas_call(
        _post_mix_kernel,
        grid=(t // bt,),
        in_specs=[row, row, vec, vec],
        out_specs=[row, row],
        out_shape=[jax.ShapeDtypeStruct((t, d), F32), jax.ShapeDtypeStruct((t, d), BF16)],
        compiler_params=_params(("arbitrary",)),
        name="post_mix",
    )(x, y, g_post.reshape(1, d), g_pre.reshape(1, d))


def _residual_norm_kernel(x_ref, y_ref, g_ref, o_ref):
    y = y_ref[...]
    o_ref[...] = x_ref[...] + (y * _rms_scale(y)) * g_ref[...]


def _residual_norm(x, y, g):
    t, d = x.shape
    bt = _tile(t, 256, SUBLANE)
    row = pl.BlockSpec((bt, d), lambda i: (i, 0))
    return pl.pallas_call(
        _residual_norm_kernel,
        grid=(t // bt,),
        in_specs=[row, row, pl.BlockSpec((1, d), lambda i: (0, 0))],
        out_specs=row,
        out_shape=jax.ShapeDtypeStruct((t, d), F32),
        compiler_params=_params(("arbitrary",)),
        name="residual_norm",
    )(x, y, g.reshape(1, d))


def _ffn_up_kernel(*refs, conv_w, s_len):
    if s_len is None:
        x_ref, wg_ref, wu_ref, cw_ref, cb_ref, out_ref, tail_ref, buf_ref = refs
    else:
        x_ref, wg_ref, wu_ref, cw_ref, cb_ref, h1_ref, h2_ref, out_ref, tail_ref, buf_ref = refs
    bm = x_ref.shape[0]
    x = x_ref[...]
    a = jnp.dot(x, wg_ref[...], preferred_element_type=F32)
    u = jnp.dot(x, wu_ref[...], preferred_element_type=F32)

    @pl.when(pl.program_id(1) == 0)
    def _():
        buf_ref[0:SUBLANE, :] = jnp.zeros((SUBLANE, buf_ref.shape[1]), F32)

    buf_ref[SUBLANE:SUBLANE + bm, :] = a
    taps = [buf_ref[SUBLANE - (conv_w - 1 - i):SUBLANE - (conv_w - 1 - i) + bm, :] for i in range(conv_w - 1)]
    if s_len is None:
        tail_ref[...] = a[bm - SUBLANE:, :]
        buf_ref[0:SUBLANE, :] = a[bm - SUBLANE:, :]
    else:
        tail_ref[...] = a
        s = lax.broadcasted_iota(jnp.int32, a.shape, 0) % s_len
        hist = [h2_ref[...], h1_ref[...]]
        taps = [jnp.where(s < (conv_w - 1 - i), hist[i], taps[i]) for i in range(conv_w - 1)]
    c = cw_ref[0:1, :] * taps[0]
    for i in range(1, conv_w - 1):
        c = c + cw_ref[i:i + 1, :] * taps[i]
    c = c + cw_ref[conv_w - 1:conv_w, :] * a
    c = cb_ref[...] + c
    out_ref[...] = (_gelu(c) * u).astype(out_ref.dtype)


def _ffn_up(h, w_gate, w_up, conv_w, conv_b, hist=None, s_len=None):
    t, d = h.shape
    ffp = w_gate.shape[1]
    cw = conv_w.shape[0]
    assert cw == 3
    bn = _tile(ffp, MATMUL_TILE, LANE)
    bm = _tile(t, 512, SUBLANE)
    x_spec = pl.BlockSpec((bm, d), lambda j, i: (i, 0))
    w_spec = pl.BlockSpec((d, bn), lambda j, i: (0, j))
    cw_spec = pl.BlockSpec((cw, bn), lambda j, i: (0, j))
    cb_spec = pl.BlockSpec((1, bn), lambda j, i: (0, j))
    out_spec = pl.BlockSpec((bm, bn), lambda j, i: (i, j))
    in_specs = [x_spec, w_spec, w_spec, cw_spec, cb_spec]
    args = [h, w_gate, w_up, conv_w, conv_b.reshape(1, ffp)]
    if s_len is None:
        tail_rows = SUBLANE
        tail_spec = pl.BlockSpec((SUBLANE, bn), lambda j, i: (0, j))
    else:
        assert bm == t
        tail_rows = t
        tail_spec = out_spec
        n_seq = t // s_len
        hp = jnp.zeros((n_seq, s_len, ffp), F32)
        h1 = hp.at[:, 0].set(hist[:, 1]).reshape(t, ffp)
        h2 = hp.at[:, 0].set(hist[:, 0]).at[:, 1].set(hist[:, 1]).reshape(t, ffp)
        in_specs += [out_spec, out_spec]
        args += [h1, h2]
    return pl.pallas_call(
        functools.partial(_ffn_up_kernel, conv_w=cw, s_len=s_len),
        grid=(ffp // bn, t // bm),
        in_specs=in_specs,
        out_specs=[out_spec, tail_spec],
        out_shape=[jax.ShapeDtypeStruct((t, ffp), BF16), jax.ShapeDtypeStruct((tail_rows, ffp), F32)],
        scratch_shapes=[pltpu.VMEM((bm + SUBLANE, bn), F32)],
        compiler_params=_params(("arbitrary", "arbitrary")),
        name="ffn_up",
    )(*args)


def _layer(x, p, s_len=None, caches=None, conv_hist=None):
    t, d = x.shape
    d_a = p["ln_g"].shape[0]
    gw = p["gw"]
    bn = p["bn"]
    o1 = 2 * d_a
    d_qkv = 3 * gw

    h = _rmsnorm_cast(x, p["g_pre_mix"])
    z = _in_proj(h, p["w_in"], bn, o1, 3 * d_qkv)

    if s_len is None:
        w_mix, b_full = p["w_mix_prompt"], p["b_mix_prompt"]
    else:
        w_mix, b_full = p["w_mix_sample"], p["b_mix_sample"]
    o_a, v_state = _sgu(z, d_a, p["ln_g"], p["ln_b"], w_mix, b_full)

    outs, lses = [], []
    for gi, (_, dil) in enumerate(DILATION_GROUPS):
        q_blk = (o1 + gi * gw) // gw
        k_blk = (o1 + d_qkv + gi * gw) // gw
        v_blk = (o1 + 2 * d_qkv + gi * gw) // gw
        bias_tab = p["rel_bias"][:, gi * (gw // HEAD_DIM):(gi + 1) * (gw // HEAD_DIM)]
        if s_len is None:
            o, lse = _attn_prompt(z, gw, q_blk, k_blk, v_blk, bias_tab, dil)
        else:
            o, lse = _attn_sample(z, s_len, gw, q_blk, k_blk, v_blk, caches[2 * gi], caches[2 * gi + 1],
                                  bias_tab, dil)
        outs.append(o)
        lses.append(lse)
    o_b = _merge(outs, lses)

    gate_a_blk = (o1 + 3 * d_qkv) // bn
    merged = _gated_proj(o_a, o_b, p["w_proj_a"], p["w_proj_b"], z, bn, gate_a_blk, gate_a_blk + d // bn)
    y = _matmul(merged, p["w_out"], "out_proj")
    x1, h2 = _post_mix(x, y, p["g_post_mix"], p["g_pre_ffn"])

    if s_len is None:
        act, a_tail = _ffn_up(h2, p["w_gate"], p["w_up"], p["conv_w"], p["conv_b"])
    else:
        act, a_tail = _ffn_up(h2, p["w_gate"], p["w_up"], p["conv_w"], p["conv_b"], conv_hist, s_len)
    f = _matmul(act, p["w_down"], "ffn_down", k_tile=p["ffn_k_tile"])
    y_out = _residual_norm(x1, f, p["g_post_ffn"])
    return y_out, z, v_state, a_tail


def _pad_cols(w, n):
    return jnp.pad(w, ((0, 0), (0, n - w.shape[1])))


def kernel(x_prompt, x_sample, cache_k_g1, cache_v_g1, cache_k_g2, cache_v_g2, cache_k_g3, cache_v_g3, state_conv, g_pre_mix, w_in, sgu_ln_g, sgu_ln_b, w_spatial, b_spatial, rel_bias, w_proj_a, w_proj_b, w_out, g_post_mix, g_pre_ffn, w_gate, w_up, conv_w, conv_b, w_down, g_post_ffn):
    depth = w_in.shape[0]
    assert depth == 1
    n_prompt, seq, d = x_prompt.shape
    assert n_prompt == 1 and seq % CHUNK == 0
    n_seq, s_len, _ = x_sample.shape
    assert s_len == SUBLANE
    d_a = sgu_ln_g.shape[1]
    n_groups = w_spatial.shape[1]
    n_heads = rel_bias.shape[1]
    hpg = n_heads // len(DILATION_GROUPS)
    gw = hpg * HEAD_DIM
    d_ff = w_gate.shape[2]
    cw = conv_w.shape[1]
    bn = _tile(math.gcd(2 * d_a, gw), MATMUL_TILE, LANE)
    ffp = -(-d_ff // MATMUL_TILE) * MATMUL_TILE if d_ff > MATMUL_TILE else d_ff
    t_s = n_seq * s_len

    tri = np.tril(np.ones((CHUNK, CHUNK), np.float32))
    w_mix_prompt = (w_spatial[0] * tri).astype(BF16)
    b_mix_prompt = jnp.repeat(b_spatial[0].T, GROUP_DIM_A, axis=1)
    w_small = w_spatial[0][:, :s_len, :s_len] * tri[:s_len, :s_len]
    eye = np.eye(n_seq, dtype=np.float32)
    w_mix_sample = jnp.einsum("ab,gpq->gapbq", eye, w_small).reshape(n_groups, t_s, t_s).astype(BF16)
    b_mix_sample = jnp.tile(jnp.repeat(b_spatial[0][:, :s_len].T, GROUP_DIM_A, axis=1), (n_seq, 1))

    p = dict(
        gw=gw, bn=bn,
        g_pre_mix=g_pre_mix[0], w_in=w_in[0].astype(BF16), ln_g=sgu_ln_g[0], ln_b=sgu_ln_b[0],
        w_mix_prompt=w_mix_prompt, b_mix_prompt=b_mix_prompt,
        w_mix_sample=w_mix_sample, b_mix_sample=b_mix_sample,
        rel_bias=rel_bias,
        w_proj_a=w_proj_a[0].astype(BF16), w_proj_b=w_proj_b[0].astype(BF16), w_out=w_out[0].astype(BF16),
        g_post_mix=g_post_mix[0], g_pre_ffn=g_pre_ffn[0],
        w_gate=_pad_cols(w_gate[0], ffp).astype(BF16), w_up=_pad_cols(w_up[0], ffp).astype(BF16),
        conv_w=_pad_cols(conv_w[0], ffp), conv_b=jnp.pad(conv_b[0], (0, ffp - d_ff)),
        w_down=jnp.pad(w_down[0], ((0, ffp - d_ff), (0, 0))).astype(BF16),
        ffn_k_tile=_tile(ffp, 3072, LANE),
        g_post_ffn=g_post_ffn[0],
    )

    caches = tuple(c[0] for c in (cache_k_g1, cache_v_g1, cache_k_g2, cache_v_g2, cache_k_g3, cache_v_g3))
    hist = jnp.pad(state_conv[0], ((0, 0), (0, 0), (0, ffp - d_ff)))

    yp, zp, vp_state, ap_tail = _layer(x_prompt[0], p)
    ys, zs, vs_state, as_all = _layer(x_sample.reshape(t_s, d), p, s_len=s_len, caches=caches, conv_hist=hist)

    o1 = 2 * d_a
    d_qkv = 3 * gw
    prompt_kv, sample_kv = [], []
    for gi, (win, _) in enumerate(DILATION_GROUPS):
        keep = min(win, seq)
        for base in (o1 + d_qkv, o1 + 2 * d_qkv):
            c0 = base + gi * gw
            prompt_kv.append(zp[seq - keep:, c0:c0 + gw].reshape(1, 1, keep, hpg, HEAD_DIM))
            sample_kv.append(zs[:, c0:c0 + gw].reshape(1, n_seq, s_len, hpg, HEAD_DIM))
    p_conv = ap_tail[SUBLANE - (cw - 1):, :d_ff].reshape(1, 1, cw - 1, d_ff)
    s_conv = as_all.reshape(n_seq, s_len, ffp)[:, s_len - (cw - 1):, :d_ff].reshape(1, n_seq, cw - 1, d_ff)
    return (yp.reshape(1, seq, d), ys.reshape(n_seq, s_len, d),
            *prompt_kv, vp_state.reshape(1, 1, CHUNK, d_a), p_conv,
            *sample_kv, vs_state.reshape(1, n_seq, s_len, d_a), s_conv)
```

```python
import functools
import math

import numpy as np
import jax
import jax.numpy as jnp
from jax import lax
from jax.experimental import pallas as pl
from jax.experimental.pallas import tpu as pltpu

F32 = jnp.float32
BF16 = jnp.bfloat16

HEAD_DIM = 128
STEPS = 128
CHUNK = 128
GROUP_DIM_A = 128
DILATION_GROUPS = ((128, 1), (512, 4), (2048, 16))
MAX_EXACT = 16
MAX_DISTANCE = 2048
EPS = 1e-6
NEG = -1e30
ATTN_SCALE = HEAD_DIM ** -0.5

LANE = 128
SUBLANE = 8
VMEM_LIMIT_BYTES = 56 * 1024 * 1024
MATMUL_TILE = 1024
ATTN_UNITS_PER_STEP = 16


def _tile(dim, target, align):
    best = None
    t = align
    while t <= min(dim, target):
        if dim % t == 0:
            best = t
        t += align
    return best if best is not None else dim


def _params(semantics):
    return pltpu.CompilerParams(dimension_semantics=semantics, vmem_limit_bytes=VMEM_LIMIT_BYTES)


def _gelu(x):
    return 0.5 * x * (1.0 + jnp.tanh(math.sqrt(2.0 / math.pi) * (x + 0.044715 * (x * x * x))))


def _sigmoid(x):
    return 1.0 / (1.0 + jnp.exp(-x))


def _rms_scale(x):
    return lax.rsqrt(jnp.mean(x * x, axis=-1, keepdims=True) + EPS)


def _lanes(c):
    return slice(c * LANE, (c + 1) * LANE)


def _rmsnorm_cast_kernel(x_ref, g_ref, o_ref):
    x = x_ref[...]
    o_ref[...] = ((x * _rms_scale(x)) * g_ref[...]).astype(o_ref.dtype)


def _rmsnorm_cast(x, g):
    t, d = x.shape
    bt = _tile(t, 256, SUBLANE)
    return pl.pallas_call(
        _rmsnorm_cast_kernel,
        grid=(t // bt,),
        in_specs=[pl.BlockSpec((bt, d), lambda i: (i, 0)), pl.BlockSpec((1, d), lambda i: (0, 0))],
        out_specs=pl.BlockSpec((bt, d), lambda i: (i, 0)),
        out_shape=jax.ShapeDtypeStruct((t, d), BF16),
        compiler_params=_params(("arbitrary",)),
        name="rmsnorm_cast",
    )(x, g.reshape(1, d))


def _in_proj_kernel(x_ref, w_ref, o_ref, *, n_gelu, n_plain):
    acc = jnp.dot(x_ref[...], w_ref[...], preferred_element_type=F32)
    j = pl.program_id(1)
    n_slabs = o_ref.shape[0]

    def store(val):
        for c in range(n_slabs):
            o_ref[c] = val[:, _lanes(c)]

    @pl.when(j < n_gelu)
    def _():
        store(_gelu(acc))

    @pl.when((j >= n_gelu) & (j < n_gelu + n_plain))
    def _():
        store(acc)

    @pl.when(j >= n_gelu + n_plain)
    def _():
        store(_sigmoid(acc))


def _in_proj(h, w, bn, n_gelu_cols, n_plain_cols):
    t, d = h.shape
    n = w.shape[1]
    bm = _tile(t, MATMUL_TILE, SUBLANE)
    kern = functools.partial(_in_proj_kernel, n_gelu=n_gelu_cols // bn, n_plain=n_plain_cols // bn)
    return pl.pallas_call(
        kern,
        grid=(t // bm, n // bn),
        in_specs=[pl.BlockSpec((bm, d), lambda i, j: (i, 0)), pl.BlockSpec((d, bn), lambda i, j: (0, j))],
        out_specs=pl.BlockSpec((bn // LANE, bm, LANE), lambda i, j: (j, i, 0)),
        out_shape=jax.ShapeDtypeStruct((n // LANE, t, LANE), F32),
        compiler_params=_params(("arbitrary", "arbitrary")),
        name="in_proj",
    )(h, w)


def _sgu_kernel(u_ref, v_ref, lg_ref, lb_ref, w_ref, b_ref, o_ref, vs_ref, *, n_groups):
    vp = v_ref[...]
    n_feat = n_groups * GROUP_DIM_A
    mu = jnp.sum(jnp.sum(vp, axis=0), axis=-1, keepdims=True) / n_feat
    vc = vp - mu
    var = jnp.sum(jnp.sum(vc * vc, axis=0), axis=-1, keepdims=True) / n_feat
    v = vc * lax.rsqrt(var + EPS) * lg_ref[...] + lb_ref[...]
    for g in range(n_groups):
        vs_ref[:, _lanes(g)] = v[g]
        mixed = jnp.dot(w_ref[g], v[g].astype(BF16), preferred_element_type=F32) + b_ref[g]
        o_ref[:, _lanes(g)] = (u_ref[g] * mixed).astype(o_ref.dtype)


def _sgu(z_sl, d_a, ln_g, ln_b, w_mix, b_mix):
    t = z_sl.shape[1]
    n_groups, c, _ = w_mix.shape
    slab = pl.BlockSpec((n_groups, 1, LANE), lambda i: (0, 0, 0))
    return pl.pallas_call(
        functools.partial(_sgu_kernel, n_groups=n_groups),
        grid=(t // c,),
        in_specs=[
            pl.BlockSpec((n_groups, c, LANE), lambda i: (0, i, 0)),
            pl.BlockSpec((n_groups, c, LANE), lambda i: (1, i, 0)),
            slab, slab,
            pl.BlockSpec((n_groups, c, c), lambda i: (0, 0, 0)),
            pl.BlockSpec((n_groups, c, LANE), lambda i: (0, 0, 0)),
        ],
        out_specs=[pl.BlockSpec((c, d_a), lambda i: (i, 0)), pl.BlockSpec((c, d_a), lambda i: (0, 0))],
        out_shape=[jax.ShapeDtypeStruct((t, d_a), BF16), jax.ShapeDtypeStruct((c, d_a), F32)],
        compiler_params=_params(("arbitrary",)),
        name="sgu",
    )(z_sl, z_sl, ln_g.reshape(n_groups, 1, LANE), ln_b.reshape(n_groups, 1, LANE), w_mix, b_mix)


def _t5_bucket(dist, n_buckets):
    n = np.asarray(dist, np.int32)
    safe = np.maximum(n, 1).astype(np.float32)
    large = MAX_EXACT + (np.log(safe / MAX_EXACT) / np.log(np.float32(MAX_DISTANCE / MAX_EXACT))
                         * (n_buckets - MAX_EXACT)).astype(np.int32)
    large = np.minimum(large, n_buckets - 1)
    return np.where(n < MAX_EXACT, n, large).astype(np.int32)


def _bias_lookup(bias_tab, bucket):
    n_buckets = bias_tab.shape[0]
    flat = np.asarray(bucket).reshape(-1)
    onehot = (jnp.asarray(flat)[None, :] == jnp.arange(n_buckets)[:, None]).astype(F32)
    out = jnp.dot(bias_tab.T, onehot, precision=lax.Precision.HIGHEST)
    return out.reshape((bias_tab.shape[1],) + tuple(np.asarray(bucket).shape))


def _attn_prompt_kernel(q_ref, kp_ref, kc_ref, vp_ref, vc_ref, bias_ref, band_ref, o_ref, lse_ref, *, hb, dil):
    b = pl.program_id(1)
    col = lax.broadcasted_iota(jnp.int32, (STEPS, 2 * STEPS), 1)
    valid = (band_ref[...] > 0.5) & ((b > 0) | (col >= STEPS))
    for hh in range(hb):
        for r in range(dil):
            rows = pl.ds(r, STEPS, stride=dil) if dil > 1 else pl.ds(0, STEPS)
            q = q_ref[hh, rows, :].astype(BF16)
            kk = jnp.concatenate([kp_ref[hh, rows, :], kc_ref[hh, rows, :]], axis=0).astype(BF16)
            vv = jnp.concatenate([vp_ref[hh, rows, :], vc_ref[hh, rows, :]], axis=0).astype(BF16)
            s = lax.dot_general(q, kk, (((1,), (1,)), ((), ())), preferred_element_type=F32) * ATTN_SCALE
            s = jnp.where(valid, s + bias_ref[hh], NEG)
            m = jnp.max(s, axis=-1, keepdims=True)
            e = jnp.exp(s - m)
            den = jnp.sum(e, axis=-1, keepdims=True)
            o_ref[hh, rows, :] = jnp.dot(e.astype(BF16), vv, preferred_element_type=F32) / den
            lse_ref[hh, rows, :] = jnp.broadcast_to(m + jnp.log(den), (STEPS, HEAD_DIM))


def _attn_prompt(z_sl, hpg, q_slab, k_slab, v_slab, bias_tab, dil):
    t = z_sl.shape[1]
    rows = STEPS * dil
    assert t % rows == 0
    nb = t // rows
    hb = min(hpg, max(1, ATTN_UNITS_PER_STEP // dil))
    assert hpg % hb == 0 and q_slab % hb == 0 and k_slab % hb == 0 and v_slab % hb == 0

    p_idx = np.arange(STEPS)[:, None]
    c_idx = np.arange(2 * STEPS)[None, :]
    steps = p_idx + STEPS - c_idx
    band = ((steps >= 0) & (steps <= STEPS)).astype(np.float32)
    bias = _bias_lookup(bias_tab, _t5_bucket(np.clip(steps, 0, STEPS) * dil, bias_tab.shape[0]))

    def cur(slab):
        return pl.BlockSpec((hb, rows, LANE), lambda hi, b: (slab // hb + hi, b, 0))

    def prev(slab):
        return pl.BlockSpec((hb, rows, LANE), lambda hi, b: (slab // hb + hi, jnp.maximum(b - 1, 0), 0))

    out_spec = pl.BlockSpec((hb, rows, LANE), lambda hi, b: (hi, b, 0))
    return pl.pallas_call(
        functools.partial(_attn_prompt_kernel, hb=hb, dil=dil),
        grid=(hpg // hb, nb),
        in_specs=[cur(q_slab), prev(k_slab), cur(k_slab), prev(v_slab), cur(v_slab),
                  pl.BlockSpec((hb, STEPS, 2 * STEPS), lambda hi, b: (hi, 0, 0)),
                  pl.BlockSpec((STEPS, 2 * STEPS), lambda hi, b: (0, 0))],
        out_specs=[out_spec, out_spec],
        out_shape=[jax.ShapeDtypeStruct((hpg, t, LANE), F32)] * 2,
        compiler_params=_params(("arbitrary", "arbitrary")),
        name=f"attn_prompt_d{dil}",
    )(z_sl, z_sl, z_sl, z_sl, z_sl, bias, jnp.asarray(band))


def _attn_sample_kernel(q_ref, kn_ref, vn_ref, ck_ref, cv_ref, bc_ref, mc_ref, bnew_ref, mnew_ref,
                        o_ref, lse_ref, *, hpg, s_len, lc):
    b = pl.program_id(0)
    n_new = kn_ref.shape[1]
    col = lax.broadcasted_iota(jnp.int32, (s_len, n_new), 1)
    own = (col >= b * s_len) & (col < (b + 1) * s_len)
    valid_new = (mnew_ref[...] > 0.5) & own
    valid_c = mc_ref[...] > 0.5
    for h in range(hpg):
        rows = pl.ds(h, lc, stride=hpg) if hpg > 1 else pl.ds(0, lc)
        q = q_ref[h].astype(BF16)
        sc = lax.dot_general(q, ck_ref[0, rows, :].astype(BF16), (((1,), (1,)), ((), ())),
                             preferred_element_type=F32) * ATTN_SCALE
        sc = jnp.where(valid_c, sc + bc_ref[h], NEG)
        sn = lax.dot_general(q, kn_ref[h].astype(BF16), (((1,), (1,)), ((), ())),
                             preferred_element_type=F32) * ATTN_SCALE
        sn = jnp.where(valid_new, sn + bnew_ref[h], NEG)
        m = jnp.maximum(jnp.max(sc, axis=-1, keepdims=True), jnp.max(sn, axis=-1, keepdims=True))
        ec = jnp.exp(sc - m)
        en = jnp.exp(sn - m)
        den = jnp.sum(ec, axis=-1, keepdims=True) + jnp.sum(en, axis=-1, keepdims=True)
        acc = jnp.dot(ec.astype(BF16), cv_ref[0, rows, :].astype(BF16), preferred_element_type=F32)
        acc = acc + jnp.dot(en.astype(BF16), vn_ref[h].astype(BF16), preferred_element_type=F32)
        o_ref[h] = acc / den
        lse_ref[h] = jnp.broadcast_to(m + jnp.log(den), (s_len, HEAD_DIM))


def _attn_sample(z_sl, s_len, hpg, q_slab, k_slab, v_slab, cache_k, cache_v, bias_tab, dil):
    t = z_sl.shape[1]
    n_seq = t // s_len
    lc = cache_k.shape[1]
    assert q_slab % hpg == 0 and k_slab % hpg == 0 and v_slab % hpg == 0
    ck = cache_k.reshape(n_seq, lc * hpg, HEAD_DIM)
    cv = cache_v.reshape(n_seq, lc * hpg, HEAD_DIM)

    j = np.arange(STEPS + 1)
    idx = lc + np.arange(s_len)[:, None] - j[None, :] * dil
    assert idx.min() >= 0
    bucket = _t5_bucket(j * dil, bias_tab.shape[0])
    mask = np.zeros((s_len, lc + s_len), np.float32)
    bsel = np.zeros((s_len, lc + s_len), np.int32)
    for s in range(s_len):
        mask[s, idx[s]] = 1.0
        bsel[s, idx[s]] = bucket
    bias_all = _bias_lookup(bias_tab, bsel)
    bias_c = bias_all[:, :, :lc]
    bias_new = jnp.tile(bias_all[:, :, lc:], (1, 1, n_seq))
    mask_c = jnp.asarray(mask[:, :lc])
    mask_new = jnp.asarray(np.tile(mask[:, lc:], (1, n_seq)))

    out_spec = pl.BlockSpec((hpg, s_len, LANE), lambda b: (0, b, 0))
    return pl.pallas_call(
        functools.partial(_attn_sample_kernel, hpg=hpg, s_len=s_len, lc=lc),
        grid=(n_seq,),
        in_specs=[pl.BlockSpec((hpg, s_len, LANE), lambda b: (q_slab // hpg, b, 0)),
                  pl.BlockSpec((hpg, t, LANE), lambda b: (k_slab // hpg, 0, 0)),
                  pl.BlockSpec((hpg, t, LANE), lambda b: (v_slab // hpg, 0, 0)),
                  pl.BlockSpec((1, lc * hpg, LANE), lambda b: (b, 0, 0)),
                  pl.BlockSpec((1, lc * hpg, LANE), lambda b: (b, 0, 0)),
                  pl.BlockSpec((hpg, s_len, lc), lambda b: (0, 0, 0)),
                  pl.BlockSpec((s_len, lc), lambda b: (0, 0)),
                  pl.BlockSpec((hpg, s_len, t), lambda b: (0, 0, 0)),
                  pl.BlockSpec((s_len, t), lambda b: (0, 0))],
        out_specs=[out_spec, out_spec],
        out_shape=[jax.ShapeDtypeStruct((hpg, t, LANE), F32)] * 2,
        compiler_params=_params(("arbitrary",)),
        name=f"attn_sample_d{dil}",
    )(z_sl, z_sl, z_sl, ck, cv, bias_c, mask_c, bias_new, mask_new)


def _merge_kernel(o1, o2, o3, l1, l2, l3, out_ref):
    a, b, c = l1[...], l2[...], l3[...]
    m = jnp.maximum(jnp.maximum(a, b), c)
    ea, eb, ec = jnp.exp(a - m), jnp.exp(b - m), jnp.exp(c - m)
    merged = (ea * o1[...] + eb * o2[...] + ec * o3[...]) / (ea + eb + ec)
    for h in range(merged.shape[0]):
        out_ref[:, _lanes(h)] = merged[h].astype(out_ref.dtype)


def _merge(outs, lses):
    hpg, t, _ = outs[0].shape
    bt = _tile(t, 256, SUBLANE)
    spec = pl.BlockSpec((hpg, bt, LANE), lambda i: (0, i, 0))
    return pl.pallas_call(
        _merge_kernel,
        grid=(t // bt,),
        in_specs=[spec] * 6,
        out_specs=pl.BlockSpec((bt, hpg * LANE), lambda i: (i, 0)),
        out_shape=jax.ShapeDtypeStruct((t, hpg * LANE), BF16),
        compiler_params=_params(("arbitrary",)),
        name="merge_groups",
    )(*outs, *lses)


def _gated_proj_kernel(a_ref, b_ref, wa_ref, wb_ref, ga_ref, gb_ref, o_ref):
    pa = jnp.dot(a_ref[...], wa_ref[...], preferred_element_type=F32)
    pb = jnp.dot(b_ref[...], wb_ref[...], preferred_element_type=F32)
    for c in range(ga_ref.shape[0]):
        o_ref[:, _lanes(c)] = (ga_ref[c] * pa[:, _lanes(c)] + gb_ref[c] * pb[:, _lanes(c)]).astype(o_ref.dtype)


def _gated_proj(o_a, o_b, w_a, w_b, z_sl, bn, gate_a_blk, gate_b_blk):
    t, d_a = o_a.shape
    d_b = o_b.shape[1]
    d = w_a.shape[1]
    bm = _tile(t, MATMUL_TILE // 2, SUBLANE)
    return pl.pallas_call(
        _gated_proj_kernel,
        grid=(t // bm, d // bn),
        in_specs=[pl.BlockSpec((bm, d_a), lambda i, j: (i, 0)),
                  pl.BlockSpec((bm, d_b), lambda i, j: (i, 0)),
                  pl.BlockSpec((d_a, bn), lambda i, j: (0, j)),
                  pl.BlockSpec((d_b, bn), lambda i, j: (0, j)),
                  pl.BlockSpec((bn // LANE, bm, LANE), lambda i, j: (gate_a_blk + j, i, 0)),
                  pl.BlockSpec((bn // LANE, bm, LANE), lambda i, j: (gate_b_blk + j, i, 0))],
        out_specs=pl.BlockSpec((bm, bn), lambda i, j: (i, j)),
        out_shape=jax.ShapeDtypeStruct((t, d), BF16),
        compiler_params=_params(("arbitrary", "arbitrary")),
        name="gated_proj",
    )(o_a, o_b, w_a, w_b, z_sl, z_sl)


def _matmul_kernel(x_ref, w_ref, o_ref, acc_ref, *, nk):
    k = pl.program_id(2)
    part = jnp.dot(x_ref[...], w_ref[...], preferred_element_type=F32)
    if nk == 1:
        o_ref[...] = part
    else:
        @pl.when(k == 0)
        def _():
            acc_ref[...] = part

        @pl.when((k > 0) & (k < nk - 1))
        def _():
            acc_ref[...] += part

        @pl.when(k == nk - 1)
        def _():
            o_ref[...] = acc_ref[...] + part


def _matmul(x, w, name, k_tile=None):
    t, kd = x.shape
    n = w.shape[1]
    bm = _tile(t, MATMUL_TILE, SUBLANE)
    bn = _tile(n, MATMUL_TILE, LANE)
    bk = kd if k_tile is None else k_tile
    nk = kd // bk
    return pl.pallas_call(
        functools.partial(_matmul_kernel, nk=nk),
        grid=(t // bm, n // bn, nk),
        in_specs=[pl.BlockSpec((bm, bk), lambda i, j, k: (i, k)),
                  pl.BlockSpec((bk, bn), lambda i, j, k: (k, j))],
        out_specs=pl.BlockSpec((bm, bn), lambda i, j, k: (i, j)),
        out_shape=jax.ShapeDtypeStruct((t, n), F32),
        scratch_shapes=[pltpu.VMEM((bm, bn) if nk > 1 else (SUBLANE, LANE), F32)],
        compiler_params=_params(("arbitrary", "arbitrary", "arbitrary")),
        name=name,
    )(x, w)


def _post_mix_kernel(x_ref, y_ref, g1_ref, g2_ref, x1_ref, h2_ref):
    y = y_ref[...]
    x1 = x_ref[...] + (y * _rms_scale(y)) * g1_ref[...]
    x1_ref[...] = x1
    h2_ref[...] = ((x1 * _rms_scale(x1)) * g2_ref[...]).astype(h2_ref.dtype)


def _post_mix(x, y, g_post, g_pre):
    t, d = x.shape
    bt = _tile(t, 256, SUBLANE)
    row = pl.BlockSpec((bt, d), lambda i: (i, 0))
    vec = pl.BlockSpec((1, d), lambda i: (0, 0))
    return pl.pallas_call(
        _post_mix_kernel,
        grid=(t // bt,),
        in_specs=[row, row, vec, vec],
        out_specs=[row, row],
        out_shape=[jax.ShapeDtypeStruct((t, d), F32), jax.ShapeDtypeStruct((t, d), BF16)],
        compiler_params=_params(("arbitrary",)),
        name="post_mix",
    )(x, y, g_post.reshape(1, d), g_pre.reshape(1, d))


def _residual_norm_kernel(x_ref, y_ref, g_ref, o_ref):
    y = y_ref[...]
    o_ref[...] = x_ref[...] + (y * _rms_scale(y)) * g_ref[...]


def _residual_norm(x, y, g):
    t, d = x.shape
    bt = _tile(t, 256, SUBLANE)
    row = pl.BlockSpec((bt, d), lambda i: (i, 0))
    return pl.pallas_call(
        _residual_norm_kernel,
        grid=(t // bt,),
        in_specs=[row, row, pl.BlockSpec((1, d), lambda i: (0, 0))],
        out_specs=row,
        out_shape=jax.ShapeDtypeStruct((t, d), F32),
        compiler_params=_params(("arbitrary",)),
        name="residual_norm",
    )(x, y, g.reshape(1, d))


def _ffn_up_kernel(*refs, conv_w, s_len, d_ff):
    if s_len is None:
        x_ref, wg_ref, wu_ref, cw_ref, cb_ref, out_ref, tail_ref, buf_ref = refs
    else:
        x_ref, wg_ref, wu_ref, cw_ref, cb_ref, h1_ref, h2_ref, out_ref, tail_ref, buf_ref = refs
    bm, bn = out_ref.shape
    x = x_ref[...]
    a = jnp.dot(x, wg_ref[...], preferred_element_type=F32)
    u = jnp.dot(x, wu_ref[...], preferred_element_type=F32)

    @pl.when(pl.program_id(1) == 0)
    def _():
        buf_ref[0:SUBLANE, :] = jnp.zeros((SUBLANE, bn), F32)

    buf_ref[SUBLANE:SUBLANE + bm, :] = a
    taps = [buf_ref[SUBLANE - (conv_w - 1 - i):SUBLANE - (conv_w - 1 - i) + bm, :] for i in range(conv_w - 1)]
    if s_len is None:
        tail_ref[...] = a[bm - SUBLANE:, :]
        buf_ref[0:SUBLANE, :] = a[bm - SUBLANE:, :]
    else:
        tail_ref[...] = a
        s = lax.broadcasted_iota(jnp.int32, a.shape, 0) % s_len
        hist = [h2_ref[...], h1_ref[...]]
        taps = [jnp.where(s < (conv_w - 1 - i), hist[i], taps[i]) for i in range(conv_w - 1)]
    c = cw_ref[0:1, :] * taps[0]
    for i in range(1, conv_w - 1):
        c = c + cw_ref[i:i + 1, :] * taps[i]
    c = c + cw_ref[conv_w - 1:conv_w, :] * a
    c = cb_ref[...] + c
    col = pl.program_id(0) * bn + lax.broadcasted_iota(jnp.int32, (bm, bn), 1)
    out_ref[...] = jnp.where(col < d_ff, _gelu(c) * u, 0.0).astype(out_ref.dtype)


def _ffn_up(h, w_gate, w_up, conv_w, conv_b, ffp, hist=None, s_len=None):
    t, d = h.shape
    d_ff = w_gate.shape[1]
    cw = conv_w.shape[0]
    assert cw == 3
    bn = _tile(ffp, MATMUL_TILE, LANE)
    bm = _tile(t, 512, SUBLANE)
    x_spec = pl.BlockSpec((bm, d), lambda j, i: (i, 0))
    w_spec = pl.BlockSpec((d, bn), lambda j, i: (0, j))
    cw_spec = pl.BlockSpec((cw, bn), lambda j, i: (0, j))
    cb_spec = pl.BlockSpec((1, bn), lambda j, i: (0, j))
    out_spec = pl.BlockSpec((bm, bn), lambda j, i: (i, j))
    in_specs = [x_spec, w_spec, w_spec, cw_spec, cb_spec]
    args = [h, w_gate, w_up, conv_w, conv_b.reshape(1, d_ff)]
    if s_len is None:
        tail_rows = SUBLANE
        tail_spec = pl.BlockSpec((SUBLANE, bn), lambda j, i: (0, j))
    else:
        assert bm == t
        tail_rows = t
        tail_spec = out_spec
        n_seq = t // s_len
        h1 = jnp.pad(hist[:, 1:2], ((0, 0), (0, s_len - 1), (0, 0))).reshape(t, d_ff)
        h2 = jnp.pad(hist, ((0, 0), (0, s_len - hist.shape[1]), (0, 0))).reshape(t, d_ff)
        in_specs += [out_spec, out_spec]
        args += [h1, h2]
    return pl.pallas_call(
        functools.partial(_ffn_up_kernel, conv_w=cw, s_len=s_len, d_ff=d_ff),
        grid=(ffp // bn, t // bm),
        in_specs=in_specs,
        out_specs=[out_spec, tail_spec],
        out_shape=[jax.ShapeDtypeStruct((t, ffp), BF16), jax.ShapeDtypeStruct((tail_rows, d_ff), F32)],
        scratch_shapes=[pltpu.VMEM((bm + SUBLANE, bn), F32)],
        compiler_params=_params(("arbitrary", "arbitrary")),
        name="ffn_up",
    )(*args)


def _layer(x, p, s_len=None, caches=None, conv_hist=None):
    t, d = x.shape
    d_a = p["ln_g"].shape[0]
    hpg = p["hpg"]
    gw = hpg * HEAD_DIM
    bn = p["bn"]
    o1 = 2 * d_a
    d_qkv = 3 * gw

    h = _rmsnorm_cast(x, p["g_pre_mix"])
    z_sl = _in_proj(h, p["w_in"], bn, o1, 3 * d_qkv)

    if s_len is None:
        w_mix, b_mix = p["w_mix_prompt"], p["b_mix_prompt"]
    else:
        w_mix, b_mix = p["w_mix_sample"], p["b_mix_sample"]
    o_a, v_state = _sgu(z_sl, d_a, p["ln_g"], p["ln_b"], w_mix, b_mix)

    outs, lses = [], []
    for gi, (_, dil) in enumerate(DILATION_GROUPS):
        q_slab = (o1 + gi * gw) // LANE
        k_slab = (o1 + d_qkv + gi * gw) // LANE
        v_slab = (o1 + 2 * d_qkv + gi * gw) // LANE
        bias_tab = p["rel_bias"][:, gi * hpg:(gi + 1) * hpg]
        if s_len is None:
            o, lse = _attn_prompt(z_sl, hpg, q_slab, k_slab, v_slab, bias_tab, dil)
        else:
            o, lse = _attn_sample(z_sl, s_len, hpg, q_slab, k_slab, v_slab, caches[2 * gi], caches[2 * gi + 1],
                                  bias_tab, dil)
        outs.append(o)
        lses.append(lse)
    o_b = _merge(outs, lses)

    gate_a_blk = (o1 + 3 * d_qkv) // bn
    merged = _gated_proj(o_a, o_b, p["w_proj_a"], p["w_proj_b"], z_sl, bn, gate_a_blk, gate_a_blk + d // bn)
    y = _matmul(merged, p["w_out"], "out_proj")
    x1, h2 = _post_mix(x, y, p["g_post_mix"], p["g_pre_ffn"])

    act, a_tail = _ffn_up(h2, p["w_gate"], p["w_up"], p["conv_w"], p["conv_b"], p["ffp"], conv_hist, s_len)
    f = _matmul(act, p["w_down"], "ffn_down", k_tile=p["ffn_k_tile"])
    y_out = _residual_norm(x1, f, p["g_post_ffn"])
    return y_out, z_sl, v_state, a_tail


def kernel(x_prompt, x_sample, cache_k_g1, cache_v_g1, cache_k_g2, cache_v_g2, cache_k_g3, cache_v_g3, state_conv, g_pre_mix, w_in, sgu_ln_g, sgu_ln_b, w_spatial, b_spatial, rel_bias, w_proj_a, w_proj_b, w_out, g_post_mix, g_pre_ffn, w_gate, w_up, conv_w, conv_b, w_down, g_post_ffn):
    depth = w_in.shape[0]
    assert depth == 1
    n_prompt, seq, d = x_prompt.shape
    assert n_prompt == 1 and seq % CHUNK == 0
    n_seq, s_len, _ = x_sample.shape
    assert s_len == SUBLANE
    d_a = sgu_ln_g.shape[1]
    n_groups = w_spatial.shape[1]
    n_heads = rel_bias.shape[1]
    hpg = n_heads // len(DILATION_GROUPS)
    gw = hpg * HEAD_DIM
    d_ff = w_gate.shape[2]
    cw = conv_w.shape[1]
    bn = _tile(math.gcd(2 * d_a, gw), MATMUL_TILE, LANE)
    ffp = -(-d_ff // MATMUL_TILE) * MATMUL_TILE if d_ff > MATMUL_TILE else d_ff
    t_s = n_seq * s_len

    tri = np.tril(np.ones((CHUNK, CHUNK), np.float32))
    w_mix_prompt = (w_spatial[0] * tri).astype(BF16)
    b_mix_prompt = jnp.broadcast_to(b_spatial[0][:, :, None], (n_groups, CHUNK, LANE))
    w_small = w_spatial[0][:, :s_len, :s_len] * tri[:s_len, :s_len]
    eye = np.eye(n_seq, dtype=np.float32)
    w_mix_sample = jnp.einsum("ab,gpq->gapbq", eye, w_small).reshape(n_groups, t_s, t_s).astype(BF16)
    b_mix_sample = jnp.broadcast_to(jnp.tile(b_spatial[0][:, :s_len], (1, n_seq))[:, :, None], (n_groups, t_s, LANE))

    p = dict(
        hpg=hpg, bn=bn, ffp=ffp,
        g_pre_mix=g_pre_mix[0], w_in=w_in[0].astype(BF16), ln_g=sgu_ln_g[0], ln_b=sgu_ln_b[0],
        w_mix_prompt=w_mix_prompt, b_mix_prompt=b_mix_prompt,
        w_mix_sample=w_mix_sample, b_mix_sample=b_mix_sample,
        rel_bias=rel_bias,
        w_proj_a=w_proj_a[0].astype(BF16), w_proj_b=w_proj_b[0].astype(BF16), w_out=w_out[0].astype(BF16),
        g_post_mix=g_post_mix[0], g_pre_ffn=g_pre_ffn[0],
        w_gate=w_gate[0].astype(BF16), w_up=w_up[0].astype(BF16),
        conv_w=conv_w[0], conv_b=conv_b[0],
        w_down=jnp.pad(w_down[0].astype(BF16), ((0, ffp - d_ff), (0, 0))),
        ffn_k_tile=_tile(ffp, 3072, LANE),
        g_post_ffn=g_post_ffn[0],
    )

    caches = tuple(c[0] for c in (cache_k_g1, cache_v_g1, cache_k_g2, cache_v_g2, cache_k_g3, cache_v_g3))

    yp, zp, vp_state, ap_tail = _layer(x_prompt[0], p)
    ys, zs, vs_state, as_all = _layer(x_sample.reshape(t_s, d), p, s_len=s_len, caches=caches,
                                      conv_hist=state_conv[0])

    o1 = 2 * d_a
    d_qkv = 3 * gw
    prompt_kv, sample_kv = [], []
    for gi, (win, _) in enumerate(DILATION_GROUPS):
        keep = min(win, seq)
        for base in (o1 + d_qkv, o1 + 2 * d_qkv):
            s0 = (base + gi * gw) // LANE
            pk = jnp.transpose(zp[s0:s0 + hpg, seq - keep:, :], (1, 0, 2))
            prompt_kv.append(pk.reshape(1, 1, keep, hpg, HEAD_DIM))
            sk = jnp.transpose(zs[s0:s0 + hpg], (1, 0, 2))
            sample_kv.append(sk.reshape(1, n_seq, s_len, hpg, HEAD_DIM))
    p_conv = ap_tail[SUBLANE - (cw - 1):].reshape(1, 1, cw - 1, d_ff)
    s_conv = as_all.reshape(n_seq, s_len, d_ff)[:, s_len - (cw - 1):].reshape(1, n_seq, cw - 1, d_ff)
    return (yp.reshape(1, seq, d), ys.reshape(n_seq, s_len, d),
            *prompt_kv, vp_state.reshape(1, 1, CHUNK, d_a), p_conv,
            *sample_kv, vs_state.reshape(1, n_seq, s_len, d_a), s_conv)
```

```python
import functools
import math

import numpy as np
import jax
import jax.numpy as jnp
from jax import lax
from jax.experimental import pallas as pl
from jax.experimental.pallas import tpu as pltpu

F32 = jnp.float32
BF16 = jnp.bfloat16

HEAD_DIM = 128
STEPS = 128
CHUNK = 128
GROUP_DIM_A = 128
DILATION_GROUPS = ((128, 1), (512, 4), (2048, 16))
MAX_EXACT = 16
MAX_DISTANCE = 2048
EPS = 1e-6
NEG = -1e30
ATTN_SCALE = HEAD_DIM ** -0.5

LANE = 128
SUBLANE = 8
VMEM_LIMIT_BYTES = 56 * 1024 * 1024
MATMUL_TILE = 1024
MXU_WIDTH = 256
FFN_TILE = 512
ATTN_UNITS_PER_STEP = 16


def _tile(dim, target, align):
    best = None
    t = align
    while t <= min(dim, target):
        if dim % t == 0:
            best = t
        t += align
    return best if best is not None else dim


def _params(semantics):
    return pltpu.CompilerParams(dimension_semantics=semantics, vmem_limit_bytes=VMEM_LIMIT_BYTES)


def _gelu(x):
    return 0.5 * x * (1.0 + jnp.tanh(math.sqrt(2.0 / math.pi) * (x + 0.044715 * (x * x * x))))


def _sigmoid(x):
    return 1.0 / (1.0 + jnp.exp(-x))


def _rms_scale(x):
    return lax.rsqrt(jnp.mean(x * x, axis=-1, keepdims=True) + EPS)


def _lanes(c):
    return slice(c * LANE, (c + 1) * LANE)


def _rmsnorm_cast_kernel(x_ref, g_ref, o_ref):
    x = x_ref[...]
    o_ref[...] = ((x * _rms_scale(x)) * g_ref[...]).astype(o_ref.dtype)


def _rmsnorm_cast(x, g):
    t, d = x.shape
    bt = _tile(t, 256, SUBLANE)
    return pl.pallas_call(
        _rmsnorm_cast_kernel,
        grid=(t // bt,),
        in_specs=[pl.BlockSpec((bt, d), lambda i: (i, 0)), pl.BlockSpec((1, d), lambda i: (0, 0))],
        out_specs=pl.BlockSpec((bt, d), lambda i: (i, 0)),
        out_shape=jax.ShapeDtypeStruct((t, d), BF16),
        compiler_params=_params(("arbitrary",)),
        name="rmsnorm_cast",
    )(x, g.reshape(1, d))


def _two_phase(step, acc_a, acc_b, body):
    @pl.when(step == 0)
    def _():
        acc_b[...] = jnp.zeros(acc_b.shape, acc_b.dtype)

    @pl.when(step % 2 == 0)
    def _():
        body(acc_a, acc_b)

    @pl.when(step % 2 == 1)
    def _():
        body(acc_b, acc_a)


def _in_proj_lagged_kernel(x_ref, w_ref, o_ref, acc_a, acc_b, *, act):
    n_slabs = o_ref.shape[0]
    per_strip = min(n_slabs, MXU_WIDTH // LANE)

    def body(fill, drain):
        for c0 in range(0, n_slabs, per_strip):
            cols = slice(c0 * LANE, (c0 + per_strip) * LANE)
            fill[:, cols] = jnp.dot(x_ref[...], w_ref[:, cols], preferred_element_type=F32)
            for c in range(c0, c0 + per_strip):
                o_ref[c] = act(drain[:, _lanes(c)])

    _two_phase(pl.program_id(0), acc_a, acc_b, body)


def _in_proj_kernel(x_ref, w_ref, o_ref, *, act):
    acc = jnp.dot(x_ref[...], w_ref[...], preferred_element_type=F32)
    for c in range(o_ref.shape[0]):
        o_ref[c] = act(acc[:, _lanes(c)])


def _in_proj(h, w, bn, col0, n_cols, act, name, lagged):
    t, d = h.shape
    bm = _tile(t, MATMUL_TILE, SUBLANE)
    nj = n_cols // bn
    n_tiles = (t // bm) * nj
    lag = 1 if lagged else 0

    def mm_tile(s):
        return jnp.minimum(s, n_tiles - 1)

    def out_tile(s):
        return jnp.maximum(s - lag, 0)

    acc = pltpu.VMEM((bm, bn), F32)
    return pl.pallas_call(
        functools.partial(_in_proj_lagged_kernel if lagged else _in_proj_kernel, act=act),
        grid=(n_tiles + lag,),
        in_specs=[pl.BlockSpec((bm, d), lambda s: (mm_tile(s) // nj, 0)),
                  pl.BlockSpec((d, bn), lambda s: (0, col0 // bn + mm_tile(s) % nj))],
        out_specs=pl.BlockSpec((bn // LANE, bm, LANE), lambda s: (out_tile(s) % nj, out_tile(s) // nj, 0)),
        out_shape=jax.ShapeDtypeStruct((n_cols // LANE, t, LANE), F32),
        scratch_shapes=[acc, acc] if lagged else [],
        compiler_params=_params(("arbitrary",)),
        name=name,
    )(h, w)


def _sgu_kernel(u_ref, v_ref, lg_ref, lb_ref, w_ref, b_ref, o_ref, vs_ref, *, n_groups):
    vp = v_ref[...]
    n_feat = n_groups * GROUP_DIM_A
    mu = jnp.sum(jnp.sum(vp, axis=0), axis=-1, keepdims=True) / n_feat
    vc = vp - mu
    var = jnp.sum(jnp.sum(vc * vc, axis=0), axis=-1, keepdims=True) / n_feat
    v = vc * lax.rsqrt(var + EPS) * lg_ref[...] + lb_ref[...]
    for g in range(n_groups):
        vs_ref[:, _lanes(g)] = v[g]
        mixed = jnp.dot(w_ref[g], v[g].astype(BF16), preferred_element_type=F32) + b_ref[g]
        o_ref[:, _lanes(g)] = (u_ref[g] * mixed).astype(o_ref.dtype)


def _sgu(z_sl, d_a, ln_g, ln_b, w_mix, b_mix):
    t = z_sl.shape[1]
    n_groups, c, _ = w_mix.shape
    slab = pl.BlockSpec((n_groups, 1, LANE), lambda i: (0, 0, 0))
    return pl.pallas_call(
        functools.partial(_sgu_kernel, n_groups=n_groups),
        grid=(t // c,),
        in_specs=[
            pl.BlockSpec((n_groups, c, LANE), lambda i: (0, i, 0)),
            pl.BlockSpec((n_groups, c, LANE), lambda i: (1, i, 0)),
            slab, slab,
            pl.BlockSpec((n_groups, c, c), lambda i: (0, 0, 0)),
            pl.BlockSpec((n_groups, c, LANE), lambda i: (0, 0, 0)),
        ],
        out_specs=[pl.BlockSpec((c, d_a), lambda i: (i, 0)), pl.BlockSpec((c, d_a), lambda i: (0, 0))],
        out_shape=[jax.ShapeDtypeStruct((t, d_a), BF16), jax.ShapeDtypeStruct((c, d_a), F32)],
        compiler_params=_params(("arbitrary",)),
        name="sgu",
    )(z_sl, z_sl, ln_g.reshape(n_groups, 1, LANE), ln_b.reshape(n_groups, 1, LANE), w_mix, b_mix)


def _t5_bucket(dist, n_buckets):
    n = np.asarray(dist, np.int32)
    safe = np.maximum(n, 1).astype(np.float32)
    large = MAX_EXACT + (np.log(safe / MAX_EXACT) / np.log(np.float32(MAX_DISTANCE / MAX_EXACT))
                         * (n_buckets - MAX_EXACT)).astype(np.int32)
    large = np.minimum(large, n_buckets - 1)
    return np.where(n < MAX_EXACT, n, large).astype(np.int32)


def _bias_lookup(bias_tab, bucket):
    n_buckets = bias_tab.shape[0]
    flat = np.asarray(bucket).reshape(-1)
    onehot = (jnp.asarray(flat)[None, :] == jnp.arange(n_buckets)[:, None]).astype(F32)
    out = jnp.dot(bias_tab.T, onehot, precision=lax.Precision.HIGHEST)
    return out.reshape((bias_tab.shape[1],) + tuple(np.asarray(bucket).shape))


def _attn_prompt_kernel(q_ref, kp_ref, kc_ref, vp_ref, vc_ref, bias_ref, band_ref, o_ref, lse_ref, *, hb, dil):
    b = pl.program_id(1)
    col = lax.broadcasted_iota(jnp.int32, (STEPS, 2 * STEPS), 1)
    valid = (band_ref[...] > 0.5) & ((b > 0) | (col >= STEPS))
    for hh in range(hb):
        for r in range(dil):
            rows = pl.ds(r, STEPS, stride=dil) if dil > 1 else pl.ds(0, STEPS)
            q = q_ref[hh, rows, :].astype(BF16)
            kk = jnp.concatenate([kp_ref[hh, rows, :], kc_ref[hh, rows, :]], axis=0).astype(BF16)
            vv = jnp.concatenate([vp_ref[hh, rows, :], vc_ref[hh, rows, :]], axis=0).astype(BF16)
            s = lax.dot_general(q, kk, (((1,), (1,)), ((), ())), preferred_element_type=F32) * ATTN_SCALE
            s = jnp.where(valid, s + bias_ref[hh], NEG)
            m = jnp.max(s, axis=-1, keepdims=True)
            e = jnp.exp(s - m)
            den = jnp.sum(e, axis=-1, keepdims=True)
            o_ref[hh, rows, :] = jnp.dot(e.astype(BF16), vv, preferred_element_type=F32) / den
            lse_ref[hh, rows, :] = jnp.broadcast_to(m + jnp.log(den), (STEPS, HEAD_DIM))


def _attn_prompt(z_sl, hpg, q_slab, k_slab, v_slab, bias_tab, dil):
    t = z_sl.shape[1]
    rows = STEPS * dil
    assert t % rows == 0
    nb = t // rows
    hb = min(hpg, max(1, ATTN_UNITS_PER_STEP // dil))
    assert hpg % hb == 0 and q_slab % hb == 0 and k_slab % hb == 0 and v_slab % hb == 0

    p_idx = np.arange(STEPS)[:, None]
    c_idx = np.arange(2 * STEPS)[None, :]
    steps = p_idx + STEPS - c_idx
    band = ((steps >= 0) & (steps <= STEPS)).astype(np.float32)
    bias = _bias_lookup(bias_tab, _t5_bucket(np.clip(steps, 0, STEPS) * dil, bias_tab.shape[0]))

    def cur(slab):
        return pl.BlockSpec((hb, rows, LANE), lambda hi, b: (slab // hb + hi, b, 0))

    def prev(slab):
        return pl.BlockSpec((hb, rows, LANE), lambda hi, b: (slab // hb + hi, jnp.maximum(b - 1, 0), 0))

    out_spec = pl.BlockSpec((hb, rows, LANE), lambda hi, b: (hi, b, 0))
    return pl.pallas_call(
        functools.partial(_attn_prompt_kernel, hb=hb, dil=dil),
        grid=(hpg // hb, nb),
        in_specs=[cur(q_slab), prev(k_slab), cur(k_slab), prev(v_slab), cur(v_slab),
                  pl.BlockSpec((hb, STEPS, 2 * STEPS), lambda hi, b: (hi, 0, 0)),
                  pl.BlockSpec((STEPS, 2 * STEPS), lambda hi, b: (0, 0))],
        out_specs=[out_spec, out_spec],
        out_shape=[jax.ShapeDtypeStruct((hpg, t, LANE), F32)] * 2,
        compiler_params=_params(("arbitrary", "arbitrary")),
        name=f"attn_prompt_d{dil}",
    )(z_sl, z_sl, z_sl, z_sl, z_sl, bias, jnp.asarray(band))


def _attn_sample_kernel(q_ref, kn_ref, vn_ref, ck_ref, cv_ref, bc_ref, mc_ref, bnew_ref, mnew_ref,
                        o_ref, lse_ref, *, hpg, s_len, grouped):
    b = pl.program_id(0)
    n_new = kn_ref.shape[1]
    n_keys = mc_ref.shape[1]
    col = lax.broadcasted_iota(jnp.int32, (s_len, n_new), 1)
    own = (col >= b * s_len) & (col < (b + 1) * s_len)
    valid_new = (mnew_ref[...] > 0.5) & own
    valid_c = mc_ref[...] > 0.5

    def cache_head(ref, h):
        if grouped:
            return ref[0, :, pl.ds(h, s_len, stride=hpg), :].reshape(n_keys, HEAD_DIM).astype(BF16)
        return ref[0, pl.ds(h, n_keys, stride=hpg) if hpg > 1 else pl.ds(0, n_keys), :].astype(BF16)

    for h in range(hpg):
        q = q_ref[h].astype(BF16)
        sc = lax.dot_general(q, cache_head(ck_ref, h), (((1,), (1,)), ((), ())),
                             preferred_element_type=F32) * ATTN_SCALE
        sc = jnp.where(valid_c, sc + bc_ref[h], NEG)
        sn = lax.dot_general(q, kn_ref[h].astype(BF16), (((1,), (1,)), ((), ())),
                             preferred_element_type=F32) * ATTN_SCALE
        sn = jnp.where(valid_new, sn + bnew_ref[h], NEG)
        m = jnp.maximum(jnp.max(sc, axis=-1, keepdims=True), jnp.max(sn, axis=-1, keepdims=True))
        ec = jnp.exp(sc - m)
        en = jnp.exp(sn - m)
        den = jnp.sum(ec, axis=-1, keepdims=True) + jnp.sum(en, axis=-1, keepdims=True)
        acc = jnp.dot(ec.astype(BF16), cache_head(cv_ref, h), preferred_element_type=F32)
        acc = acc + jnp.dot(en.astype(BF16), vn_ref[h].astype(BF16), preferred_element_type=F32)
        o_ref[h] = acc / den
        lse_ref[h] = jnp.broadcast_to(m + jnp.log(den), (s_len, HEAD_DIM))


def _attn_sample(z_sl, s_len, hpg, q_slab, k_slab, v_slab, cache_k, cache_v, bias_tab, dil):
    t = z_sl.shape[1]
    n_seq = t // s_len
    lc = cache_k.shape[1]
    assert q_slab % hpg == 0 and k_slab % hpg == 0 and v_slab % hpg == 0

    j = np.arange(STEPS + 1)
    idx = lc + np.arange(s_len)[:, None] - j[None, :] * dil
    assert idx.min() >= 0
    bucket = _t5_bucket(j * dil, bias_tab.shape[0])
    mask = np.zeros((s_len, lc + s_len), np.float32)
    bsel = np.zeros((s_len, lc + s_len), np.int32)
    for s in range(s_len):
        mask[s, idx[s]] = 1.0
        bsel[s, idx[s]] = bucket

    grouped = dil > s_len and lc % dil == 0 and s_len == SUBLANE
    if grouped:
        pos = (np.arange(lc // dil)[:, None] * dil + np.arange(s_len)[None, :]).reshape(-1)
        assert mask[:, :lc].sum() == mask[:, pos].sum()
        ck = cache_k.reshape(n_seq, lc // dil, dil * hpg, HEAD_DIM)
        cv = cache_v.reshape(n_seq, lc // dil, dil * hpg, HEAD_DIM)
        cache_spec = pl.BlockSpec((1, lc // dil, s_len * hpg, LANE), lambda b: (b, 0, 0, 0))
    else:
        pos = np.arange(lc)
        ck = cache_k.reshape(n_seq, lc * hpg, HEAD_DIM)
        cv = cache_v.reshape(n_seq, lc * hpg, HEAD_DIM)
        cache_spec = pl.BlockSpec((1, lc * hpg, LANE), lambda b: (b, 0, 0))
    n_keys = len(pos)
    bias_c = _bias_lookup(bias_tab, bsel[:, pos])
    bias_new = jnp.tile(_bias_lookup(bias_tab, bsel[:, lc:]), (1, 1, n_seq))
    mask_c = jnp.asarray(mask[:, pos])
    mask_new = jnp.asarray(np.tile(mask[:, lc:], (1, n_seq)))

    out_spec = pl.BlockSpec((hpg, s_len, LANE), lambda b: (0, b, 0))
    return pl.pallas_call(
        functools.partial(_attn_sample_kernel, hpg=hpg, s_len=s_len, grouped=grouped),
        grid=(n_seq,),
        in_specs=[pl.BlockSpec((hpg, s_len, LANE), lambda b: (q_slab // hpg, b, 0)),
                  pl.BlockSpec((hpg, t, LANE), lambda b: (k_slab // hpg, 0, 0)),
                  pl.BlockSpec((hpg, t, LANE), lambda b: (v_slab // hpg, 0, 0)),
                  cache_spec, cache_spec,
                  pl.BlockSpec((hpg, s_len, n_keys), lambda b: (0, 0, 0)),
                  pl.BlockSpec((s_len, n_keys), lambda b: (0, 0)),
                  pl.BlockSpec((hpg, s_len, t), lambda b: (0, 0, 0)),
                  pl.BlockSpec((s_len, t), lambda b: (0, 0))],
        out_specs=[out_spec, out_spec],
        out_shape=[jax.ShapeDtypeStruct((hpg, t, LANE), F32)] * 2,
        compiler_params=_params(("arbitrary",)),
        name=f"attn_sample_d{dil}",
    )(z_sl, z_sl, z_sl, ck, cv, bias_c, mask_c, bias_new, mask_new)


def _merge_kernel(o1, o2, o3, l1, l2, l3, out_ref):
    a, b, c = l1[...], l2[...], l3[...]
    m = jnp.maximum(jnp.maximum(a, b), c)
    ea, eb, ec = jnp.exp(a - m), jnp.exp(b - m), jnp.exp(c - m)
    merged = (ea * o1[...] + eb * o2[...] + ec * o3[...]) / (ea + eb + ec)
    for h in range(merged.shape[0]):
        out_ref[:, _lanes(h)] = merged[h].astype(out_ref.dtype)


def _merge(outs, lses):
    hpg, t, _ = outs[0].shape
    bt = _tile(t, 256, SUBLANE)
    spec = pl.BlockSpec((hpg, bt, LANE), lambda i: (0, i, 0))
    return pl.pallas_call(
        _merge_kernel,
        grid=(t // bt,),
        in_specs=[spec] * 6,
        out_specs=pl.BlockSpec((bt, hpg * LANE), lambda i: (i, 0)),
        out_shape=jax.ShapeDtypeStruct((t, hpg * LANE), BF16),
        compiler_params=_params(("arbitrary",)),
        name="merge_groups",
    )(*outs, *lses)


def _gated_proj_kernel(a_ref, b_ref, wa_ref, wb_ref, ga_ref, gb_ref, o_ref):
    pa = jnp.dot(a_ref[...], wa_ref[...], preferred_element_type=F32)
    pb = jnp.dot(b_ref[...], wb_ref[...], preferred_element_type=F32)
    for c in range(ga_ref.shape[0]):
        o_ref[:, _lanes(c)] = (ga_ref[c] * pa[:, _lanes(c)] + gb_ref[c] * pb[:, _lanes(c)]).astype(o_ref.dtype)


def _gated_proj(o_a, o_b, w_a, w_b, z_sl, bn, gate_a_blk, gate_b_blk):
    t, d_a = o_a.shape
    d_b = o_b.shape[1]
    d = w_a.shape[1]
    bm = _tile(t, MATMUL_TILE // 2, SUBLANE)
    return pl.pallas_call(
        _gated_proj_kernel,
        grid=(t // bm, d // bn),
        in_specs=[pl.BlockSpec((bm, d_a), lambda i, j: (i, 0)),
                  pl.BlockSpec((bm, d_b), lambda i, j: (i, 0)),
                  pl.BlockSpec((d_a, bn), lambda i, j: (0, j)),
                  pl.BlockSpec((d_b, bn), lambda i, j: (0, j)),
                  pl.BlockSpec((bn // LANE, bm, LANE), lambda i, j: (gate_a_blk + j, i, 0)),
                  pl.BlockSpec((bn // LANE, bm, LANE), lambda i, j: (gate_b_blk + j, i, 0))],
        out_specs=pl.BlockSpec((bm, bn), lambda i, j: (i, j)),
        out_shape=jax.ShapeDtypeStruct((t, d), BF16),
        compiler_params=_params(("arbitrary", "arbitrary")),
        name="gated_proj",
    )(o_a, o_b, w_a, w_b, z_sl, z_sl)


def _matmul_kernel(x_ref, w_ref, o_ref, *, nk):
    if nk == 1:
        o_ref[...] = jnp.dot(x_ref[...], w_ref[...], preferred_element_type=F32)
    else:
        @pl.when(pl.program_id(2) == 0)
        def _():
            o_ref[...] = jnp.zeros(o_ref.shape, o_ref.dtype)

        o_ref[...] += jnp.dot(x_ref[...], w_ref[...], preferred_element_type=F32)


def _matmul(x, w, name, k_tile=None):
    t, kd = x.shape
    n = w.shape[1]
    bm = _tile(t, MATMUL_TILE, SUBLANE)
    bn = _tile(n, MATMUL_TILE, LANE)
    bk = kd if k_tile is None else k_tile
    nk = kd // bk
    return pl.pallas_call(
        functools.partial(_matmul_kernel, nk=nk),
        grid=(t // bm, n // bn, nk),
        in_specs=[pl.BlockSpec((bm, bk), lambda i, j, k: (i, k)),
                  pl.BlockSpec((bk, bn), lambda i, j, k: (k, j))],
        out_specs=pl.BlockSpec((bm, bn), lambda i, j, k: (i, j)),
        out_shape=jax.ShapeDtypeStruct((t, n), F32),
        compiler_params=_params(("arbitrary", "arbitrary", "arbitrary")),
        name=name,
    )(x, w)


def _post_mix_kernel(x_ref, y_ref, g1_ref, g2_ref, x1_ref, h2_ref):
    y = y_ref[...]
    x1 = x_ref[...] + (y * _rms_scale(y)) * g1_ref[...]
    x1_ref[...] = x1
    h2_ref[...] = ((x1 * _rms_scale(x1)) * g2_ref[...]).astype(h2_ref.dtype)


def _post_mix(x, y, g_post, g_pre):
    t, d = x.shape
    bt = _tile(t, 256, SUBLANE)
    row = pl.BlockSpec((bt, d), lambda i: (i, 0))
    vec = pl.BlockSpec((1, d), lambda i: (0, 0))
    return pl.pallas_call(
        _post_mix_kernel,
        grid=(t // bt,),
        in_specs=[row, row, vec, vec],
        out_specs=[row, row],
        out_shape=[jax.ShapeDtypeStruct((t, d), F32), jax.ShapeDtypeStruct((t, d), BF16)],
        compiler_params=_params(("arbitrary",)),
        name="post_mix",
    )(x, y, g_post.reshape(1, d), g_pre.reshape(1, d))


def _residual_norm_kernel(x_ref, y_ref, g_ref, o_ref):
    y = y_ref[...]
    o_ref[...] = x_ref[...] + (y * _rms_scale(y)) * g_ref[...]


def _residual_norm(x, y, g):
    t, d = x.shape
    bt = _tile(t, 256, SUBLANE)
    row = pl.BlockSpec((bt, d), lambda i: (i, 0))
    return pl.pallas_call(
        _residual_norm_kernel,
        grid=(t // bt,),
        in_specs=[row, row, pl.BlockSpec((1, d), lambda i: (0, 0))],
        out_specs=row,
        out_shape=jax.ShapeDtypeStruct((t, d), F32),
        compiler_params=_params(("arbitrary",)),
        name="residual_norm",
    )(x, y, g.reshape(1, d))


def _ffn_up_kernel(*refs, conv_w, s_len, d_ff):
    if s_len is None:
        x_ref, wg_ref, wu_ref, cw_ref, cb_ref, out_ref, tail_ref, a_buf, u_buf, o_buf = refs
        hist_refs = None
    else:
        x_ref, wg_ref, wu_ref, cw_ref, cb_ref, h1_ref, h2_ref, out_ref, tail_ref, a_buf, u_buf, o_buf = refs
        hist_refs = {1: h1_ref, 2: h2_ref}
    bm, bn = out_ref.shape
    half = bm // 2
    top = SUBLANE
    x = x_ref[...]
    a = jnp.dot(x, wg_ref[...], preferred_element_type=F32)
    u = jnp.dot(x, wu_ref[...], preferred_element_type=F32)

    @pl.when(pl.program_id(1) == 0)
    def _():
        a_buf[:, 0:top, :] = jnp.zeros((a_buf.shape[0], top, LANE), F32)

    for c in range(bn // LANE):
        a_buf[c, top:top + bm, :] = a[:, _lanes(c)]
        u_buf[c] = u[:, _lanes(c)]
    tail_ref[...] = a if s_len is not None else a[bm - SUBLANE:, :]

    lane = lax.broadcasted_iota(jnp.int32, (1, LANE), 1)
    for c in range(bn // LANE):
        in_range = (pl.program_id(0) * bn + c * LANE + lane) < d_ff
        for e in range(2):
            taps = [a_buf[c, pl.ds(top + e - lag, half, stride=2), :] for lag in range(conv_w)]
            if hist_refs is not None:
                s = 2 * (lax.broadcasted_iota(jnp.int32, (half, LANE), 0) % (s_len // 2)) + e
                for lag in range(1, conv_w):
                    taps[lag] = jnp.where(s < lag, hist_refs[lag][c, pl.ds(e, half, stride=2), :], taps[lag])
            acc = cw_ref[0:1, _lanes(c)] * taps[conv_w - 1]
            for i in range(1, conv_w):
                acc = acc + cw_ref[i:i + 1, _lanes(c)] * taps[conv_w - 1 - i]
            acc = cb_ref[:, _lanes(c)] + acc
            val = _gelu(acc) * u_buf[c, pl.ds(e, half, stride=2), :]
            o_buf[c, pl.ds(e, half, stride=2), :] = jnp.where(in_range, val, 0.0)
        out_ref[:, _lanes(c)] = o_buf[c].astype(out_ref.dtype)
    if s_len is None:
        for c in range(bn // LANE):
            a_buf[c, 0:top, :] = a[bm - top:, _lanes(c)]


def _ffn_up(h, w_gate, w_up, conv_w, conv_b, ffp, hist=None, s_len=None):
    t, d = h.shape
    d_ff = w_gate.shape[1]
    cw = conv_w.shape[0]
    assert cw == 3 and d_ff % LANE == 0
    bn = _tile(ffp, MATMUL_TILE, LANE)
    assert ffp - d_ff < bn
    bm = _tile(t, 512, 2 * SUBLANE)
    n_slabs = bn // LANE
    x_spec = pl.BlockSpec((bm, d), lambda j, i: (i, 0))
    w_spec = pl.BlockSpec((d, bn), lambda j, i: (0, j))
    cw_spec = pl.BlockSpec((cw, bn), lambda j, i: (0, j))
    cb_spec = pl.BlockSpec((1, bn), lambda j, i: (0, j))
    out_spec = pl.BlockSpec((bm, bn), lambda j, i: (i, j))
    in_specs = [x_spec, w_spec, w_spec, cw_spec, cb_spec]
    args = [h, w_gate, w_up, conv_w, conv_b.reshape(1, d_ff)]
    if s_len is None:
        tail_rows = SUBLANE
        tail_spec = pl.BlockSpec((SUBLANE, bn), lambda j, i: (0, j))
    else:
        assert bm == t and s_len % 2 == 0
        tail_rows = t
        tail_spec = out_spec
        def slabs(rows):
            return jnp.transpose(rows.reshape(t, d_ff // LANE, LANE), (1, 0, 2))
        h1 = slabs(jnp.pad(hist[:, 1:2], ((0, 0), (0, s_len - 1), (0, 0))))
        h2 = slabs(jnp.pad(hist, ((0, 0), (0, s_len - hist.shape[1]), (0, 0))))
        hist_spec = pl.BlockSpec((n_slabs, bm, LANE), lambda j, i: (j, i, 0))
        in_specs += [hist_spec, hist_spec]
        args += [h1, h2]
    return pl.pallas_call(
        functools.partial(_ffn_up_kernel, conv_w=cw, s_len=s_len, d_ff=d_ff),
        grid=(ffp // bn, t // bm),
        in_specs=in_specs,
        out_specs=[out_spec, tail_spec],
        out_shape=[jax.ShapeDtypeStruct((t, ffp), BF16), jax.ShapeDtypeStruct((tail_rows, d_ff), F32)],
        scratch_shapes=[pltpu.VMEM((n_slabs, bm + SUBLANE, LANE), F32), pltpu.VMEM((n_slabs, bm, LANE), F32),
                        pltpu.VMEM((n_slabs, bm, LANE), F32)],
        compiler_params=_params(("arbitrary", "arbitrary")),
        name="ffn_up",
    )(*args)


def _layer(x, p, s_len=None, caches=None, conv_hist=None):
    t, d = x.shape
    d_a = p["ln_g"].shape[0]
    hpg = p["hpg"]
    gw = hpg * HEAD_DIM
    bn = p["bn"]
    o1 = 2 * d_a
    d_qkv = 3 * gw

    h = _rmsnorm_cast(x, p["g_pre_mix"])
    za_sl = _in_proj(h, p["w_in"], bn, 0, o1, _gelu, "in_proj_sgu", lagged=False)
    qkv_sl = _in_proj(h, p["w_in"], bn, o1, 3 * d_qkv, lambda v: v, "in_proj_qkv", lagged=True)
    gate_sl = _in_proj(h, p["w_in"], bn, o1 + 3 * d_qkv, 2 * d, _sigmoid, "in_proj_gates", lagged=True)

    if s_len is None:
        w_mix, b_mix = p["w_mix_prompt"], p["b_mix_prompt"]
    else:
        w_mix, b_mix = p["w_mix_sample"], p["b_mix_sample"]
    o_a, v_state = _sgu(za_sl, d_a, p["ln_g"], p["ln_b"], w_mix, b_mix)

    outs, lses = [], []
    for gi, (_, dil) in enumerate(DILATION_GROUPS):
        q_slab = gi * hpg
        k_slab = d_qkv // LANE + gi * hpg
        v_slab = 2 * d_qkv // LANE + gi * hpg
        bias_tab = p["rel_bias"][:, gi * hpg:(gi + 1) * hpg]
        if s_len is None:
            o, lse = _attn_prompt(qkv_sl, hpg, q_slab, k_slab, v_slab, bias_tab, dil)
        else:
            o, lse = _attn_sample(qkv_sl, s_len, hpg, q_slab, k_slab, v_slab, caches[2 * gi], caches[2 * gi + 1],
                                  bias_tab, dil)
        outs.append(o)
        lses.append(lse)
    o_b = _merge(outs, lses)

    merged = _gated_proj(o_a, o_b, p["w_proj_a"], p["w_proj_b"], gate_sl, bn, 0, d // bn)
    y = _matmul(merged, p["w_out"], "out_proj")
    x1, h2 = _post_mix(x, y, p["g_post_mix"], p["g_pre_ffn"])

    act, a_tail = _ffn_up(h2, p["w_gate"], p["w_up"], p["conv_w"], p["conv_b"], p["ffp"], conv_hist, s_len)
    f = _matmul(act, p["w_down"], "ffn_down", k_tile=p["ffn_k_tile"])
    y_out = _residual_norm(x1, f, p["g_post_ffn"])
    return y_out, qkv_sl, v_state, a_tail


def kernel(x_prompt, x_sample, cache_k_g1, cache_v_g1, cache_k_g2, cache_v_g2, cache_k_g3, cache_v_g3, state_conv, g_pre_mix, w_in, sgu_ln_g, sgu_ln_b, w_spatial, b_spatial, rel_bias, w_proj_a, w_proj_b, w_out, g_post_mix, g_pre_ffn, w_gate, w_up, conv_w, conv_b, w_down, g_post_ffn):
    depth = w_in.shape[0]
    assert depth == 1
    n_prompt, seq, d = x_prompt.shape
    assert n_prompt == 1 and seq % CHUNK == 0
    n_seq, s_len, _ = x_sample.shape
    assert s_len == SUBLANE
    d_a = sgu_ln_g.shape[1]
    n_groups = w_spatial.shape[1]
    n_heads = rel_bias.shape[1]
    hpg = n_heads // len(DILATION_GROUPS)
    gw = hpg * HEAD_DIM
    d_ff = w_gate.shape[2]
    cw = conv_w.shape[1]
    bn = _tile(math.gcd(2 * d_a, gw), MATMUL_TILE, LANE)
    ffp = -(-d_ff // FFN_TILE) * FFN_TILE
    t_s = n_seq * s_len

    tri = np.tril(np.ones((CHUNK, CHUNK), np.float32))
    w_mix_prompt = (w_spatial[0] * tri).astype(BF16)
    b_mix_prompt = jnp.broadcast_to(b_spatial[0][:, :, None], (n_groups, CHUNK, LANE))
    w_small = w_spatial[0][:, :s_len, :s_len] * tri[:s_len, :s_len]
    eye = np.eye(n_seq, dtype=np.float32)
    w_mix_sample = jnp.einsum("ab,gpq->gapbq", eye, w_small).reshape(n_groups, t_s, t_s).astype(BF16)
    b_mix_sample = jnp.broadcast_to(jnp.tile(b_spatial[0][:, :s_len], (1, n_seq))[:, :, None], (n_groups, t_s, LANE))

    p = dict(
        hpg=hpg, bn=bn, ffp=ffp,
        g_pre_mix=g_pre_mix[0], w_in=w_in[0].astype(BF16), ln_g=sgu_ln_g[0], ln_b=sgu_ln_b[0],
        w_mix_prompt=w_mix_prompt, b_mix_prompt=b_mix_prompt,
        w_mix_sample=w_mix_sample, b_mix_sample=b_mix_sample,
        rel_bias=rel_bias,
        w_proj_a=w_proj_a[0].astype(BF16), w_proj_b=w_proj_b[0].astype(BF16), w_out=w_out[0].astype(BF16),
        g_post_mix=g_post_mix[0], g_pre_ffn=g_pre_ffn[0],
        w_gate=w_gate[0].astype(BF16), w_up=w_up[0].astype(BF16),
        conv_w=conv_w[0], conv_b=conv_b[0],
        w_down=jnp.pad(w_down[0].astype(BF16), ((0, ffp - d_ff), (0, 0))),
        ffn_k_tile=_tile(ffp, 3072, LANE),
        g_post_ffn=g_post_ffn[0],
    )

    caches = tuple(c[0] for c in (cache_k_g1, cache_v_g1, cache_k_g2, cache_v_g2, cache_k_g3, cache_v_g3))

    yp, zp, vp_state, ap_tail = _layer(x_prompt[0], p)
    ys, zs, vs_state, as_all = _layer(x_sample.reshape(t_s, d), p, s_len=s_len, caches=caches,
                                      conv_hist=state_conv[0])

    d_qkv = 3 * gw
    prompt_kv, sample_kv = [], []
    for gi, (win, _) in enumerate(DILATION_GROUPS):
        keep = min(win, seq)
        for base in (d_qkv, 2 * d_qkv):
            s0 = (base + gi * gw) // LANE
            pk = jnp.transpose(zp[s0:s0 + hpg, seq - keep:, :], (1, 0, 2))
            prompt_kv.append(pk.reshape(1, 1, keep, hpg, HEAD_DIM))
            sk = jnp.transpose(zs[s0:s0 + hpg], (1, 0, 2))
            sample_kv.append(sk.reshape(1, n_seq, s_len, hpg, HEAD_DIM))
    p_conv = ap_tail[SUBLANE - (cw - 1):].reshape(1, 1, cw - 1, d_ff)
    s_conv = as_all.reshape(n_seq, s_len, d_ff)[:, s_len - (cw - 1):].reshape(1, n_seq, cw - 1, d_ff)
    return (yp.reshape(1, seq, d), ys.reshape(n_seq, s_len, d),
            *prompt_kv, vp_state.reshape(1, 1, CHUNK, d_a), p_conv,
            *sample_kv, vs_state.reshape(1, n_seq, s_len, d_a), s_conv)
```

```python
import functools
import math

import numpy as np
import jax
import jax.numpy as jnp
from jax import lax
from jax.experimental import pallas as pl
from jax.experimental.pallas import tpu as pltpu

F32 = jnp.float32
BF16 = jnp.bfloat16

HEAD_DIM = 128
STEPS = 128
CHUNK = 128
GROUP_DIM_A = 128
DILATION_GROUPS = ((128, 1), (512, 4), (2048, 16))
MAX_EXACT = 16
MAX_DISTANCE = 2048
EPS = 1e-6
NEG = -1e30
ATTN_SCALE = HEAD_DIM ** -0.5

LANE = 128
SUBLANE = 8
VMEM_LIMIT_BYTES = 56 * 1024 * 1024
MATMUL_TILE = 1024
MXU_WIDTH = 256
FFN_TILE = 512
ATTN_UNITS_PER_STEP = 16


def _tile(dim, target, align):
    best = None
    t = align
    while t <= min(dim, target):
        if dim % t == 0:
            best = t
        t += align
    return best if best is not None else dim


def _params(semantics):
    return pltpu.CompilerParams(dimension_semantics=semantics, vmem_limit_bytes=VMEM_LIMIT_BYTES)


def _gelu(x):
    return 0.5 * x * (1.0 + jnp.tanh(math.sqrt(2.0 / math.pi) * (x + 0.044715 * (x * x * x))))


def _sigmoid(x):
    return 1.0 / (1.0 + jnp.exp(-x))


def _rms_scale(x):
    return lax.rsqrt(jnp.mean(x * x, axis=-1, keepdims=True) + EPS)


def _lanes(c):
    return slice(c * LANE, (c + 1) * LANE)


def _rmsnorm_cast_kernel(x_ref, g_ref, o_ref):
    x = x_ref[...]
    o_ref[...] = ((x * _rms_scale(x)) * g_ref[...]).astype(o_ref.dtype)


def _rmsnorm_cast(x, g):
    t, d = x.shape
    bt = _tile(t, 256, SUBLANE)
    return pl.pallas_call(
        _rmsnorm_cast_kernel,
        grid=(t // bt,),
        in_specs=[pl.BlockSpec((bt, d), lambda i: (i, 0)), pl.BlockSpec((1, d), lambda i: (0, 0))],
        out_specs=pl.BlockSpec((bt, d), lambda i: (i, 0)),
        out_shape=jax.ShapeDtypeStruct((t, d), BF16),
        compiler_params=_params(("arbitrary",)),
        name="rmsnorm_cast",
    )(x, g.reshape(1, d))


def _in_proj_kernel(x_ref, w_ref, o_ref, *, act):
    acc = jnp.dot(x_ref[...], w_ref[...], preferred_element_type=F32)
    for c in range(o_ref.shape[0]):
        o_ref[c] = act(acc[:, _lanes(c)])


def _in_proj(h, w, bn, col0, n_cols, act, name):
    t, d = h.shape
    bm = _tile(t, MATMUL_TILE, SUBLANE)
    return pl.pallas_call(
        functools.partial(_in_proj_kernel, act=act),
        grid=(t // bm, n_cols // bn),
        in_specs=[pl.BlockSpec((bm, d), lambda i, j: (i, 0)),
                  pl.BlockSpec((d, bn), lambda i, j: (0, col0 // bn + j))],
        out_specs=pl.BlockSpec((bn // LANE, bm, LANE), lambda i, j: (j, i, 0)),
        out_shape=jax.ShapeDtypeStruct((n_cols // LANE, t, LANE), F32),
        compiler_params=_params(("arbitrary", "arbitrary")),
        name=name,
    )(h, w)


def _sgu_kernel(u_ref, v_ref, lg_ref, lb_ref, w_ref, b_ref, o_ref, vs_ref, *, n_groups):
    vp = v_ref[...]
    n_feat = n_groups * GROUP_DIM_A
    mu = jnp.sum(jnp.sum(vp, axis=0), axis=-1, keepdims=True) / n_feat
    vc = vp - mu
    var = jnp.sum(jnp.sum(vc * vc, axis=0), axis=-1, keepdims=True) / n_feat
    v = vc * lax.rsqrt(var + EPS) * lg_ref[...] + lb_ref[...]
    for g in range(n_groups):
        vs_ref[:, _lanes(g)] = v[g]
        mixed = jnp.dot(w_ref[g], v[g].astype(BF16), preferred_element_type=F32) + b_ref[g]
        o_ref[:, _lanes(g)] = (u_ref[g] * mixed).astype(o_ref.dtype)


def _sgu(z_sl, d_a, ln_g, ln_b, w_mix, b_mix):
    t = z_sl.shape[1]
    n_groups, c, _ = w_mix.shape
    slab = pl.BlockSpec((n_groups, 1, LANE), lambda i: (0, 0, 0))
    return pl.pallas_call(
        functools.partial(_sgu_kernel, n_groups=n_groups),
        grid=(t // c,),
        in_specs=[
            pl.BlockSpec((n_groups, c, LANE), lambda i: (0, i, 0)),
            pl.BlockSpec((n_groups, c, LANE), lambda i: (1, i, 0)),
            slab, slab,
            pl.BlockSpec((n_groups, c, c), lambda i: (0, 0, 0)),
            pl.BlockSpec((n_groups, c, LANE), lambda i: (0, 0, 0)),
        ],
        out_specs=[pl.BlockSpec((c, d_a), lambda i: (i, 0)), pl.BlockSpec((c, d_a), lambda i: (0, 0))],
        out_shape=[jax.ShapeDtypeStruct((t, d_a), BF16), jax.ShapeDtypeStruct((c, d_a), F32)],
        compiler_params=_params(("arbitrary",)),
        name="sgu",
    )(z_sl, z_sl, ln_g.reshape(n_groups, 1, LANE), ln_b.reshape(n_groups, 1, LANE), w_mix, b_mix)


def _t5_bucket(dist, n_buckets):
    n = np.asarray(dist, np.int32)
    safe = np.maximum(n, 1).astype(np.float32)
    large = MAX_EXACT + (np.log(safe / MAX_EXACT) / np.log(np.float32(MAX_DISTANCE / MAX_EXACT))
                         * (n_buckets - MAX_EXACT)).astype(np.int32)
    large = np.minimum(large, n_buckets - 1)
    return np.where(n < MAX_EXACT, n, large).astype(np.int32)


def _bias_lookup(bias_tab, bucket):
    n_buckets = bias_tab.shape[0]
    flat = np.asarray(bucket).reshape(-1)
    onehot = (jnp.asarray(flat)[None, :] == jnp.arange(n_buckets)[:, None]).astype(F32)
    out = jnp.dot(bias_tab.T, onehot, precision=lax.Precision.HIGHEST)
    return out.reshape((bias_tab.shape[1],) + tuple(np.asarray(bucket).shape))


def _merge_by_lse(outs, lses):
    m = functools.reduce(jnp.maximum, lses)
    w = [jnp.exp(l - m) for l in lses]
    num = functools.reduce(lambda x, y: x + y, [wi * oi for wi, oi in zip(w, outs)])
    return num / functools.reduce(lambda x, y: x + y, w)


def _attn_prompt_kernel(*refs, hb, dil, n_other):
    q_ref, kc_ref, vc_ref, bias_ref, band_ref = refs[:5]
    others = refs[5:5 + 2 * n_other]
    n_out = 1 if n_other else 2
    outs = refs[5 + 2 * n_other:5 + 2 * n_other + n_out]
    kp_ref, vp_ref = refs[5 + 2 * n_other + n_out:][:2]
    o_acc, lse_acc = refs[5 + 2 * n_other + n_out + 2:] if n_other else outs
    b = pl.program_id(1)

    @pl.when(b == 0)
    def _():
        kp_ref[...] = jnp.zeros(kp_ref.shape, F32)
        vp_ref[...] = jnp.zeros(vp_ref.shape, F32)

    col = lax.broadcasted_iota(jnp.int32, (STEPS, 2 * STEPS), 1)
    valid = (band_ref[...] > 0.5) & ((b > 0) | (col >= STEPS))
    for hh in range(hb):
        for r in range(dil):
            rows = pl.ds(r, STEPS, stride=dil) if dil > 1 else pl.ds(0, STEPS)
            q = q_ref[hh, rows, :].astype(BF16)
            kk = jnp.concatenate([kp_ref[hh, rows, :], kc_ref[hh, rows, :]], axis=0).astype(BF16)
            vv = jnp.concatenate([vp_ref[hh, rows, :], vc_ref[hh, rows, :]], axis=0).astype(BF16)
            s = lax.dot_general(q, kk, (((1,), (1,)), ((), ())), preferred_element_type=F32) * ATTN_SCALE
            s = jnp.where(valid, s + bias_ref[hh], NEG)
            m = jnp.max(s, axis=-1, keepdims=True)
            e = jnp.exp(s - m)
            den = jnp.sum(e, axis=-1, keepdims=True)
            o_acc[hh, rows, :] = jnp.dot(e.astype(BF16), vv, preferred_element_type=F32) / den
            lse_acc[hh, rows, :] = jnp.broadcast_to(m + jnp.log(den), (STEPS, HEAD_DIM))
    kp_ref[...] = kc_ref[...]
    vp_ref[...] = vc_ref[...]
    if n_other:
        for hh in range(hb):
            merged = _merge_by_lse([r[hh] for r in others[:n_other]] + [o_acc[hh]],
                                   [r[hh] for r in others[n_other:]] + [lse_acc[hh]])
            outs[0][:, _lanes(hh)] = merged.astype(outs[0].dtype)


def _attn_prompt(z_sl, hpg, q_slab, k_slab, v_slab, bias_tab, dil, merge_with=None):
    t = z_sl.shape[1]
    rows = STEPS * dil
    assert t % rows == 0
    nb = t // rows
    hb = min(hpg, max(1, ATTN_UNITS_PER_STEP // dil))
    assert hpg % hb == 0 and q_slab % hb == 0 and k_slab % hb == 0 and v_slab % hb == 0

    p_idx = np.arange(STEPS)[:, None]
    c_idx = np.arange(2 * STEPS)[None, :]
    steps = p_idx + STEPS - c_idx
    band = ((steps >= 0) & (steps <= STEPS)).astype(np.float32)
    bias = _bias_lookup(bias_tab, _t5_bucket(np.clip(steps, 0, STEPS) * dil, bias_tab.shape[0]))

    def cur(slab):
        return pl.BlockSpec((hb, rows, LANE), lambda hi, b: (slab // hb + hi, b, 0))

    head_major = pl.BlockSpec((hb, rows, LANE), lambda hi, b: (hi, b, 0))
    block = pltpu.VMEM((hb, rows, LANE), F32)
    others = [] if merge_with is None else list(merge_with[0]) + list(merge_with[1])
    if merge_with is None:
        out_specs = [head_major, head_major]
        out_shape = [jax.ShapeDtypeStruct((hpg, t, LANE), F32)] * 2
        scratch = [block, block]
    else:
        out_specs = [pl.BlockSpec((rows, hb * LANE), lambda hi, b: (b, hi))]
        out_shape = [jax.ShapeDtypeStruct((t, hpg * LANE), BF16)]
        scratch = [block, block, block, block]
    res = pl.pallas_call(
        functools.partial(_attn_prompt_kernel, hb=hb, dil=dil, n_other=len(others) // 2),
        grid=(hpg // hb, nb),
        in_specs=[cur(q_slab), cur(k_slab), cur(v_slab),
                  pl.BlockSpec((hb, STEPS, 2 * STEPS), lambda hi, b: (hi, 0, 0)),
                  pl.BlockSpec((STEPS, 2 * STEPS), lambda hi, b: (0, 0))] + [head_major] * len(others),
        out_specs=out_specs,
        out_shape=out_shape,
        scratch_shapes=scratch,
        compiler_params=_params(("arbitrary", "arbitrary")),
        name=f"attn_prompt_d{dil}",
    )(z_sl, z_sl, z_sl, bias, jnp.asarray(band), *others)
    return res[0] if merge_with is not None else res


def _attn_sample_kernel(q_ref, kn_ref, vn_ref, ck_ref, cv_ref, bc_ref, mc_ref, bnew_ref, mnew_ref,
                        o_ref, lse_ref, *, hpg, s_len, grouped):
    b = pl.program_id(0)
    n_new = kn_ref.shape[1]
    n_keys = mc_ref.shape[1]
    col = lax.broadcasted_iota(jnp.int32, (s_len, n_new), 1)
    own = (col >= b * s_len) & (col < (b + 1) * s_len)
    valid_new = (mnew_ref[...] > 0.5) & own
    valid_c = mc_ref[...] > 0.5

    def cache_head(ref, h):
        if grouped:
            return ref[0, :, pl.ds(h, s_len, stride=hpg), :].reshape(n_keys, HEAD_DIM).astype(BF16)
        return ref[0, pl.ds(h, n_keys, stride=hpg) if hpg > 1 else pl.ds(0, n_keys), :].astype(BF16)

    for h in range(hpg):
        q = q_ref[h].astype(BF16)
        sc = lax.dot_general(q, cache_head(ck_ref, h), (((1,), (1,)), ((), ())),
                             preferred_element_type=F32) * ATTN_SCALE
        sc = jnp.where(valid_c, sc + bc_ref[h], NEG)
        sn = lax.dot_general(q, kn_ref[h].astype(BF16), (((1,), (1,)), ((), ())),
                             preferred_element_type=F32) * ATTN_SCALE
        sn = jnp.where(valid_new, sn + bnew_ref[h], NEG)
        m = jnp.maximum(jnp.max(sc, axis=-1, keepdims=True), jnp.max(sn, axis=-1, keepdims=True))
        ec = jnp.exp(sc - m)
        en = jnp.exp(sn - m)
        den = jnp.sum(ec, axis=-1, keepdims=True) + jnp.sum(en, axis=-1, keepdims=True)
        acc = jnp.dot(ec.astype(BF16), cache_head(cv_ref, h), preferred_element_type=F32)
        acc = acc + jnp.dot(en.astype(BF16), vn_ref[h].astype(BF16), preferred_element_type=F32)
        o_ref[h] = acc / den
        lse_ref[h] = jnp.broadcast_to(m + jnp.log(den), (s_len, HEAD_DIM))


def _attn_sample(z_sl, s_len, hpg, q_slab, k_slab, v_slab, cache_k, cache_v, bias_tab, dil):
    t = z_sl.shape[1]
    n_seq = t // s_len
    lc = cache_k.shape[1]
    assert q_slab % hpg == 0 and k_slab % hpg == 0 and v_slab % hpg == 0

    j = np.arange(STEPS + 1)
    idx = lc + np.arange(s_len)[:, None] - j[None, :] * dil
    assert idx.min() >= 0
    bucket = _t5_bucket(j * dil, bias_tab.shape[0])
    mask = np.zeros((s_len, lc + s_len), np.float32)
    bsel = np.zeros((s_len, lc + s_len), np.int32)
    for s in range(s_len):
        mask[s, idx[s]] = 1.0
        bsel[s, idx[s]] = bucket

    grouped = dil > s_len and lc % dil == 0 and s_len == SUBLANE
    if grouped:
        pos = (np.arange(lc // dil)[:, None] * dil + np.arange(s_len)[None, :]).reshape(-1)
        assert mask[:, :lc].sum() == mask[:, pos].sum()
        ck = cache_k.reshape(n_seq, lc // dil, dil * hpg, HEAD_DIM)
        cv = cache_v.reshape(n_seq, lc // dil, dil * hpg, HEAD_DIM)
        cache_spec = pl.BlockSpec((1, lc // dil, s_len * hpg, LANE), lambda b: (b, 0, 0, 0))
    else:
        pos = np.arange(lc)
        ck = cache_k.reshape(n_seq, lc * hpg, HEAD_DIM)
        cv = cache_v.reshape(n_seq, lc * hpg, HEAD_DIM)
        cache_spec = pl.BlockSpec((1, lc * hpg, LANE), lambda b: (b, 0, 0))
    n_keys = len(pos)
    bias_c = _bias_lookup(bias_tab, bsel[:, pos])
    bias_new = jnp.tile(_bias_lookup(bias_tab, bsel[:, lc:]), (1, 1, n_seq))
    mask_c = jnp.asarray(mask[:, pos])
    mask_new = jnp.asarray(np.tile(mask[:, lc:], (1, n_seq)))

    out_spec = pl.BlockSpec((hpg, s_len, LANE), lambda b: (0, b, 0))
    return pl.pallas_call(
        functools.partial(_attn_sample_kernel, hpg=hpg, s_len=s_len, grouped=grouped),
        grid=(n_seq,),
        in_specs=[pl.BlockSpec((hpg, s_len, LANE), lambda b: (q_slab // hpg, b, 0)),
                  pl.BlockSpec((hpg, t, LANE), lambda b: (k_slab // hpg, 0, 0)),
                  pl.BlockSpec((hpg, t, LANE), lambda b: (v_slab // hpg, 0, 0)),
                  cache_spec, cache_spec,
                  pl.BlockSpec((hpg, s_len, n_keys), lambda b: (0, 0, 0)),
                  pl.BlockSpec((s_len, n_keys), lambda b: (0, 0)),
                  pl.BlockSpec((hpg, s_len, t), lambda b: (0, 0, 0)),
                  pl.BlockSpec((s_len, t), lambda b: (0, 0))],
        out_specs=[out_spec, out_spec],
        out_shape=[jax.ShapeDtypeStruct((hpg, t, LANE), F32)] * 2,
        compiler_params=_params(("arbitrary",)),
        name=f"attn_sample_d{dil}",
    )(z_sl, z_sl, z_sl, ck, cv, bias_c, mask_c, bias_new, mask_new)


def _merge_kernel(o1, o2, o3, l1, l2, l3, out_ref):
    merged = _merge_by_lse([o1[...], o2[...], o3[...]], [l1[...], l2[...], l3[...]])
    for h in range(merged.shape[0]):
        out_ref[:, _lanes(h)] = merged[h].astype(out_ref.dtype)


def _merge(outs, lses):
    hpg, t, _ = outs[0].shape
    bt = _tile(t, 256, SUBLANE)
    spec = pl.BlockSpec((hpg, bt, LANE), lambda i: (0, i, 0))
    return pl.pallas_call(
        _merge_kernel,
        grid=(t // bt,),
        in_specs=[spec] * 6,
        out_specs=pl.BlockSpec((bt, hpg * LANE), lambda i: (i, 0)),
        out_shape=jax.ShapeDtypeStruct((t, hpg * LANE), BF16),
        compiler_params=_params(("arbitrary",)),
        name="merge_groups",
    )(*outs, *lses)


def _gated_proj_kernel(a_ref, b_ref, wa_ref, wb_ref, ga_ref, gb_ref, o_ref):
    n_slabs = ga_ref.shape[0]
    per_strip = min(n_slabs, MXU_WIDTH // LANE)
    for c0 in range(0, n_slabs, per_strip):
        c1 = min(c0 + per_strip, n_slabs)
        cols = slice(c0 * LANE, c1 * LANE)
        pa = jnp.dot(a_ref[...], wa_ref[:, cols], preferred_element_type=F32)
        pb = jnp.dot(b_ref[...], wb_ref[:, cols], preferred_element_type=F32)
        for c in range(c0, c1):
            gated = ga_ref[c] * pa[:, _lanes(c - c0)] + gb_ref[c] * pb[:, _lanes(c - c0)]
            o_ref[:, _lanes(c)] = gated.astype(o_ref.dtype)


def _gated_proj(o_a, o_b, w_a, w_b, z_sl, bn, gate_a_blk, gate_b_blk):
    t, d_a = o_a.shape
    d_b = o_b.shape[1]
    d = w_a.shape[1]
    bm = _tile(t, MATMUL_TILE // 2, SUBLANE)
    return pl.pallas_call(
        _gated_proj_kernel,
        grid=(t // bm, d // bn),
        in_specs=[pl.BlockSpec((bm, d_a), lambda i, j: (i, 0)),
                  pl.BlockSpec((bm, d_b), lambda i, j: (i, 0)),
                  pl.BlockSpec((d_a, bn), lambda i, j: (0, j)),
                  pl.BlockSpec((d_b, bn), lambda i, j: (0, j)),
                  pl.BlockSpec((bn // LANE, bm, LANE), lambda i, j: (gate_a_blk + j, i, 0)),
                  pl.BlockSpec((bn // LANE, bm, LANE), lambda i, j: (gate_b_blk + j, i, 0))],
        out_specs=pl.BlockSpec((bm, bn), lambda i, j: (i, j)),
        out_shape=jax.ShapeDtypeStruct((t, d), BF16),
        compiler_params=_params(("arbitrary", "arbitrary")),
        name="gated_proj",
    )(o_a, o_b, w_a, w_b, z_sl, z_sl)


def _matmul_kernel(x_ref, w_ref, o_ref, *, nk):
    if nk == 1:
        o_ref[...] = jnp.dot(x_ref[...], w_ref[...], preferred_element_type=F32)
    else:
        @pl.when(pl.program_id(2) == 0)
        def _():
            o_ref[...] = jnp.zeros(o_ref.shape, o_ref.dtype)

        o_ref[...] += jnp.dot(x_ref[...], w_ref[...], preferred_element_type=F32)


def _matmul(x, w, name, k_tile=None):
    t, kd = x.shape
    n = w.shape[1]
    bm = _tile(t, MATMUL_TILE, SUBLANE)
    bn = _tile(n, MATMUL_TILE, LANE)
    bk = kd if k_tile is None else k_tile
    nk = kd // bk
    return pl.pallas_call(
        functools.partial(_matmul_kernel, nk=nk),
        grid=(t // bm, n // bn, nk),
        in_specs=[pl.BlockSpec((bm, bk), lambda i, j, k: (i, k)),
                  pl.BlockSpec((bk, bn), lambda i, j, k: (k, j))],
        out_specs=pl.BlockSpec((bm, bn), lambda i, j, k: (i, j)),
        out_shape=jax.ShapeDtypeStruct((t, n), F32),
        compiler_params=_params(("arbitrary", "arbitrary", "arbitrary")),
        name=name,
    )(x, w)


def _post_mix_kernel(x_ref, y_ref, g1_ref, g2_ref, x1_ref, h2_ref):
    y = y_ref[...]
    x1 = x_ref[...] + (y * _rms_scale(y)) * g1_ref[...]
    x1_ref[...] = x1
    h2_ref[...] = ((x1 * _rms_scale(x1)) * g2_ref[...]).astype(h2_ref.dtype)


def _post_mix(x, y, g_post, g_pre):
    t, d = x.shape
    bt = _tile(t, 256, SUBLANE)
    row = pl.BlockSpec((bt, d), lambda i: (i, 0))
    vec = pl.BlockSpec((1, d), lambda i: (0, 0))
    return pl.pallas_call(
        _post_mix_kernel,
        grid=(t // bt,),
        in_specs=[row, row, vec, vec],
        out_specs=[row, row],
        out_shape=[jax.ShapeDtypeStruct((t, d), F32), jax.ShapeDtypeStruct((t, d), BF16)],
        compiler_params=_params(("arbitrary",)),
        name="post_mix",
    )(x, y, g_post.reshape(1, d), g_pre.reshape(1, d))


def _residual_norm_kernel(x_ref, y_ref, g_ref, o_ref):
    y = y_ref[...]
    o_ref[...] = x_ref[...] + (y * _rms_scale(y)) * g_ref[...]


def _residual_norm(x, y, g):
    t, d = x.shape
    bt = _tile(t, 256, SUBLANE)
    row = pl.BlockSpec((bt, d), lambda i: (i, 0))
    return pl.pallas_call(
        _residual_norm_kernel,
        grid=(t // bt,),
        in_specs=[row, row, pl.BlockSpec((1, d), lambda i: (0, 0))],
        out_specs=row,
        out_shape=jax.ShapeDtypeStruct((t, d), F32),
        compiler_params=_params(("arbitrary",)),
        name="residual_norm",
    )(x, y, g.reshape(1, d))


def _ffn_up_kernel(*refs, conv_w, s_len, d_ff):
    if s_len is None:
        x_ref, wg_ref, wu_ref, cw_ref, cb_ref, out_ref, tail_ref, a_buf, u_buf, o_buf = refs
        hist_refs = None
    else:
        x_ref, wg_ref, wu_ref, cw_ref, cb_ref, h1_ref, h2_ref, out_ref, tail_ref, a_buf, u_buf, o_buf = refs
        hist_refs = {1: h1_ref, 2: h2_ref}
    bm, bn = out_ref.shape
    half = bm // 2
    top = SUBLANE
    n_slabs = bn // LANE
    per_strip = min(n_slabs, MXU_WIDTH // LANE)

    @pl.when(pl.program_id(1) == 0)
    def _():
        a_buf[:, 0:top, :] = jnp.zeros((n_slabs, top, LANE), F32)

    lane = lax.broadcasted_iota(jnp.int32, (1, LANE), 1)
    x = x_ref[...]
    for c0 in range(0, n_slabs, per_strip):
        c1 = min(c0 + per_strip, n_slabs)
        cols = slice(c0 * LANE, c1 * LANE)
        a = jnp.dot(x, wg_ref[:, cols], preferred_element_type=F32)
        u = jnp.dot(x, wu_ref[:, cols], preferred_element_type=F32)
        tail_ref[:, cols] = a if s_len is not None else a[bm - SUBLANE:, :]
        for c in range(c0, c1):
            a_buf[c, top:top + bm, :] = a[:, _lanes(c - c0)]
            u_buf[c] = u[:, _lanes(c - c0)]
            in_range = (pl.program_id(0) * bn + c * LANE + lane) < d_ff
            for e in range(2):
                taps = [a_buf[c, pl.ds(top + e - lag, half, stride=2), :] for lag in range(conv_w)]
                if hist_refs is not None:
                    s = 2 * (lax.broadcasted_iota(jnp.int32, (half, LANE), 0) % (s_len // 2)) + e
                    for lag in range(1, conv_w):
                        taps[lag] = jnp.where(s < lag, hist_refs[lag][c, pl.ds(e, half, stride=2), :], taps[lag])
                acc = cw_ref[0:1, _lanes(c)] * taps[conv_w - 1]
                for i in range(1, conv_w):
                    acc = acc + cw_ref[i:i + 1, _lanes(c)] * taps[conv_w - 1 - i]
                acc = cb_ref[:, _lanes(c)] + acc
                val = _gelu(acc) * u_buf[c, pl.ds(e, half, stride=2), :]
                o_buf[c, pl.ds(e, half, stride=2), :] = jnp.where(in_range, val, 0.0)
            out_ref[:, _lanes(c)] = o_buf[c].astype(out_ref.dtype)
            if s_len is None:
                a_buf[c, 0:top, :] = a[bm - top:, _lanes(c - c0)]


def _ffn_up(h, w_gate, w_up, conv_w, conv_b, ffp, hist=None, s_len=None):
    t, d = h.shape
    d_ff = w_gate.shape[1]
    cw = conv_w.shape[0]
    assert cw == 3 and d_ff % LANE == 0
    bn = _tile(ffp, MATMUL_TILE, LANE)
    assert ffp - d_ff < bn
    bm = _tile(t, 512, 2 * SUBLANE)
    n_slabs = bn // LANE
    x_spec = pl.BlockSpec((bm, d), lambda j, i: (i, 0))
    w_spec = pl.BlockSpec((d, bn), lambda j, i: (0, j))
    cw_spec = pl.BlockSpec((cw, bn), lambda j, i: (0, j))
    cb_spec = pl.BlockSpec((1, bn), lambda j, i: (0, j))
    out_spec = pl.BlockSpec((bm, bn), lambda j, i: (i, j))
    in_specs = [x_spec, w_spec, w_spec, cw_spec, cb_spec]
    args = [h, w_gate, w_up, conv_w, conv_b.reshape(1, d_ff)]
    if s_len is None:
        tail_rows = SUBLANE
        tail_spec = pl.BlockSpec((SUBLANE, bn), lambda j, i: (0, j))
    else:
        assert bm == t and s_len % 2 == 0
        tail_rows = t
        tail_spec = out_spec
        def slabs(rows):
            return jnp.transpose(rows.reshape(t, d_ff // LANE, LANE), (1, 0, 2))
        h1 = slabs(jnp.pad(hist[:, 1:2], ((0, 0), (0, s_len - 1), (0, 0))))
        h2 = slabs(jnp.pad(hist, ((0, 0), (0, s_len - hist.shape[1]), (0, 0))))
        hist_spec = pl.BlockSpec((n_slabs, bm, LANE), lambda j, i: (j, i, 0))
        in_specs += [hist_spec, hist_spec]
        args += [h1, h2]
    return pl.pallas_call(
        functools.partial(_ffn_up_kernel, conv_w=cw, s_len=s_len, d_ff=d_ff),
        grid=(ffp // bn, t // bm),
        in_specs=in_specs,
        out_specs=[out_spec, tail_spec],
        out_shape=[jax.ShapeDtypeStruct((t, ffp), BF16), jax.ShapeDtypeStruct((tail_rows, d_ff), F32)],
        scratch_shapes=[pltpu.VMEM((n_slabs, bm + SUBLANE, LANE), F32), pltpu.VMEM((n_slabs, bm, LANE), F32),
                        pltpu.VMEM((n_slabs, bm, LANE), F32)],
        compiler_params=_params(("arbitrary", "arbitrary")),
        name="ffn_up",
    )(*args)


def _layer(x, p, s_len=None, caches=None, conv_hist=None):
    t, d = x.shape
    d_a = p["ln_g"].shape[0]
    hpg = p["hpg"]
    gw = hpg * HEAD_DIM
    bn = p["bn"]
    o1 = 2 * d_a
    d_qkv = 3 * gw

    h = _rmsnorm_cast(x, p["g_pre_mix"])
    za_sl = _in_proj(h, p["w_in"], bn, 0, o1, _gelu, "in_proj_sgu")
    qkv_sl = _in_proj(h, p["w_in"], bn, o1, 3 * d_qkv, lambda v: v, "in_proj_qkv")
    gate_sl = _in_proj(h, p["w_in"], bn, o1 + 3 * d_qkv, 2 * d, _sigmoid, "in_proj_gates")

    if s_len is None:
        w_mix, b_mix = p["w_mix_prompt"], p["b_mix_prompt"]
    else:
        w_mix, b_mix = p["w_mix_sample"], p["b_mix_sample"]
    o_a, v_state = _sgu(za_sl, d_a, p["ln_g"], p["ln_b"], w_mix, b_mix)

    outs, lses = [], []
    for gi, (_, dil) in enumerate(DILATION_GROUPS):
        q_slab = gi * hpg
        k_slab = d_qkv // LANE + gi * hpg
        v_slab = 2 * d_qkv // LANE + gi * hpg
        bias_tab = p["rel_bias"][:, gi * hpg:(gi + 1) * hpg]
        if s_len is not None:
            o, lse = _attn_sample(qkv_sl, s_len, hpg, q_slab, k_slab, v_slab, caches[2 * gi], caches[2 * gi + 1],
                                  bias_tab, dil)
        elif gi < len(DILATION_GROUPS) - 1:
            o, lse = _attn_prompt(qkv_sl, hpg, q_slab, k_slab, v_slab, bias_tab, dil)
        else:
            o_b = _attn_prompt(qkv_sl, hpg, q_slab, k_slab, v_slab, bias_tab, dil, merge_with=(outs, lses))
            break
        outs.append(o)
        lses.append(lse)
    if s_len is not None:
        o_b = _merge(outs, lses)

    merged = _gated_proj(o_a, o_b, p["w_proj_a"], p["w_proj_b"], gate_sl, bn, 0, d // bn)
    y = _matmul(merged, p["w_out"], "out_proj")
    x1, h2 = _post_mix(x, y, p["g_post_mix"], p["g_pre_ffn"])

    act, a_tail = _ffn_up(h2, p["w_gate"], p["w_up"], p["conv_w"], p["conv_b"], p["ffp"], conv_hist, s_len)
    f = _matmul(act, p["w_down"], "ffn_down", k_tile=p["ffn_k_tile"])
    y_out = _residual_norm(x1, f, p["g_post_ffn"])
    return y_out, qkv_sl, v_state, a_tail


def kernel(x_prompt, x_sample, cache_k_g1, cache_v_g1, cache_k_g2, cache_v_g2, cache_k_g3, cache_v_g3, state_conv, g_pre_mix, w_in, sgu_ln_g, sgu_ln_b, w_spatial, b_spatial, rel_bias, w_proj_a, w_proj_b, w_out, g_post_mix, g_pre_ffn, w_gate, w_up, conv_w, conv_b, w_down, g_post_ffn):
    depth = w_in.shape[0]
    assert depth == 1
    n_prompt, seq, d = x_prompt.shape
    assert n_prompt == 1 and seq % CHUNK == 0
    n_seq, s_len, _ = x_sample.shape
    assert s_len == SUBLANE
    d_a = sgu_ln_g.shape[1]
    n_groups = w_spatial.shape[1]
    n_heads = rel_bias.shape[1]
    hpg = n_heads // len(DILATION_GROUPS)
    gw = hpg * HEAD_DIM
    d_ff = w_gate.shape[2]
    cw = conv_w.shape[1]
    bn = _tile(math.gcd(2 * d_a, gw), MATMUL_TILE, LANE)
    ffp = -(-d_ff // FFN_TILE) * FFN_TILE
    t_s = n_seq * s_len

    tri = np.tril(np.ones((CHUNK, CHUNK), np.float32))
    w_mix_prompt = (w_spatial[0] * tri).astype(BF16)
    b_mix_prompt = jnp.broadcast_to(b_spatial[0][:, :, None], (n_groups, CHUNK, LANE))
    w_small = w_spatial[0][:, :s_len, :s_len] * tri[:s_len, :s_len]
    eye = np.eye(n_seq, dtype=np.float32)
    w_mix_sample = jnp.einsum("ab,gpq->gapbq", eye, w_small).reshape(n_groups, t_s, t_s).astype(BF16)
    b_mix_sample = jnp.broadcast_to(jnp.tile(b_spatial[0][:, :s_len], (1, n_seq))[:, :, None], (n_groups, t_s, LANE))

    p = dict(
        hpg=hpg, bn=bn, ffp=ffp,
        g_pre_mix=g_pre_mix[0], w_in=w_in[0].astype(BF16), ln_g=sgu_ln_g[0], ln_b=sgu_ln_b[0],
        w_mix_prompt=w_mix_prompt, b_mix_prompt=b_mix_prompt,
        w_mix_sample=w_mix_sample, b_mix_sample=b_mix_sample,
        rel_bias=rel_bias,
        w_proj_a=w_proj_a[0].astype(BF16), w_proj_b=w_proj_b[0].astype(BF16), w_out=w_out[0].astype(BF16),
        g_post_mix=g_post_mix[0], g_pre_ffn=g_pre_ffn[0],
        w_gate=w_gate[0].astype(BF16), w_up=w_up[0].astype(BF16),
        conv_w=conv_w[0], conv_b=conv_b[0],
        w_down=jnp.pad(w_down[0].astype(BF16), ((0, ffp - d_ff), (0, 0))),
        ffn_k_tile=_tile(ffp, 3072, LANE),
        g_post_ffn=g_post_ffn[0],
    )

    caches = tuple(c[0] for c in (cache_k_g1, cache_v_g1, cache_k_g2, cache_v_g2, cache_k_g3, cache_v_g3))

    yp, zp, vp_state, ap_tail = _layer(x_prompt[0], p)
    ys, zs, vs_state, as_all = _layer(x_sample.reshape(t_s, d), p, s_len=s_len, caches=caches,
                                      conv_hist=state_conv[0])

    d_qkv = 3 * gw
    prompt_kv, sample_kv = [], []
    for gi, (win, _) in enumerate(DILATION_GROUPS):
        keep = min(win, seq)
        for base in (d_qkv, 2 * d_qkv):
            s0 = (base + gi * gw) // LANE
            pk = jnp.transpose(zp[s0:s0 + hpg, seq - keep:, :], (1, 0, 2))
            prompt_kv.append(pk.reshape(1, 1, keep, hpg, HEAD_DIM))
            sk = jnp.transpose(zs[s0:s0 + hpg], (1, 0, 2))
            sample_kv.append(sk.reshape(1, n_seq, s_len, hpg, HEAD_DIM))
    p_conv = ap_tail[SUBLANE - (cw - 1):].reshape(1, 1, cw - 1, d_ff)
    s_conv = as_all.reshape(n_seq, s_len, d_ff)[:, s_len - (cw - 1):].reshape(1, n_seq, cw - 1, d_ff)
    return (yp.reshape(1, seq, d), ys.reshape(n_seq, s_len, d),
            *prompt_kv, vp_state.reshape(1, 1, CHUNK, d_a), p_conv,
            *sample_kv, vs_state.reshape(1, n_seq, s_len, d_a), s_conv)
```

```python
import functools
import math

import numpy as np
import jax
import jax.numpy as jnp
from jax import lax
from jax.experimental import pallas as pl
from jax.experimental.pallas import tpu as pltpu

F32 = jnp.float32
BF16 = jnp.bfloat16

HEAD_DIM = 128
STEPS = 128
CHUNK = 128
GROUP_DIM_A = 128
DILATION_GROUPS = ((128, 1), (512, 4), (2048, 16))
MAX_EXACT = 16
MAX_DISTANCE = 2048
EPS = 1e-6
NEG = -1e30
ATTN_SCALE = HEAD_DIM ** -0.5

LANE = 128
SUBLANE = 8
VMEM_LIMIT_BYTES = 56 * 1024 * 1024
MATMUL_TILE = 1024
MXU_WIDTH = 256
FFN_TILE = 512
IN_PROJ_ROUND_TILE = 1024
ATTN_UNITS_PER_STEP = 16


def _tile(dim, target, align):
    best = None
    t = align
    while t <= min(dim, target):
        if dim % t == 0:
            best = t
        t += align
    return best if best is not None else dim


def _params(semantics):
    return pltpu.CompilerParams(dimension_semantics=semantics, vmem_limit_bytes=VMEM_LIMIT_BYTES)


def _gelu(x):
    return 0.5 * x * (1.0 + jnp.tanh(math.sqrt(2.0 / math.pi) * (x + 0.044715 * (x * x * x))))


def _sigmoid(x):
    return 1.0 / (1.0 + jnp.exp(-x))


def _rms_scale(x):
    return lax.rsqrt(jnp.mean(x * x, axis=-1, keepdims=True) + EPS)


def _lanes(c):
    return slice(c * LANE, (c + 1) * LANE)


def _rmsnorm_cast_kernel(x_ref, g_ref, o_ref):
    x = x_ref[...]
    o_ref[...] = ((x * _rms_scale(x)) * g_ref[...]).astype(o_ref.dtype)


def _rmsnorm_cast(x, g):
    t, d = x.shape
    bt = _tile(t, 256, SUBLANE)
    return pl.pallas_call(
        _rmsnorm_cast_kernel,
        grid=(t // bt,),
        in_specs=[pl.BlockSpec((bt, d), lambda i: (i, 0)), pl.BlockSpec((1, d), lambda i: (0, 0))],
        out_specs=pl.BlockSpec((bt, d), lambda i: (i, 0)),
        out_shape=jax.ShapeDtypeStruct((t, d), BF16),
        compiler_params=_params(("arbitrary",)),
        name="rmsnorm_cast",
    )(x, g.reshape(1, d))


def _in_proj_kernel(x_ref, w_ref, o_ref, *, act):
    acc = jnp.dot(x_ref[...], w_ref[...], preferred_element_type=F32)
    for c in range(o_ref.shape[0]):
        o_ref[c] = act(acc[:, _lanes(c)])


def _in_proj_round_kernel(x_ref, w_in, o_ref, w_out, w_buf, *, act):
    jj, i = pl.program_id(0), pl.program_id(1)
    ck = w_in.shape[0]
    slot = (jj + 1) % 2
    n_slabs, bm, _ = o_ref.shape
    row_split = 2 if bm % (4 * SUBLANE) == 0 else 1
    hm = bm // row_split
    per_strip = min(n_slabs, MXU_WIDTH // LANE)

    def round_chunk():
        chunk = w_in[...].astype(BF16)
        w_buf[jj % 2, pl.ds(pl.multiple_of(i * ck, ck), ck), :] = chunk
        w_out[...] = chunk

    pl.when(jj == 0)(round_chunk)

    @pl.when(jj > 0)
    def _():
        round_chunk()
        for h in range(row_split):
            x = x_ref[h * hm:(h + 1) * hm, :]
            for c0 in range(0, n_slabs, per_strip):
                c1 = min(c0 + per_strip, n_slabs)
                acc = jnp.dot(x, w_buf[slot, :, c0 * LANE:c1 * LANE], preferred_element_type=F32)
                for c in range(c0, c1):
                    o_ref[c, h * hm:(h + 1) * hm, :] = act(acc[:, _lanes(c - c0)])


def _in_proj_round(h, w, col0, n_cols, act, name):
    t, d = h.shape
    bn = _tile(math.gcd(n_cols, col0) if col0 else n_cols, IN_PROJ_ROUND_TILE, LANE)
    bm = _tile(t, MATMUL_TILE, SUBLANE)
    ni, nj = t // bm, n_cols // bn
    assert d % ni == 0
    ck = d // ni

    def fin(jj, i):
        return jnp.maximum(jj - 1, 0), jnp.where(jj == 0, 0, i)

    def chunk(jj, i):
        return jnp.where(jj == nj, ni - 1, i), jnp.minimum(jj, nj - 1)

    return pl.pallas_call(
        functools.partial(_in_proj_round_kernel, act=act),
        grid=(nj + 1, ni),
        in_specs=[pl.BlockSpec((bm, d), lambda jj, i: (fin(jj, i)[1], 0)),
                  pl.BlockSpec((ck, bn), lambda jj, i: (chunk(jj, i)[0], col0 // bn + chunk(jj, i)[1]))],
        out_specs=[pl.BlockSpec((bn // LANE, bm, LANE), lambda jj, i: fin(jj, i) + (0,)),
                   pl.BlockSpec((ck, bn), chunk)],
        out_shape=[jax.ShapeDtypeStruct((n_cols // LANE, t, LANE), F32),
                   jax.ShapeDtypeStruct((d, n_cols), BF16)],
        scratch_shapes=[pltpu.VMEM((2, d, bn), BF16)],
        compiler_params=_params(("arbitrary", "arbitrary")),
        name=name,
    )(h, w)


def _in_proj(h, w, bn, col0, n_cols, act, name):
    t, d = h.shape
    bm = _tile(t, MATMUL_TILE, SUBLANE)
    return pl.pallas_call(
        functools.partial(_in_proj_kernel, act=act),
        grid=(t // bm, n_cols // bn),
        in_specs=[pl.BlockSpec((bm, d), lambda i, j: (i, 0)),
                  pl.BlockSpec((d, bn), lambda i, j: (0, col0 // bn + j))],
        out_specs=pl.BlockSpec((bn // LANE, bm, LANE), lambda i, j: (j, i, 0)),
        out_shape=jax.ShapeDtypeStruct((n_cols // LANE, t, LANE), F32),
        compiler_params=_params(("arbitrary", "arbitrary")),
        name=name,
    )(h, w)


def _sgu_kernel(u_ref, v_ref, lg_ref, lb_ref, w_ref, b_ref, o_ref, vs_ref, *, n_groups):
    vp = v_ref[...]
    n_feat = n_groups * GROUP_DIM_A
    mu = jnp.sum(jnp.sum(vp, axis=0), axis=-1, keepdims=True) / n_feat
    vc = vp - mu
    var = jnp.sum(jnp.sum(vc * vc, axis=0), axis=-1, keepdims=True) / n_feat
    v = vc * lax.rsqrt(var + EPS) * lg_ref[...] + lb_ref[...]
    for g in range(n_groups):
        vs_ref[:, _lanes(g)] = v[g]
        mixed = jnp.dot(w_ref[g], v[g].astype(BF16), preferred_element_type=F32) + b_ref[g]
        o_ref[:, _lanes(g)] = (u_ref[g] * mixed).astype(o_ref.dtype)


def _sgu(z_sl, d_a, ln_g, ln_b, w_mix, b_mix):
    t = z_sl.shape[1]
    n_groups, c, _ = w_mix.shape
    slab = pl.BlockSpec((n_groups, 1, LANE), lambda i: (0, 0, 0))
    return pl.pallas_call(
        functools.partial(_sgu_kernel, n_groups=n_groups),
        grid=(t // c,),
        in_specs=[
            pl.BlockSpec((n_groups, c, LANE), lambda i: (0, i, 0)),
            pl.BlockSpec((n_groups, c, LANE), lambda i: (1, i, 0)),
            slab, slab,
            pl.BlockSpec((n_groups, c, c), lambda i: (0, 0, 0)),
            pl.BlockSpec((n_groups, c, LANE), lambda i: (0, 0, 0)),
        ],
        out_specs=[pl.BlockSpec((c, d_a), lambda i: (i, 0)), pl.BlockSpec((c, d_a), lambda i: (0, 0))],
        out_shape=[jax.ShapeDtypeStruct((t, d_a), BF16), jax.ShapeDtypeStruct((c, d_a), F32)],
        compiler_params=_params(("arbitrary",)),
        name="sgu",
    )(z_sl, z_sl, ln_g.reshape(n_groups, 1, LANE), ln_b.reshape(n_groups, 1, LANE), w_mix, b_mix)


def _t5_bucket(dist, n_buckets):
    n = np.asarray(dist, np.int32)
    safe = np.maximum(n, 1).astype(np.float32)
    large = MAX_EXACT + (np.log(safe / MAX_EXACT) / np.log(np.float32(MAX_DISTANCE / MAX_EXACT))
                         * (n_buckets - MAX_EXACT)).astype(np.int32)
    large = np.minimum(large, n_buckets - 1)
    return np.where(n < MAX_EXACT, n, large).astype(np.int32)


def _bias_lookup(bias_tab, bucket):
    n_buckets = bias_tab.shape[0]
    flat = np.asarray(bucket).reshape(-1)
    onehot = (jnp.asarray(flat)[None, :] == jnp.arange(n_buckets)[:, None]).astype(F32)
    out = jnp.dot(bias_tab.T, onehot, precision=lax.Precision.HIGHEST)
    return out.reshape((bias_tab.shape[1],) + tuple(np.asarray(bucket).shape))


def _merge_by_lse(outs, lses):
    m = functools.reduce(jnp.maximum, lses)
    w = [jnp.exp(l - m) for l in lses]
    num = functools.reduce(lambda x, y: x + y, [wi * oi for wi, oi in zip(w, outs)])
    return num / functools.reduce(lambda x, y: x + y, w)


def _attn_prompt_kernel(*refs, hb, dil, n_other):
    q_ref, kc_ref, vc_ref, bias_ref, band_ref = refs[:5]
    others = refs[5:5 + 2 * n_other]
    n_out = 1 if n_other else 2
    outs = refs[5 + 2 * n_other:5 + 2 * n_other + n_out]
    kp_ref, vp_ref = refs[5 + 2 * n_other + n_out:][:2]
    o_acc, lse_acc = refs[5 + 2 * n_other + n_out + 2:] if n_other else outs
    b = pl.program_id(1)

    @pl.when(b == 0)
    def _():
        kp_ref[...] = jnp.zeros(kp_ref.shape, F32)
        vp_ref[...] = jnp.zeros(vp_ref.shape, F32)

    col = lax.broadcasted_iota(jnp.int32, (STEPS, 2 * STEPS), 1)
    valid = (band_ref[...] > 0.5) & ((b > 0) | (col >= STEPS))
    for hh in range(hb):
        for r in range(dil):
            rows = pl.ds(r, STEPS, stride=dil) if dil > 1 else pl.ds(0, STEPS)
            q = q_ref[hh, rows, :].astype(BF16)
            kk = jnp.concatenate([kp_ref[hh, rows, :], kc_ref[hh, rows, :]], axis=0).astype(BF16)
            vv = jnp.concatenate([vp_ref[hh, rows, :], vc_ref[hh, rows, :]], axis=0).astype(BF16)
            s = lax.dot_general(q, kk, (((1,), (1,)), ((), ())), preferred_element_type=F32) * ATTN_SCALE
            s = jnp.where(valid, s + bias_ref[hh], NEG)
            m = jnp.max(s, axis=-1, keepdims=True)
            e = jnp.exp(s - m)
            den = jnp.sum(e, axis=-1, keepdims=True)
            o_acc[hh, rows, :] = jnp.dot(e.astype(BF16), vv, preferred_element_type=F32) / den
            lse_acc[hh, rows, :] = jnp.broadcast_to(m + jnp.log(den), (STEPS, HEAD_DIM))
    kp_ref[...] = kc_ref[...]
    vp_ref[...] = vc_ref[...]
    if n_other:
        for hh in range(hb):
            merged = _merge_by_lse([r[hh] for r in others[:n_other]] + [o_acc[hh]],
                                   [r[hh] for r in others[n_other:]] + [lse_acc[hh]])
            outs[0][:, _lanes(hh)] = merged.astype(outs[0].dtype)


def _attn_prompt(z_sl, hpg, q_slab, k_slab, v_slab, bias_tab, dil, merge_with=None):
    t = z_sl.shape[1]
    rows = STEPS * dil
    assert t % rows == 0
    nb = t // rows
    hb = min(hpg, max(1, ATTN_UNITS_PER_STEP // dil))
    assert hpg % hb == 0 and q_slab % hb == 0 and k_slab % hb == 0 and v_slab % hb == 0

    p_idx = np.arange(STEPS)[:, None]
    c_idx = np.arange(2 * STEPS)[None, :]
    steps = p_idx + STEPS - c_idx
    band = ((steps >= 0) & (steps <= STEPS)).astype(np.float32)
    bias = _bias_lookup(bias_tab, _t5_bucket(np.clip(steps, 0, STEPS) * dil, bias_tab.shape[0]))

    def cur(slab):
        return pl.BlockSpec((hb, rows, LANE), lambda hi, b: (slab // hb + hi, b, 0))

    head_major = pl.BlockSpec((hb, rows, LANE), lambda hi, b: (hi, b, 0))
    block = pltpu.VMEM((hb, rows, LANE), F32)
    others = [] if merge_with is None else list(merge_with[0]) + list(merge_with[1])
    if merge_with is None:
        out_specs = [head_major, head_major]
        out_shape = [jax.ShapeDtypeStruct((hpg, t, LANE), F32)] * 2
        scratch = [block, block]
    else:
        out_specs = [pl.BlockSpec((rows, hb * LANE), lambda hi, b: (b, hi))]
        out_shape = [jax.ShapeDtypeStruct((t, hpg * LANE), BF16)]
        scratch = [block, block, block, block]
    res = pl.pallas_call(
        functools.partial(_attn_prompt_kernel, hb=hb, dil=dil, n_other=len(others) // 2),
        grid=(hpg // hb, nb),
        in_specs=[cur(q_slab), cur(k_slab), cur(v_slab),
                  pl.BlockSpec((hb, STEPS, 2 * STEPS), lambda hi, b: (hi, 0, 0)),
                  pl.BlockSpec((STEPS, 2 * STEPS), lambda hi, b: (0, 0))] + [head_major] * len(others),
        out_specs=out_specs,
        out_shape=out_shape,
        scratch_shapes=scratch,
        compiler_params=_params(("arbitrary", "arbitrary")),
        name=f"attn_prompt_d{dil}",
    )(z_sl, z_sl, z_sl, bias, jnp.asarray(band), *others)
    return res[0] if merge_with is not None else res


def _attn_sample_kernel(q_ref, kn_ref, vn_ref, ck_ref, cv_ref, bc_ref, mc_ref, bnew_ref, mnew_ref,
                        o_ref, lse_ref, *, hpg, s_len, grouped):
    b = pl.program_id(0)
    n_new = kn_ref.shape[1]
    n_keys = mc_ref.shape[1]
    col = lax.broadcasted_iota(jnp.int32, (s_len, n_new), 1)
    own = (col >= b * s_len) & (col < (b + 1) * s_len)
    valid_new = (mnew_ref[...] > 0.5) & own
    valid_c = mc_ref[...] > 0.5

    def cache_head(ref, h):
        if grouped:
            return ref[0, :, pl.ds(h, s_len, stride=hpg), :].reshape(n_keys, HEAD_DIM).astype(BF16)
        return ref[0, pl.ds(h, n_keys, stride=hpg) if hpg > 1 else pl.ds(0, n_keys), :].astype(BF16)

    for h in range(hpg):
        q = q_ref[h].astype(BF16)
        sc = lax.dot_general(q, cache_head(ck_ref, h), (((1,), (1,)), ((), ())),
                             preferred_element_type=F32) * ATTN_SCALE
        sc = jnp.where(valid_c, sc + bc_ref[h], NEG)
        sn = lax.dot_general(q, kn_ref[h].astype(BF16), (((1,), (1,)), ((), ())),
                             preferred_element_type=F32) * ATTN_SCALE
        sn = jnp.where(valid_new, sn + bnew_ref[h], NEG)
        m = jnp.maximum(jnp.max(sc, axis=-1, keepdims=True), jnp.max(sn, axis=-1, keepdims=True))
        ec = jnp.exp(sc - m)
        en = jnp.exp(sn - m)
        den = jnp.sum(ec, axis=-1, keepdims=True) + jnp.sum(en, axis=-1, keepdims=True)
        acc = jnp.dot(ec.astype(BF16), cache_head(cv_ref, h), preferred_element_type=F32)
        acc = acc + jnp.dot(en.astype(BF16), vn_ref[h].astype(BF16), preferred_element_type=F32)
        o_ref[h] = acc / den
        lse_ref[h] = jnp.broadcast_to(m + jnp.log(den), (s_len, HEAD_DIM))


def _attn_sample(z_sl, s_len, hpg, q_slab, k_slab, v_slab, cache_k, cache_v, bias_tab, dil):
    t = z_sl.shape[1]
    n_seq = t // s_len
    lc = cache_k.shape[1]
    assert q_slab % hpg == 0 and k_slab % hpg == 0 and v_slab % hpg == 0

    j = np.arange(STEPS + 1)
    idx = lc + np.arange(s_len)[:, None] - j[None, :] * dil
    assert idx.min() >= 0
    bucket = _t5_bucket(j * dil, bias_tab.shape[0])
    mask = np.zeros((s_len, lc + s_len), np.float32)
    bsel = np.zeros((s_len, lc + s_len), np.int32)
    for s in range(s_len):
        mask[s, idx[s]] = 1.0
        bsel[s, idx[s]] = bucket

    grouped = dil > s_len and lc % dil == 0 and s_len == SUBLANE
    if grouped:
        pos = (np.arange(lc // dil)[:, None] * dil + np.arange(s_len)[None, :]).reshape(-1)
        assert mask[:, :lc].sum() == mask[:, pos].sum()
        ck = cache_k.reshape(n_seq, lc // dil, dil * hpg, HEAD_DIM)
        cv = cache_v.reshape(n_seq, lc // dil, dil * hpg, HEAD_DIM)
        cache_spec = pl.BlockSpec((1, lc // dil, s_len * hpg, LANE), lambda b: (b, 0, 0, 0))
    else:
        pos = np.arange(lc)
        ck = cache_k.reshape(n_seq, lc * hpg, HEAD_DIM)
        cv = cache_v.reshape(n_seq, lc * hpg, HEAD_DIM)
        cache_spec = pl.BlockSpec((1, lc * hpg, LANE), lambda b: (b, 0, 0))
    n_keys = len(pos)
    bias_c = _bias_lookup(bias_tab, bsel[:, pos])
    bias_new = jnp.tile(_bias_lookup(bias_tab, bsel[:, lc:]), (1, 1, n_seq))
    mask_c = jnp.asarray(mask[:, pos])
    mask_new = jnp.asarray(np.tile(mask[:, lc:], (1, n_seq)))

    out_spec = pl.BlockSpec((hpg, s_len, LANE), lambda b: (0, b, 0))
    return pl.pallas_call(
        functools.partial(_attn_sample_kernel, hpg=hpg, s_len=s_len, grouped=grouped),
        grid=(n_seq,),
        in_specs=[pl.BlockSpec((hpg, s_len, LANE), lambda b: (q_slab // hpg, b, 0)),
                  pl.BlockSpec((hpg, t, LANE), lambda b: (k_slab // hpg, 0, 0)),
                  pl.BlockSpec((hpg, t, LANE), lambda b: (v_slab // hpg, 0, 0)),
                  cache_spec, cache_spec,
                  pl.BlockSpec((hpg, s_len, n_keys), lambda b: (0, 0, 0)),
                  pl.BlockSpec((s_len, n_keys), lambda b: (0, 0)),
                  pl.BlockSpec((hpg, s_len, t), lambda b: (0, 0, 0)),
                  pl.BlockSpec((s_len, t), lambda b: (0, 0))],
        out_specs=[out_spec, out_spec],
        out_shape=[jax.ShapeDtypeStruct((hpg, t, LANE), F32)] * 2,
        compiler_params=_params(("arbitrary",)),
        name=f"attn_sample_d{dil}",
    )(z_sl, z_sl, z_sl, ck, cv, bias_c, mask_c, bias_new, mask_new)


def _merge_kernel(o1, o2, o3, l1, l2, l3, out_ref):
    merged = _merge_by_lse([o1[...], o2[...], o3[...]], [l1[...], l2[...], l3[...]])
    for h in range(merged.shape[0]):
        out_ref[:, _lanes(h)] = merged[h].astype(out_ref.dtype)


def _merge(outs, lses):
    hpg, t, _ = outs[0].shape
    bt = _tile(t, 256, SUBLANE)
    spec = pl.BlockSpec((hpg, bt, LANE), lambda i: (0, i, 0))
    return pl.pallas_call(
        _merge_kernel,
        grid=(t // bt,),
        in_specs=[spec] * 6,
        out_specs=pl.BlockSpec((bt, hpg * LANE), lambda i: (i, 0)),
        out_shape=jax.ShapeDtypeStruct((t, hpg * LANE), BF16),
        compiler_params=_params(("arbitrary",)),
        name="merge_groups",
    )(*outs, *lses)


def _gated_proj_kernel(a_ref, b_ref, wa_ref, wb_ref, ga_ref, gb_ref, o_ref):
    n_slabs = ga_ref.shape[0]
    per_strip = min(n_slabs, MXU_WIDTH // LANE)
    for c0 in range(0, n_slabs, per_strip):
        c1 = min(c0 + per_strip, n_slabs)
        cols = slice(c0 * LANE, c1 * LANE)
        pa = jnp.dot(a_ref[...], wa_ref[:, cols], preferred_element_type=F32)
        pb = jnp.dot(b_ref[...], wb_ref[:, cols], preferred_element_type=F32)
        for c in range(c0, c1):
            gated = ga_ref[c] * pa[:, _lanes(c - c0)] + gb_ref[c] * pb[:, _lanes(c - c0)]
            o_ref[:, _lanes(c)] = gated.astype(o_ref.dtype)


def _gated_proj(o_a, o_b, w_a, w_b, z_sl, bn, gate_a_blk, gate_b_blk):
    t, d_a = o_a.shape
    d_b = o_b.shape[1]
    d = w_a.shape[1]
    bm = _tile(t, MATMUL_TILE // 2, SUBLANE)
    return pl.pallas_call(
        _gated_proj_kernel,
        grid=(t // bm, d // bn),
        in_specs=[pl.BlockSpec((bm, d_a), lambda i, j: (i, 0)),
                  pl.BlockSpec((bm, d_b), lambda i, j: (i, 0)),
                  pl.BlockSpec((d_a, bn), lambda i, j: (0, j)),
                  pl.BlockSpec((d_b, bn), lambda i, j: (0, j)),
                  pl.BlockSpec((bn // LANE, bm, LANE), lambda i, j: (gate_a_blk + j, i, 0)),
                  pl.BlockSpec((bn // LANE, bm, LANE), lambda i, j: (gate_b_blk + j, i, 0))],
        out_specs=pl.BlockSpec((bm, bn), lambda i, j: (i, j)),
        out_shape=jax.ShapeDtypeStruct((t, d), BF16),
        compiler_params=_params(("arbitrary", "arbitrary")),
        name="gated_proj",
    )(o_a, o_b, w_a, w_b, z_sl, z_sl)


def _matmul_kernel(x_ref, w_ref, o_ref, *, nk):
    if nk == 1:
        o_ref[...] = jnp.dot(x_ref[...], w_ref[...], preferred_element_type=F32)
    else:
        @pl.when(pl.program_id(2) == 0)
        def _():
            o_ref[...] = jnp.zeros(o_ref.shape, o_ref.dtype)

        o_ref[...] += jnp.dot(x_ref[...], w_ref[...], preferred_element_type=F32)


def _matmul(x, w, name, k_tile=None):
    t, kd = x.shape
    n = w.shape[1]
    bm = _tile(t, MATMUL_TILE, SUBLANE)
    bn = _tile(n, MATMUL_TILE, LANE)
    bk = kd if k_tile is None else k_tile
    nk = kd // bk
    return pl.pallas_call(
        functools.partial(_matmul_kernel, nk=nk),
        grid=(t // bm, n // bn, nk),
        in_specs=[pl.BlockSpec((bm, bk), lambda i, j, k: (i, k)),
                  pl.BlockSpec((bk, bn), lambda i, j, k: (k, j))],
        out_specs=pl.BlockSpec((bm, bn), lambda i, j, k: (i, j)),
        out_shape=jax.ShapeDtypeStruct((t, n), F32),
        compiler_params=_params(("arbitrary", "arbitrary", "arbitrary")),
        name=name,
    )(x, w)


def _post_mix_kernel(x_ref, y_ref, g1_ref, g2_ref, x1_ref, h2_ref):
    y = y_ref[...]
    x1 = x_ref[...] + (y * _rms_scale(y)) * g1_ref[...]
    x1_ref[...] = x1
    h2_ref[...] = ((x1 * _rms_scale(x1)) * g2_ref[...]).astype(h2_ref.dtype)


def _post_mix(x, y, g_post, g_pre):
    t, d = x.shape
    bt = _tile(t, 256, SUBLANE)
    row = pl.BlockSpec((bt, d), lambda i: (i, 0))
    vec = pl.BlockSpec((1, d), lambda i: (0, 0))
    return pl.pallas_call(
        _post_mix_kernel,
        grid=(t // bt,),
        in_specs=[row, row, vec, vec],
        out_specs=[row, row],
        out_shape=[jax.ShapeDtypeStruct((t, d), F32), jax.ShapeDtypeStruct((t, d), BF16)],
        compiler_params=_params(("arbitrary",)),
        name="post_mix",
    )(x, y, g_post.reshape(1, d), g_pre.reshape(1, d))


def _residual_norm_kernel(x_ref, y_ref, g_ref, o_ref):
    y = y_ref[...]
    o_ref[...] = x_ref[...] + (y * _rms_scale(y)) * g_ref[...]


def _residual_norm(x, y, g):
    t, d = x.shape
    bt = _tile(t, 256, SUBLANE)
    row = pl.BlockSpec((bt, d), lambda i: (i, 0))
    return pl.pallas_call(
        _residual_norm_kernel,
        grid=(t // bt,),
        in_specs=[row, row, pl.BlockSpec((1, d), lambda i: (0, 0))],
        out_specs=row,
        out_shape=jax.ShapeDtypeStruct((t, d), F32),
        compiler_params=_params(("arbitrary",)),
        name="residual_norm",
    )(x, y, g.reshape(1, d))


def _ffn_up_kernel(*refs, conv_w, s_len, d_ff):
    if s_len is None:
        (x_ref, wg_in, wu_in, cw_ref, cb_ref, out_ref, tail_ref, wg_out, wu_out,
         wg_buf, wu_buf, a_buf, u_buf, o_buf) = refs
        hist_refs = None
        jj, i = pl.program_id(0), pl.program_id(1)
        ck = wg_in.shape[0]

        def round_chunks():
            rows_c = pl.ds(pl.multiple_of(i * ck, ck), ck)
            for src, buf, dst in ((wg_in, wg_buf, wg_out), (wu_in, wu_buf, wu_out)):
                chunk = src[...].astype(BF16)
                buf[jj % 2, rows_c, :] = chunk
                dst[...] = chunk

        slot = (jj + 1) % 2
        wg = lambda cols: wg_buf[slot, :, cols]
        wu = lambda cols: wu_buf[slot, :, cols]
        col_blk = jj - 1
        row_split = 2 if out_ref.shape[0] % (4 * SUBLANE) == 0 else 1
    else:
        x_ref, wg_ref, wu_ref, cw_ref, cb_ref, h1_ref, h2_ref, out_ref, tail_ref, a_buf, u_buf, o_buf = refs
        hist_refs = {1: h1_ref, 2: h2_ref}
        i = pl.program_id(1)
        wg = lambda cols: wg_ref[:, cols]
        wu = lambda cols: wu_ref[:, cols]
        col_blk = pl.program_id(0)
        row_split = 1
    bm, bn = out_ref.shape
    hm = bm // row_split
    half = hm // 2
    top = SUBLANE
    n_slabs = bn // LANE
    per_strip = min(n_slabs, MXU_WIDTH // LANE)

    def units():
        @pl.when(i == 0)
        def _():
            a_buf[:, 0:top, :] = jnp.zeros((n_slabs, top, LANE), F32)

        if s_len is None:
            round_chunks()

        lane = lax.broadcasted_iota(jnp.int32, (1, LANE), 1)
        for h in range(row_split):
            r0 = h * hm
            x = x_ref[r0:r0 + hm, :]
            for c0 in range(0, n_slabs, per_strip):
                c1 = min(c0 + per_strip, n_slabs)
                cols = slice(c0 * LANE, c1 * LANE)
                a = jnp.dot(x, wg(cols), preferred_element_type=F32)
                u = jnp.dot(x, wu(cols), preferred_element_type=F32)
                if s_len is not None:
                    tail_ref[r0:r0 + hm, cols] = a
                elif h == row_split - 1:
                    tail_ref[:, cols] = a[hm - SUBLANE:, :]
                for c in range(c0, c1):
                    a_buf[c, top + r0:top + r0 + hm, :] = a[:, _lanes(c - c0)]
                    u_buf[c, r0:r0 + hm, :] = u[:, _lanes(c - c0)]
                    in_range = (col_blk * bn + c * LANE + lane) < d_ff
                    for e in range(2):
                        taps = [a_buf[c, pl.ds(top + r0 + e - lag, half, stride=2), :] for lag in range(conv_w)]
                        if hist_refs is not None:
                            s = 2 * (lax.broadcasted_iota(jnp.int32, (half, LANE), 0) % (s_len // 2)) + e
                            for lag in range(1, conv_w):
                                hist = hist_refs[lag][c, pl.ds(r0 + e, half, stride=2), :]
                                taps[lag] = jnp.where(s < lag, hist, taps[lag])
                        acc = cw_ref[0:1, _lanes(c)] * taps[conv_w - 1]
                        for k in range(1, conv_w):
                            acc = acc + cw_ref[k:k + 1, _lanes(c)] * taps[conv_w - 1 - k]
                        acc = cb_ref[:, _lanes(c)] + acc
                        val = _gelu(acc) * u_buf[c, pl.ds(r0 + e, half, stride=2), :]
                        o_buf[c, pl.ds(r0 + e, half, stride=2), :] = jnp.where(in_range, val, 0.0)
                    out_ref[r0:r0 + hm, _lanes(c)] = o_buf[c, r0:r0 + hm, :].astype(out_ref.dtype)
        if s_len is None:
            a_buf[:, 0:top, :] = a_buf[:, bm:bm + top, :]

    if s_len is None:
        pl.when(jj == 0)(round_chunks)
        pl.when(jj > 0)(units)
    else:
        units()


def _ffn_up_prompt(h, w_gate, w_up, conv_w, conv_b, ffp):
    t, d = h.shape
    d_ff = w_gate.shape[1]
    cw = conv_w.shape[0]
    assert cw == 3 and d_ff % LANE == 0
    bn = FFN_TILE
    assert ffp % bn == 0 and ffp - d_ff < bn
    bm = _tile(t, MATMUL_TILE, 4 * SUBLANE)
    ni, nj = t // bm, ffp // bn
    assert d % ni == 0
    ck = d // ni
    n_slabs = bn // LANE

    def fin_i(jj, i):
        return jnp.where(jj == 0, 0, i)

    def fin_j(jj):
        return jnp.maximum(jj - 1, 0)

    def chunk(jj, i):
        return jnp.where(jj == nj, ni - 1, i), jnp.minimum(jj, nj - 1)

    w_in_spec = pl.BlockSpec((ck, bn), chunk)
    wbuf = pltpu.VMEM((2, d, bn), BF16)
    act, tail, wg16, wu16 = pl.pallas_call(
        functools.partial(_ffn_up_kernel, conv_w=cw, s_len=None, d_ff=d_ff),
        grid=(nj + 1, ni),
        in_specs=[pl.BlockSpec((bm, d), lambda jj, i: (fin_i(jj, i), 0)), w_in_spec, w_in_spec,
                  pl.BlockSpec((cw, bn), lambda jj, i: (0, fin_j(jj))),
                  pl.BlockSpec((1, bn), lambda jj, i: (0, fin_j(jj)))],
        out_specs=[pl.BlockSpec((bm, bn), lambda jj, i: (fin_i(jj, i), fin_j(jj))),
                   pl.BlockSpec((SUBLANE, bn), lambda jj, i: (0, fin_j(jj))),
                   w_in_spec, w_in_spec],
        out_shape=[jax.ShapeDtypeStruct((t, ffp), BF16), jax.ShapeDtypeStruct((SUBLANE, d_ff), F32),
                   jax.ShapeDtypeStruct((d, d_ff), BF16), jax.ShapeDtypeStruct((d, d_ff), BF16)],
        scratch_shapes=[wbuf, wbuf,
                        pltpu.VMEM((n_slabs, bm + SUBLANE, LANE), F32), pltpu.VMEM((n_slabs, bm, LANE), F32),
                        pltpu.VMEM((n_slabs, bm, LANE), F32)],
        compiler_params=_params(("arbitrary", "arbitrary")),
        name="ffn_up",
    )(h, w_gate, w_up, conv_w, conv_b.reshape(1, d_ff))
    return act, tail, wg16, wu16


def _ffn_up_sample(h, w_gate, w_up, conv_w, conv_b, ffp, hist, s_len):
    t, d = h.shape
    d_ff = w_gate.shape[1]
    cw = conv_w.shape[0]
    assert cw == 3 and d_ff % LANE == 0 and s_len % 2 == 0
    bn = _tile(ffp, MATMUL_TILE, LANE)
    assert ffp - d_ff < bn
    n_slabs = bn // LANE
    w_spec = pl.BlockSpec((d, bn), lambda j, i: (0, j))
    out_spec = pl.BlockSpec((t, bn), lambda j, i: (i, j))

    def slabs(rows):
        return jnp.transpose(rows.reshape(t, d_ff // LANE, LANE), (1, 0, 2))
    h1 = slabs(jnp.pad(hist[:, 1:2], ((0, 0), (0, s_len - 1), (0, 0))))
    h2 = slabs(jnp.pad(hist, ((0, 0), (0, s_len - hist.shape[1]), (0, 0))))
    hist_spec = pl.BlockSpec((n_slabs, t, LANE), lambda j, i: (j, i, 0))
    return pl.pallas_call(
        functools.partial(_ffn_up_kernel, conv_w=cw, s_len=s_len, d_ff=d_ff),
        grid=(ffp // bn, 1),
        in_specs=[pl.BlockSpec((t, d), lambda j, i: (i, 0)), w_spec, w_spec,
                  pl.BlockSpec((cw, bn), lambda j, i: (0, j)), pl.BlockSpec((1, bn), lambda j, i: (0, j)),
                  hist_spec, hist_spec],
        out_specs=[out_spec, out_spec],
        out_shape=[jax.ShapeDtypeStruct((t, ffp), BF16), jax.ShapeDtypeStruct((t, d_ff), F32)],
        scratch_shapes=[pltpu.VMEM((n_slabs, t + SUBLANE, LANE), F32), pltpu.VMEM((n_slabs, t, LANE), F32),
                        pltpu.VMEM((n_slabs, t, LANE), F32)],
        compiler_params=_params(("arbitrary", "arbitrary")),
        name="ffn_up_decode",
    )(h, w_gate, w_up, conv_w, conv_b.reshape(1, d_ff), h1, h2)


def _layer(x, p, s_len=None, caches=None, conv_hist=None):
    t, d = x.shape
    d_a = p["ln_g"].shape[0]
    hpg = p["hpg"]
    gw = hpg * HEAD_DIM
    bn = p["bn"]
    o1 = 2 * d_a
    d_qkv = 3 * gw

    h = _rmsnorm_cast(x, p["g_pre_mix"])
    w16 = dict(p.get("w16", {}))
    z_parts = []
    for name, col0, n_cols, act in (("in_proj_sgu", 0, o1, _gelu),
                                    ("in_proj_qkv", o1, 3 * d_qkv, lambda v: v),
                                    ("in_proj_gates", o1 + 3 * d_qkv, 2 * d, _sigmoid)):
        if s_len is None:
            z_part, w16[name] = _in_proj_round(h, p["w_in"], col0, n_cols, act, name)
        else:
            z_part = _in_proj(h, w16[name], bn, 0, n_cols, act, name + "_decode")
        z_parts.append(z_part)
    za_sl, qkv_sl, gate_sl = z_parts

    if s_len is None:
        w_mix, b_mix = p["w_mix_prompt"], p["b_mix_prompt"]
    else:
        w_mix, b_mix = p["w_mix_sample"], p["b_mix_sample"]
    o_a, v_state = _sgu(za_sl, d_a, p["ln_g"], p["ln_b"], w_mix, b_mix)

    outs, lses = [], []
    for gi, (_, dil) in enumerate(DILATION_GROUPS):
        q_slab = gi * hpg
        k_slab = d_qkv // LANE + gi * hpg
        v_slab = 2 * d_qkv // LANE + gi * hpg
        bias_tab = p["rel_bias"][:, gi * hpg:(gi + 1) * hpg]
        if s_len is not None:
            o, lse = _attn_sample(qkv_sl, s_len, hpg, q_slab, k_slab, v_slab, caches[2 * gi], caches[2 * gi + 1],
                                  bias_tab, dil)
        elif gi < len(DILATION_GROUPS) - 1:
            o, lse = _attn_prompt(qkv_sl, hpg, q_slab, k_slab, v_slab, bias_tab, dil)
        else:
            o_b = _attn_prompt(qkv_sl, hpg, q_slab, k_slab, v_slab, bias_tab, dil, merge_with=(outs, lses))
            break
        outs.append(o)
        lses.append(lse)
    if s_len is not None:
        o_b = _merge(outs, lses)

    merged = _gated_proj(o_a, o_b, p["w_proj_a"], p["w_proj_b"], gate_sl, bn, 0, d // bn)
    y = _matmul(merged, p["w_out"], "out_proj")
    x1, h2 = _post_mix(x, y, p["g_post_mix"], p["g_pre_ffn"])

    if s_len is None:
        act, a_tail, w16["ffn_gate"], w16["ffn_up"] = _ffn_up_prompt(h2, p["w_gate"], p["w_up"], p["conv_w"],
                                                                     p["conv_b"], p["ffp"])
    else:
        act, a_tail = _ffn_up_sample(h2, w16["ffn_gate"], w16["ffn_up"], p["conv_w"], p["conv_b"], p["ffp"],
                                     conv_hist, s_len)
    f = _matmul(act, p["w_down"], "ffn_down", k_tile=p["ffn_k_tile"])
    y_out = _residual_norm(x1, f, p["g_post_ffn"])
    return y_out, qkv_sl, v_state, a_tail, w16


def kernel(x_prompt, x_sample, cache_k_g1, cache_v_g1, cache_k_g2, cache_v_g2, cache_k_g3, cache_v_g3, state_conv, g_pre_mix, w_in, sgu_ln_g, sgu_ln_b, w_spatial, b_spatial, rel_bias, w_proj_a, w_proj_b, w_out, g_post_mix, g_pre_ffn, w_gate, w_up, conv_w, conv_b, w_down, g_post_ffn):
    depth = w_in.shape[0]
    assert depth == 1
    n_prompt, seq, d = x_prompt.shape
    assert n_prompt == 1 and seq % CHUNK == 0
    n_seq, s_len, _ = x_sample.shape
    assert s_len == SUBLANE
    d_a = sgu_ln_g.shape[1]
    n_groups = w_spatial.shape[1]
    n_heads = rel_bias.shape[1]
    hpg = n_heads // len(DILATION_GROUPS)
    gw = hpg * HEAD_DIM
    d_ff = w_gate.shape[2]
    cw = conv_w.shape[1]
    bn = _tile(math.gcd(2 * d_a, gw), MATMUL_TILE, LANE)
    ffp = -(-d_ff // FFN_TILE) * FFN_TILE
    t_s = n_seq * s_len

    tri = np.tril(np.ones((CHUNK, CHUNK), np.float32))
    w_mix_prompt = (w_spatial[0] * tri).astype(BF16)
    b_mix_prompt = jnp.broadcast_to(b_spatial[0][:, :, None], (n_groups, CHUNK, LANE))
    w_small = w_spatial[0][:, :s_len, :s_len] * tri[:s_len, :s_len]
    eye = np.eye(n_seq, dtype=np.float32)
    w_mix_sample = jnp.einsum("ab,gpq->gapbq", eye, w_small).reshape(n_groups, t_s, t_s).astype(BF16)
    b_mix_sample = jnp.broadcast_to(jnp.tile(b_spatial[0][:, :s_len], (1, n_seq))[:, :, None], (n_groups, t_s, LANE))

    p = dict(
        hpg=hpg, bn=bn, ffp=ffp,
        g_pre_mix=g_pre_mix[0], w_in=w_in[0], ln_g=sgu_ln_g[0], ln_b=sgu_ln_b[0],
        w_mix_prompt=w_mix_prompt, b_mix_prompt=b_mix_prompt,
        w_mix_sample=w_mix_sample, b_mix_sample=b_mix_sample,
        rel_bias=rel_bias,
        w_proj_a=w_proj_a[0].astype(BF16), w_proj_b=w_proj_b[0].astype(BF16), w_out=w_out[0].astype(BF16),
        g_post_mix=g_post_mix[0], g_pre_ffn=g_pre_ffn[0],
        w_gate=w_gate[0], w_up=w_up[0],
        conv_w=conv_w[0], conv_b=conv_b[0],
        w_down=jnp.pad(w_down[0].astype(BF16), ((0, ffp - d_ff), (0, 0))),
        ffn_k_tile=_tile(ffp, 3072, LANE),
        g_post_ffn=g_post_ffn[0],
    )

    caches = tuple(c[0] for c in (cache_k_g1, cache_v_g1, cache_k_g2, cache_v_g2, cache_k_g3, cache_v_g3))

    yp, zp, vp_state, ap_tail, w16 = _layer(x_prompt[0], p)
    ys, zs, vs_state, as_all, _ = _layer(x_sample.reshape(t_s, d), dict(p, w16=w16), s_len=s_len,
                                         caches=caches, conv_hist=state_conv[0])

    d_qkv = 3 * gw
    prompt_kv, sample_kv = [], []
    for gi, (win, _) in enumerate(DILATION_GROUPS):
        keep = min(win, seq)
        for base in (d_qkv, 2 * d_qkv):
            s0 = (base + gi * gw) // LANE
            pk = jnp.transpose(zp[s0:s0 + hpg, seq - keep:, :], (1, 0, 2))
            prompt_kv.append(pk.reshape(1, 1, keep, hpg, HEAD_DIM))
            sk = jnp.transpose(zs[s0:s0 + hpg], (1, 0, 2))
            sample_kv.append(sk.reshape(1, n_seq, s_len, hpg, HEAD_DIM))
    p_conv = ap_tail[SUBLANE - (cw - 1):].reshape(1, 1, cw - 1, d_ff)
    s_conv = as_all.reshape(n_seq, s_len, d_ff)[:, s_len - (cw - 1):].reshape(1, n_seq, cw - 1, d_ff)
    return (yp.reshape(1, seq, d), ys.reshape(n_seq, s_len, d),
            *prompt_kv, vp_state.reshape(1, 1, CHUNK, d_a), p_conv,
            *sample_kv, vs_state.reshape(1, n_seq, s_len, d_a), s_conv)
```

```python
import functools
import math

import numpy as np
import jax
import jax.numpy as jnp
from jax import lax
from jax.experimental import pallas as pl
from jax.experimental.pallas import tpu as pltpu

F32 = jnp.float32
BF16 = jnp.bfloat16

HEAD_DIM = 128
STEPS = 128
CHUNK = 128
GROUP_DIM_A = 128
DILATION_GROUPS = ((128, 1), (512, 4), (2048, 16))
MAX_EXACT = 16
MAX_DISTANCE = 2048
EPS = 1e-6
NEG = -1e30
ATTN_SCALE = HEAD_DIM ** -0.5

LANE = 128
SUBLANE = 8
VMEM_LIMIT_BYTES = 56 * 1024 * 1024
MATMUL_TILE = 1024
MXU_WIDTH = 256
FFN_TILE = 512
FFN_DOWN_TILE = 512
IN_PROJ_ROUND_TILE = 1024
ATTN_UNITS_PER_STEP = 16


def _tile(dim, target, align):
    best = None
    t = align
    while t <= min(dim, target):
        if dim % t == 0:
            best = t
        t += align
    return best if best is not None else dim


def _params(semantics):
    return pltpu.CompilerParams(dimension_semantics=semantics, vmem_limit_bytes=VMEM_LIMIT_BYTES)


def _gelu(x):
    return 0.5 * x * (1.0 + jnp.tanh(math.sqrt(2.0 / math.pi) * (x + 0.044715 * (x * x * x))))


def _sigmoid(x):
    return 1.0 / (1.0 + jnp.exp(-x))


def _rms_scale(x):
    return lax.rsqrt(jnp.mean(x * x, axis=-1, keepdims=True) + EPS)


def _lanes(c):
    return slice(c * LANE, (c + 1) * LANE)


def _rmsnorm_cast_kernel(x_ref, g_ref, o_ref):
    x = x_ref[...]
    o_ref[...] = ((x * _rms_scale(x)) * g_ref[...]).astype(o_ref.dtype)


def _rmsnorm_cast(x, g):
    t, d = x.shape
    bt = _tile(t, 256, SUBLANE)
    return pl.pallas_call(
        _rmsnorm_cast_kernel,
        grid=(t // bt,),
        in_specs=[pl.BlockSpec((bt, d), lambda i: (i, 0)), pl.BlockSpec((1, d), lambda i: (0, 0))],
        out_specs=pl.BlockSpec((bt, d), lambda i: (i, 0)),
        out_shape=jax.ShapeDtypeStruct((t, d), BF16),
        compiler_params=_params(("arbitrary",)),
        name="rmsnorm_cast",
    )(x, g.reshape(1, d))


def _in_proj_kernel(x_ref, w_ref, o_ref, *, act):
    acc = jnp.dot(x_ref[...], w_ref[...], preferred_element_type=F32)
    for c in range(o_ref.shape[0]):
        o_ref[c] = act(acc[:, _lanes(c)])


def _in_proj_round_kernel(x_ref, w_in, o_ref, w_out, w_buf, *, act):
    jj, i = pl.program_id(0), pl.program_id(1)
    ck = w_in.shape[0]
    slot = (jj + 1) % 2
    n_slabs, bm, _ = o_ref.shape
    row_split = 2 if bm % (4 * SUBLANE) == 0 else 1
    hm = bm // row_split
    per_strip = min(n_slabs, MXU_WIDTH // LANE)

    def round_chunk():
        chunk = w_in[...].astype(BF16)
        w_buf[jj % 2, pl.ds(pl.multiple_of(i * ck, ck), ck), :] = chunk
        w_out[...] = chunk

    pl.when(jj == 0)(round_chunk)

    @pl.when(jj > 0)
    def _():
        round_chunk()
        for h in range(row_split):
            x = x_ref[h * hm:(h + 1) * hm, :]
            for c0 in range(0, n_slabs, per_strip):
                c1 = min(c0 + per_strip, n_slabs)
                acc = jnp.dot(x, w_buf[slot, :, c0 * LANE:c1 * LANE], preferred_element_type=F32)
                for c in range(c0, c1):
                    o_ref[c, h * hm:(h + 1) * hm, :] = act(acc[:, _lanes(c - c0)])


def _in_proj_round(h, w, col0, n_cols, act, name):
    t, d = h.shape
    bn = _tile(math.gcd(n_cols, col0) if col0 else n_cols, IN_PROJ_ROUND_TILE, LANE)
    bm = _tile(t, MATMUL_TILE, SUBLANE)
    ni, nj = t // bm, n_cols // bn
    assert d % ni == 0
    ck = d // ni

    def fin(jj, i):
        return jnp.maximum(jj - 1, 0), jnp.where(jj == 0, 0, i)

    def chunk(jj, i):
        return jnp.where(jj == nj, ni - 1, i), jnp.minimum(jj, nj - 1)

    return pl.pallas_call(
        functools.partial(_in_proj_round_kernel, act=act),
        grid=(nj + 1, ni),
        in_specs=[pl.BlockSpec((bm, d), lambda jj, i: (fin(jj, i)[1], 0)),
                  pl.BlockSpec((ck, bn), lambda jj, i: (chunk(jj, i)[0], col0 // bn + chunk(jj, i)[1]))],
        out_specs=[pl.BlockSpec((bn // LANE, bm, LANE), lambda jj, i: fin(jj, i) + (0,)),
                   pl.BlockSpec((ck, bn), chunk)],
        out_shape=[jax.ShapeDtypeStruct((n_cols // LANE, t, LANE), F32),
                   jax.ShapeDtypeStruct((d, n_cols), BF16)],
        scratch_shapes=[pltpu.VMEM((2, d, bn), BF16)],
        compiler_params=_params(("arbitrary", "arbitrary")),
        name=name,
    )(h, w)


def _in_proj(h, w, bn, col0, n_cols, act, name):
    t, d = h.shape
    bm = _tile(t, MATMUL_TILE, SUBLANE)
    return pl.pallas_call(
        functools.partial(_in_proj_kernel, act=act),
        grid=(t // bm, n_cols // bn),
        in_specs=[pl.BlockSpec((bm, d), lambda i, j: (i, 0)),
                  pl.BlockSpec((d, bn), lambda i, j: (0, col0 // bn + j))],
        out_specs=pl.BlockSpec((bn // LANE, bm, LANE), lambda i, j: (j, i, 0)),
        out_shape=jax.ShapeDtypeStruct((n_cols // LANE, t, LANE), F32),
        compiler_params=_params(("arbitrary", "arbitrary")),
        name=name,
    )(h, w)


def _sgu_kernel(u_ref, v_ref, lg_ref, lb_ref, w_ref, b_ref, o_ref, vs_ref, *, n_groups):
    vp = v_ref[...]
    n_feat = n_groups * GROUP_DIM_A
    mu = jnp.sum(jnp.sum(vp, axis=0), axis=-1, keepdims=True) / n_feat
    vc = vp - mu
    var = jnp.sum(jnp.sum(vc * vc, axis=0), axis=-1, keepdims=True) / n_feat
    v = vc * lax.rsqrt(var + EPS) * lg_ref[...] + lb_ref[...]
    for g in range(n_groups):
        vs_ref[:, _lanes(g)] = v[g]
        mixed = jnp.dot(w_ref[g], v[g].astype(BF16), preferred_element_type=F32) + b_ref[g]
        o_ref[:, _lanes(g)] = (u_ref[g] * mixed).astype(o_ref.dtype)


def _sgu(z_sl, d_a, ln_g, ln_b, w_mix, b_mix):
    t = z_sl.shape[1]
    n_groups, c, _ = w_mix.shape
    slab = pl.BlockSpec((n_groups, 1, LANE), lambda i: (0, 0, 0))
    return pl.pallas_call(
        functools.partial(_sgu_kernel, n_groups=n_groups),
        grid=(t // c,),
        in_specs=[
            pl.BlockSpec((n_groups, c, LANE), lambda i: (0, i, 0)),
            pl.BlockSpec((n_groups, c, LANE), lambda i: (1, i, 0)),
            slab, slab,
            pl.BlockSpec((n_groups, c, c), lambda i: (0, 0, 0)),
            pl.BlockSpec((n_groups, c, LANE), lambda i: (0, 0, 0)),
        ],
        out_specs=[pl.BlockSpec((c, d_a), lambda i: (i, 0)), pl.BlockSpec((c, d_a), lambda i: (0, 0))],
        out_shape=[jax.ShapeDtypeStruct((t, d_a), BF16), jax.ShapeDtypeStruct((c, d_a), F32)],
        compiler_params=_params(("arbitrary",)),
        name="sgu",
    )(z_sl, z_sl, ln_g.reshape(n_groups, 1, LANE), ln_b.reshape(n_groups, 1, LANE), w_mix, b_mix)


def _t5_bucket(dist, n_buckets):
    n = np.asarray(dist, np.int32)
    safe = np.maximum(n, 1).astype(np.float32)
    large = MAX_EXACT + (np.log(safe / MAX_EXACT) / np.log(np.float32(MAX_DISTANCE / MAX_EXACT))
                         * (n_buckets - MAX_EXACT)).astype(np.int32)
    large = np.minimum(large, n_buckets - 1)
    return np.where(n < MAX_EXACT, n, large).astype(np.int32)


def _bias_lookup(bias_tab, bucket):
    n_buckets = bias_tab.shape[0]
    flat = np.asarray(bucket).reshape(-1)
    onehot = (jnp.asarray(flat)[None, :] == jnp.arange(n_buckets)[:, None]).astype(F32)
    out = jnp.dot(bias_tab.T, onehot, precision=lax.Precision.HIGHEST)
    return out.reshape((bias_tab.shape[1],) + tuple(np.asarray(bucket).shape))


def _merge_by_lse(outs, lses):
    m = functools.reduce(jnp.maximum, lses)
    w = [jnp.exp(l - m) for l in lses]
    num = functools.reduce(lambda x, y: x + y, [wi * oi for wi, oi in zip(w, outs)])
    return num / functools.reduce(lambda x, y: x + y, w)


def _attn_prompt_kernel(*refs, hb, dil, n_other):
    q_ref, kc_ref, vc_ref, bias_ref, band_ref = refs[:5]
    others = refs[5:5 + 2 * n_other]
    n_out = 1 if n_other else 2
    outs = refs[5 + 2 * n_other:5 + 2 * n_other + n_out]
    kp_ref, vp_ref = refs[5 + 2 * n_other + n_out:][:2]
    o_acc, lse_acc = refs[5 + 2 * n_other + n_out + 2:] if n_other else outs
    b = pl.program_id(1)

    @pl.when(b == 0)
    def _():
        kp_ref[...] = jnp.zeros(kp_ref.shape, F32)
        vp_ref[...] = jnp.zeros(vp_ref.shape, F32)

    col = lax.broadcasted_iota(jnp.int32, (STEPS, 2 * STEPS), 1)
    valid = (band_ref[...] > 0.5) & ((b > 0) | (col >= STEPS))
    for hh in range(hb):
        for r in range(dil):
            rows = pl.ds(r, STEPS, stride=dil) if dil > 1 else pl.ds(0, STEPS)
            q = q_ref[hh, rows, :].astype(BF16)
            kk = jnp.concatenate([kp_ref[hh, rows, :], kc_ref[hh, rows, :]], axis=0).astype(BF16)
            vv = jnp.concatenate([vp_ref[hh, rows, :], vc_ref[hh, rows, :]], axis=0).astype(BF16)
            s = lax.dot_general(q, kk, (((1,), (1,)), ((), ())), preferred_element_type=F32) * ATTN_SCALE
            s = jnp.where(valid, s + bias_ref[hh], NEG)
            m = jnp.max(s, axis=-1, keepdims=True)
            e = jnp.exp(s - m)
            den = jnp.sum(e, axis=-1, keepdims=True)
            o_acc[hh, rows, :] = jnp.dot(e.astype(BF16), vv, preferred_element_type=F32) / den
            lse_acc[hh, rows, :] = jnp.broadcast_to(m + jnp.log(den), (STEPS, HEAD_DIM))
    kp_ref[...] = kc_ref[...]
    vp_ref[...] = vc_ref[...]
    if n_other:
        for hh in range(hb):
            merged = _merge_by_lse([r[hh] for r in others[:n_other]] + [o_acc[hh]],
                                   [r[hh] for r in others[n_other:]] + [lse_acc[hh]])
            outs[0][:, _lanes(hh)] = merged.astype(outs[0].dtype)


def _attn_prompt(z_sl, hpg, q_slab, k_slab, v_slab, bias_tab, dil, merge_with=None):
    t = z_sl.shape[1]
    rows = STEPS * dil
    assert t % rows == 0
    nb = t // rows
    hb = min(hpg, max(1, ATTN_UNITS_PER_STEP // dil))
    assert hpg % hb == 0 and q_slab % hb == 0 and k_slab % hb == 0 and v_slab % hb == 0

    p_idx = np.arange(STEPS)[:, None]
    c_idx = np.arange(2 * STEPS)[None, :]
    steps = p_idx + STEPS - c_idx
    band = ((steps >= 0) & (steps <= STEPS)).astype(np.float32)
    bias = _bias_lookup(bias_tab, _t5_bucket(np.clip(steps, 0, STEPS) * dil, bias_tab.shape[0]))

    def cur(slab):
        return pl.BlockSpec((hb, rows, LANE), lambda hi, b: (slab // hb + hi, b, 0))

    head_major = pl.BlockSpec((hb, rows, LANE), lambda hi, b: (hi, b, 0))
    block = pltpu.VMEM((hb, rows, LANE), F32)
    others = [] if merge_with is None else list(merge_with[0]) + list(merge_with[1])
    if merge_with is None:
        out_specs = [head_major, head_major]
        out_shape = [jax.ShapeDtypeStruct((hpg, t, LANE), F32)] * 2
        scratch = [block, block]
    else:
        out_specs = [pl.BlockSpec((rows, hb * LANE), lambda hi, b: (b, hi))]
        out_shape = [jax.ShapeDtypeStruct((t, hpg * LANE), BF16)]
        scratch = [block, block, block, block]
    res = pl.pallas_call(
        functools.partial(_attn_prompt_kernel, hb=hb, dil=dil, n_other=len(others) // 2),
        grid=(hpg // hb, nb),
        in_specs=[cur(q_slab), cur(k_slab), cur(v_slab),
                  pl.BlockSpec((hb, STEPS, 2 * STEPS), lambda hi, b: (hi, 0, 0)),
                  pl.BlockSpec((STEPS, 2 * STEPS), lambda hi, b: (0, 0))] + [head_major] * len(others),
        out_specs=out_specs,
        out_shape=out_shape,
        scratch_shapes=scratch,
        compiler_params=_params(("arbitrary", "arbitrary")),
        name=f"attn_prompt_d{dil}",
    )(z_sl, z_sl, z_sl, bias, jnp.asarray(band), *others)
    return res[0] if merge_with is not None else res


def _attn_sample_kernel(q_ref, kn_ref, vn_ref, ck_ref, cv_ref, bc_ref, mc_ref, bnew_ref, mnew_ref,
                        o_ref, lse_ref, *, hpg, s_len, grouped):
    b = pl.program_id(0)
    n_new = kn_ref.shape[1]
    n_keys = mc_ref.shape[1]
    col = lax.broadcasted_iota(jnp.int32, (s_len, n_new), 1)
    own = (col >= b * s_len) & (col < (b + 1) * s_len)
    valid_new = (mnew_ref[...] > 0.5) & own
    valid_c = mc_ref[...] > 0.5

    def cache_head(ref, h):
        if grouped:
            return ref[0, :, pl.ds(h, s_len, stride=hpg), :].reshape(n_keys, HEAD_DIM).astype(BF16)
        return ref[0, pl.ds(h, n_keys, stride=hpg) if hpg > 1 else pl.ds(0, n_keys), :].astype(BF16)

    for h in range(hpg):
        q = q_ref[h].astype(BF16)
        sc = lax.dot_general(q, cache_head(ck_ref, h), (((1,), (1,)), ((), ())),
                             preferred_element_type=F32) * ATTN_SCALE
        sc = jnp.where(valid_c, sc + bc_ref[h], NEG)
        sn = lax.dot_general(q, kn_ref[h].astype(BF16), (((1,), (1,)), ((), ())),
                             preferred_element_type=F32) * ATTN_SCALE
        sn = jnp.where(valid_new, sn + bnew_ref[h], NEG)
        m = jnp.maximum(jnp.max(sc, axis=-1, keepdims=True), jnp.max(sn, axis=-1, keepdims=True))
        ec = jnp.exp(sc - m)
        en = jnp.exp(sn - m)
        den = jnp.sum(ec, axis=-1, keepdims=True) + jnp.sum(en, axis=-1, keepdims=True)
        acc = jnp.dot(ec.astype(BF16), cache_head(cv_ref, h), preferred_element_type=F32)
        acc = acc + jnp.dot(en.astype(BF16), vn_ref[h].astype(BF16), preferred_element_type=F32)
        o_ref[h] = acc / den
        lse_ref[h] = jnp.broadcast_to(m + jnp.log(den), (s_len, HEAD_DIM))


def _attn_sample(z_sl, s_len, hpg, q_slab, k_slab, v_slab, cache_k, cache_v, bias_tab, dil):
    t = z_sl.shape[1]
    n_seq = t // s_len
    lc = cache_k.shape[1]
    assert q_slab % hpg == 0 and k_slab % hpg == 0 and v_slab % hpg == 0

    j = np.arange(STEPS + 1)
    idx = lc + np.arange(s_len)[:, None] - j[None, :] * dil
    assert idx.min() >= 0
    bucket = _t5_bucket(j * dil, bias_tab.shape[0])
    mask = np.zeros((s_len, lc + s_len), np.float32)
    bsel = np.zeros((s_len, lc + s_len), np.int32)
    for s in range(s_len):
        mask[s, idx[s]] = 1.0
        bsel[s, idx[s]] = bucket

    grouped = dil > s_len and lc % dil == 0 and s_len == SUBLANE
    if grouped:
        pos = (np.arange(lc // dil)[:, None] * dil + np.arange(s_len)[None, :]).reshape(-1)
        assert mask[:, :lc].sum() == mask[:, pos].sum()
        ck = cache_k.reshape(n_seq, lc // dil, dil * hpg, HEAD_DIM)
        cv = cache_v.reshape(n_seq, lc // dil, dil * hpg, HEAD_DIM)
        cache_spec = pl.BlockSpec((1, lc // dil, s_len * hpg, LANE), lambda b: (b, 0, 0, 0))
    else:
        pos = np.arange(lc)
        ck = cache_k.reshape(n_seq, lc * hpg, HEAD_DIM)
        cv = cache_v.reshape(n_seq, lc * hpg, HEAD_DIM)
        cache_spec = pl.BlockSpec((1, lc * hpg, LANE), lambda b: (b, 0, 0))
    n_keys = len(pos)
    bias_c = _bias_lookup(bias_tab, bsel[:, pos])
    bias_new = jnp.tile(_bias_lookup(bias_tab, bsel[:, lc:]), (1, 1, n_seq))
    mask_c = jnp.asarray(mask[:, pos])
    mask_new = jnp.asarray(np.tile(mask[:, lc:], (1, n_seq)))

    out_spec = pl.BlockSpec((hpg, s_len, LANE), lambda b: (0, b, 0))
    return pl.pallas_call(
        functools.partial(_attn_sample_kernel, hpg=hpg, s_len=s_len, grouped=grouped),
        grid=(n_seq,),
        in_specs=[pl.BlockSpec((hpg, s_len, LANE), lambda b: (q_slab // hpg, b, 0)),
                  pl.BlockSpec((hpg, t, LANE), lambda b: (k_slab // hpg, 0, 0)),
                  pl.BlockSpec((hpg, t, LANE), lambda b: (v_slab // hpg, 0, 0)),
                  cache_spec, cache_spec,
                  pl.BlockSpec((hpg, s_len, n_keys), lambda b: (0, 0, 0)),
                  pl.BlockSpec((s_len, n_keys), lambda b: (0, 0)),
                  pl.BlockSpec((hpg, s_len, t), lambda b: (0, 0, 0)),
                  pl.BlockSpec((s_len, t), lambda b: (0, 0))],
        out_specs=[out_spec, out_spec],
        out_shape=[jax.ShapeDtypeStruct((hpg, t, LANE), F32)] * 2,
        compiler_params=_params(("arbitrary",)),
        name=f"attn_sample_d{dil}",
    )(z_sl, z_sl, z_sl, ck, cv, bias_c, mask_c, bias_new, mask_new)


def _merge_kernel(o1, o2, o3, l1, l2, l3, out_ref):
    merged = _merge_by_lse([o1[...], o2[...], o3[...]], [l1[...], l2[...], l3[...]])
    for h in range(merged.shape[0]):
        out_ref[:, _lanes(h)] = merged[h].astype(out_ref.dtype)


def _merge(outs, lses):
    hpg, t, _ = outs[0].shape
    bt = _tile(t, 256, SUBLANE)
    spec = pl.BlockSpec((hpg, bt, LANE), lambda i: (0, i, 0))
    return pl.pallas_call(
        _merge_kernel,
        grid=(t // bt,),
        in_specs=[spec] * 6,
        out_specs=pl.BlockSpec((bt, hpg * LANE), lambda i: (i, 0)),
        out_shape=jax.ShapeDtypeStruct((t, hpg * LANE), BF16),
        compiler_params=_params(("arbitrary",)),
        name="merge_groups",
    )(*outs, *lses)


def _gated_proj_kernel(a_ref, b_ref, wa_ref, wb_ref, ga_ref, gb_ref, o_ref):
    n_slabs = ga_ref.shape[0]
    per_strip = min(n_slabs, MXU_WIDTH // LANE)
    for c0 in range(0, n_slabs, per_strip):
        c1 = min(c0 + per_strip, n_slabs)
        cols = slice(c0 * LANE, c1 * LANE)
        pa = jnp.dot(a_ref[...], wa_ref[:, cols], preferred_element_type=F32)
        pb = jnp.dot(b_ref[...], wb_ref[:, cols], preferred_element_type=F32)
        for c in range(c0, c1):
            gated = ga_ref[c] * pa[:, _lanes(c - c0)] + gb_ref[c] * pb[:, _lanes(c - c0)]
            o_ref[:, _lanes(c)] = gated.astype(o_ref.dtype)


def _gated_proj(o_a, o_b, w_a, w_b, z_sl, bn, gate_a_blk, gate_b_blk):
    t, d_a = o_a.shape
    d_b = o_b.shape[1]
    d = w_a.shape[1]
    bm = _tile(t, MATMUL_TILE, SUBLANE)
    return pl.pallas_call(
        _gated_proj_kernel,
        grid=(t // bm, d // bn),
        in_specs=[pl.BlockSpec((bm, d_a), lambda i, j: (i, 0)),
                  pl.BlockSpec((bm, d_b), lambda i, j: (i, 0)),
                  pl.BlockSpec((d_a, bn), lambda i, j: (0, j)),
                  pl.BlockSpec((d_b, bn), lambda i, j: (0, j)),
                  pl.BlockSpec((bn // LANE, bm, LANE), lambda i, j: (gate_a_blk + j, i, 0)),
                  pl.BlockSpec((bn // LANE, bm, LANE), lambda i, j: (gate_b_blk + j, i, 0))],
        out_specs=pl.BlockSpec((bm, bn), lambda i, j: (i, j)),
        out_shape=jax.ShapeDtypeStruct((t, d), BF16),
        compiler_params=_params(("arbitrary", "arbitrary")),
        name="gated_proj",
    )(o_a, o_b, w_a, w_b, z_sl, z_sl)


def _matmul_kernel(x_ref, w_ref, o_ref, *, nk):
    if nk == 1:
        o_ref[...] = jnp.dot(x_ref[...], w_ref[...], preferred_element_type=F32)
    else:
        @pl.when(pl.program_id(2) == 0)
        def _():
            o_ref[...] = jnp.zeros(o_ref.shape, o_ref.dtype)

        o_ref[...] += jnp.dot(x_ref[...], w_ref[...], preferred_element_type=F32)


def _matmul(x, w, name, k_tile=None):
    t, kd = x.shape
    n = w.shape[1]
    bm = _tile(t, MATMUL_TILE, SUBLANE)
    bn = _tile(n, MATMUL_TILE, LANE)
    bk = kd if k_tile is None else k_tile
    nk = kd // bk
    return pl.pallas_call(
        functools.partial(_matmul_kernel, nk=nk),
        grid=(t // bm, n // bn, nk),
        in_specs=[pl.BlockSpec((bm, bk), lambda i, j, k: (i, k)),
                  pl.BlockSpec((bk, bn), lambda i, j, k: (k, j))],
        out_specs=pl.BlockSpec((bm, bn), lambda i, j, k: (i, j)),
        out_shape=jax.ShapeDtypeStruct((t, n), F32),
        compiler_params=_params(("arbitrary", "arbitrary", "arbitrary")),
        name=name,
    )(x, w)


def _ffn_down_round_kernel(x_ref, w_in, o_ref, w_out, w_buf, acc_ref, *, nk, k_rows):
    bb, i = pl.program_id(0), pl.program_id(1)
    ni = pl.num_programs(1)
    ck, bn = w_in.shape
    bm = x_ref.shape[0]
    slot = (bb + 1) % 2

    def round_chunk():
        last = pl.num_programs(0) - 1
        k_c = jnp.minimum(bb, last - 1) % nk
        i_c = jnp.where(bb == last, ni - 1, i)
        row = (k_c * ni + i_c) * ck + lax.broadcasted_iota(jnp.int32, (ck, bn), 0)
        chunk = jnp.where(row < k_rows, w_in[...], 0.0).astype(BF16)
        w_buf[bb % 2, pl.ds(pl.multiple_of(i * ck, ck), ck), :] = chunk
        w_out[...] = chunk

    pl.when(bb == 0)(round_chunk)

    @pl.when(bb > 0)
    def _():
        round_chunk()
        k = (bb - 1) % nk
        row_split = 2 if bm % (4 * SUBLANE) == 0 else 1
        hm = bm // row_split
        strip = min(bn, MXU_WIDTH)
        for h in range(row_split):
            rows = pl.ds(pl.multiple_of(i * bm + h * hm, hm), hm)
            x = x_ref[h * hm:(h + 1) * hm, :]
            for c0 in range(0, bn, strip):
                cols = slice(c0, c0 + strip)
                prev = jnp.where(k == 0, 0.0, acc_ref[rows, cols])
                total = prev + jnp.dot(x, w_buf[slot, :, cols], preferred_element_type=F32)
                acc_ref[rows, cols] = total
                o_ref[h * hm:(h + 1) * hm, cols] = total


def _ffn_down_round(x, w, k_tile, name):
    t, kp = x.shape
    k_rows, n = w.shape
    bm = _tile(t, MATMUL_TILE, SUBLANE)
    bn = _tile(n, FFN_DOWN_TILE, LANE)
    ni, nj, nk = t // bm, n // bn, kp // k_tile
    assert k_tile % (2 * SUBLANE * ni) == 0 and kp - k_rows < k_tile // ni
    ck = k_tile // ni
    nb = nj * nk

    def mm(bb, i):
        b = jnp.maximum(bb - 1, 0)
        return jnp.where(bb == 0, 0, i), b % nk, b // nk

    def chunk(bb, i):
        b = jnp.minimum(bb, nb - 1)
        return (b % nk) * ni + jnp.where(bb == nb, ni - 1, i), b // nk

    def out_idx(bb, i):
        i_m, k_m, j_m = mm(bb, i)
        return jnp.where(k_m == nk - 1, i_m, 0), j_m

    return pl.pallas_call(
        functools.partial(_ffn_down_round_kernel, nk=nk, k_rows=k_rows),
        grid=(nb + 1, ni),
        in_specs=[pl.BlockSpec((bm, k_tile), lambda bb, i: mm(bb, i)[:2]),
                  pl.BlockSpec((ck, bn), chunk)],
        out_specs=[pl.BlockSpec((bm, bn), out_idx), pl.BlockSpec((ck, bn), chunk)],
        out_shape=[jax.ShapeDtypeStruct((t, n), F32), jax.ShapeDtypeStruct((kp, n), BF16)],
        scratch_shapes=[pltpu.VMEM((2, k_tile, bn), BF16), pltpu.VMEM((t, bn), F32)],
        compiler_params=_params(("arbitrary", "arbitrary")),
        name=name,
    )(x, w)


def _post_mix_kernel(x_ref, y_ref, g1_ref, g2_ref, x1_ref, h2_ref):
    y = y_ref[...]
    x1 = x_ref[...] + (y * _rms_scale(y)) * g1_ref[...]
    x1_ref[...] = x1
    h2_ref[...] = ((x1 * _rms_scale(x1)) * g2_ref[...]).astype(h2_ref.dtype)


def _post_mix(x, y, g_post, g_pre):
    t, d = x.shape
    bt = _tile(t, 256, SUBLANE)
    row = pl.BlockSpec((bt, d), lambda i: (i, 0))
    vec = pl.BlockSpec((1, d), lambda i: (0, 0))
    return pl.pallas_call(
        _post_mix_kernel,
        grid=(t // bt,),
        in_specs=[row, row, vec, vec],
        out_specs=[row, row],
        out_shape=[jax.ShapeDtypeStruct((t, d), F32), jax.ShapeDtypeStruct((t, d), BF16)],
        compiler_params=_params(("arbitrary",)),
        name="post_mix",
    )(x, y, g_post.reshape(1, d), g_pre.reshape(1, d))


def _residual_norm_kernel(x_ref, y_ref, g_ref, o_ref):
    y = y_ref[...]
    o_ref[...] = x_ref[...] + (y * _rms_scale(y)) * g_ref[...]


def _residual_norm(x, y, g):
    t, d = x.shape
    bt = _tile(t, 256, SUBLANE)
    row = pl.BlockSpec((bt, d), lambda i: (i, 0))
    return pl.pallas_call(
        _residual_norm_kernel,
        grid=(t // bt,),
        in_specs=[row, row, pl.BlockSpec((1, d), lambda i: (0, 0))],
        out_specs=row,
        out_shape=jax.ShapeDtypeStruct((t, d), F32),
        compiler_params=_params(("arbitrary",)),
        name="residual_norm",
    )(x, y, g.reshape(1, d))


def _ffn_up_kernel(*refs, conv_w, s_len, d_ff):
    if s_len is None:
        (x_ref, wg_in, wu_in, cw_ref, cb_ref, out_ref, tail_ref, wg_out, wu_out,
         wg_buf, wu_buf, a_buf, u_buf, o_buf) = refs
        hist_refs = None
        jj, i = pl.program_id(0), pl.program_id(1)
        ck = wg_in.shape[0]

        def round_chunks():
            rows_c = pl.ds(pl.multiple_of(i * ck, ck), ck)
            for src, buf, dst in ((wg_in, wg_buf, wg_out), (wu_in, wu_buf, wu_out)):
                chunk = src[...].astype(BF16)
                buf[jj % 2, rows_c, :] = chunk
                dst[...] = chunk

        slot = (jj + 1) % 2
        wg = lambda cols: wg_buf[slot, :, cols]
        wu = lambda cols: wu_buf[slot, :, cols]
        col_blk = jj - 1
        row_split = 2 if out_ref.shape[0] % (4 * SUBLANE) == 0 else 1
    else:
        x_ref, wg_ref, wu_ref, cw_ref, cb_ref, h1_ref, h2_ref, out_ref, tail_ref, a_buf, u_buf, o_buf = refs
        hist_refs = {1: h1_ref, 2: h2_ref}
        i = pl.program_id(1)
        wg = lambda cols: wg_ref[:, cols]
        wu = lambda cols: wu_ref[:, cols]
        col_blk = pl.program_id(0)
        row_split = 1
    bm, bn = out_ref.shape
    hm = bm // row_split
    half = hm // 2
    top = SUBLANE
    n_slabs = bn // LANE
    per_strip = min(n_slabs, MXU_WIDTH // LANE)

    def units():
        @pl.when(i == 0)
        def _():
            a_buf[:, 0:top, :] = jnp.zeros((n_slabs, top, LANE), F32)

        if s_len is None:
            round_chunks()

        lane = lax.broadcasted_iota(jnp.int32, (1, LANE), 1)
        for h in range(row_split):
            r0 = h * hm
            x = x_ref[r0:r0 + hm, :]
            for c0 in range(0, n_slabs, per_strip):
                c1 = min(c0 + per_strip, n_slabs)
                cols = slice(c0 * LANE, c1 * LANE)
                a = jnp.dot(x, wg(cols), preferred_element_type=F32)
                u = jnp.dot(x, wu(cols), preferred_element_type=F32)
                if s_len is not None:
                    tail_ref[r0:r0 + hm, cols] = a
                elif h == row_split - 1:
                    tail_ref[:, cols] = a[hm - SUBLANE:, :]
                for c in range(c0, c1):
                    a_buf[c, top + r0:top + r0 + hm, :] = a[:, _lanes(c - c0)]
                    u_buf[c, r0:r0 + hm, :] = u[:, _lanes(c - c0)]
                    in_range = (col_blk * bn + c * LANE + lane) < d_ff
                    for e in range(2):
                        taps = [a_buf[c, pl.ds(top + r0 + e - lag, half, stride=2), :] for lag in range(conv_w)]
                        if hist_refs is not None:
                            s = 2 * (lax.broadcasted_iota(jnp.int32, (half, LANE), 0) % (s_len // 2)) + e
                            for lag in range(1, conv_w):
                                hist = hist_refs[lag][c, pl.ds(r0 + e, half, stride=2), :]
                                taps[lag] = jnp.where(s < lag, hist, taps[lag])
                        acc = cw_ref[0:1, _lanes(c)] * taps[conv_w - 1]
                        for k in range(1, conv_w):
                            acc = acc + cw_ref[k:k + 1, _lanes(c)] * taps[conv_w - 1 - k]
                        acc = cb_ref[:, _lanes(c)] + acc
                        val = _gelu(acc) * u_buf[c, pl.ds(r0 + e, half, stride=2), :]
                        o_buf[c, pl.ds(r0 + e, half, stride=2), :] = jnp.where(in_range, val, 0.0)
                    out_ref[r0:r0 + hm, _lanes(c)] = o_buf[c, r0:r0 + hm, :].astype(out_ref.dtype)
        if s_len is None:
            a_buf[:, 0:top, :] = a_buf[:, bm:bm + top, :]

    if s_len is None:
        pl.when(jj == 0)(round_chunks)
        pl.when(jj > 0)(units)
    else:
        units()


def _ffn_up_prompt(h, w_gate, w_up, conv_w, conv_b, ffp):
    t, d = h.shape
    d_ff = w_gate.shape[1]
    cw = conv_w.shape[0]
    assert cw == 3 and d_ff % LANE == 0
    bn = FFN_TILE
    assert ffp % bn == 0 and ffp - d_ff < bn
    bm = _tile(t, MATMUL_TILE, 4 * SUBLANE)
    ni, nj = t // bm, ffp // bn
    assert d % ni == 0
    ck = d // ni
    n_slabs = bn // LANE

    def fin_i(jj, i):
        return jnp.where(jj == 0, 0, i)

    def fin_j(jj):
        return jnp.maximum(jj - 1, 0)

    def chunk(jj, i):
        return jnp.where(jj == nj, ni - 1, i), jnp.minimum(jj, nj - 1)

    w_in_spec = pl.BlockSpec((ck, bn), chunk)
    wbuf = pltpu.VMEM((2, d, bn), BF16)
    act, tail, wg16, wu16 = pl.pallas_call(
        functools.partial(_ffn_up_kernel, conv_w=cw, s_len=None, d_ff=d_ff),
        grid=(nj + 1, ni),
        in_specs=[pl.BlockSpec((bm, d), lambda jj, i: (fin_i(jj, i), 0)), w_in_spec, w_in_spec,
                  pl.BlockSpec((cw, bn), lambda jj, i: (0, fin_j(jj))),
                  pl.BlockSpec((1, bn), lambda jj, i: (0, fin_j(jj)))],
        out_specs=[pl.BlockSpec((bm, bn), lambda jj, i: (fin_i(jj, i), fin_j(jj))),
                   pl.BlockSpec((SUBLANE, bn), lambda jj, i: (0, fin_j(jj))),
                   w_in_spec, w_in_spec],
        out_shape=[jax.ShapeDtypeStruct((t, ffp), BF16), jax.ShapeDtypeStruct((SUBLANE, d_ff), F32),
                   jax.ShapeDtypeStruct((d, d_ff), BF16), jax.ShapeDtypeStruct((d, d_ff), BF16)],
        scratch_shapes=[wbuf, wbuf,
                        pltpu.VMEM((n_slabs, bm + SUBLANE, LANE), F32), pltpu.VMEM((n_slabs, bm, LANE), F32),
                        pltpu.VMEM((n_slabs, bm, LANE), F32)],
        compiler_params=_params(("arbitrary", "arbitrary")),
        name="ffn_up",
    )(h, w_gate, w_up, conv_w, conv_b.reshape(1, d_ff))
    return act, tail, wg16, wu16


def _ffn_up_sample(h, w_gate, w_up, conv_w, conv_b, ffp, hist, s_len):
    t, d = h.shape
    d_ff = w_gate.shape[1]
    cw = conv_w.shape[0]
    assert cw == 3 and d_ff % LANE == 0 and s_len % 2 == 0
    bn = _tile(ffp, MATMUL_TILE, LANE)
    assert ffp - d_ff < bn
    n_slabs = bn // LANE
    w_spec = pl.BlockSpec((d, bn), lambda j, i: (0, j))
    out_spec = pl.BlockSpec((t, bn), lambda j, i: (i, j))

    def slabs(rows):
        return jnp.transpose(rows.reshape(t, d_ff // LANE, LANE), (1, 0, 2))
    h1 = slabs(jnp.pad(hist[:, 1:2], ((0, 0), (0, s_len - 1), (0, 0))))
    h2 = slabs(jnp.pad(hist, ((0, 0), (0, s_len - hist.shape[1]), (0, 0))))
    hist_spec = pl.BlockSpec((n_slabs, t, LANE), lambda j, i: (j, i, 0))
    return pl.pallas_call(
        functools.partial(_ffn_up_kernel, conv_w=cw, s_len=s_len, d_ff=d_ff),
        grid=(ffp // bn, 1),
        in_specs=[pl.BlockSpec((t, d), lambda j, i: (i, 0)), w_spec, w_spec,
                  pl.BlockSpec((cw, bn), lambda j, i: (0, j)), pl.BlockSpec((1, bn), lambda j, i: (0, j)),
                  hist_spec, hist_spec],
        out_specs=[out_spec, out_spec],
        out_shape=[jax.ShapeDtypeStruct((t, ffp), BF16), jax.ShapeDtypeStruct((t, d_ff), F32)],
        scratch_shapes=[pltpu.VMEM((n_slabs, t + SUBLANE, LANE), F32), pltpu.VMEM((n_slabs, t, LANE), F32),
                        pltpu.VMEM((n_slabs, t, LANE), F32)],
        compiler_params=_params(("arbitrary", "arbitrary")),
        name="ffn_up_decode",
    )(h, w_gate, w_up, conv_w, conv_b.reshape(1, d_ff), h1, h2)


def _layer(x, p, s_len=None, caches=None, conv_hist=None):
    t, d = x.shape
    d_a = p["ln_g"].shape[0]
    hpg = p["hpg"]
    gw = hpg * HEAD_DIM
    bn = p["bn"]
    o1 = 2 * d_a
    d_qkv = 3 * gw

    h = _rmsnorm_cast(x, p["g_pre_mix"])
    w16 = dict(p.get("w16", {}))
    z_parts = []
    for name, col0, n_cols, act in (("in_proj_sgu", 0, o1, _gelu),
                                    ("in_proj_qkv", o1, 3 * d_qkv, lambda v: v),
                                    ("in_proj_gates", o1 + 3 * d_qkv, 2 * d, _sigmoid)):
        if s_len is None:
            z_part, w16[name] = _in_proj_round(h, p["w_in"], col0, n_cols, act, name)
        else:
            z_part = _in_proj(h, w16[name], bn, 0, n_cols, act, name + "_decode")
        z_parts.append(z_part)
    za_sl, qkv_sl, gate_sl = z_parts

    if s_len is None:
        w_mix, b_mix = p["w_mix_prompt"], p["b_mix_prompt"]
    else:
        w_mix, b_mix = p["w_mix_sample"], p["b_mix_sample"]
    o_a, v_state = _sgu(za_sl, d_a, p["ln_g"], p["ln_b"], w_mix, b_mix)

    outs, lses = [], []
    for gi, (_, dil) in enumerate(DILATION_GROUPS):
        q_slab = gi * hpg
        k_slab = d_qkv // LANE + gi * hpg
        v_slab = 2 * d_qkv // LANE + gi * hpg
        bias_tab = p["rel_bias"][:, gi * hpg:(gi + 1) * hpg]
        if s_len is not None:
            o, lse = _attn_sample(qkv_sl, s_len, hpg, q_slab, k_slab, v_slab, caches[2 * gi], caches[2 * gi + 1],
                                  bias_tab, dil)
        elif gi < len(DILATION_GROUPS) - 1:
            o, lse = _attn_prompt(qkv_sl, hpg, q_slab, k_slab, v_slab, bias_tab, dil)
        else:
            o_b = _attn_prompt(qkv_sl, hpg, q_slab, k_slab, v_slab, bias_tab, dil, merge_with=(outs, lses))
            break
        outs.append(o)
        lses.append(lse)
    if s_len is not None:
        o_b = _merge(outs, lses)

    merged = _gated_proj(o_a, o_b, p["w_proj_a"], p["w_proj_b"], gate_sl, bn, 0, d // bn)
    y = _matmul(merged, p["w_out"], "out_proj")
    x1, h2 = _post_mix(x, y, p["g_post_mix"], p["g_pre_ffn"])

    if s_len is None:
        act, a_tail, w16["ffn_gate"], w16["ffn_up"] = _ffn_up_prompt(h2, p["w_gate"], p["w_up"], p["conv_w"],
                                                                     p["conv_b"], p["ffp"])
    else:
        act, a_tail = _ffn_up_sample(h2, w16["ffn_gate"], w16["ffn_up"], p["conv_w"], p["conv_b"], p["ffp"],
                                     conv_hist, s_len)
    if s_len is None:
        f, w16["ffn_down"] = _ffn_down_round(act, p["w_down"], p["ffn_k_tile"], "ffn_down")
    else:
        f = _matmul(act, w16["ffn_down"], "ffn_down_decode", k_tile=p["ffn_k_tile"])
    y_out = _residual_norm(x1, f, p["g_post_ffn"])
    return y_out, qkv_sl, v_state, a_tail, w16


def kernel(x_prompt, x_sample, cache_k_g1, cache_v_g1, cache_k_g2, cache_v_g2, cache_k_g3, cache_v_g3, state_conv, g_pre_mix, w_in, sgu_ln_g, sgu_ln_b, w_spatial, b_spatial, rel_bias, w_proj_a, w_proj_b, w_out, g_post_mix, g_pre_ffn, w_gate, w_up, conv_w, conv_b, w_down, g_post_ffn):
    depth = w_in.shape[0]
    assert depth == 1
    n_prompt, seq, d = x_prompt.shape
    assert n_prompt == 1 and seq % CHUNK == 0
    n_seq, s_len, _ = x_sample.shape
    assert s_len == SUBLANE
    d_a = sgu_ln_g.shape[1]
    n_groups = w_spatial.shape[1]
    n_heads = rel_bias.shape[1]
    hpg = n_heads // len(DILATION_GROUPS)
    gw = hpg * HEAD_DIM
    d_ff = w_gate.shape[2]
    cw = conv_w.shape[1]
    bn = _tile(math.gcd(2 * d_a, gw), MATMUL_TILE, LANE)
    ffp = -(-d_ff // FFN_TILE) * FFN_TILE
    t_s = n_seq * s_len

    tri = np.tril(np.ones((CHUNK, CHUNK), np.float32))
    w_mix_prompt = (w_spatial[0] * tri).astype(BF16)
    b_mix_prompt = jnp.broadcast_to(b_spatial[0][:, :, None], (n_groups, CHUNK, LANE))
    w_small = w_spatial[0][:, :s_len, :s_len] * tri[:s_len, :s_len]
    eye = np.eye(n_seq, dtype=np.float32)
    w_mix_sample = jnp.einsum("ab,gpq->gapbq", eye, w_small).reshape(n_groups, t_s, t_s).astype(BF16)
    b_mix_sample = jnp.broadcast_to(jnp.tile(b_spatial[0][:, :s_len], (1, n_seq))[:, :, None], (n_groups, t_s, LANE))

    p = dict(
        hpg=hpg, bn=bn, ffp=ffp,
        g_pre_mix=g_pre_mix[0], w_in=w_in[0], ln_g=sgu_ln_g[0], ln_b=sgu_ln_b[0],
        w_mix_prompt=w_mix_prompt, b_mix_prompt=b_mix_prompt,
        w_mix_sample=w_mix_sample, b_mix_sample=b_mix_sample,
        rel_bias=rel_bias,
        w_proj_a=w_proj_a[0].astype(BF16), w_proj_b=w_proj_b[0].astype(BF16), w_out=w_out[0].astype(BF16),
        g_post_mix=g_post_mix[0], g_pre_ffn=g_pre_ffn[0],
        w_gate=w_gate[0], w_up=w_up[0],
        conv_w=conv_w[0], conv_b=conv_b[0],
        w_down=w_down[0],
        ffn_k_tile=_tile(ffp, 3072, LANE),
        g_post_ffn=g_post_ffn[0],
    )

    caches = tuple(c[0] for c in (cache_k_g1, cache_v_g1, cache_k_g2, cache_v_g2, cache_k_g3, cache_v_g3))

    yp, zp, vp_state, ap_tail, w16 = _layer(x_prompt[0], p)
    ys, zs, vs_state, as_all, _ = _layer(x_sample.reshape(t_s, d), dict(p, w16=w16), s_len=s_len,
                                         caches=caches, conv_hist=state_conv[0])

    d_qkv = 3 * gw
    prompt_kv, sample_kv = [], []
    for gi, (win, _) in enumerate(DILATION_GROUPS):
        keep = min(win, seq)
        for base in (d_qkv, 2 * d_qkv):
            s0 = (base + gi * gw) // LANE
            pk = jnp.transpose(zp[s0:s0 + hpg, seq - keep:, :], (1, 0, 2))
            prompt_kv.append(pk.reshape(1, 1, keep, hpg, HEAD_DIM))
            sk = jnp.transpose(zs[s0:s0 + hpg], (1, 0, 2))
            sample_kv.append(sk.reshape(1, n_seq, s_len, hpg, HEAD_DIM))
    p_conv = ap_tail[SUBLANE - (cw - 1):].reshape(1, 1, cw - 1, d_ff)
    s_conv = as_all.reshape(n_seq, s_len, d_ff)[:, s_len - (cw - 1):].reshape(1, n_seq, cw - 1, d_ff)
    return (yp.reshape(1, seq, d), ys.reshape(n_seq, s_len, d),
            *prompt_kv, vp_state.reshape(1, 1, CHUNK, d_a), p_conv,
            *sample_kv, vs_state.reshape(1, n_seq, s_len, d_a), s_conv)
```

```python
import functools
import math

import numpy as np
import jax
import jax.numpy as jnp
from jax import lax
from jax.experimental import pallas as pl
from jax.experimental.pallas import tpu as pltpu

F32 = jnp.float32
BF16 = jnp.bfloat16

HEAD_DIM = 128
STEPS = 128
CHUNK = 128
GROUP_DIM_A = 128
DILATION_GROUPS = ((128, 1), (512, 4), (2048, 16))
MAX_EXACT = 16
MAX_DISTANCE = 2048
EPS = 1e-6
NEG = -1e30
ATTN_SCALE = HEAD_DIM ** -0.5

LANE = 128
SUBLANE = 8
VMEM_LIMIT_BYTES = 56 * 1024 * 1024
MATMUL_TILE = 1024
MXU_WIDTH = 256
FFN_TILE = 512
FFN_DOWN_TILE = 512
IN_PROJ_ROUND_TILE = 1024
ATTN_UNITS_PER_STEP = 16


def _tile(dim, target, align):
    best = None
    t = align
    while t <= min(dim, target):
        if dim % t == 0:
            best = t
        t += align
    return best if best is not None else dim


def _params(semantics):
    return pltpu.CompilerParams(dimension_semantics=semantics, vmem_limit_bytes=VMEM_LIMIT_BYTES)


def _gelu(x):
    return 0.5 * x * (1.0 + jnp.tanh(math.sqrt(2.0 / math.pi) * (x + 0.044715 * (x * x * x))))


def _sigmoid(x):
    return 1.0 / (1.0 + jnp.exp(-x))


def _rms_scale(x):
    return lax.rsqrt(jnp.mean(x * x, axis=-1, keepdims=True) + EPS)


def _lanes(c):
    return slice(c * LANE, (c + 1) * LANE)


def _rmsnorm_cast_kernel(x_ref, g_ref, o_ref):
    x = x_ref[...]
    o_ref[...] = ((x * _rms_scale(x)) * g_ref[...]).astype(o_ref.dtype)


def _rmsnorm_cast(x, g):
    t, d = x.shape
    bt = _tile(t, 256, SUBLANE)
    return pl.pallas_call(
        _rmsnorm_cast_kernel,
        grid=(t // bt,),
        in_specs=[pl.BlockSpec((bt, d), lambda i: (i, 0)), pl.BlockSpec((1, d), lambda i: (0, 0))],
        out_specs=pl.BlockSpec((bt, d), lambda i: (i, 0)),
        out_shape=jax.ShapeDtypeStruct((t, d), BF16),
        compiler_params=_params(("arbitrary",)),
        name="rmsnorm_cast",
    )(x, g.reshape(1, d))


def _in_proj_kernel(x_ref, w_ref, o_ref, *, act):
    acc = jnp.dot(x_ref[...], w_ref[...], preferred_element_type=F32)
    for c in range(o_ref.shape[0]):
        o_ref[c] = act(acc[:, _lanes(c)])


def _in_proj_round_kernel(x_ref, w_in, o_ref, w_out, w_buf, *, act, slab_out):
    jj, i = pl.program_id(0), pl.program_id(1)
    ck = w_in.shape[0]
    slot = (jj + 1) % 2
    bm = o_ref.shape[-2]
    n_slabs = w_in.shape[1] // LANE
    row_split = 2 if bm % (4 * SUBLANE) == 0 else 1
    hm = bm // row_split
    per_strip = min(n_slabs, MXU_WIDTH // LANE)

    def round_chunk():
        chunk = w_in[...].astype(BF16)
        w_buf[jj % 2, pl.ds(pl.multiple_of(i * ck, ck), ck), :] = chunk
        w_out[...] = chunk

    pl.when(jj == 0)(round_chunk)

    @pl.when(jj > 0)
    def _():
        round_chunk()
        for h in range(row_split):
            x = x_ref[h * hm:(h + 1) * hm, :]
            for c0 in range(0, n_slabs, per_strip):
                c1 = min(c0 + per_strip, n_slabs)
                acc = jnp.dot(x, w_buf[slot, :, c0 * LANE:c1 * LANE], preferred_element_type=F32)
                for c in range(c0, c1):
                    val = act(acc[:, _lanes(c - c0)])
                    if slab_out:
                        o_ref[c, h * hm:(h + 1) * hm, :] = val
                    else:
                        o_ref[h * hm:(h + 1) * hm, _lanes(c)] = val


def _in_proj_round(h, w, col0, n_cols, act, name, slab_out=True):
    t, d = h.shape
    bn = _tile(math.gcd(n_cols, col0) if col0 else n_cols, IN_PROJ_ROUND_TILE, LANE)
    bm = _tile(t, MATMUL_TILE, SUBLANE)
    ni, nj = t // bm, n_cols // bn
    assert d % ni == 0
    ck = d // ni

    def fin(jj, i):
        return jnp.maximum(jj - 1, 0), jnp.where(jj == 0, 0, i)

    def chunk(jj, i):
        return jnp.where(jj == nj, ni - 1, i), jnp.minimum(jj, nj - 1)

    if slab_out:
        out_spec = pl.BlockSpec((bn // LANE, bm, LANE), lambda jj, i: fin(jj, i) + (0,))
        out_shape = jax.ShapeDtypeStruct((n_cols // LANE, t, LANE), F32)
    else:
        out_spec = pl.BlockSpec((bm, bn), lambda jj, i: fin(jj, i)[::-1])
        out_shape = jax.ShapeDtypeStruct((t, n_cols), F32)
    return pl.pallas_call(
        functools.partial(_in_proj_round_kernel, act=act, slab_out=slab_out),
        grid=(nj + 1, ni),
        in_specs=[pl.BlockSpec((bm, d), lambda jj, i: (fin(jj, i)[1], 0)),
                  pl.BlockSpec((ck, bn), lambda jj, i: (chunk(jj, i)[0], col0 // bn + chunk(jj, i)[1]))],
        out_specs=[out_spec, pl.BlockSpec((ck, bn), chunk)],
        out_shape=[out_shape, jax.ShapeDtypeStruct((d, n_cols), BF16)],
        scratch_shapes=[pltpu.VMEM((2, d, bn), BF16)],
        compiler_params=_params(("arbitrary", "arbitrary")),
        name=name,
    )(h, w)


def _in_proj(h, w, bn, col0, n_cols, act, name):
    t, d = h.shape
    bm = _tile(t, MATMUL_TILE, SUBLANE)
    return pl.pallas_call(
        functools.partial(_in_proj_kernel, act=act),
        grid=(t // bm, n_cols // bn),
        in_specs=[pl.BlockSpec((bm, d), lambda i, j: (i, 0)),
                  pl.BlockSpec((d, bn), lambda i, j: (0, col0 // bn + j))],
        out_specs=pl.BlockSpec((bn // LANE, bm, LANE), lambda i, j: (j, i, 0)),
        out_shape=jax.ShapeDtypeStruct((n_cols // LANE, t, LANE), F32),
        compiler_params=_params(("arbitrary", "arbitrary")),
        name=name,
    )(h, w)


def _sgu_kernel(u_ref, v_ref, lg_ref, lb_ref, w_ref, b_ref, o_ref, vs_ref, *, n_groups):
    vp = v_ref[...]
    n_feat = n_groups * GROUP_DIM_A
    mu = jnp.sum(jnp.sum(vp, axis=0), axis=-1, keepdims=True) / n_feat
    vc = vp - mu
    var = jnp.sum(jnp.sum(vc * vc, axis=0), axis=-1, keepdims=True) / n_feat
    v = vc * lax.rsqrt(var + EPS) * lg_ref[...] + lb_ref[...]
    for g in range(n_groups):
        vs_ref[:, _lanes(g)] = v[g]
        mixed = jnp.dot(w_ref[g], v[g].astype(BF16), preferred_element_type=F32) + b_ref[g]
        o_ref[:, _lanes(g)] = (u_ref[g] * mixed).astype(o_ref.dtype)


def _sgu(z_sl, d_a, ln_g, ln_b, w_mix, b_mix):
    t = z_sl.shape[1]
    n_groups, c, _ = w_mix.shape
    slab = pl.BlockSpec((n_groups, 1, LANE), lambda i: (0, 0, 0))
    return pl.pallas_call(
        functools.partial(_sgu_kernel, n_groups=n_groups),
        grid=(t // c,),
        in_specs=[
            pl.BlockSpec((n_groups, c, LANE), lambda i: (0, i, 0)),
            pl.BlockSpec((n_groups, c, LANE), lambda i: (1, i, 0)),
            slab, slab,
            pl.BlockSpec((n_groups, c, c), lambda i: (0, 0, 0)),
            pl.BlockSpec((n_groups, c, LANE), lambda i: (0, 0, 0)),
        ],
        out_specs=[pl.BlockSpec((c, d_a), lambda i: (i, 0)), pl.BlockSpec((c, d_a), lambda i: (0, 0))],
        out_shape=[jax.ShapeDtypeStruct((t, d_a), BF16), jax.ShapeDtypeStruct((c, d_a), F32)],
        compiler_params=_params(("arbitrary",)),
        name="sgu",
    )(z_sl, z_sl, ln_g.reshape(n_groups, 1, LANE), ln_b.reshape(n_groups, 1, LANE), w_mix, b_mix)


def _t5_bucket(dist, n_buckets):
    n = np.asarray(dist, np.int32)
    safe = np.maximum(n, 1).astype(np.float32)
    large = MAX_EXACT + (np.log(safe / MAX_EXACT) / np.log(np.float32(MAX_DISTANCE / MAX_EXACT))
                         * (n_buckets - MAX_EXACT)).astype(np.int32)
    large = np.minimum(large, n_buckets - 1)
    return np.where(n < MAX_EXACT, n, large).astype(np.int32)


def _bias_lookup(bias_tab, bucket):
    n_buckets = bias_tab.shape[0]
    flat = np.asarray(bucket).reshape(-1)
    onehot = (jnp.asarray(flat)[None, :] == jnp.arange(n_buckets)[:, None]).astype(F32)
    out = jnp.dot(bias_tab.T, onehot, precision=lax.Precision.HIGHEST)
    return out.reshape((bias_tab.shape[1],) + tuple(np.asarray(bucket).shape))


def _merge_by_lse(outs, lses):
    m = functools.reduce(jnp.maximum, lses)
    w = [jnp.exp(l - m) for l in lses]
    num = functools.reduce(lambda x, y: x + y, [wi * oi for wi, oi in zip(w, outs)])
    return num / functools.reduce(lambda x, y: x + y, w)


def _attn_prompt_kernel(*refs, hb, dil, n_other):
    q_ref, kc_ref, vc_ref, bias_ref, band_ref = refs[:5]
    others = refs[5:5 + 2 * n_other]
    n_out = 1 if n_other else 2
    outs = refs[5 + 2 * n_other:5 + 2 * n_other + n_out]
    kp_ref, vp_ref = refs[5 + 2 * n_other + n_out:][:2]
    o_acc, lse_acc = refs[5 + 2 * n_other + n_out + 2:] if n_other else outs
    b = pl.program_id(1)

    @pl.when(b == 0)
    def _():
        kp_ref[...] = jnp.zeros(kp_ref.shape, F32)
        vp_ref[...] = jnp.zeros(vp_ref.shape, F32)

    col = lax.broadcasted_iota(jnp.int32, (STEPS, 2 * STEPS), 1)
    valid = (band_ref[...] > 0.5) & ((b > 0) | (col >= STEPS))
    for hh in range(hb):
        for r in range(dil):
            rows = pl.ds(r, STEPS, stride=dil) if dil > 1 else pl.ds(0, STEPS)
            q = q_ref[hh, rows, :].astype(BF16)
            kk = jnp.concatenate([kp_ref[hh, rows, :], kc_ref[hh, rows, :]], axis=0).astype(BF16)
            vv = jnp.concatenate([vp_ref[hh, rows, :], vc_ref[hh, rows, :]], axis=0).astype(BF16)
            s = lax.dot_general(q, kk, (((1,), (1,)), ((), ())), preferred_element_type=F32) * ATTN_SCALE
            s = jnp.where(valid, s + bias_ref[hh], NEG)
            m = jnp.max(s, axis=-1, keepdims=True)
            e = jnp.exp(s - m)
            den = jnp.sum(e, axis=-1, keepdims=True)
            o_acc[hh, rows, :] = jnp.dot(e.astype(BF16), vv, preferred_element_type=F32) / den
            lse_acc[hh, rows, :] = jnp.broadcast_to(m + jnp.log(den), (STEPS, HEAD_DIM))
    kp_ref[...] = kc_ref[...]
    vp_ref[...] = vc_ref[...]
    if n_other:
        for hh in range(hb):
            merged = _merge_by_lse([r[hh] for r in others[:n_other]] + [o_acc[hh]],
                                   [r[hh] for r in others[n_other:]] + [lse_acc[hh]])
            outs[0][:, _lanes(hh)] = merged.astype(outs[0].dtype)


def _attn_prompt(z_sl, hpg, q_slab, k_slab, v_slab, bias_tab, dil, merge_with=None):
    t = z_sl.shape[1]
    rows = STEPS * dil
    assert t % rows == 0
    nb = t // rows
    hb = min(hpg, max(1, ATTN_UNITS_PER_STEP // dil))
    assert hpg % hb == 0 and q_slab % hb == 0 and k_slab % hb == 0 and v_slab % hb == 0

    p_idx = np.arange(STEPS)[:, None]
    c_idx = np.arange(2 * STEPS)[None, :]
    steps = p_idx + STEPS - c_idx
    band = ((steps >= 0) & (steps <= STEPS)).astype(np.float32)
    bias = _bias_lookup(bias_tab, _t5_bucket(np.clip(steps, 0, STEPS) * dil, bias_tab.shape[0]))

    def cur(slab):
        return pl.BlockSpec((hb, rows, LANE), lambda hi, b: (slab // hb + hi, b, 0))

    head_major = pl.BlockSpec((hb, rows, LANE), lambda hi, b: (hi, b, 0))
    block = pltpu.VMEM((hb, rows, LANE), F32)
    others = [] if merge_with is None else list(merge_with[0]) + list(merge_with[1])
    if merge_with is None:
        out_specs = [head_major, head_major]
        out_shape = [jax.ShapeDtypeStruct((hpg, t, LANE), F32)] * 2
        scratch = [block, block]
    else:
        out_specs = [pl.BlockSpec((rows, hb * LANE), lambda hi, b: (b, hi))]
        out_shape = [jax.ShapeDtypeStruct((t, hpg * LANE), BF16)]
        scratch = [block, block, block, block]
    res = pl.pallas_call(
        functools.partial(_attn_prompt_kernel, hb=hb, dil=dil, n_other=len(others) // 2),
        grid=(hpg // hb, nb),
        in_specs=[cur(q_slab), cur(k_slab), cur(v_slab),
                  pl.BlockSpec((hb, STEPS, 2 * STEPS), lambda hi, b: (hi, 0, 0)),
                  pl.BlockSpec((STEPS, 2 * STEPS), lambda hi, b: (0, 0))] + [head_major] * len(others),
        out_specs=out_specs,
        out_shape=out_shape,
        scratch_shapes=scratch,
        compiler_params=_params(("arbitrary", "arbitrary")),
        name=f"attn_prompt_d{dil}",
    )(z_sl, z_sl, z_sl, bias, jnp.asarray(band), *others)
    return res[0] if merge_with is not None else res


def _attn_sample_kernel(q_ref, kn_ref, vn_ref, ck_ref, cv_ref, bc_ref, mc_ref, bnew_ref, mnew_ref,
                        o_ref, lse_ref, *, hpg, s_len, grouped):
    n_keys = mc_ref.shape[1]
    n_new = hpg * s_len
    valid_c = mc_ref[...] > 0.5

    def cache_head(ref, h):
        if grouped:
            return ref[0, :, pl.ds(h, s_len, stride=hpg), :].reshape(n_keys, HEAD_DIM).astype(BF16)
        return ref[0, pl.ds(h, n_keys, stride=hpg) if hpg > 1 else pl.ds(0, n_keys), :].astype(BF16)

    q_all = q_ref[...].reshape(n_new, HEAD_DIM).astype(BF16)
    s_new = lax.dot_general(q_all, kn_ref[...].reshape(n_new, HEAD_DIM).astype(BF16), (((1,), (1,)), ((), ())),
                            preferred_element_type=F32) * ATTN_SCALE
    key_head = lax.broadcasted_iota(jnp.int32, (s_len, n_new), 1) // s_len
    new_ok = mnew_ref[...] > 0.5

    partial, e_new = [], []
    for h in range(hpg):
        q = q_all[h * s_len:(h + 1) * s_len, :]
        sc = lax.dot_general(q, cache_head(ck_ref, h), (((1,), (1,)), ((), ())),
                             preferred_element_type=F32) * ATTN_SCALE
        sc = jnp.where(valid_c, sc + bc_ref[h], NEG)
        sn = jnp.where(new_ok & (key_head == h), s_new[h * s_len:(h + 1) * s_len, :] + bnew_ref[h], NEG)
        m = jnp.maximum(jnp.max(sc, axis=-1, keepdims=True), jnp.max(sn, axis=-1, keepdims=True))
        ec = jnp.exp(sc - m)
        en = jnp.exp(sn - m)
        den = jnp.sum(ec, axis=-1, keepdims=True) + jnp.sum(en, axis=-1, keepdims=True)
        acc = jnp.dot(ec.astype(BF16), cache_head(cv_ref, h), preferred_element_type=F32)
        partial.append((acc, den, m))
        e_new.append(en)
    o_new = jnp.dot(jnp.concatenate(e_new, axis=0).astype(BF16),
                    vn_ref[...].reshape(n_new, HEAD_DIM).astype(BF16), preferred_element_type=F32)
    for h, (acc, den, m) in enumerate(partial):
        o_ref[h] = (acc + o_new[h * s_len:(h + 1) * s_len, :]) / den
        lse_ref[h] = jnp.broadcast_to(m + jnp.log(den), (s_len, HEAD_DIM))


def _attn_sample(z_sl, s_len, hpg, q_slab, k_slab, v_slab, cache_k, cache_v, bias_tab, dil):
    t = z_sl.shape[1]
    n_seq = t // s_len
    lc = cache_k.shape[1]
    assert q_slab % hpg == 0 and k_slab % hpg == 0 and v_slab % hpg == 0

    j = np.arange(STEPS + 1)
    idx = lc + np.arange(s_len)[:, None] - j[None, :] * dil
    assert idx.min() >= 0
    bucket = _t5_bucket(j * dil, bias_tab.shape[0])
    mask = np.zeros((s_len, lc + s_len), np.float32)
    bsel = np.zeros((s_len, lc + s_len), np.int32)
    for s in range(s_len):
        mask[s, idx[s]] = 1.0
        bsel[s, idx[s]] = bucket

    grouped = dil > s_len and lc % dil == 0 and s_len == SUBLANE
    if grouped:
        pos = (np.arange(lc // dil)[:, None] * dil + np.arange(s_len)[None, :]).reshape(-1)
        assert mask[:, :lc].sum() == mask[:, pos].sum()
        ck = cache_k.reshape(n_seq, lc // dil, dil * hpg, HEAD_DIM)
        cv = cache_v.reshape(n_seq, lc // dil, dil * hpg, HEAD_DIM)
        cache_spec = pl.BlockSpec((1, lc // dil, s_len * hpg, LANE), lambda b: (b, 0, 0, 0))
    else:
        pos = np.arange(lc)
        ck = cache_k.reshape(n_seq, lc * hpg, HEAD_DIM)
        cv = cache_v.reshape(n_seq, lc * hpg, HEAD_DIM)
        cache_spec = pl.BlockSpec((1, lc * hpg, LANE), lambda b: (b, 0, 0))
    n_keys = len(pos)
    bias_c = _bias_lookup(bias_tab, bsel[:, pos])
    bias_new = jnp.tile(_bias_lookup(bias_tab, bsel[:, lc:]), (1, 1, hpg))
    mask_c = jnp.asarray(mask[:, pos])
    mask_new = jnp.asarray(np.tile(mask[:, lc:], (1, hpg)))

    out_spec = pl.BlockSpec((hpg, s_len, LANE), lambda b: (0, b, 0))
    return pl.pallas_call(
        functools.partial(_attn_sample_kernel, hpg=hpg, s_len=s_len, grouped=grouped),
        grid=(n_seq,),
        in_specs=[pl.BlockSpec((hpg, s_len, LANE), lambda b: (q_slab // hpg, b, 0)),
                  pl.BlockSpec((hpg, s_len, LANE), lambda b: (k_slab // hpg, b, 0)),
                  pl.BlockSpec((hpg, s_len, LANE), lambda b: (v_slab // hpg, b, 0)),
                  cache_spec, cache_spec,
                  pl.BlockSpec((hpg, s_len, n_keys), lambda b: (0, 0, 0)),
                  pl.BlockSpec((s_len, n_keys), lambda b: (0, 0)),
                  pl.BlockSpec((hpg, s_len, hpg * s_len), lambda b: (0, 0, 0)),
                  pl.BlockSpec((s_len, hpg * s_len), lambda b: (0, 0))],
        out_specs=[out_spec, out_spec],
        out_shape=[jax.ShapeDtypeStruct((hpg, t, LANE), F32)] * 2,
        compiler_params=_params(("arbitrary",)),
        name=f"attn_sample_d{dil}",
    )(z_sl, z_sl, z_sl, ck, cv, bias_c, mask_c, bias_new, mask_new)


def _merge_kernel(o1, o2, o3, l1, l2, l3, out_ref):
    merged = _merge_by_lse([o1[...], o2[...], o3[...]], [l1[...], l2[...], l3[...]])
    for h in range(merged.shape[0]):
        out_ref[:, _lanes(h)] = merged[h].astype(out_ref.dtype)


def _merge(outs, lses):
    hpg, t, _ = outs[0].shape
    bt = _tile(t, 256, SUBLANE)
    spec = pl.BlockSpec((hpg, bt, LANE), lambda i: (0, i, 0))
    return pl.pallas_call(
        _merge_kernel,
        grid=(t // bt,),
        in_specs=[spec] * 6,
        out_specs=pl.BlockSpec((bt, hpg * LANE), lambda i: (i, 0)),
        out_shape=jax.ShapeDtypeStruct((t, hpg * LANE), BF16),
        compiler_params=_params(("arbitrary",)),
        name="merge_groups",
    )(*outs, *lses)


def _gated_proj_kernel(a_ref, b_ref, wa_ref, wb_ref, ga_ref, gb_ref, o_ref):
    n_slabs = ga_ref.shape[0]
    per_strip = min(n_slabs, MXU_WIDTH // LANE)
    for c0 in range(0, n_slabs, per_strip):
        c1 = min(c0 + per_strip, n_slabs)
        cols = slice(c0 * LANE, c1 * LANE)
        pa = jnp.dot(a_ref[...], wa_ref[:, cols], preferred_element_type=F32)
        pb = jnp.dot(b_ref[...], wb_ref[:, cols], preferred_element_type=F32)
        for c in range(c0, c1):
            gated = ga_ref[c] * pa[:, _lanes(c - c0)] + gb_ref[c] * pb[:, _lanes(c - c0)]
            o_ref[:, _lanes(c)] = gated.astype(o_ref.dtype)


def _gated_proj(o_a, o_b, w_a, w_b, z_sl, bn, gate_a_blk, gate_b_blk):
    t, d_a = o_a.shape
    d_b = o_b.shape[1]
    d = w_a.shape[1]
    bm = _tile(t, MATMUL_TILE, SUBLANE)
    return pl.pallas_call(
        _gated_proj_kernel,
        grid=(t // bm, d // bn),
        in_specs=[pl.BlockSpec((bm, d_a), lambda i, j: (i, 0)),
                  pl.BlockSpec((bm, d_b), lambda i, j: (i, 0)),
                  pl.BlockSpec((d_a, bn), lambda i, j: (0, j)),
                  pl.BlockSpec((d_b, bn), lambda i, j: (0, j)),
                  pl.BlockSpec((bn // LANE, bm, LANE), lambda i, j: (gate_a_blk + j, i, 0)),
                  pl.BlockSpec((bn // LANE, bm, LANE), lambda i, j: (gate_b_blk + j, i, 0))],
        out_specs=pl.BlockSpec((bm, bn), lambda i, j: (i, j)),
        out_shape=jax.ShapeDtypeStruct((t, d), BF16),
        compiler_params=_params(("arbitrary", "arbitrary")),
        name="gated_proj",
    )(o_a, o_b, w_a, w_b, z_sl, z_sl)


def _matmul_kernel(x_ref, w_ref, o_ref, *, nk):
    if nk == 1:
        o_ref[...] = jnp.dot(x_ref[...], w_ref[...], preferred_element_type=F32)
    else:
        @pl.when(pl.program_id(2) == 0)
        def _():
            o_ref[...] = jnp.zeros(o_ref.shape, o_ref.dtype)

        o_ref[...] += jnp.dot(x_ref[...], w_ref[...], preferred_element_type=F32)


def _matmul(x, w, name, k_tile=None):
    t, kd = x.shape
    n = w.shape[1]
    bm = _tile(t, MATMUL_TILE, SUBLANE)
    bn = _tile(n, MATMUL_TILE, LANE)
    bk = kd if k_tile is None else k_tile
    nk = kd // bk
    return pl.pallas_call(
        functools.partial(_matmul_kernel, nk=nk),
        grid=(t // bm, n // bn, nk),
        in_specs=[pl.BlockSpec((bm, bk), lambda i, j, k: (i, k)),
                  pl.BlockSpec((bk, bn), lambda i, j, k: (k, j))],
        out_specs=pl.BlockSpec((bm, bn), lambda i, j, k: (i, j)),
        out_shape=jax.ShapeDtypeStruct((t, n), F32),
        compiler_params=_params(("arbitrary", "arbitrary", "arbitrary")),
        name=name,
    )(x, w)


def _ffn_down_round_kernel(x_ref, w_in, o_ref, w_out, w_buf, acc_ref, *, nk, k_rows):
    bb, i = pl.program_id(0), pl.program_id(1)
    ni = pl.num_programs(1)
    ck, bn = w_in.shape
    bm = x_ref.shape[0]
    slot = (bb + 1) % 2

    def round_chunk():
        last = pl.num_programs(0) - 1
        k_c = jnp.minimum(bb, last - 1) % nk
        i_c = jnp.where(bb == last, ni - 1, i)
        row = (k_c * ni + i_c) * ck + lax.broadcasted_iota(jnp.int32, (ck, bn), 0)
        chunk = jnp.where(row < k_rows, w_in[...], 0.0).astype(BF16)
        w_buf[bb % 2, pl.ds(pl.multiple_of(i * ck, ck), ck), :] = chunk
        w_out[...] = chunk

    pl.when(bb == 0)(round_chunk)

    @pl.when(bb > 0)
    def _():
        round_chunk()
        k = (bb - 1) % nk
        row_split = 2 if bm % (4 * SUBLANE) == 0 else 1
        hm = bm // row_split
        strip = min(bn, MXU_WIDTH)
        for h in range(row_split):
            rows = pl.ds(pl.multiple_of(i * bm + h * hm, hm), hm)
            x = x_ref[h * hm:(h + 1) * hm, :]
            for c0 in range(0, bn, strip):
                cols = slice(c0, c0 + strip)
                prev = jnp.where(k == 0, 0.0, acc_ref[rows, cols])
                total = prev + jnp.dot(x, w_buf[slot, :, cols], preferred_element_type=F32)
                acc_ref[rows, cols] = total
                o_ref[h * hm:(h + 1) * hm, cols] = total


def _ffn_down_round(x, w, k_tile, name):
    t, kp = x.shape
    k_rows, n = w.shape
    bm = _tile(t, MATMUL_TILE, SUBLANE)
    bn = _tile(n, FFN_DOWN_TILE, LANE)
    ni, nj, nk = t // bm, n // bn, kp // k_tile
    assert k_tile % (2 * SUBLANE * ni) == 0 and kp - k_rows < k_tile // ni
    ck = k_tile // ni
    nb = nj * nk

    def mm(bb, i):
        b = jnp.maximum(bb - 1, 0)
        return jnp.where(bb == 0, 0, i), b % nk, b // nk

    def chunk(bb, i):
        b = jnp.minimum(bb, nb - 1)
        return (b % nk) * ni + jnp.where(bb == nb, ni - 1, i), b // nk

    def out_idx(bb, i):
        i_m, k_m, j_m = mm(bb, i)
        return jnp.where(k_m == nk - 1, i_m, 0), j_m

    return pl.pallas_call(
        functools.partial(_ffn_down_round_kernel, nk=nk, k_rows=k_rows),
        grid=(nb + 1, ni),
        in_specs=[pl.BlockSpec((bm, k_tile), lambda bb, i: mm(bb, i)[:2]),
                  pl.BlockSpec((ck, bn), chunk)],
        out_specs=[pl.BlockSpec((bm, bn), out_idx), pl.BlockSpec((ck, bn), chunk)],
        out_shape=[jax.ShapeDtypeStruct((t, n), F32), jax.ShapeDtypeStruct((kp, n), BF16)],
        scratch_shapes=[pltpu.VMEM((2, k_tile, bn), BF16), pltpu.VMEM((t, bn), F32)],
        compiler_params=_params(("arbitrary", "arbitrary")),
        name=name,
    )(x, w)


def _post_mix_kernel(x_ref, y_ref, g1_ref, g2_ref, x1_ref, h2_ref):
    y = y_ref[...]
    x1 = x_ref[...] + (y * _rms_scale(y)) * g1_ref[...]
    x1_ref[...] = x1
    h2_ref[...] = ((x1 * _rms_scale(x1)) * g2_ref[...]).astype(h2_ref.dtype)


def _post_mix(x, y, g_post, g_pre):
    t, d = x.shape
    bt = _tile(t, 256, SUBLANE)
    row = pl.BlockSpec((bt, d), lambda i: (i, 0))
    vec = pl.BlockSpec((1, d), lambda i: (0, 0))
    return pl.pallas_call(
        _post_mix_kernel,
        grid=(t // bt,),
        in_specs=[row, row, vec, vec],
        out_specs=[row, row],
        out_shape=[jax.ShapeDtypeStruct((t, d), F32), jax.ShapeDtypeStruct((t, d), BF16)],
        compiler_params=_params(("arbitrary",)),
        name="post_mix",
    )(x, y, g_post.reshape(1, d), g_pre.reshape(1, d))


def _residual_norm_kernel(x_ref, y_ref, g_ref, o_ref):
    y = y_ref[...]
    o_ref[...] = x_ref[...] + (y * _rms_scale(y)) * g_ref[...]


def _residual_norm(x, y, g):
    t, d = x.shape
    bt = _tile(t, 256, SUBLANE)
    row = pl.BlockSpec((bt, d), lambda i: (i, 0))
    return pl.pallas_call(
        _residual_norm_kernel,
        grid=(t // bt,),
        in_specs=[row, row, pl.BlockSpec((1, d), lambda i: (0, 0))],
        out_specs=row,
        out_shape=jax.ShapeDtypeStruct((t, d), F32),
        compiler_params=_params(("arbitrary",)),
        name="residual_norm",
    )(x, y, g.reshape(1, d))


def _ffn_up_kernel(*refs, conv_w, s_len, d_ff):
    if s_len is None:
        (x_ref, wg_in, wu_in, cw_ref, cb_ref, out_ref, tail_ref, wg_out, wu_out,
         wg_buf, wu_buf, a_buf, u_buf, o_buf) = refs
        hist_refs = None
        jj, i = pl.program_id(0), pl.program_id(1)
        ck = wg_in.shape[0]

        def round_chunks():
            rows_c = pl.ds(pl.multiple_of(i * ck, ck), ck)
            for src, buf, dst in ((wg_in, wg_buf, wg_out), (wu_in, wu_buf, wu_out)):
                chunk = src[...].astype(BF16)
                buf[jj % 2, rows_c, :] = chunk
                dst[...] = chunk

        slot = (jj + 1) % 2
        wg = lambda cols: wg_buf[slot, :, cols]
        wu = lambda cols: wu_buf[slot, :, cols]
        col_blk = jj - 1
        row_split = 2 if out_ref.shape[0] % (4 * SUBLANE) == 0 else 1
    else:
        x_ref, wg_ref, wu_ref, cw_ref, cb_ref, h1_ref, h2_ref, out_ref, tail_ref, a_buf, u_buf, o_buf = refs
        hist_refs = {1: h1_ref, 2: h2_ref}
        i = pl.program_id(1)
        wg = lambda cols: wg_ref[:, cols]
        wu = lambda cols: wu_ref[:, cols]
        col_blk = pl.program_id(0)
        row_split = 1
    bm, bn = out_ref.shape
    hm = bm // row_split
    half = hm // 2
    top = SUBLANE
    n_slabs = bn // LANE
    per_strip = min(n_slabs, MXU_WIDTH // LANE)

    def units():
        @pl.when(i == 0)
        def _():
            a_buf[:, 0:top, :] = jnp.zeros((n_slabs, top, LANE), F32)

        lane = lax.broadcasted_iota(jnp.int32, (1, LANE), 1)
        for h in range(row_split):
            r0 = h * hm
            x = x_ref[r0:r0 + hm, :]
            for c0 in range(0, n_slabs, per_strip):
                c1 = min(c0 + per_strip, n_slabs)
                cols = slice(c0 * LANE, c1 * LANE)
                a = jnp.dot(x, wg(cols), preferred_element_type=F32)
                u = jnp.dot(x, wu(cols), preferred_element_type=F32)
                if s_len is None and h == 0 and c0 == 0:
                    round_chunks()
                if s_len is not None:
                    tail_ref[r0:r0 + hm, cols] = a
                elif h == row_split - 1:
                    tail_ref[:, cols] = a[hm - SUBLANE:, :]
                for c in range(c0, c1):
                    a_buf[c, top + r0:top + r0 + hm, :] = a[:, _lanes(c - c0)]
                    u_buf[c, r0:r0 + hm, :] = u[:, _lanes(c - c0)]
                    in_range = (col_blk * bn + c * LANE + lane) < d_ff
                    for e in range(2):
                        taps = [a_buf[c, pl.ds(top + r0 + e - lag, half, stride=2), :] for lag in range(conv_w)]
                        if hist_refs is not None:
                            s = 2 * (lax.broadcasted_iota(jnp.int32, (half, LANE), 0) % (s_len // 2)) + e
                            for lag in range(1, conv_w):
                                hist = hist_refs[lag][c, pl.ds(r0 + e, half, stride=2), :]
                                taps[lag] = jnp.where(s < lag, hist, taps[lag])
                        acc = cw_ref[0:1, _lanes(c)] * taps[conv_w - 1]
                        for k in range(1, conv_w):
                            acc = acc + cw_ref[k:k + 1, _lanes(c)] * taps[conv_w - 1 - k]
                        acc = cb_ref[:, _lanes(c)] + acc
                        val = _gelu(acc) * u_buf[c, pl.ds(r0 + e, half, stride=2), :]
                        o_buf[c, pl.ds(r0 + e, half, stride=2), :] = jnp.where(in_range, val, 0.0)
                    out_ref[r0:r0 + hm, _lanes(c)] = o_buf[c, r0:r0 + hm, :].astype(out_ref.dtype)
        if s_len is None:
            a_buf[:, 0:top, :] = a_buf[:, bm:bm + top, :]

    if s_len is None:
        pl.when(jj == 0)(round_chunks)
        pl.when(jj > 0)(units)
    else:
        units()


def _ffn_up_prompt(h, w_gate, w_up, conv_w, conv_b, ffp):
    t, d = h.shape
    d_ff = w_gate.shape[1]
    cw = conv_w.shape[0]
    assert cw == 3 and d_ff % LANE == 0
    bn = FFN_TILE
    assert ffp % bn == 0 and ffp - d_ff < bn
    bm = _tile(t, MATMUL_TILE, 4 * SUBLANE)
    ni, nj = t // bm, ffp // bn
    assert d % ni == 0
    ck = d // ni
    n_slabs = bn // LANE

    def fin_i(jj, i):
        return jnp.where(jj == 0, 0, i)

    def fin_j(jj):
        return jnp.maximum(jj - 1, 0)

    def chunk(jj, i):
        return jnp.where(jj == nj, ni - 1, i), jnp.minimum(jj, nj - 1)

    w_in_spec = pl.BlockSpec((ck, bn), chunk)
    wbuf = pltpu.VMEM((2, d, bn), BF16)
    act, tail, wg16, wu16 = pl.pallas_call(
        functools.partial(_ffn_up_kernel, conv_w=cw, s_len=None, d_ff=d_ff),
        grid=(nj + 1, ni),
        in_specs=[pl.BlockSpec((bm, d), lambda jj, i: (fin_i(jj, i), 0)), w_in_spec, w_in_spec,
                  pl.BlockSpec((cw, bn), lambda jj, i: (0, fin_j(jj))),
                  pl.BlockSpec((1, bn), lambda jj, i: (0, fin_j(jj)))],
        out_specs=[pl.BlockSpec((bm, bn), lambda jj, i: (fin_i(jj, i), fin_j(jj))),
                   pl.BlockSpec((SUBLANE, bn), lambda jj, i: (0, fin_j(jj))),
                   w_in_spec, w_in_spec],
        out_shape=[jax.ShapeDtypeStruct((t, ffp), BF16), jax.ShapeDtypeStruct((SUBLANE, d_ff), F32),
                   jax.ShapeDtypeStruct((d, d_ff), BF16), jax.ShapeDtypeStruct((d, d_ff), BF16)],
        scratch_shapes=[wbuf, wbuf,
                        pltpu.VMEM((n_slabs, bm + SUBLANE, LANE), F32), pltpu.VMEM((n_slabs, bm, LANE), F32),
                        pltpu.VMEM((n_slabs, bm, LANE), F32)],
        compiler_params=_params(("arbitrary", "arbitrary")),
        name="ffn_up",
    )(h, w_gate, w_up, conv_w, conv_b.reshape(1, d_ff))
    return act, tail, wg16, wu16


def _ffn_up_sample(h, w_gate, w_up, conv_w, conv_b, ffp, hist, s_len):
    t, d = h.shape
    d_ff = w_gate.shape[1]
    cw = conv_w.shape[0]
    assert cw == 3 and d_ff % LANE == 0 and s_len % 2 == 0
    bn = _tile(ffp, MATMUL_TILE, LANE)
    assert ffp - d_ff < bn
    n_slabs = bn // LANE
    w_spec = pl.BlockSpec((d, bn), lambda j, i: (0, j))
    out_spec = pl.BlockSpec((t, bn), lambda j, i: (i, j))

    def slabs(rows):
        return jnp.transpose(rows.reshape(t, d_ff // LANE, LANE), (1, 0, 2))
    h1 = slabs(jnp.pad(hist[:, 1:2], ((0, 0), (0, s_len - 1), (0, 0))))
    h2 = slabs(jnp.pad(hist, ((0, 0), (0, s_len - hist.shape[1]), (0, 0))))
    hist_spec = pl.BlockSpec((n_slabs, t, LANE), lambda j, i: (j, i, 0))
    return pl.pallas_call(
        functools.partial(_ffn_up_kernel, conv_w=cw, s_len=s_len, d_ff=d_ff),
        grid=(ffp // bn, 1),
        in_specs=[pl.BlockSpec((t, d), lambda j, i: (i, 0)), w_spec, w_spec,
                  pl.BlockSpec((cw, bn), lambda j, i: (0, j)), pl.BlockSpec((1, bn), lambda j, i: (0, j)),
                  hist_spec, hist_spec],
        out_specs=[out_spec, out_spec],
        out_shape=[jax.ShapeDtypeStruct((t, ffp), BF16), jax.ShapeDtypeStruct((t, d_ff), F32)],
        scratch_shapes=[pltpu.VMEM((n_slabs, t + SUBLANE, LANE), F32), pltpu.VMEM((n_slabs, t, LANE), F32),
                        pltpu.VMEM((n_slabs, t, LANE), F32)],
        compiler_params=_params(("arbitrary", "arbitrary")),
        name="ffn_up_decode",
    )(h, w_gate, w_up, conv_w, conv_b.reshape(1, d_ff), h1, h2)


def _layer(x, p, s_len=None, caches=None, conv_hist=None):
    t, d = x.shape
    d_a = p["ln_g"].shape[0]
    hpg = p["hpg"]
    gw = hpg * HEAD_DIM
    bn = p["bn"]
    o1 = 2 * d_a
    d_qkv = 3 * gw

    h = _rmsnorm_cast(x, p["g_pre_mix"])
    w16 = dict(p.get("w16", {}))
    z_parts = []
    for name, col0, n_cols, act in (("in_proj_sgu", 0, o1, _gelu),
                                    ("in_proj_qkv", o1, 3 * d_qkv, lambda v: v),
                                    ("in_proj_gates", o1 + 3 * d_qkv, 2 * d, _sigmoid)):
        if s_len is None:
            z_part, w16[name] = _in_proj_round(h, p["w_in"], col0, n_cols, act, name)
        else:
            z_part = _in_proj(h, w16[name], bn, 0, n_cols, act, name + "_decode")
        z_parts.append(z_part)
    za_sl, qkv_sl, gate_sl = z_parts

    if s_len is None:
        w_mix, b_mix = p["w_mix_prompt"], p["b_mix_prompt"]
    else:
        w_mix, b_mix = p["w_mix_sample"], p["b_mix_sample"]
    o_a, v_state = _sgu(za_sl, d_a, p["ln_g"], p["ln_b"], w_mix, b_mix)

    outs, lses = [], []
    for gi, (_, dil) in enumerate(DILATION_GROUPS):
        q_slab = gi * hpg
        k_slab = d_qkv // LANE + gi * hpg
        v_slab = 2 * d_qkv // LANE + gi * hpg
        bias_tab = p["rel_bias"][:, gi * hpg:(gi + 1) * hpg]
        if s_len is not None:
            o, lse = _attn_sample(qkv_sl, s_len, hpg, q_slab, k_slab, v_slab, caches[2 * gi], caches[2 * gi + 1],
                                  bias_tab, dil)
        elif gi < len(DILATION_GROUPS) - 1:
            o, lse = _attn_prompt(qkv_sl, hpg, q_slab, k_slab, v_slab, bias_tab, dil)
        else:
            o_b = _attn_prompt(qkv_sl, hpg, q_slab, k_slab, v_slab, bias_tab, dil, merge_with=(outs, lses))
            break
        outs.append(o)
        lses.append(lse)
    if s_len is not None:
        o_b = _merge(outs, lses)

    merged = _gated_proj(o_a, o_b, p["w_proj_a"], p["w_proj_b"], gate_sl, bn, 0, d // bn)
    if s_len is None:
        y, w16["out_proj"] = _in_proj_round(merged, p["w_out"], 0, d, lambda v: v, "out_proj", slab_out=False)
    else:
        y = _matmul(merged, w16["out_proj"], "out_proj_decode")
    x1, h2 = _post_mix(x, y, p["g_post_mix"], p["g_pre_ffn"])

    if s_len is None:
        act, a_tail, w16["ffn_gate"], w16["ffn_up"] = _ffn_up_prompt(h2, p["w_gate"], p["w_up"], p["conv_w"],
                                                                     p["conv_b"], p["ffp"])
    else:
        act, a_tail = _ffn_up_sample(h2, w16["ffn_gate"], w16["ffn_up"], p["conv_w"], p["conv_b"], p["ffp"],
                                     conv_hist, s_len)
    if s_len is None:
        f, w16["ffn_down"] = _ffn_down_round(act, p["w_down"], p["ffn_k_tile"], "ffn_down")
    else:
        f = _matmul(act, w16["ffn_down"], "ffn_down_decode", k_tile=p["ffn_k_tile"])
    y_out = _residual_norm(x1, f, p["g_post_ffn"])
    return y_out, qkv_sl, v_state, a_tail, w16


def kernel(x_prompt, x_sample, cache_k_g1, cache_v_g1, cache_k_g2, cache_v_g2, cache_k_g3, cache_v_g3, state_conv, g_pre_mix, w_in, sgu_ln_g, sgu_ln_b, w_spatial, b_spatial, rel_bias, w_proj_a, w_proj_b, w_out, g_post_mix, g_pre_ffn, w_gate, w_up, conv_w, conv_b, w_down, g_post_ffn):
    depth = w_in.shape[0]
    assert depth == 1
    n_prompt, seq, d = x_prompt.shape
    assert n_prompt == 1 and seq % CHUNK == 0
    n_seq, s_len, _ = x_sample.shape
    assert s_len == SUBLANE
    d_a = sgu_ln_g.shape[1]
    n_groups = w_spatial.shape[1]
    n_heads = rel_bias.shape[1]
    hpg = n_heads // len(DILATION_GROUPS)
    gw = hpg * HEAD_DIM
    d_ff = w_gate.shape[2]
    cw = conv_w.shape[1]
    bn = _tile(math.gcd(2 * d_a, gw), MATMUL_TILE, LANE)
    ffp = -(-d_ff // FFN_TILE) * FFN_TILE
    t_s = n_seq * s_len

    tri = np.tril(np.ones((CHUNK, CHUNK), np.float32))
    w_mix_prompt = (w_spatial[0] * tri).astype(BF16)
    b_mix_prompt = jnp.broadcast_to(b_spatial[0][:, :, None], (n_groups, CHUNK, LANE))
    w_small = w_spatial[0][:, :s_len, :s_len] * tri[:s_len, :s_len]
    eye = np.eye(n_seq, dtype=np.float32)
    w_mix_sample = jnp.einsum("ab,gpq->gapbq", eye, w_small).reshape(n_groups, t_s, t_s).astype(BF16)
    b_mix_sample = jnp.broadcast_to(jnp.tile(b_spatial[0][:, :s_len], (1, n_seq))[:, :, None], (n_groups, t_s, LANE))

    p = dict(
        hpg=hpg, bn=bn, ffp=ffp,
        g_pre_mix=g_pre_mix[0], w_in=w_in[0], ln_g=sgu_ln_g[0], ln_b=sgu_ln_b[0],
        w_mix_prompt=w_mix_prompt, b_mix_prompt=b_mix_prompt,
        w_mix_sample=w_mix_sample, b_mix_sample=b_mix_sample,
        rel_bias=rel_bias,
        w_proj_a=w_proj_a[0].astype(BF16), w_proj_b=w_proj_b[0].astype(BF16), w_out=w_out[0],
        g_post_mix=g_post_mix[0], g_pre_ffn=g_pre_ffn[0],
        w_gate=w_gate[0], w_up=w_up[0],
        conv_w=conv_w[0], conv_b=conv_b[0],
        w_down=w_down[0],
        ffn_k_tile=_tile(ffp, 3072, LANE),
        g_post_ffn=g_post_ffn[0],
    )

    caches = tuple(c[0] for c in (cache_k_g1, cache_v_g1, cache_k_g2, cache_v_g2, cache_k_g3, cache_v_g3))

    yp, zp, vp_state, ap_tail, w16 = _layer(x_prompt[0], p)
    ys, zs, vs_state, as_all, _ = _layer(x_sample.reshape(t_s, d), dict(p, w16=w16), s_len=s_len,
                                         caches=caches, conv_hist=state_conv[0])

    d_qkv = 3 * gw
    prompt_kv, sample_kv = [], []
    for gi, (win, _) in enumerate(DILATION_GROUPS):
        keep = min(win, seq)
        for base in (d_qkv, 2 * d_qkv):
            s0 = (base + gi * gw) // LANE
            pk = jnp.transpose(zp[s0:s0 + hpg, seq - keep:, :], (1, 0, 2))
            prompt_kv.append(pk.reshape(1, 1, keep, hpg, HEAD_DIM))
            sk = jnp.transpose(zs[s0:s0 + hpg], (1, 0, 2))
            sample_kv.append(sk.reshape(1, n_seq, s_len, hpg, HEAD_DIM))
    p_conv = ap_tail[SUBLANE - (cw - 1):].reshape(1, 1, cw - 1, d_ff)
    s_conv = as_all.reshape(n_seq, s_len, d_ff)[:, s_len - (cw - 1):].reshape(1, n_seq, cw - 1, d_ff)
    return (yp.reshape(1, seq, d), ys.reshape(n_seq, s_len, d),
            *prompt_kv, vp_state.reshape(1, 1, CHUNK, d_a), p_conv,
            *sample_kv, vs_state.reshape(1, n_seq, s_len, d_a), s_conv)
```

```python
import functools
import math

import numpy as np
import jax
import jax.numpy as jnp
from jax import lax
from jax.experimental import pallas as pl
from jax.experimental.pallas import tpu as pltpu

F32 = jnp.float32
BF16 = jnp.bfloat16

HEAD_DIM = 128
STEPS = 128
CHUNK = 128
GROUP_DIM_A = 128
DILATION_GROUPS = ((128, 1), (512, 4), (2048, 16))
MAX_EXACT = 16
MAX_DISTANCE = 2048
EPS = 1e-6
NEG = -1e30
ATTN_SCALE = HEAD_DIM ** -0.5

LANE = 128
SUBLANE = 8
VMEM_LIMIT_BYTES = 56 * 1024 * 1024
MATMUL_TILE = 1024
MXU_WIDTH = 256
FFN_TILE = 512
FFN_DOWN_TILE = 512
IN_PROJ_ROUND_TILE = 1024
ATTN_UNITS_PER_STEP = 16
SGU_ROWS_PER_STEP = 512


def _tile(dim, target, align):
    best = None
    t = align
    while t <= min(dim, target):
        if dim % t == 0:
            best = t
        t += align
    return best if best is not None else dim


def _params(semantics):
    return pltpu.CompilerParams(dimension_semantics=semantics, vmem_limit_bytes=VMEM_LIMIT_BYTES)


def _gelu(x):
    k0 = -2.0 * math.sqrt(2.0 / math.pi)
    return x / (1.0 + jnp.exp(x * (k0 + (k0 * 0.044715) * (x * x))))


def _sigmoid(x):
    return 1.0 / (1.0 + jnp.exp(-x))


def _rms_scale(x):
    return lax.rsqrt(jnp.mean(x * x, axis=-1, keepdims=True) + EPS)


def _lanes(c):
    return slice(c * LANE, (c + 1) * LANE)


def _rmsnorm_cast_kernel(x_ref, g_ref, o_ref):
    x = x_ref[...]
    o_ref[...] = ((x * _rms_scale(x)) * g_ref[...]).astype(o_ref.dtype)


def _rmsnorm_cast(x, g):
    t, d = x.shape
    bt = _tile(t, 256, SUBLANE)
    return pl.pallas_call(
        _rmsnorm_cast_kernel,
        grid=(t // bt,),
        in_specs=[pl.BlockSpec((bt, d), lambda i: (i, 0)), pl.BlockSpec((1, d), lambda i: (0, 0))],
        out_specs=pl.BlockSpec((bt, d), lambda i: (i, 0)),
        out_shape=jax.ShapeDtypeStruct((t, d), BF16),
        compiler_params=_params(("arbitrary",)),
        name="rmsnorm_cast",
    )(x, g.reshape(1, d))


def _in_proj_kernel(x_ref, w_ref, o_ref, *, act):
    acc = jnp.dot(x_ref[...], w_ref[...], preferred_element_type=F32)
    for c in range(o_ref.shape[0]):
        o_ref[c] = act(acc[:, _lanes(c)])


def _in_proj_round_kernel(x_ref, w_in, o_ref, w_out, w_buf, *, act, slab_out):
    jj, i = pl.program_id(0), pl.program_id(1)
    ck = w_in.shape[0]
    slot = (jj + 1) % 2
    bm = o_ref.shape[-2]
    n_slabs = w_in.shape[1] // LANE
    row_split = 2 if bm % (4 * SUBLANE) == 0 else 1
    hm = bm // row_split
    per_strip = min(n_slabs, MXU_WIDTH // LANE)

    def round_chunk():
        chunk = w_in[...].astype(BF16)
        w_buf[jj % 2, pl.ds(pl.multiple_of(i * ck, ck), ck), :] = chunk
        w_out[...] = chunk

    pl.when(jj == 0)(round_chunk)

    @pl.when(jj > 0)
    def _():
        round_chunk()
        for h in range(row_split):
            x = x_ref[h * hm:(h + 1) * hm, :]
            for c0 in range(0, n_slabs, per_strip):
                c1 = min(c0 + per_strip, n_slabs)
                acc = jnp.dot(x, w_buf[slot, :, c0 * LANE:c1 * LANE], preferred_element_type=F32)
                for c in range(c0, c1):
                    val = act(acc[:, _lanes(c - c0)])
                    if slab_out:
                        o_ref[c, h * hm:(h + 1) * hm, :] = val
                    else:
                        o_ref[h * hm:(h + 1) * hm, _lanes(c)] = val


def _in_proj_round(h, w, col0, n_cols, act, name, slab_out=True):
    t, d = h.shape
    bn = _tile(math.gcd(n_cols, col0) if col0 else n_cols, IN_PROJ_ROUND_TILE, LANE)
    bm = _tile(t, MATMUL_TILE, SUBLANE)
    ni, nj = t // bm, n_cols // bn
    assert d % ni == 0
    ck = d // ni

    def fin(jj, i):
        return jnp.maximum(jj - 1, 0), jnp.where(jj == 0, 0, i)

    def chunk(jj, i):
        return jnp.where(jj == nj, ni - 1, i), jnp.minimum(jj, nj - 1)

    if slab_out:
        out_spec = pl.BlockSpec((bn // LANE, bm, LANE), lambda jj, i: fin(jj, i) + (0,))
        out_shape = jax.ShapeDtypeStruct((n_cols // LANE, t, LANE), F32)
    else:
        out_spec = pl.BlockSpec((bm, bn), lambda jj, i: fin(jj, i)[::-1])
        out_shape = jax.ShapeDtypeStruct((t, n_cols), F32)
    return pl.pallas_call(
        functools.partial(_in_proj_round_kernel, act=act, slab_out=slab_out),
        grid=(nj + 1, ni),
        in_specs=[pl.BlockSpec((bm, d), lambda jj, i: (fin(jj, i)[1], 0)),
                  pl.BlockSpec((ck, bn), lambda jj, i: (chunk(jj, i)[0], col0 // bn + chunk(jj, i)[1]))],
        out_specs=[out_spec, pl.BlockSpec((ck, bn), chunk)],
        out_shape=[out_shape, jax.ShapeDtypeStruct((d, n_cols), BF16)],
        scratch_shapes=[pltpu.VMEM((2, d, bn), BF16)],
        compiler_params=_params(("arbitrary", "arbitrary")),
        name=name,
    )(h, w)


def _in_proj(h, w, bn, col0, n_cols, act, name):
    t, d = h.shape
    bm = _tile(t, MATMUL_TILE, SUBLANE)
    return pl.pallas_call(
        functools.partial(_in_proj_kernel, act=act),
        grid=(t // bm, n_cols // bn),
        in_specs=[pl.BlockSpec((bm, d), lambda i, j: (i, 0)),
                  pl.BlockSpec((d, bn), lambda i, j: (0, col0 // bn + j))],
        out_specs=pl.BlockSpec((bn // LANE, bm, LANE), lambda i, j: (j, i, 0)),
        out_shape=jax.ShapeDtypeStruct((n_cols // LANE, t, LANE), F32),
        compiler_params=_params(("arbitrary", "arbitrary")),
        name=name,
    )(h, w)


def _sgu_kernel(u_ref, v_ref, lg_ref, lb_ref, w_ref, b_ref, o_ref, vs_ref, *, n_groups):
    c = w_ref.shape[1]
    n_feat = n_groups * GROUP_DIM_A
    for r0 in range(0, o_ref.shape[0], c):
        vp = v_ref[:, r0:r0 + c, :]
        mu = jnp.sum(jnp.sum(vp, axis=0), axis=-1, keepdims=True) / n_feat
        vc = vp - mu
        var = jnp.sum(jnp.sum(vc * vc, axis=0), axis=-1, keepdims=True) / n_feat
        v = vc * lax.rsqrt(var + EPS) * lg_ref[...] + lb_ref[...]
        for g in range(n_groups):
            if r0 + c == o_ref.shape[0]:
                vs_ref[:, _lanes(g)] = v[g]
            mixed = jnp.dot(w_ref[g], v[g].astype(BF16), preferred_element_type=F32) + b_ref[g]
            o_ref[r0:r0 + c, _lanes(g)] = (u_ref[g, r0:r0 + c, :] * mixed).astype(o_ref.dtype)


def _sgu(z_sl, d_a, ln_g, ln_b, w_mix, b_mix):
    t = z_sl.shape[1]
    n_groups, c, _ = w_mix.shape
    rows = _tile(t, SGU_ROWS_PER_STEP, c)
    slab = pl.BlockSpec((n_groups, 1, LANE), lambda i: (0, 0, 0))
    return pl.pallas_call(
        functools.partial(_sgu_kernel, n_groups=n_groups),
        grid=(t // rows,),
        in_specs=[
            pl.BlockSpec((n_groups, rows, LANE), lambda i: (0, i, 0)),
            pl.BlockSpec((n_groups, rows, LANE), lambda i: (1, i, 0)),
            slab, slab,
            pl.BlockSpec((n_groups, c, c), lambda i: (0, 0, 0)),
            pl.BlockSpec((n_groups, c, LANE), lambda i: (0, 0, 0)),
        ],
        out_specs=[pl.BlockSpec((rows, d_a), lambda i: (i, 0)), pl.BlockSpec((c, d_a), lambda i: (0, 0))],
        out_shape=[jax.ShapeDtypeStruct((t, d_a), BF16), jax.ShapeDtypeStruct((c, d_a), F32)],
        compiler_params=_params(("arbitrary",)),
        name="sgu",
    )(z_sl, z_sl, ln_g.reshape(n_groups, 1, LANE), ln_b.reshape(n_groups, 1, LANE), w_mix, b_mix)


def _t5_bucket(dist, n_buckets):
    n = np.asarray(dist, np.int32)
    safe = np.maximum(n, 1).astype(np.float32)
    large = MAX_EXACT + (np.log(safe / MAX_EXACT) / np.log(np.float32(MAX_DISTANCE / MAX_EXACT))
                         * (n_buckets - MAX_EXACT)).astype(np.int32)
    large = np.minimum(large, n_buckets - 1)
    return np.where(n < MAX_EXACT, n, large).astype(np.int32)


def _bias_lookup(bias_tab, bucket):
    n_buckets = bias_tab.shape[0]
    flat = np.asarray(bucket).reshape(-1)
    onehot = (jnp.asarray(flat)[None, :] == jnp.arange(n_buckets)[:, None]).astype(F32)
    out = jnp.dot(bias_tab.T, onehot, precision=lax.Precision.HIGHEST)
    return out.reshape((bias_tab.shape[1],) + tuple(np.asarray(bucket).shape))


def _merge_by_lse(outs, lses):
    m = functools.reduce(jnp.maximum, lses)
    w = [jnp.exp(l - m) for l in lses]
    num = functools.reduce(lambda x, y: x + y, [wi * oi for wi, oi in zip(w, outs)])
    return num / functools.reduce(lambda x, y: x + y, w)


def _attn_prompt_kernel(*refs, hb, dil, n_other):
    q_ref, kc_ref, vc_ref, bias_ref, band_ref = refs[:5]
    others = refs[5:5 + 2 * n_other]
    n_out = 1 if n_other else 2
    outs = refs[5 + 2 * n_other:5 + 2 * n_other + n_out]
    kp_ref, vp_ref = refs[5 + 2 * n_other + n_out:][:2]
    o_acc, lse_acc = refs[5 + 2 * n_other + n_out + 2:] if n_other else outs
    b = pl.program_id(1)

    @pl.when(b == 0)
    def _():
        kp_ref[...] = jnp.zeros(kp_ref.shape, F32)
        vp_ref[...] = jnp.zeros(vp_ref.shape, F32)

    col = lax.broadcasted_iota(jnp.int32, (STEPS, 2 * STEPS), 1)
    valid = (band_ref[...] > 0.5) & ((b > 0) | (col >= STEPS))
    for hh in range(hb):
        for r in range(dil):
            rows = pl.ds(r, STEPS, stride=dil) if dil > 1 else pl.ds(0, STEPS)
            q = q_ref[hh, rows, :].astype(BF16)
            kk = jnp.concatenate([kp_ref[hh, rows, :], kc_ref[hh, rows, :]], axis=0).astype(BF16)
            vv = jnp.concatenate([vp_ref[hh, rows, :], vc_ref[hh, rows, :]], axis=0).astype(BF16)
            s = lax.dot_general(q, kk, (((1,), (1,)), ((), ())), preferred_element_type=F32) * ATTN_SCALE
            s = jnp.where(valid, s + bias_ref[hh], NEG)
            m = jnp.max(s, axis=-1, keepdims=True)
            e = jnp.exp(s - m)
            den = jnp.sum(e, axis=-1, keepdims=True)
            o_acc[hh, rows, :] = jnp.dot(e.astype(BF16), vv, preferred_element_type=F32) / den
            lse_acc[hh, rows, :] = jnp.broadcast_to(m + jnp.log(den), (STEPS, HEAD_DIM))
    kp_ref[...] = kc_ref[...]
    vp_ref[...] = vc_ref[...]
    if n_other:
        for hh in range(hb):
            merged = _merge_by_lse([r[hh] for r in others[:n_other]] + [o_acc[hh]],
                                   [r[hh] for r in others[n_other:]] + [lse_acc[hh]])
            outs[0][:, _lanes(hh)] = merged.astype(outs[0].dtype)


def _attn_prompt(z_sl, hpg, q_slab, k_slab, v_slab, bias_tab, dil, merge_with=None):
    t = z_sl.shape[1]
    rows = STEPS * dil
    assert t % rows == 0
    nb = t // rows
    hb = min(hpg, max(1, ATTN_UNITS_PER_STEP // dil))
    assert hpg % hb == 0 and q_slab % hb == 0 and k_slab % hb == 0 and v_slab % hb == 0

    p_idx = np.arange(STEPS)[:, None]
    c_idx = np.arange(2 * STEPS)[None, :]
    steps = p_idx + STEPS - c_idx
    band = ((steps >= 0) & (steps <= STEPS)).astype(np.float32)
    bias = _bias_lookup(bias_tab, _t5_bucket(np.clip(steps, 0, STEPS) * dil, bias_tab.shape[0]))

    def cur(slab):
        return pl.BlockSpec((hb, rows, LANE), lambda hi, b: (slab // hb + hi, b, 0))

    head_major = pl.BlockSpec((hb, rows, LANE), lambda hi, b: (hi, b, 0))
    block = pltpu.VMEM((hb, rows, LANE), F32)
    others = [] if merge_with is None else list(merge_with[0]) + list(merge_with[1])
    if merge_with is None:
        out_specs = [head_major, head_major]
        out_shape = [jax.ShapeDtypeStruct((hpg, t, LANE), F32)] * 2
        scratch = [block, block]
    else:
        out_specs = [pl.BlockSpec((rows, hb * LANE), lambda hi, b: (b, hi))]
        out_shape = [jax.ShapeDtypeStruct((t, hpg * LANE), BF16)]
        scratch = [block, block, block, block]
    res = pl.pallas_call(
        functools.partial(_attn_prompt_kernel, hb=hb, dil=dil, n_other=len(others) // 2),
        grid=(hpg // hb, nb),
        in_specs=[cur(q_slab), cur(k_slab), cur(v_slab),
                  pl.BlockSpec((hb, STEPS, 2 * STEPS), lambda hi, b: (hi, 0, 0)),
                  pl.BlockSpec((STEPS, 2 * STEPS), lambda hi, b: (0, 0))] + [head_major] * len(others),
        out_specs=out_specs,
        out_shape=out_shape,
        scratch_shapes=scratch,
        compiler_params=_params(("arbitrary", "arbitrary")),
        name=f"attn_prompt_d{dil}",
    )(z_sl, z_sl, z_sl, bias, jnp.asarray(band), *others)
    return res[0] if merge_with is not None else res


def _attn_sample_kernel(q_ref, kn_ref, vn_ref, ck_ref, cv_ref, bc_ref, mc_ref, bnew_ref, mnew_ref,
                        o_ref, lse_ref, *, hpg, s_len, grouped):
    n_keys = mc_ref.shape[1]
    n_new = hpg * s_len
    valid_c = mc_ref[...] > 0.5

    def cache_head(ref, h):
        if grouped:
            return ref[0, :, pl.ds(h, s_len, stride=hpg), :].reshape(n_keys, HEAD_DIM).astype(BF16)
        return ref[0, pl.ds(h, n_keys, stride=hpg) if hpg > 1 else pl.ds(0, n_keys), :].astype(BF16)

    q_all = q_ref[...].reshape(n_new, HEAD_DIM).astype(BF16)
    s_new = lax.dot_general(q_all, kn_ref[...].reshape(n_new, HEAD_DIM).astype(BF16), (((1,), (1,)), ((), ())),
                            preferred_element_type=F32) * ATTN_SCALE
    key_head = lax.broadcasted_iota(jnp.int32, (s_len, n_new), 1) // s_len
    new_ok = mnew_ref[...] > 0.5

    partial, e_new = [], []
    for h in range(hpg):
        q = q_all[h * s_len:(h + 1) * s_len, :]
        sc = lax.dot_general(q, cache_head(ck_ref, h), (((1,), (1,)), ((), ())),
                             preferred_element_type=F32) * ATTN_SCALE
        sc = jnp.where(valid_c, sc + bc_ref[h], NEG)
        sn = jnp.where(new_ok & (key_head == h), s_new[h * s_len:(h + 1) * s_len, :] + bnew_ref[h], NEG)
        m = jnp.maximum(jnp.max(sc, axis=-1, keepdims=True), jnp.max(sn, axis=-1, keepdims=True))
        ec = jnp.exp(sc - m)
        en = jnp.exp(sn - m)
        den = jnp.sum(ec, axis=-1, keepdims=True) + jnp.sum(en, axis=-1, keepdims=True)
        acc = jnp.dot(ec.astype(BF16), cache_head(cv_ref, h), preferred_element_type=F32)
        partial.append((acc, den, m))
        e_new.append(en)
    o_new = jnp.dot(jnp.concatenate(e_new, axis=0).astype(BF16),
                    vn_ref[...].reshape(n_new, HEAD_DIM).astype(BF16), preferred_element_type=F32)
    for h, (acc, den, m) in enumerate(partial):
        o_ref[h] = (acc + o_new[h * s_len:(h + 1) * s_len, :]) / den
        lse_ref[h] = jnp.broadcast_to(m + jnp.log(den), (s_len, HEAD_DIM))


def _attn_sample(z_sl, s_len, hpg, q_slab, k_slab, v_slab, cache_k, cache_v, bias_tab, dil):
    t = z_sl.shape[1]
    n_seq = t // s_len
    lc = cache_k.shape[1]
    assert q_slab % hpg == 0 and k_slab % hpg == 0 and v_slab % hpg == 0

    j = np.arange(STEPS + 1)
    idx = lc + np.arange(s_len)[:, None] - j[None, :] * dil
    assert idx.min() >= 0
    bucket = _t5_bucket(j * dil, bias_tab.shape[0])
    mask = np.zeros((s_len, lc + s_len), np.float32)
    bsel = np.zeros((s_len, lc + s_len), np.int32)
    for s in range(s_len):
        mask[s, idx[s]] = 1.0
        bsel[s, idx[s]] = bucket

    grouped = dil > s_len and lc % dil == 0 and s_len == SUBLANE
    if grouped:
        pos = (np.arange(lc // dil)[:, None] * dil + np.arange(s_len)[None, :]).reshape(-1)
        assert mask[:, :lc].sum() == mask[:, pos].sum()
        ck = cache_k.reshape(n_seq, lc // dil, dil * hpg, HEAD_DIM)
        cv = cache_v.reshape(n_seq, lc // dil, dil * hpg, HEAD_DIM)
        cache_spec = pl.BlockSpec((1, lc // dil, s_len * hpg, LANE), lambda b: (b, 0, 0, 0))
    else:
        pos = np.arange(lc)
        ck = cache_k.reshape(n_seq, lc * hpg, HEAD_DIM)
        cv = cache_v.reshape(n_seq, lc * hpg, HEAD_DIM)
        cache_spec = pl.BlockSpec((1, lc * hpg, LANE), lambda b: (b, 0, 0))
    n_keys = len(pos)
    bias_c = _bias_lookup(bias_tab, bsel[:, pos])
    bias_new = jnp.tile(_bias_lookup(bias_tab, bsel[:, lc:]), (1, 1, hpg))
    mask_c = jnp.asarray(mask[:, pos])
    mask_new = jnp.asarray(np.tile(mask[:, lc:], (1, hpg)))

    out_spec = pl.BlockSpec((hpg, s_len, LANE), lambda b: (0, b, 0))
    return pl.pallas_call(
        functools.partial(_attn_sample_kernel, hpg=hpg, s_len=s_len, grouped=grouped),
        grid=(n_seq,),
        in_specs=[pl.BlockSpec((hpg, s_len, LANE), lambda b: (q_slab // hpg, b, 0)),
                  pl.BlockSpec((hpg, s_len, LANE), lambda b: (k_slab // hpg, b, 0)),
                  pl.BlockSpec((hpg, s_len, LANE), lambda b: (v_slab // hpg, b, 0)),
                  cache_spec, cache_spec,
                  pl.BlockSpec((hpg, s_len, n_keys), lambda b: (0, 0, 0)),
                  pl.BlockSpec((s_len, n_keys), lambda b: (0, 0)),
                  pl.BlockSpec((hpg, s_len, hpg * s_len), lambda b: (0, 0, 0)),
                  pl.BlockSpec((s_len, hpg * s_len), lambda b: (0, 0))],
        out_specs=[out_spec, out_spec],
        out_shape=[jax.ShapeDtypeStruct((hpg, t, LANE), F32)] * 2,
        compiler_params=_params(("arbitrary",)),
        name=f"attn_sample_d{dil}",
    )(z_sl, z_sl, z_sl, ck, cv, bias_c, mask_c, bias_new, mask_new)


def _merge_kernel(o1, o2, o3, l1, l2, l3, out_ref):
    merged = _merge_by_lse([o1[...], o2[...], o3[...]], [l1[...], l2[...], l3[...]])
    for h in range(merged.shape[0]):
        out_ref[:, _lanes(h)] = merged[h].astype(out_ref.dtype)


def _merge(outs, lses):
    hpg, t, _ = outs[0].shape
    bt = _tile(t, 256, SUBLANE)
    spec = pl.BlockSpec((hpg, bt, LANE), lambda i: (0, i, 0))
    return pl.pallas_call(
        _merge_kernel,
        grid=(t // bt,),
        in_specs=[spec] * 6,
        out_specs=pl.BlockSpec((bt, hpg * LANE), lambda i: (i, 0)),
        out_shape=jax.ShapeDtypeStruct((t, hpg * LANE), BF16),
        compiler_params=_params(("arbitrary",)),
        name="merge_groups",
    )(*outs, *lses)


def _gated_proj_kernel(a_ref, b_ref, wa_ref, wb_ref, ga_ref, gb_ref, o_ref):
    n_slabs = ga_ref.shape[0]
    per_strip = min(n_slabs, MXU_WIDTH // LANE)
    for c0 in range(0, n_slabs, per_strip):
        c1 = min(c0 + per_strip, n_slabs)
        cols = slice(c0 * LANE, c1 * LANE)
        pa = jnp.dot(a_ref[...], wa_ref[:, cols], preferred_element_type=F32)
        pb = jnp.dot(b_ref[...], wb_ref[:, cols], preferred_element_type=F32)
        for c in range(c0, c1):
            gated = ga_ref[c] * pa[:, _lanes(c - c0)] + gb_ref[c] * pb[:, _lanes(c - c0)]
            o_ref[:, _lanes(c)] = gated.astype(o_ref.dtype)


def _gated_proj(o_a, o_b, w_a, w_b, z_sl, bn, gate_a_blk, gate_b_blk):
    t, d_a = o_a.shape
    d_b = o_b.shape[1]
    d = w_a.shape[1]
    bm = _tile(t, MATMUL_TILE, SUBLANE)
    return pl.pallas_call(
        _gated_proj_kernel,
        grid=(t // bm, d // bn),
        in_specs=[pl.BlockSpec((bm, d_a), lambda i, j: (i, 0)),
                  pl.BlockSpec((bm, d_b), lambda i, j: (i, 0)),
                  pl.BlockSpec((d_a, bn), lambda i, j: (0, j)),
                  pl.BlockSpec((d_b, bn), lambda i, j: (0, j)),
                  pl.BlockSpec((bn // LANE, bm, LANE), lambda i, j: (gate_a_blk + j, i, 0)),
                  pl.BlockSpec((bn // LANE, bm, LANE), lambda i, j: (gate_b_blk + j, i, 0))],
        out_specs=pl.BlockSpec((bm, bn), lambda i, j: (i, j)),
        out_shape=jax.ShapeDtypeStruct((t, d), BF16),
        compiler_params=_params(("arbitrary", "arbitrary")),
        name="gated_proj",
    )(o_a, o_b, w_a, w_b, z_sl, z_sl)


def _matmul_kernel(x_ref, w_ref, o_ref, *, nk):
    if nk == 1:
        o_ref[...] = jnp.dot(x_ref[...], w_ref[...], preferred_element_type=F32)
    else:
        @pl.when(pl.program_id(2) == 0)
        def _():
            o_ref[...] = jnp.zeros(o_ref.shape, o_ref.dtype)

        o_ref[...] += jnp.dot(x_ref[...], w_ref[...], preferred_element_type=F32)


def _matmul(x, w, name, k_tile=None):
    t, kd = x.shape
    n = w.shape[1]
    bm = _tile(t, MATMUL_TILE, SUBLANE)
    bn = _tile(n, MATMUL_TILE, LANE)
    bk = kd if k_tile is None else k_tile
    nk = kd // bk
    return pl.pallas_call(
        functools.partial(_matmul_kernel, nk=nk),
        grid=(t // bm, n // bn, nk),
        in_specs=[pl.BlockSpec((bm, bk), lambda i, j, k: (i, k)),
                  pl.BlockSpec((bk, bn), lambda i, j, k: (k, j))],
        out_specs=pl.BlockSpec((bm, bn), lambda i, j, k: (i, j)),
        out_shape=jax.ShapeDtypeStruct((t, n), F32),
        compiler_params=_params(("arbitrary", "arbitrary", "arbitrary")),
        name=name,
    )(x, w)


def _ffn_down_round_kernel(x_ref, w_in, o_ref, w_out, w_buf, acc_ref, *, nk, k_rows):
    bb, i = pl.program_id(0), pl.program_id(1)
    ni = pl.num_programs(1)
    ck, bn = w_in.shape
    bm = x_ref.shape[0]
    slot = (bb + 1) % 2

    def round_chunk():
        last = pl.num_programs(0) - 1
        k_c = jnp.minimum(bb, last - 1) % nk
        i_c = jnp.where(bb == last, ni - 1, i)
        row = (k_c * ni + i_c) * ck + lax.broadcasted_iota(jnp.int32, (ck, bn), 0)
        chunk = jnp.where(row < k_rows, w_in[...], 0.0).astype(BF16)
        w_buf[bb % 2, pl.ds(pl.multiple_of(i * ck, ck), ck), :] = chunk
        w_out[...] = chunk

    pl.when(bb == 0)(round_chunk)

    @pl.when(bb > 0)
    def _():
        round_chunk()
        k = (bb - 1) % nk
        row_split = 2 if bm % (4 * SUBLANE) == 0 else 1
        hm = bm // row_split
        strip = min(bn, MXU_WIDTH)
        for h in range(row_split):
            rows = pl.ds(pl.multiple_of(i * bm + h * hm, hm), hm)
            x = x_ref[h * hm:(h + 1) * hm, :]
            for c0 in range(0, bn, strip):
                cols = slice(c0, c0 + strip)
                prev = jnp.where(k == 0, 0.0, acc_ref[rows, cols])
                total = prev + jnp.dot(x, w_buf[slot, :, cols], preferred_element_type=F32)
                acc_ref[rows, cols] = total
                o_ref[h * hm:(h + 1) * hm, cols] = total


def _ffn_down_round(x, w, k_tile, name):
    t, kp = x.shape
    k_rows, n = w.shape
    bm = _tile(t, MATMUL_TILE, SUBLANE)
    bn = _tile(n, FFN_DOWN_TILE, LANE)
    ni, nj, nk = t // bm, n // bn, kp // k_tile
    assert k_tile % (2 * SUBLANE * ni) == 0 and kp - k_rows < k_tile // ni
    ck = k_tile // ni
    nb = nj * nk

    def mm(bb, i):
        b = jnp.maximum(bb - 1, 0)
        return jnp.where(bb == 0, 0, i), b % nk, b // nk

    def chunk(bb, i):
        b = jnp.minimum(bb, nb - 1)
        return (b % nk) * ni + jnp.where(bb == nb, ni - 1, i), b // nk

    def out_idx(bb, i):
        i_m, k_m, j_m = mm(bb, i)
        return jnp.where(k_m == nk - 1, i_m, 0), j_m

    return pl.pallas_call(
        functools.partial(_ffn_down_round_kernel, nk=nk, k_rows=k_rows),
        grid=(nb + 1, ni),
        in_specs=[pl.BlockSpec((bm, k_tile), lambda bb, i: mm(bb, i)[:2]),
                  pl.BlockSpec((ck, bn), chunk)],
        out_specs=[pl.BlockSpec((bm, bn), out_idx), pl.BlockSpec((ck, bn), chunk)],
        out_shape=[jax.ShapeDtypeStruct((t, n), F32), jax.ShapeDtypeStruct((kp, n), BF16)],
        scratch_shapes=[pltpu.VMEM((2, k_tile, bn), BF16), pltpu.VMEM((t, bn), F32)],
        compiler_params=_params(("arbitrary", "arbitrary")),
        name=name,
    )(x, w)


def _post_mix_kernel(x_ref, y_ref, g1_ref, g2_ref, x1_ref, h2_ref):
    y = y_ref[...]
    x1 = x_ref[...] + (y * _rms_scale(y)) * g1_ref[...]
    x1_ref[...] = x1
    h2_ref[...] = ((x1 * _rms_scale(x1)) * g2_ref[...]).astype(h2_ref.dtype)


def _post_mix(x, y, g_post, g_pre):
    t, d = x.shape
    bt = _tile(t, 256, SUBLANE)
    row = pl.BlockSpec((bt, d), lambda i: (i, 0))
    vec = pl.BlockSpec((1, d), lambda i: (0, 0))
    return pl.pallas_call(
        _post_mix_kernel,
        grid=(t // bt,),
        in_specs=[row, row, vec, vec],
        out_specs=[row, row],
        out_shape=[jax.ShapeDtypeStruct((t, d), F32), jax.ShapeDtypeStruct((t, d), BF16)],
        compiler_params=_params(("arbitrary",)),
        name="post_mix",
    )(x, y, g_post.reshape(1, d), g_pre.reshape(1, d))


def _residual_norm_kernel(x_ref, y_ref, g_ref, o_ref):
    y = y_ref[...]
    o_ref[...] = x_ref[...] + (y * _rms_scale(y)) * g_ref[...]


def _residual_norm(x, y, g):
    t, d = x.shape
    bt = _tile(t, 256, SUBLANE)
    row = pl.BlockSpec((bt, d), lambda i: (i, 0))
    return pl.pallas_call(
        _residual_norm_kernel,
        grid=(t // bt,),
        in_specs=[row, row, pl.BlockSpec((1, d), lambda i: (0, 0))],
        out_specs=row,
        out_shape=jax.ShapeDtypeStruct((t, d), F32),
        compiler_params=_params(("arbitrary",)),
        name="residual_norm",
    )(x, y, g.reshape(1, d))


def _ffn_up_kernel(*refs, conv_w, s_len, d_ff):
    if s_len is None:
        (x_ref, wg_in, wu_in, cw_ref, cb_ref, out_ref, tail_ref, wg_out, wu_out,
         wg_buf, wu_buf, a_buf, u_buf, o_buf) = refs
        hist_refs = None
        jj, i = pl.program_id(0), pl.program_id(1)
        ck = wg_in.shape[0]

        def round_chunks():
            rows_c = pl.ds(pl.multiple_of(i * ck, ck), ck)
            for src, buf, dst in ((wg_in, wg_buf, wg_out), (wu_in, wu_buf, wu_out)):
                chunk = src[...].astype(BF16)
                buf[jj % 2, rows_c, :] = chunk
                dst[...] = chunk

        slot = (jj + 1) % 2
        wg = lambda cols: wg_buf[slot, :, cols]
        wu = lambda cols: wu_buf[slot, :, cols]
        col_blk = jj - 1
        row_split = 2 if out_ref.shape[0] % (4 * SUBLANE) == 0 else 1
    else:
        x_ref, wg_ref, wu_ref, cw_ref, cb_ref, h1_ref, h2_ref, out_ref, tail_ref, a_buf, u_buf, o_buf = refs
        hist_refs = {1: h1_ref, 2: h2_ref}
        i = pl.program_id(1)
        wg = lambda cols: wg_ref[:, cols]
        wu = lambda cols: wu_ref[:, cols]
        col_blk = pl.program_id(0)
        row_split = 1
    bm, bn = out_ref.shape
    hm = bm // row_split
    half = hm // 2
    top = SUBLANE
    n_slabs = bn // LANE
    per_strip = min(n_slabs, MXU_WIDTH // LANE)

    def units():
        @pl.when(i == 0)
        def _():
            a_buf[:, 0:top, :] = jnp.zeros((n_slabs, top, LANE), F32)

        lane = lax.broadcasted_iota(jnp.int32, (1, LANE), 1)
        for h in range(row_split):
            r0 = h * hm
            x = x_ref[r0:r0 + hm, :]
            for c0 in range(0, n_slabs, per_strip):
                c1 = min(c0 + per_strip, n_slabs)
                cols = slice(c0 * LANE, c1 * LANE)
                a = jnp.dot(x, wg(cols), preferred_element_type=F32)
                u = jnp.dot(x, wu(cols), preferred_element_type=F32)
                if s_len is None and h == 0 and c0 == 0:
                    round_chunks()
                if s_len is not None:
                    tail_ref[r0:r0 + hm, cols] = a
                elif h == row_split - 1:
                    tail_ref[:, cols] = a[hm - SUBLANE:, :]
                for c in range(c0, c1):
                    a_buf[c, top + r0:top + r0 + hm, :] = a[:, _lanes(c - c0)]
                    u_buf[c, r0:r0 + hm, :] = u[:, _lanes(c - c0)]
                    in_range = (col_blk * bn + c * LANE + lane) < d_ff
                    for e in range(2):
                        taps = [a_buf[c, pl.ds(top + r0 + e - lag, half, stride=2), :] for lag in range(conv_w)]
                        if hist_refs is not None:
                            s = 2 * (lax.broadcasted_iota(jnp.int32, (half, LANE), 0) % (s_len // 2)) + e
                            for lag in range(1, conv_w):
                                hist = hist_refs[lag][c, pl.ds(r0 + e, half, stride=2), :]
                                taps[lag] = jnp.where(s < lag, hist, taps[lag])
                        acc = cw_ref[0:1, _lanes(c)] * taps[conv_w - 1]
                        for k in range(1, conv_w):
                            acc = acc + cw_ref[k:k + 1, _lanes(c)] * taps[conv_w - 1 - k]
                        acc = cb_ref[:, _lanes(c)] + acc
                        val = _gelu(acc) * u_buf[c, pl.ds(r0 + e, half, stride=2), :]
                        o_buf[c, pl.ds(r0 + e, half, stride=2), :] = jnp.where(in_range, val, 0.0)
                    out_ref[r0:r0 + hm, _lanes(c)] = o_buf[c, r0:r0 + hm, :].astype(out_ref.dtype)
        if s_len is None:
            a_buf[:, 0:top, :] = a_buf[:, bm:bm + top, :]

    if s_len is None:
        pl.when(jj == 0)(round_chunks)
        pl.when(jj > 0)(units)
    else:
        units()


def _ffn_up_prompt(h, w_gate, w_up, conv_w, conv_b, ffp):
    t, d = h.shape
    d_ff = w_gate.shape[1]
    cw = conv_w.shape[0]
    assert cw == 3 and d_ff % LANE == 0
    bn = FFN_TILE
    assert ffp % bn == 0 and ffp - d_ff < bn
    bm = _tile(t, MATMUL_TILE, 4 * SUBLANE)
    ni, nj = t // bm, ffp // bn
    assert d % ni == 0
    ck = d // ni
    n_slabs = bn // LANE

    def fin_i(jj, i):
        return jnp.where(jj == 0, 0, i)

    def fin_j(jj):
        return jnp.maximum(jj - 1, 0)

    def chunk(jj, i):
        return jnp.where(jj == nj, ni - 1, i), jnp.minimum(jj, nj - 1)

    w_in_spec = pl.BlockSpec((ck, bn), chunk)
    wbuf = pltpu.VMEM((2, d, bn), BF16)
    act, tail, wg16, wu16 = pl.pallas_call(
        functools.partial(_ffn_up_kernel, conv_w=cw, s_len=None, d_ff=d_ff),
        grid=(nj + 1, ni),
        in_specs=[pl.BlockSpec((bm, d), lambda jj, i: (fin_i(jj, i), 0)), w_in_spec, w_in_spec,
                  pl.BlockSpec((cw, bn), lambda jj, i: (0, fin_j(jj))),
                  pl.BlockSpec((1, bn), lambda jj, i: (0, fin_j(jj)))],
        out_specs=[pl.BlockSpec((bm, bn), lambda jj, i: (fin_i(jj, i), fin_j(jj))),
                   pl.BlockSpec((SUBLANE, bn), lambda jj, i: (0, fin_j(jj))),
                   w_in_spec, w_in_spec],
        out_shape=[jax.ShapeDtypeStruct((t, ffp), BF16), jax.ShapeDtypeStruct((SUBLANE, d_ff), F32),
                   jax.ShapeDtypeStruct((d, d_ff), BF16), jax.ShapeDtypeStruct((d, d_ff), BF16)],
        scratch_shapes=[wbuf, wbuf,
                        pltpu.VMEM((n_slabs, bm + SUBLANE, LANE), F32), pltpu.VMEM((n_slabs, bm, LANE), F32),
                        pltpu.VMEM((n_slabs, bm, LANE), F32)],
        compiler_params=_params(("arbitrary", "arbitrary")),
        name="ffn_up",
    )(h, w_gate, w_up, conv_w, conv_b.reshape(1, d_ff))
    return act, tail, wg16, wu16


def _ffn_up_sample(h, w_gate, w_up, conv_w, conv_b, ffp, hist, s_len):
    t, d = h.shape
    d_ff = w_gate.shape[1]
    cw = conv_w.shape[0]
    assert cw == 3 and d_ff % LANE == 0 and s_len % 2 == 0
    bn = _tile(ffp, MATMUL_TILE, LANE)
    assert ffp - d_ff < bn
    n_slabs = bn // LANE
    w_spec = pl.BlockSpec((d, bn), lambda j, i: (0, j))
    out_spec = pl.BlockSpec((t, bn), lambda j, i: (i, j))

    def slabs(rows):
        return jnp.transpose(rows.reshape(t, d_ff // LANE, LANE), (1, 0, 2))
    h1 = slabs(jnp.pad(hist[:, 1:2], ((0, 0), (0, s_len - 1), (0, 0))))
    h2 = slabs(jnp.pad(hist, ((0, 0), (0, s_len - hist.shape[1]), (0, 0))))
    hist_spec = pl.BlockSpec((n_slabs, t, LANE), lambda j, i: (j, i, 0))
    return pl.pallas_call(
        functools.partial(_ffn_up_kernel, conv_w=cw, s_len=s_len, d_ff=d_ff),
        grid=(ffp // bn, 1),
        in_specs=[pl.BlockSpec((t, d), lambda j, i: (i, 0)), w_spec, w_spec,
                  pl.BlockSpec((cw, bn), lambda j, i: (0, j)), pl.BlockSpec((1, bn), lambda j, i: (0, j)),
                  hist_spec, hist_spec],
        out_specs=[out_spec, out_spec],
        out_shape=[jax.ShapeDtypeStruct((t, ffp), BF16), jax.ShapeDtypeStruct((t, d_ff), F32)],
        scratch_shapes=[pltpu.VMEM((n_slabs, t + SUBLANE, LANE), F32), pltpu.VMEM((n_slabs, t, LANE), F32),
                        pltpu.VMEM((n_slabs, t, LANE), F32)],
        compiler_params=_params(("arbitrary", "arbitrary")),
        name="ffn_up_decode",
    )(h, w_gate, w_up, conv_w, conv_b.reshape(1, d_ff), h1, h2)


def _layer(x, p, s_len=None, caches=None, conv_hist=None):
    t, d = x.shape
    d_a = p["ln_g"].shape[0]
    hpg = p["hpg"]
    gw = hpg * HEAD_DIM
    bn = p["bn"]
    o1 = 2 * d_a
    d_qkv = 3 * gw

    h = _rmsnorm_cast(x, p["g_pre_mix"])
    w16 = dict(p.get("w16", {}))
    z_parts = []
    for name, col0, n_cols, act in (("in_proj_sgu", 0, o1, _gelu),
                                    ("in_proj_qkv", o1, 3 * d_qkv, lambda v: v),
                                    ("in_proj_gates", o1 + 3 * d_qkv, 2 * d, _sigmoid)):
        if s_len is None:
            z_part, w16[name] = _in_proj_round(h, p["w_in"], col0, n_cols, act, name)
        else:
            z_part = _in_proj(h, w16[name], bn, 0, n_cols, act, name + "_decode")
        z_parts.append(z_part)
    za_sl, qkv_sl, gate_sl = z_parts

    if s_len is None:
        w_mix, b_mix = p["w_mix_prompt"], p["b_mix_prompt"]
    else:
        w_mix, b_mix = p["w_mix_sample"], p["b_mix_sample"]
    o_a, v_state = _sgu(za_sl, d_a, p["ln_g"], p["ln_b"], w_mix, b_mix)

    outs, lses = [], []
    for gi, (_, dil) in enumerate(DILATION_GROUPS):
        q_slab = gi * hpg
        k_slab = d_qkv // LANE + gi * hpg
        v_slab = 2 * d_qkv // LANE + gi * hpg
        bias_tab = p["rel_bias"][:, gi * hpg:(gi + 1) * hpg]
        if s_len is not None:
            o, lse = _attn_sample(qkv_sl, s_len, hpg, q_slab, k_slab, v_slab, caches[2 * gi], caches[2 * gi + 1],
                                  bias_tab, dil)
        elif gi < len(DILATION_GROUPS) - 1:
            o, lse = _attn_prompt(qkv_sl, hpg, q_slab, k_slab, v_slab, bias_tab, dil)
        else:
            o_b = _attn_prompt(qkv_sl, hpg, q_slab, k_slab, v_slab, bias_tab, dil, merge_with=(outs, lses))
            break
        outs.append(o)
        lses.append(lse)
    if s_len is not None:
        o_b = _merge(outs, lses)

    merged = _gated_proj(o_a, o_b, p["w_proj_a"], p["w_proj_b"], gate_sl, bn, 0, d // bn)
    if s_len is None:
        y, w16["out_proj"] = _in_proj_round(merged, p["w_out"], 0, d, lambda v: v, "out_proj", slab_out=False)
    else:
        y = _matmul(merged, w16["out_proj"], "out_proj_decode")
    x1, h2 = _post_mix(x, y, p["g_post_mix"], p["g_pre_ffn"])

    if s_len is None:
        act, a_tail, w16["ffn_gate"], w16["ffn_up"] = _ffn_up_prompt(h2, p["w_gate"], p["w_up"], p["conv_w"],
                                                                     p["conv_b"], p["ffp"])
    else:
        act, a_tail = _ffn_up_sample(h2, w16["ffn_gate"], w16["ffn_up"], p["conv_w"], p["conv_b"], p["ffp"],
                                     conv_hist, s_len)
    if s_len is None:
        f, w16["ffn_down"] = _ffn_down_round(act, p["w_down"], p["ffn_k_tile"], "ffn_down")
    else:
        f = _matmul(act, w16["ffn_down"], "ffn_down_decode", k_tile=p["ffn_k_tile"])
    y_out = _residual_norm(x1, f, p["g_post_ffn"])
    return y_out, qkv_sl, v_state, a_tail, w16


def kernel(x_prompt, x_sample, cache_k_g1, cache_v_g1, cache_k_g2, cache_v_g2, cache_k_g3, cache_v_g3, state_conv, g_pre_mix, w_in, sgu_ln_g, sgu_ln_b, w_spatial, b_spatial, rel_bias, w_proj_a, w_proj_b, w_out, g_post_mix, g_pre_ffn, w_gate, w_up, conv_w, conv_b, w_down, g_post_ffn):
    depth = w_in.shape[0]
    assert depth == 1
    n_prompt, seq, d = x_prompt.shape
    assert n_prompt == 1 and seq % CHUNK == 0
    n_seq, s_len, _ = x_sample.shape
    assert s_len == SUBLANE
    d_a = sgu_ln_g.shape[1]
    n_groups = w_spatial.shape[1]
    n_heads = rel_bias.shape[1]
    hpg = n_heads // len(DILATION_GROUPS)
    gw = hpg * HEAD_DIM
    d_ff = w_gate.shape[2]
    cw = conv_w.shape[1]
    bn = _tile(math.gcd(2 * d_a, gw), MATMUL_TILE, LANE)
    ffp = -(-d_ff // FFN_TILE) * FFN_TILE
    t_s = n_seq * s_len

    tri = np.tril(np.ones((CHUNK, CHUNK), np.float32))
    w_mix_prompt = (w_spatial[0] * tri).astype(BF16)
    b_mix_prompt = jnp.broadcast_to(b_spatial[0][:, :, None], (n_groups, CHUNK, LANE))
    w_small = w_spatial[0][:, :s_len, :s_len] * tri[:s_len, :s_len]
    eye = np.eye(n_seq, dtype=np.float32)
    w_mix_sample = jnp.einsum("ab,gpq->gapbq", eye, w_small).reshape(n_groups, t_s, t_s).astype(BF16)
    b_mix_sample = jnp.broadcast_to(jnp.tile(b_spatial[0][:, :s_len], (1, n_seq))[:, :, None], (n_groups, t_s, LANE))

    p = dict(
        hpg=hpg, bn=bn, ffp=ffp,
        g_pre_mix=g_pre_mix[0], w_in=w_in[0], ln_g=sgu_ln_g[0], ln_b=sgu_ln_b[0],
        w_mix_prompt=w_mix_prompt, b_mix_prompt=b_mix_prompt,
        w_mix_sample=w_mix_sample, b_mix_sample=b_mix_sample,
        rel_bias=rel_bias,
        w_proj_a=w_proj_a[0].astype(BF16), w_proj_b=w_proj_b[0].astype(BF16), w_out=w_out[0],
        g_post_mix=g_post_mix[0], g_pre_ffn=g_pre_ffn[0],
        w_gate=w_gate[0], w_up=w_up[0],
        conv_w=conv_w[0], conv_b=conv_b[0],
        w_down=w_down[0],
        ffn_k_tile=_tile(ffp, 3072, LANE),
        g_post_ffn=g_post_ffn[0],
    )

    caches = tuple(c[0] for c in (cache_k_g1, cache_v_g1, cache_k_g2, cache_v_g2, cache_k_g3, cache_v_g3))

    yp, zp, vp_state, ap_tail, w16 = _layer(x_prompt[0], p)
    ys, zs, vs_state, as_all, _ = _layer(x_sample.reshape(t_s, d), dict(p, w16=w16), s_len=s_len,
                                         caches=caches, conv_hist=state_conv[0])

    d_qkv = 3 * gw
    prompt_kv, sample_kv = [], []
    for gi, (win, _) in enumerate(DILATION_GROUPS):
        keep = min(win, seq)
        for base in (d_qkv, 2 * d_qkv):
            s0 = (base + gi * gw) // LANE
            pk = jnp.transpose(zp[s0:s0 + hpg, seq - keep:, :], (1, 0, 2))
            prompt_kv.append(pk.reshape(1, 1, keep, hpg, HEAD_DIM))
            sk = jnp.transpose(zs[s0:s0 + hpg], (1, 0, 2))
            sample_kv.append(sk.reshape(1, n_seq, s_len, hpg, HEAD_DIM))
    p_conv = ap_tail[SUBLANE - (cw - 1):].reshape(1, 1, cw - 1, d_ff)
    s_conv = as_all.reshape(n_seq, s_len, d_ff)[:, s_len - (cw - 1):].reshape(1, n_seq, cw - 1, d_ff)
    return (yp.reshape(1, seq, d), ys.reshape(n_seq, s_len, d),
            *prompt_kv, vp_state.reshape(1, 1, CHUNK, d_a), p_conv,
            *sample_kv, vs_state.reshape(1, n_seq, s_len, d_a), s_conv)
```

```python
import functools
import math

import numpy as np
import jax
import jax.numpy as jnp
from jax import lax
from jax.experimental import pallas as pl
from jax.experimental.pallas import tpu as pltpu

F32 = jnp.float32
BF16 = jnp.bfloat16

HEAD_DIM = 128
STEPS = 128
CHUNK = 128
GROUP_DIM_A = 128
DILATION_GROUPS = ((128, 1), (512, 4), (2048, 16))
MAX_EXACT = 16
MAX_DISTANCE = 2048
EPS = 1e-6
NEG = -1e30
ATTN_SCALE = HEAD_DIM ** -0.5

LANE = 128
SUBLANE = 8
VMEM_LIMIT_BYTES = 56 * 1024 * 1024
MATMUL_TILE = 1024
MXU_WIDTH = 256
FFN_TILE = 512
FFN_DOWN_TILE = 512
IN_PROJ_ROUND_TILE = 1024
ATTN_UNITS_PER_STEP = 16
SGU_ROWS_PER_STEP = 512


def _tile(dim, target, align):
    best = None
    t = align
    while t <= min(dim, target):
        if dim % t == 0:
            best = t
        t += align
    return best if best is not None else dim


def _params(semantics):
    return pltpu.CompilerParams(dimension_semantics=semantics, vmem_limit_bytes=VMEM_LIMIT_BYTES)


def _gelu(x):
    return 0.5 * x * (1.0 + jnp.tanh(math.sqrt(2.0 / math.pi) * (x + 0.044715 * (x * x * x))))


def _sigmoid(x):
    return 1.0 / (1.0 + jnp.exp(-x))


def _rms_scale(x):
    return lax.rsqrt(jnp.mean(x * x, axis=-1, keepdims=True) + EPS)


def _lanes(c):
    return slice(c * LANE, (c + 1) * LANE)


def _rmsnorm_cast_kernel(x_ref, g_ref, o_ref):
    x = x_ref[...]
    o_ref[...] = ((x * _rms_scale(x)) * g_ref[...]).astype(o_ref.dtype)


def _rmsnorm_cast(x, g):
    t, d = x.shape
    bt = _tile(t, 256, SUBLANE)
    return pl.pallas_call(
        _rmsnorm_cast_kernel,
        grid=(t // bt,),
        in_specs=[pl.BlockSpec((bt, d), lambda i: (i, 0)), pl.BlockSpec((1, d), lambda i: (0, 0))],
        out_specs=pl.BlockSpec((bt, d), lambda i: (i, 0)),
        out_shape=jax.ShapeDtypeStruct((t, d), BF16),
        compiler_params=_params(("arbitrary",)),
        name="rmsnorm_cast",
    )(x, g.reshape(1, d))


def _in_proj_kernel(x_ref, w_ref, o_ref, *, act):
    acc = jnp.dot(x_ref[...], w_ref[...], preferred_element_type=F32)
    for c in range(o_ref.shape[0]):
        o_ref[c] = act(acc[:, _lanes(c)])


def _in_proj_round_kernel(x_ref, w_in, o_ref, w_out, w_buf, *, act, slab_out):
    jj, i = pl.program_id(0), pl.program_id(1)
    ck = w_in.shape[0]
    slot = (jj + 1) % 2
    bm = o_ref.shape[-2]
    n_slabs = w_in.shape[1] // LANE
    row_split = 2 if bm % (4 * SUBLANE) == 0 else 1
    hm = bm // row_split
    per_strip = min(n_slabs, MXU_WIDTH // LANE)

    def round_chunk():
        chunk = w_in[...].astype(BF16)
        w_buf[jj % 2, pl.ds(pl.multiple_of(i * ck, ck), ck), :] = chunk
        w_out[...] = chunk

    pl.when(jj == 0)(round_chunk)

    @pl.when(jj > 0)
    def _():
        round_chunk()
        for h in range(row_split):
            x = x_ref[h * hm:(h + 1) * hm, :]
            for c0 in range(0, n_slabs, per_strip):
                c1 = min(c0 + per_strip, n_slabs)
                acc = jnp.dot(x, w_buf[slot, :, c0 * LANE:c1 * LANE], preferred_element_type=F32)
                for c in range(c0, c1):
                    val = act(acc[:, _lanes(c - c0)])
                    if slab_out:
                        o_ref[c, h * hm:(h + 1) * hm, :] = val
                    else:
                        o_ref[h * hm:(h + 1) * hm, _lanes(c)] = val


def _in_proj_round(h, w, col0, n_cols, act, name, slab_out=True):
    t, d = h.shape
    bn = _tile(math.gcd(n_cols, col0) if col0 else n_cols, IN_PROJ_ROUND_TILE, LANE)
    bm = _tile(t, MATMUL_TILE, SUBLANE)
    ni, nj = t // bm, n_cols // bn
    assert d % ni == 0
    ck = d // ni

    def fin(jj, i):
        return jnp.maximum(jj - 1, 0), jnp.where(jj == 0, 0, i)

    def chunk(jj, i):
        return jnp.where(jj == nj, ni - 1, i), jnp.minimum(jj, nj - 1)

    if slab_out:
        out_spec = pl.BlockSpec((bn // LANE, bm, LANE), lambda jj, i: fin(jj, i) + (0,))
        out_shape = jax.ShapeDtypeStruct((n_cols // LANE, t, LANE), F32)
    else:
        out_spec = pl.BlockSpec((bm, bn), lambda jj, i: fin(jj, i)[::-1])
        out_shape = jax.ShapeDtypeStruct((t, n_cols), F32)
    return pl.pallas_call(
        functools.partial(_in_proj_round_kernel, act=act, slab_out=slab_out),
        grid=(nj + 1, ni),
        in_specs=[pl.BlockSpec((bm, d), lambda jj, i: (fin(jj, i)[1], 0)),
                  pl.BlockSpec((ck, bn), lambda jj, i: (chunk(jj, i)[0], col0 // bn + chunk(jj, i)[1]))],
        out_specs=[out_spec, pl.BlockSpec((ck, bn), chunk)],
        out_shape=[out_shape, jax.ShapeDtypeStruct((d, n_cols), BF16)],
        scratch_shapes=[pltpu.VMEM((2, d, bn), BF16)],
        compiler_params=_params(("arbitrary", "arbitrary")),
        name=name,
    )(h, w)


def _in_proj(h, w, bn, col0, n_cols, act, name):
    t, d = h.shape
    bm = _tile(t, MATMUL_TILE, SUBLANE)
    return pl.pallas_call(
        functools.partial(_in_proj_kernel, act=act),
        grid=(t // bm, n_cols // bn),
        in_specs=[pl.BlockSpec((bm, d), lambda i, j: (i, 0)),
                  pl.BlockSpec((d, bn), lambda i, j: (0, col0 // bn + j))],
        out_specs=pl.BlockSpec((bn // LANE, bm, LANE), lambda i, j: (j, i, 0)),
        out_shape=jax.ShapeDtypeStruct((n_cols // LANE, t, LANE), F32),
        compiler_params=_params(("arbitrary", "arbitrary")),
        name=name,
    )(h, w)


def _sgu_kernel(u_ref, v_ref, lg_ref, lb_ref, w_ref, b_ref, o_ref, vs_ref, *, n_groups):
    c = w_ref.shape[1]
    n_feat = n_groups * GROUP_DIM_A
    for r0 in range(0, o_ref.shape[0], c):
        vp = v_ref[:, r0:r0 + c, :]
        mu = jnp.sum(jnp.sum(vp, axis=0), axis=-1, keepdims=True) / n_feat
        vc = vp - mu
        var = jnp.sum(jnp.sum(vc * vc, axis=0), axis=-1, keepdims=True) / n_feat
        v = vc * lax.rsqrt(var + EPS) * lg_ref[...] + lb_ref[...]
        for g in range(n_groups):
            if r0 + c == o_ref.shape[0]:
                vs_ref[:, _lanes(g)] = v[g]
            mixed = jnp.dot(w_ref[g], v[g].astype(BF16), preferred_element_type=F32) + b_ref[g]
            o_ref[r0:r0 + c, _lanes(g)] = (u_ref[g, r0:r0 + c, :] * mixed).astype(o_ref.dtype)


def _sgu(z_sl, d_a, ln_g, ln_b, w_mix, b_mix):
    t = z_sl.shape[1]
    n_groups, c, _ = w_mix.shape
    rows = _tile(t, SGU_ROWS_PER_STEP, c)
    slab = pl.BlockSpec((n_groups, 1, LANE), lambda i: (0, 0, 0))
    return pl.pallas_call(
        functools.partial(_sgu_kernel, n_groups=n_groups),
        grid=(t // rows,),
        in_specs=[
            pl.BlockSpec((n_groups, rows, LANE), lambda i: (0, i, 0)),
            pl.BlockSpec((n_groups, rows, LANE), lambda i: (1, i, 0)),
            slab, slab,
            pl.BlockSpec((n_groups, c, c), lambda i: (0, 0, 0)),
            pl.BlockSpec((n_groups, c, LANE), lambda i: (0, 0, 0)),
        ],
        out_specs=[pl.BlockSpec((rows, d_a), lambda i: (i, 0)), pl.BlockSpec((c, d_a), lambda i: (0, 0))],
        out_shape=[jax.ShapeDtypeStruct((t, d_a), BF16), jax.ShapeDtypeStruct((c, d_a), F32)],
        compiler_params=_params(("arbitrary",)),
        name="sgu",
    )(z_sl, z_sl, ln_g.reshape(n_groups, 1, LANE), ln_b.reshape(n_groups, 1, LANE), w_mix, b_mix)


def _t5_bucket(dist, n_buckets):
    n = np.asarray(dist, np.int32)
    safe = np.maximum(n, 1).astype(np.float32)
    large = MAX_EXACT + (np.log(safe / MAX_EXACT) / np.log(np.float32(MAX_DISTANCE / MAX_EXACT))
                         * (n_buckets - MAX_EXACT)).astype(np.int32)
    large = np.minimum(large, n_buckets - 1)
    return np.where(n < MAX_EXACT, n, large).astype(np.int32)


def _bias_lookup(bias_tab, bucket):
    n_buckets = bias_tab.shape[0]
    flat = np.asarray(bucket).reshape(-1)
    onehot = (jnp.asarray(flat)[None, :] == jnp.arange(n_buckets)[:, None]).astype(F32)
    out = jnp.dot(bias_tab.T, onehot, precision=lax.Precision.HIGHEST)
    return out.reshape((bias_tab.shape[1],) + tuple(np.asarray(bucket).shape))


def _merge_by_lse(outs, lses):
    m = functools.reduce(jnp.maximum, lses)
    w = [jnp.exp(l - m) for l in lses]
    num = functools.reduce(lambda x, y: x + y, [wi * oi for wi, oi in zip(w, outs)])
    return num / functools.reduce(lambda x, y: x + y, w)


def _attn_prompt_kernel(*refs, hb, dil, n_other):
    q_ref, kc_ref, vc_ref, bias_ref, band_ref = refs[:5]
    others = refs[5:5 + 2 * n_other]
    n_out = 1 if n_other else 2
    outs = refs[5 + 2 * n_other:5 + 2 * n_other + n_out]
    kp_ref, vp_ref = refs[5 + 2 * n_other + n_out:][:2]
    o_acc, lse_acc = refs[5 + 2 * n_other + n_out + 2:] if n_other else outs
    b = pl.program_id(1)

    @pl.when(b == 0)
    def _():
        kp_ref[...] = jnp.zeros(kp_ref.shape, F32)
        vp_ref[...] = jnp.zeros(vp_ref.shape, F32)

    span = STEPS * dil
    n_pb = q_ref.shape[1] // span
    col = lax.broadcasted_iota(jnp.int32, (STEPS, 2 * STEPS), 1)
    band = band_ref[...] > 0.5
    first = band & ((b > 0) | (col >= STEPS))
    for hh in range(hb):
        for pb in range(n_pb):
            for r in range(dil):
                def rows(block):
                    return pl.ds(block * span + r, STEPS, stride=dil) if dil > 1 else pl.ds(block * span, STEPS)
                k_prev = kp_ref[hh, rows(0), :] if pb == 0 else kc_ref[hh, rows(pb - 1), :]
                v_prev = vp_ref[hh, rows(0), :] if pb == 0 else vc_ref[hh, rows(pb - 1), :]
                q = q_ref[hh, rows(pb), :].astype(BF16)
                kk = jnp.concatenate([k_prev, kc_ref[hh, rows(pb), :]], axis=0).astype(BF16)
                vv = jnp.concatenate([v_prev, vc_ref[hh, rows(pb), :]], axis=0).astype(BF16)
                s = lax.dot_general(q, kk, (((1,), (1,)), ((), ())), preferred_element_type=F32) * ATTN_SCALE
                s = jnp.where(first if pb == 0 else band, s + bias_ref[hh], NEG)
                m = jnp.max(s, axis=-1, keepdims=True)
                e = jnp.exp(s - m)
                den = jnp.sum(e, axis=-1, keepdims=True)
                o_acc[hh, rows(pb), :] = jnp.dot(e.astype(BF16), vv, preferred_element_type=F32) / den
                lse_acc[hh, rows(pb), :] = jnp.broadcast_to(m + jnp.log(den), (STEPS, HEAD_DIM))
    kp_ref[...] = kc_ref[:, (n_pb - 1) * span:, :]
    vp_ref[...] = vc_ref[:, (n_pb - 1) * span:, :]
    if n_other:
        for hh in range(hb):
            merged = _merge_by_lse([r[hh] for r in others[:n_other]] + [o_acc[hh]],
                                   [r[hh] for r in others[n_other:]] + [lse_acc[hh]])
            outs[0][:, _lanes(hh)] = merged.astype(outs[0].dtype)


def _attn_prompt(z_sl, hpg, q_slab, k_slab, v_slab, bias_tab, dil, merge_with=None):
    t = z_sl.shape[1]
    span = STEPS * dil
    assert t % span == 0
    hb = min(hpg, max(1, ATTN_UNITS_PER_STEP // dil))
    assert hpg % hb == 0 and q_slab % hb == 0 and k_slab % hb == 0 and v_slab % hb == 0
    n_pb = math.gcd(t // span, max(1, ATTN_UNITS_PER_STEP // (hb * dil)))
    rows = span * n_pb
    nb = t // rows

    p_idx = np.arange(STEPS)[:, None]
    c_idx = np.arange(2 * STEPS)[None, :]
    steps = p_idx + STEPS - c_idx
    band = ((steps >= 0) & (steps <= STEPS)).astype(np.float32)
    bias = _bias_lookup(bias_tab, _t5_bucket(np.clip(steps, 0, STEPS) * dil, bias_tab.shape[0]))

    def cur(slab):
        return pl.BlockSpec((hb, rows, LANE), lambda hi, b: (slab // hb + hi, b, 0))

    head_major = pl.BlockSpec((hb, rows, LANE), lambda hi, b: (hi, b, 0))
    prev_block = pltpu.VMEM((hb, span, LANE), F32)
    block = pltpu.VMEM((hb, rows, LANE), F32)
    others = [] if merge_with is None else list(merge_with[0]) + list(merge_with[1])
    if merge_with is None:
        out_specs = [head_major, head_major]
        out_shape = [jax.ShapeDtypeStruct((hpg, t, LANE), F32)] * 2
        scratch = [prev_block, prev_block]
    else:
        out_specs = [pl.BlockSpec((rows, hb * LANE), lambda hi, b: (b, hi))]
        out_shape = [jax.ShapeDtypeStruct((t, hpg * LANE), BF16)]
        scratch = [prev_block, prev_block, block, block]
    res = pl.pallas_call(
        functools.partial(_attn_prompt_kernel, hb=hb, dil=dil, n_other=len(others) // 2),
        grid=(hpg // hb, nb),
        in_specs=[cur(q_slab), cur(k_slab), cur(v_slab),
                  pl.BlockSpec((hb, STEPS, 2 * STEPS), lambda hi, b: (hi, 0, 0)),
                  pl.BlockSpec((STEPS, 2 * STEPS), lambda hi, b: (0, 0))] + [head_major] * len(others),
        out_specs=out_specs,
        out_shape=out_shape,
        scratch_shapes=scratch,
        compiler_params=_params(("arbitrary", "arbitrary")),
        name=f"attn_prompt_d{dil}",
    )(z_sl, z_sl, z_sl, bias, jnp.asarray(band), *others)
    return res[0] if merge_with is not None else res


def _attn_sample_kernel(q_ref, kn_ref, vn_ref, ck_ref, cv_ref, bc_ref, mc_ref, bnew_ref, mnew_ref,
                        o_ref, lse_ref, *, hpg, s_len, grouped):
    n_keys = mc_ref.shape[1]
    n_new = hpg * s_len
    valid_c = mc_ref[...] > 0.5

    def cache_head(ref, h):
        if grouped:
            return ref[0, :, pl.ds(h, s_len, stride=hpg), :].reshape(n_keys, HEAD_DIM).astype(BF16)
        return ref[0, pl.ds(h, n_keys, stride=hpg) if hpg > 1 else pl.ds(0, n_keys), :].astype(BF16)

    q_all = q_ref[...].reshape(n_new, HEAD_DIM).astype(BF16)
    s_new = lax.dot_general(q_all, kn_ref[...].reshape(n_new, HEAD_DIM).astype(BF16), (((1,), (1,)), ((), ())),
                            preferred_element_type=F32) * ATTN_SCALE
    key_head = lax.broadcasted_iota(jnp.int32, (s_len, n_new), 1) // s_len
    new_ok = mnew_ref[...] > 0.5

    partial, e_new = [], []
    for h in range(hpg):
        q = q_all[h * s_len:(h + 1) * s_len, :]
        sc = lax.dot_general(q, cache_head(ck_ref, h), (((1,), (1,)), ((), ())),
                             preferred_element_type=F32) * ATTN_SCALE
        sc = jnp.where(valid_c, sc + bc_ref[h], NEG)
        sn = jnp.where(new_ok & (key_head == h), s_new[h * s_len:(h + 1) * s_len, :] + bnew_ref[h], NEG)
        m = jnp.maximum(jnp.max(sc, axis=-1, keepdims=True), jnp.max(sn, axis=-1, keepdims=True))
        ec = jnp.exp(sc - m)
        en = jnp.exp(sn - m)
        den = jnp.sum(ec, axis=-1, keepdims=True) + jnp.sum(en, axis=-1, keepdims=True)
        acc = jnp.dot(ec.astype(BF16), cache_head(cv_ref, h), preferred_element_type=F32)
        partial.append((acc, den, m))
        e_new.append(en)
    o_new = jnp.dot(jnp.concatenate(e_new, axis=0).astype(BF16),
                    vn_ref[...].reshape(n_new, HEAD_DIM).astype(BF16), preferred_element_type=F32)
    for h, (acc, den, m) in enumerate(partial):
        o_ref[h] = (acc + o_new[h * s_len:(h + 1) * s_len, :]) / den
        lse_ref[h] = jnp.broadcast_to(m + jnp.log(den), (s_len, HEAD_DIM))


def _attn_sample(z_sl, s_len, hpg, q_slab, k_slab, v_slab, cache_k, cache_v, bias_tab, dil):
    t = z_sl.shape[1]
    n_seq = t // s_len
    lc = cache_k.shape[1]
    assert q_slab % hpg == 0 and k_slab % hpg == 0 and v_slab % hpg == 0

    j = np.arange(STEPS + 1)
    idx = lc + np.arange(s_len)[:, None] - j[None, :] * dil
    assert idx.min() >= 0
    bucket = _t5_bucket(j * dil, bias_tab.shape[0])
    mask = np.zeros((s_len, lc + s_len), np.float32)
    bsel = np.zeros((s_len, lc + s_len), np.int32)
    for s in range(s_len):
        mask[s, idx[s]] = 1.0
        bsel[s, idx[s]] = bucket

    grouped = dil > s_len and lc % dil == 0 and s_len == SUBLANE
    if grouped:
        pos = (np.arange(lc // dil)[:, None] * dil + np.arange(s_len)[None, :]).reshape(-1)
        assert mask[:, :lc].sum() == mask[:, pos].sum()
        ck = cache_k.reshape(n_seq, lc // dil, dil * hpg, HEAD_DIM)
        cv = cache_v.reshape(n_seq, lc // dil, dil * hpg, HEAD_DIM)
        cache_spec = pl.BlockSpec((1, lc // dil, s_len * hpg, LANE), lambda b: (b, 0, 0, 0))
    else:
        pos = np.arange(lc)
        ck = cache_k.reshape(n_seq, lc * hpg, HEAD_DIM)
        cv = cache_v.reshape(n_seq, lc * hpg, HEAD_DIM)
        cache_spec = pl.BlockSpec((1, lc * hpg, LANE), lambda b: (b, 0, 0))
    n_keys = len(pos)
    bias_c = _bias_lookup(bias_tab, bsel[:, pos])
    bias_new = jnp.tile(_bias_lookup(bias_tab, bsel[:, lc:]), (1, 1, hpg))
    mask_c = jnp.asarray(mask[:, pos])
    mask_new = jnp.asarray(np.tile(mask[:, lc:], (1, hpg)))

    out_spec = pl.BlockSpec((hpg, s_len, LANE), lambda b: (0, b, 0))
    return pl.pallas_call(
        functools.partial(_attn_sample_kernel, hpg=hpg, s_len=s_len, grouped=grouped),
        grid=(n_seq,),
        in_specs=[pl.BlockSpec((hpg, s_len, LANE), lambda b: (q_slab // hpg, b, 0)),
                  pl.BlockSpec((hpg, s_len, LANE), lambda b: (k_slab // hpg, b, 0)),
                  pl.BlockSpec((hpg, s_len, LANE), lambda b: (v_slab // hpg, b, 0)),
                  cache_spec, cache_spec,
                  pl.BlockSpec((hpg, s_len, n_keys), lambda b: (0, 0, 0)),
                  pl.BlockSpec((s_len, n_keys), lambda b: (0, 0)),
                  pl.BlockSpec((hpg, s_len, hpg * s_len), lambda b: (0, 0, 0)),
                  pl.BlockSpec((s_len, hpg * s_len), lambda b: (0, 0))],
        out_specs=[out_spec, out_spec],
        out_shape=[jax.ShapeDtypeStruct((hpg, t, LANE), F32)] * 2,
        compiler_params=_params(("arbitrary",)),
        name=f"attn_sample_d{dil}",
    )(z_sl, z_sl, z_sl, ck, cv, bias_c, mask_c, bias_new, mask_new)


def _merge_kernel(o1, o2, o3, l1, l2, l3, out_ref):
    merged = _merge_by_lse([o1[...], o2[...], o3[...]], [l1[...], l2[...], l3[...]])
    for h in range(merged.shape[0]):
        out_ref[:, _lanes(h)] = merged[h].astype(out_ref.dtype)


def _merge(outs, lses):
    hpg, t, _ = outs[0].shape
    bt = _tile(t, 256, SUBLANE)
    spec = pl.BlockSpec((hpg, bt, LANE), lambda i: (0, i, 0))
    return pl.pallas_call(
        _merge_kernel,
        grid=(t // bt,),
        in_specs=[spec] * 6,
        out_specs=pl.BlockSpec((bt, hpg * LANE), lambda i: (i, 0)),
        out_shape=jax.ShapeDtypeStruct((t, hpg * LANE), BF16),
        compiler_params=_params(("arbitrary",)),
        name="merge_groups",
    )(*outs, *lses)


def _gated_proj_kernel(a_ref, b_ref, wa_ref, wb_ref, ga_ref, gb_ref, o_ref):
    n_slabs = ga_ref.shape[0]
    per_strip = min(n_slabs, MXU_WIDTH // LANE)
    for c0 in range(0, n_slabs, per_strip):
        c1 = min(c0 + per_strip, n_slabs)
        cols = slice(c0 * LANE, c1 * LANE)
        pa = jnp.dot(a_ref[...], wa_ref[:, cols], preferred_element_type=F32)
        pb = jnp.dot(b_ref[...], wb_ref[:, cols], preferred_element_type=F32)
        for c in range(c0, c1):
            gated = ga_ref[c] * pa[:, _lanes(c - c0)] + gb_ref[c] * pb[:, _lanes(c - c0)]
            o_ref[:, _lanes(c)] = gated.astype(o_ref.dtype)


def _gated_proj(o_a, o_b, w_a, w_b, z_sl, bn, gate_a_blk, gate_b_blk):
    t, d_a = o_a.shape
    d_b = o_b.shape[1]
    d = w_a.shape[1]
    bm = _tile(t, MATMUL_TILE, SUBLANE)
    return pl.pallas_call(
        _gated_proj_kernel,
        grid=(t // bm, d // bn),
        in_specs=[pl.BlockSpec((bm, d_a), lambda i, j: (i, 0)),
                  pl.BlockSpec((bm, d_b), lambda i, j: (i, 0)),
                  pl.BlockSpec((d_a, bn), lambda i, j: (0, j)),
                  pl.BlockSpec((d_b, bn), lambda i, j: (0, j)),
                  pl.BlockSpec((bn // LANE, bm, LANE), lambda i, j: (gate_a_blk + j, i, 0)),
                  pl.BlockSpec((bn // LANE, bm, LANE), lambda i, j: (gate_b_blk + j, i, 0))],
        out_specs=pl.BlockSpec((bm, bn), lambda i, j: (i, j)),
        out_shape=jax.ShapeDtypeStruct((t, d), BF16),
        compiler_params=_params(("arbitrary", "arbitrary")),
        name="gated_proj",
    )(o_a, o_b, w_a, w_b, z_sl, z_sl)


def _matmul_kernel(x_ref, w_ref, o_ref, *, nk):
    if nk == 1:
        o_ref[...] = jnp.dot(x_ref[...], w_ref[...], preferred_element_type=F32)
    else:
        @pl.when(pl.program_id(2) == 0)
        def _():
            o_ref[...] = jnp.zeros(o_ref.shape, o_ref.dtype)

        o_ref[...] += jnp.dot(x_ref[...], w_ref[...], preferred_element_type=F32)


def _matmul(x, w, name, k_tile=None):
    t, kd = x.shape
    n = w.shape[1]
    bm = _tile(t, MATMUL_TILE, SUBLANE)
    bn = _tile(n, MATMUL_TILE, LANE)
    bk = kd if k_tile is None else k_tile
    nk = kd // bk
    return pl.pallas_call(
        functools.partial(_matmul_kernel, nk=nk),
        grid=(t // bm, n // bn, nk),
        in_specs=[pl.BlockSpec((bm, bk), lambda i, j, k: (i, k)),
                  pl.BlockSpec((bk, bn), lambda i, j, k: (k, j))],
        out_specs=pl.BlockSpec((bm, bn), lambda i, j, k: (i, j)),
        out_shape=jax.ShapeDtypeStruct((t, n), F32),
        compiler_params=_params(("arbitrary", "arbitrary", "arbitrary")),
        name=name,
    )(x, w)


def _ffn_down_round_kernel(x_ref, w_in, o_ref, w_out, w_buf, acc_ref, *, nk, k_rows):
    bb, i = pl.program_id(0), pl.program_id(1)
    ni = pl.num_programs(1)
    ck, bn = w_in.shape
    bm = x_ref.shape[0]
    slot = (bb + 1) % 2

    def round_chunk():
        last = pl.num_programs(0) - 1
        k_c = jnp.minimum(bb, last - 1) % nk
        i_c = jnp.where(bb == last, ni - 1, i)
        row = (k_c * ni + i_c) * ck + lax.broadcasted_iota(jnp.int32, (ck, bn), 0)
        chunk = jnp.where(row < k_rows, w_in[...], 0.0).astype(BF16)
        w_buf[bb % 2, pl.ds(pl.multiple_of(i * ck, ck), ck), :] = chunk
        w_out[...] = chunk

    pl.when(bb == 0)(round_chunk)

    @pl.when(bb > 0)
    def _():
        round_chunk()
        k = (bb - 1) % nk
        row_split = 2 if bm % (4 * SUBLANE) == 0 else 1
        hm = bm // row_split
        strip = min(bn, MXU_WIDTH)
        for h in range(row_split):
            rows = pl.ds(pl.multiple_of(i * bm + h * hm, hm), hm)
            x = x_ref[h * hm:(h + 1) * hm, :]
            for c0 in range(0, bn, strip):
                cols = slice(c0, c0 + strip)
                prev = jnp.where(k == 0, 0.0, acc_ref[rows, cols])
                total = prev + jnp.dot(x, w_buf[slot, :, cols], preferred_element_type=F32)
                acc_ref[rows, cols] = total
                o_ref[h * hm:(h + 1) * hm, cols] = total


def _ffn_down_round(x, w, k_tile, name):
    t, kp = x.shape
    k_rows, n = w.shape
    bm = _tile(t, MATMUL_TILE, SUBLANE)
    bn = _tile(n, FFN_DOWN_TILE, LANE)
    ni, nj, nk = t // bm, n // bn, kp // k_tile
    assert k_tile % (2 * SUBLANE * ni) == 0 and kp - k_rows < k_tile // ni
    ck = k_tile // ni
    nb = nj * nk

    def mm(bb, i):
        b = jnp.maximum(bb - 1, 0)
        return jnp.where(bb == 0, 0, i), b % nk, b // nk

    def chunk(bb, i):
        b = jnp.minimum(bb, nb - 1)
        return (b % nk) * ni + jnp.where(bb == nb, ni - 1, i), b // nk

    def out_idx(bb, i):
        i_m, k_m, j_m = mm(bb, i)
        return jnp.where(k_m == nk - 1, i_m, 0), j_m

    return pl.pallas_call(
        functools.partial(_ffn_down_round_kernel, nk=nk, k_rows=k_rows),
        grid=(nb + 1, ni),
        in_specs=[pl.BlockSpec((bm, k_tile), lambda bb, i: mm(bb, i)[:2]),
                  pl.BlockSpec((ck, bn), chunk)],
        out_specs=[pl.BlockSpec((bm, bn), out_idx), pl.BlockSpec((ck, bn), chunk)],
        out_shape=[jax.ShapeDtypeStruct((t, n), F32), jax.ShapeDtypeStruct((kp, n), BF16)],
        scratch_shapes=[pltpu.VMEM((2, k_tile, bn), BF16), pltpu.VMEM((t, bn), F32)],
        compiler_params=_params(("arbitrary", "arbitrary")),
        name=name,
    )(x, w)


def _post_mix_kernel(x_ref, y_ref, g1_ref, g2_ref, x1_ref, h2_ref):
    y = y_ref[...]
    x1 = x_ref[...] + (y * _rms_scale(y)) * g1_ref[...]
    x1_ref[...] = x1
    h2_ref[...] = ((x1 * _rms_scale(x1)) * g2_ref[...]).astype(h2_ref.dtype)


def _post_mix(x, y, g_post, g_pre):
    t, d = x.shape
    bt = _tile(t, 256, SUBLANE)
    row = pl.BlockSpec((bt, d), lambda i: (i, 0))
    vec = pl.BlockSpec((1, d), lambda i: (0, 0))
    return pl.pallas_call(
        _post_mix_kernel,
        grid=(t // bt,),
        in_specs=[row, row, vec, vec],
        out_specs=[row, row],
        out_shape=[jax.ShapeDtypeStruct((t, d), F32), jax.ShapeDtypeStruct((t, d), BF16)],
        compiler_params=_params(("arbitrary",)),
        name="post_mix",
    )(x, y, g_post.reshape(1, d), g_pre.reshape(1, d))


def _residual_norm_kernel(x_ref, y_ref, g_ref, o_ref):
    y = y_ref[...]
    o_ref[...] = x_ref[...] + (y * _rms_scale(y)) * g_ref[...]


def _residual_norm(x, y, g):
    t, d = x.shape
    bt = _tile(t, 256, SUBLANE)
    row = pl.BlockSpec((bt, d), lambda i: (i, 0))
    return pl.pallas_call(
        _residual_norm_kernel,
        grid=(t // bt,),
        in_specs=[row, row, pl.BlockSpec((1, d), lambda i: (0, 0))],
        out_specs=row,
        out_shape=jax.ShapeDtypeStruct((t, d), F32),
        compiler_params=_params(("arbitrary",)),
        name="residual_norm",
    )(x, y, g.reshape(1, d))


def _ffn_up_kernel(*refs, conv_w, s_len, d_ff):
    if s_len is None:
        (x_ref, wg_in, wu_in, cw_ref, cb_ref, out_ref, tail_ref, wg_out, wu_out,
         wg_buf, wu_buf, a_buf, u_buf, o_buf) = refs
        hist_refs = None
        jj, i = pl.program_id(0), pl.program_id(1)
        ck = wg_in.shape[0]

        def round_chunks():
            rows_c = pl.ds(pl.multiple_of(i * ck, ck), ck)
            for src, buf, dst in ((wg_in, wg_buf, wg_out), (wu_in, wu_buf, wu_out)):
                chunk = src[...].astype(BF16)
                buf[jj % 2, rows_c, :] = chunk
                dst[...] = chunk

        slot = (jj + 1) % 2
        wg = lambda cols: wg_buf[slot, :, cols]
        wu = lambda cols: wu_buf[slot, :, cols]
        col_blk = jj - 1
        row_split = 2 if out_ref.shape[0] % (4 * SUBLANE) == 0 else 1
    else:
        x_ref, wg_ref, wu_ref, cw_ref, cb_ref, h1_ref, h2_ref, out_ref, tail_ref, a_buf, u_buf, o_buf = refs
        hist_refs = {1: h1_ref, 2: h2_ref}
        i = pl.program_id(1)
        wg = lambda cols: wg_ref[:, cols]
        wu = lambda cols: wu_ref[:, cols]
        col_blk = pl.program_id(0)
        row_split = 1
    bm, bn = out_ref.shape
    hm = bm // row_split
    half = hm // 2
    top = SUBLANE
    n_slabs = bn // LANE
    per_strip = min(n_slabs, MXU_WIDTH // LANE)

    def units():
        @pl.when(i == 0)
        def _():
            a_buf[:, 0:top, :] = jnp.zeros((n_slabs, top, LANE), F32)

        lane = lax.broadcasted_iota(jnp.int32, (1, LANE), 1)
        for h in range(row_split):
            r0 = h * hm
            x = x_ref[r0:r0 + hm, :]
            for c0 in range(0, n_slabs, per_strip):
                c1 = min(c0 + per_strip, n_slabs)
                cols = slice(c0 * LANE, c1 * LANE)
                a = jnp.dot(x, wg(cols), preferred_element_type=F32)
                u = jnp.dot(x, wu(cols), preferred_element_type=F32)
                if s_len is None and h == 0 and c0 == 0:
                    round_chunks()
                if s_len is not None:
                    tail_ref[r0:r0 + hm, cols] = a
                elif h == row_split - 1:
                    tail_ref[:, cols] = a[hm - SUBLANE:, :]
                for c in range(c0, c1):
                    a_buf[c, top + r0:top + r0 + hm, :] = a[:, _lanes(c - c0)]
                    u_buf[c, r0:r0 + hm, :] = u[:, _lanes(c - c0)]
                    in_range = (col_blk * bn + c * LANE + lane) < d_ff
                    for e in range(2):
                        taps = [a_buf[c, pl.ds(top + r0 + e - lag, half, stride=2), :] for lag in range(conv_w)]
                        if hist_refs is not None:
                            s = 2 * (lax.broadcasted_iota(jnp.int32, (half, LANE), 0) % (s_len // 2)) + e
                            for lag in range(1, conv_w):
                                hist = hist_refs[lag][c, pl.ds(r0 + e, half, stride=2), :]
                                taps[lag] = jnp.where(s < lag, hist, taps[lag])
                        acc = cw_ref[0:1, _lanes(c)] * taps[conv_w - 1]
                        for k in range(1, conv_w):
                            acc = acc + cw_ref[k:k + 1, _lanes(c)] * taps[conv_w - 1 - k]
                        acc = cb_ref[:, _lanes(c)] + acc
                        val = _gelu(acc) * u_buf[c, pl.ds(r0 + e, half, stride=2), :]
                        o_buf[c, pl.ds(r0 + e, half, stride=2), :] = jnp.where(in_range, val, 0.0)
                    out_ref[r0:r0 + hm, _lanes(c)] = o_buf[c, r0:r0 + hm, :].astype(out_ref.dtype)
        if s_len is None:
            a_buf[:, 0:top, :] = a_buf[:, bm:bm + top, :]

    if s_len is None:
        pl.when(jj == 0)(round_chunks)
        pl.when(jj > 0)(units)
    else:
        units()


def _ffn_up_prompt(h, w_gate, w_up, conv_w, conv_b, ffp):
    t, d = h.shape
    d_ff = w_gate.shape[1]
    cw = conv_w.shape[0]
    assert cw == 3 and d_ff % LANE == 0
    bn = FFN_TILE
    assert ffp % bn == 0 and ffp - d_ff < bn
    bm = _tile(t, MATMUL_TILE, 4 * SUBLANE)
    ni, nj = t // bm, ffp // bn
    assert d % ni == 0
    ck = d // ni
    n_slabs = bn // LANE

    def fin_i(jj, i):
        return jnp.where(jj == 0, 0, i)

    def fin_j(jj):
        return jnp.maximum(jj - 1, 0)

    def chunk(jj, i):
        return jnp.where(jj == nj, ni - 1, i), jnp.minimum(jj, nj - 1)

    w_in_spec = pl.BlockSpec((ck, bn), chunk)
    wbuf = pltpu.VMEM((2, d, bn), BF16)
    act, tail, wg16, wu16 = pl.pallas_call(
        functools.partial(_ffn_up_kernel, conv_w=cw, s_len=None, d_ff=d_ff),
        grid=(nj + 1, ni),
        in_specs=[pl.BlockSpec((bm, d), lambda jj, i: (fin_i(jj, i), 0)), w_in_spec, w_in_spec,
                  pl.BlockSpec((cw, bn), lambda jj, i: (0, fin_j(jj))),
                  pl.BlockSpec((1, bn), lambda jj, i: (0, fin_j(jj)))],
        out_specs=[pl.BlockSpec((bm, bn), lambda jj, i: (fin_i(jj, i), fin_j(jj))),
                   pl.BlockSpec((SUBLANE, bn), lambda jj, i: (0, fin_j(jj))),
                   w_in_spec, w_in_spec],
        out_shape=[jax.ShapeDtypeStruct((t, ffp), BF16), jax.ShapeDtypeStruct((SUBLANE, d_ff), F32),
                   jax.ShapeDtypeStruct((d, d_ff), BF16), jax.ShapeDtypeStruct((d, d_ff), BF16)],
        scratch_shapes=[wbuf, wbuf,
                        pltpu.VMEM((n_slabs, bm + SUBLANE, LANE), F32), pltpu.VMEM((n_slabs, bm, LANE), F32),
                        pltpu.VMEM((n_slabs, bm, LANE), F32)],
        compiler_params=_params(("arbitrary", "arbitrary")),
        name="ffn_up",
    )(h, w_gate, w_up, conv_w, conv_b.reshape(1, d_ff))
    return act, tail, wg16, wu16


def _ffn_up_sample(h, w_gate, w_up, conv_w, conv_b, ffp, hist, s_len):
    t, d = h.shape
    d_ff = w_gate.shape[1]
    cw = conv_w.shape[0]
    assert cw == 3 and d_ff % LANE == 0 and s_len % 2 == 0
    bn = _tile(ffp, MATMUL_TILE, LANE)
    assert ffp - d_ff < bn
    n_slabs = bn // LANE
    w_spec = pl.BlockSpec((d, bn), lambda j, i: (0, j))
    out_spec = pl.BlockSpec((t, bn), lambda j, i: (i, j))

    def slabs(rows):
        return jnp.transpose(rows.reshape(t, d_ff // LANE, LANE), (1, 0, 2))
    h1 = slabs(jnp.pad(hist[:, 1:2], ((0, 0), (0, s_len - 1), (0, 0))))
    h2 = slabs(jnp.pad(hist, ((0, 0), (0, s_len - hist.shape[1]), (0, 0))))
    hist_spec = pl.BlockSpec((n_slabs, t, LANE), lambda j, i: (j, i, 0))
    return pl.pallas_call(
        functools.partial(_ffn_up_kernel, conv_w=cw, s_len=s_len, d_ff=d_ff),
        grid=(ffp // bn, 1),
        in_specs=[pl.BlockSpec((t, d), lambda j, i: (i, 0)), w_spec, w_spec,
                  pl.BlockSpec((cw, bn), lambda j, i: (0, j)), pl.BlockSpec((1, bn), lambda j, i: (0, j)),
                  hist_spec, hist_spec],
        out_specs=[out_spec, out_spec],
        out_shape=[jax.ShapeDtypeStruct((t, ffp), BF16), jax.ShapeDtypeStruct((t, d_ff), F32)],
        scratch_shapes=[pltpu.VMEM((n_slabs, t + SUBLANE, LANE), F32), pltpu.VMEM((n_slabs, t, LANE), F32),
                        pltpu.VMEM((n_slabs, t, LANE), F32)],
        compiler_params=_params(("arbitrary", "arbitrary")),
        name="ffn_up_decode",
    )(h, w_gate, w_up, conv_w, conv_b.reshape(1, d_ff), h1, h2)


def _layer(x, p, s_len=None, caches=None, conv_hist=None):
    t, d = x.shape
    d_a = p["ln_g"].shape[0]
    hpg = p["hpg"]
    gw = hpg * HEAD_DIM
    bn = p["bn"]
    o1 = 2 * d_a
    d_qkv = 3 * gw

    h = _rmsnorm_cast(x, p["g_pre_mix"])
    w16 = dict(p.get("w16", {}))
    z_parts = []
    for name, col0, n_cols, act in (("in_proj_sgu", 0, o1, _gelu),
                                    ("in_proj_qkv", o1, 3 * d_qkv, lambda v: v),
                                    ("in_proj_gates", o1 + 3 * d_qkv, 2 * d, _sigmoid)):
        if s_len is None:
            z_part, w16[name] = _in_proj_round(h, p["w_in"], col0, n_cols, act, name)
        else:
            z_part = _in_proj(h, w16[name], _tile(n_cols, 2 * MATMUL_TILE, LANE), 0, n_cols, act, name + "_decode")
        z_parts.append(z_part)
    za_sl, qkv_sl, gate_sl = z_parts

    if s_len is None:
        w_mix, b_mix = p["w_mix_prompt"], p["b_mix_prompt"]
    else:
        w_mix, b_mix = p["w_mix_sample"], p["b_mix_sample"]
    o_a, v_state = _sgu(za_sl, d_a, p["ln_g"], p["ln_b"], w_mix, b_mix)

    outs, lses = [], []
    for gi, (_, dil) in enumerate(DILATION_GROUPS):
        q_slab = gi * hpg
        k_slab = d_qkv // LANE + gi * hpg
        v_slab = 2 * d_qkv // LANE + gi * hpg
        bias_tab = p["rel_bias"][:, gi * hpg:(gi + 1) * hpg]
        if s_len is not None:
            o, lse = _attn_sample(qkv_sl, s_len, hpg, q_slab, k_slab, v_slab, caches[2 * gi], caches[2 * gi + 1],
                                  bias_tab, dil)
        elif gi < len(DILATION_GROUPS) - 1:
            o, lse = _attn_prompt(qkv_sl, hpg, q_slab, k_slab, v_slab, bias_tab, dil)
        else:
            o_b = _attn_prompt(qkv_sl, hpg, q_slab, k_slab, v_slab, bias_tab, dil, merge_with=(outs, lses))
            break
        outs.append(o)
        lses.append(lse)
    if s_len is not None:
        o_b = _merge(outs, lses)

    merged = _gated_proj(o_a, o_b, p["w_proj_a"], p["w_proj_b"], gate_sl, bn, 0, d // bn)
    if s_len is None:
        y, w16["out_proj"] = _in_proj_round(merged, p["w_out"], 0, d, lambda v: v, "out_proj", slab_out=False)
    else:
        y = _matmul(merged, w16["out_proj"], "out_proj_decode")
    x1, h2 = _post_mix(x, y, p["g_post_mix"], p["g_pre_ffn"])

    if s_len is None:
        act, a_tail, w16["ffn_gate"], w16["ffn_up"] = _ffn_up_prompt(h2, p["w_gate"], p["w_up"], p["conv_w"],
                                                                     p["conv_b"], p["ffp"])
    else:
        act, a_tail = _ffn_up_sample(h2, w16["ffn_gate"], w16["ffn_up"], p["conv_w"], p["conv_b"], p["ffp"],
                                     conv_hist, s_len)
    if s_len is None:
        f, w16["ffn_down"] = _ffn_down_round(act, p["w_down"], p["ffn_k_tile"], "ffn_down")
    else:
        f = _matmul(act, w16["ffn_down"], "ffn_down_decode", k_tile=p["ffn_k_tile"])
    y_out = _residual_norm(x1, f, p["g_post_ffn"])
    return y_out, qkv_sl, v_state, a_tail, w16


def kernel(x_prompt, x_sample, cache_k_g1, cache_v_g1, cache_k_g2, cache_v_g2, cache_k_g3, cache_v_g3, state_conv, g_pre_mix, w_in, sgu_ln_g, sgu_ln_b, w_spatial, b_spatial, rel_bias, w_proj_a, w_proj_b, w_out, g_post_mix, g_pre_ffn, w_gate, w_up, conv_w, conv_b, w_down, g_post_ffn):
    depth = w_in.shape[0]
    assert depth == 1
    n_prompt, seq, d = x_prompt.shape
    assert n_prompt == 1 and seq % CHUNK == 0
    n_seq, s_len, _ = x_sample.shape
    assert s_len == SUBLANE
    d_a = sgu_ln_g.shape[1]
    n_groups = w_spatial.shape[1]
    n_heads = rel_bias.shape[1]
    hpg = n_heads // len(DILATION_GROUPS)
    gw = hpg * HEAD_DIM
    d_ff = w_gate.shape[2]
    cw = conv_w.shape[1]
    bn = _tile(math.gcd(2 * d_a, gw), MATMUL_TILE, LANE)
    ffp = -(-d_ff // FFN_TILE) * FFN_TILE
    t_s = n_seq * s_len

    tri = np.tril(np.ones((CHUNK, CHUNK), np.float32))
    w_mix_prompt = (w_spatial[0] * tri).astype(BF16)
    b_mix_prompt = jnp.broadcast_to(b_spatial[0][:, :, None], (n_groups, CHUNK, LANE))
    w_small = w_spatial[0][:, :s_len, :s_len] * tri[:s_len, :s_len]
    eye = np.eye(n_seq, dtype=np.float32)
    w_mix_sample = jnp.einsum("ab,gpq->gapbq", eye, w_small).reshape(n_groups, t_s, t_s).astype(BF16)
    b_mix_sample = jnp.broadcast_to(jnp.tile(b_spatial[0][:, :s_len], (1, n_seq))[:, :, None], (n_groups, t_s, LANE))

    p = dict(
        hpg=hpg, bn=bn, ffp=ffp,
        g_pre_mix=g_pre_mix[0], w_in=w_in[0], ln_g=sgu_ln_g[0], ln_b=sgu_ln_b[0],
        w_mix_prompt=w_mix_prompt, b_mix_prompt=b_mix_prompt,
        w_mix_sample=w_mix_sample, b_mix_sample=b_mix_sample,
        rel_bias=rel_bias,
        w_proj_a=w_proj_a[0].astype(BF16), w_proj_b=w_proj_b[0].astype(BF16), w_out=w_out[0],
        g_post_mix=g_post_mix[0], g_pre_ffn=g_pre_ffn[0],
        w_gate=w_gate[0], w_up=w_up[0],
        conv_w=conv_w[0], conv_b=conv_b[0],
        w_down=w_down[0],
        ffn_k_tile=_tile(ffp, 3072, LANE),
        g_post_ffn=g_post_ffn[0],
    )

    caches = tuple(c[0] for c in (cache_k_g1, cache_v_g1, cache_k_g2, cache_v_g2, cache_k_g3, cache_v_g3))

    yp, zp, vp_state, ap_tail, w16 = _layer(x_prompt[0], p)
    ys, zs, vs_state, as_all, _ = _layer(x_sample.reshape(t_s, d), dict(p, w16=w16), s_len=s_len,
                                         caches=caches, conv_hist=state_conv[0])

    d_qkv = 3 * gw
    prompt_kv, sample_kv = [], []
    for gi, (win, _) in enumerate(DILATION_GROUPS):
        keep = min(win, seq)
        for base in (d_qkv, 2 * d_qkv):
            s0 = (base + gi * gw) // LANE
            pk = jnp.transpose(zp[s0:s0 + hpg, seq - keep:, :], (1, 0, 2))
            prompt_kv.append(pk.reshape(1, 1, keep, hpg, HEAD_DIM))
            sk = jnp.transpose(zs[s0:s0 + hpg], (1, 0, 2))
            sample_kv.append(sk.reshape(1, n_seq, s_len, hpg, HEAD_DIM))
    p_conv = ap_tail[SUBLANE - (cw - 1):].reshape(1, 1, cw - 1, d_ff)
    s_conv = as_all.reshape(n_seq, s_len, d_ff)[:, s_len - (cw - 1):].reshape(1, n_seq, cw - 1, d_ff)
    return (yp.reshape(1, seq, d), ys.reshape(n_seq, s_len, d),
            *prompt_kv, vp_state.reshape(1, 1, CHUNK, d_a), p_conv,
            *sample_kv, vs_state.reshape(1, n_seq, s_len, d_a), s_conv)
```

```python
import functools
import math

import numpy as np
import jax
import jax.numpy as jnp
from jax import lax
from jax.experimental import pallas as pl
from jax.experimental.pallas import tpu as pltpu

F32 = jnp.float32
BF16 = jnp.bfloat16

HEAD_DIM = 128
STEPS = 128
CHUNK = 128
GROUP_DIM_A = 128
DILATION_GROUPS = ((128, 1), (512, 4), (2048, 16))
MAX_EXACT = 16
MAX_DISTANCE = 2048
EPS = 1e-6
NEG = -1e30
ATTN_SCALE = HEAD_DIM ** -0.5

LANE = 128
SUBLANE = 8
VMEM_LIMIT_BYTES = 56 * 1024 * 1024
MATMUL_TILE = 1024
MXU_WIDTH = 256
FFN_TILE = 512
FFN_DOWN_TILE = 512
IN_PROJ_ROUND_TILE = 1024
ATTN_UNITS_PER_STEP = 16
SGU_ROWS_PER_STEP = 512


def _tile(dim, target, align):
    best = None
    t = align
    while t <= min(dim, target):
        if dim % t == 0:
            best = t
        t += align
    return best if best is not None else dim


def _params(semantics):
    return pltpu.CompilerParams(dimension_semantics=semantics, vmem_limit_bytes=VMEM_LIMIT_BYTES)


def _gelu(x):
    return 0.5 * x * (1.0 + jnp.tanh(math.sqrt(2.0 / math.pi) * (x + 0.044715 * (x * x * x))))


def _sigmoid(x):
    return 1.0 / (1.0 + jnp.exp(-x))


def _rms_scale(x):
    return lax.rsqrt(jnp.mean(x * x, axis=-1, keepdims=True) + EPS)


def _lanes(c):
    return slice(c * LANE, (c + 1) * LANE)


def _rmsnorm_cast_kernel(x_ref, g_ref, o_ref):
    x = x_ref[...]
    o_ref[...] = ((x * _rms_scale(x)) * g_ref[...]).astype(o_ref.dtype)


def _rmsnorm_cast(x, g):
    t, d = x.shape
    bt = _tile(t, 256, SUBLANE)
    return pl.pallas_call(
        _rmsnorm_cast_kernel,
        grid=(t // bt,),
        in_specs=[pl.BlockSpec((bt, d), lambda i: (i, 0)), pl.BlockSpec((1, d), lambda i: (0, 0))],
        out_specs=pl.BlockSpec((bt, d), lambda i: (i, 0)),
        out_shape=jax.ShapeDtypeStruct((t, d), BF16),
        compiler_params=_params(("arbitrary",)),
        name="rmsnorm_cast",
    )(x, g.reshape(1, d))


def _in_proj_kernel(x_ref, w_ref, o_ref, *, act):
    acc = jnp.dot(x_ref[...], w_ref[...], preferred_element_type=F32)
    for c in range(o_ref.shape[0]):
        o_ref[c] = act(acc[:, _lanes(c)])


def _in_proj_round_kernel(x_ref, w_in, o_ref, w_out, w_buf, *, act, slab_out):
    jj, i = pl.program_id(0), pl.program_id(1)
    ck = w_in.shape[0]
    slot = (jj + 1) % 2
    bm = o_ref.shape[-2]
    n_slabs = w_in.shape[1] // LANE
    row_split = 2 if bm % (4 * SUBLANE) == 0 else 1
    hm = bm // row_split
    per_strip = min(n_slabs, MXU_WIDTH // LANE)

    def round_chunk():
        chunk = w_in[...].astype(BF16)
        w_buf[jj % 2, pl.ds(pl.multiple_of(i * ck, ck), ck), :] = chunk
        w_out[...] = chunk

    pl.when(jj == 0)(round_chunk)

    @pl.when(jj > 0)
    def _():
        round_chunk()
        for h in range(row_split):
            x = x_ref[h * hm:(h + 1) * hm, :]
            for c0 in range(0, n_slabs, per_strip):
                c1 = min(c0 + per_strip, n_slabs)
                acc = jnp.dot(x, w_buf[slot, :, c0 * LANE:c1 * LANE], preferred_element_type=F32)
                for c in range(c0, c1):
                    val = act(acc[:, _lanes(c - c0)])
                    if slab_out:
                        o_ref[c, h * hm:(h + 1) * hm, :] = val
                    else:
                        o_ref[h * hm:(h + 1) * hm, _lanes(c)] = val


def _in_proj_round(h, w, col0, n_cols, act, name, slab_out=True):
    t, d = h.shape
    bn = _tile(math.gcd(n_cols, col0) if col0 else n_cols, IN_PROJ_ROUND_TILE, LANE)
    bm = _tile(t, MATMUL_TILE, SUBLANE)
    ni, nj = t // bm, n_cols // bn
    assert d % ni == 0
    ck = d // ni

    def fin(jj, i):
        return jnp.maximum(jj - 1, 0), jnp.where(jj == 0, 0, i)

    def chunk(jj, i):
        return jnp.where(jj == nj, ni - 1, i), jnp.minimum(jj, nj - 1)

    if slab_out:
        out_spec = pl.BlockSpec((bn // LANE, bm, LANE), lambda jj, i: fin(jj, i) + (0,))
        out_shape = jax.ShapeDtypeStruct((n_cols // LANE, t, LANE), F32)
    else:
        out_spec = pl.BlockSpec((bm, bn), lambda jj, i: fin(jj, i)[::-1])
        out_shape = jax.ShapeDtypeStruct((t, n_cols), F32)
    return pl.pallas_call(
        functools.partial(_in_proj_round_kernel, act=act, slab_out=slab_out),
        grid=(nj + 1, ni),
        in_specs=[pl.BlockSpec((bm, d), lambda jj, i: (fin(jj, i)[1], 0)),
                  pl.BlockSpec((ck, bn), lambda jj, i: (chunk(jj, i)[0], col0 // bn + chunk(jj, i)[1]))],
        out_specs=[out_spec, pl.BlockSpec((ck, bn), chunk)],
        out_shape=[out_shape, jax.ShapeDtypeStruct((d, n_cols), BF16)],
        scratch_shapes=[pltpu.VMEM((2, d, bn), BF16)],
        compiler_params=_params(("arbitrary", "arbitrary")),
        name=name,
    )(h, w)


def _in_proj(h, w, bn, col0, n_cols, act, name):
    t, d = h.shape
    bm = _tile(t, MATMUL_TILE, SUBLANE)
    return pl.pallas_call(
        functools.partial(_in_proj_kernel, act=act),
        grid=(t // bm, n_cols // bn),
        in_specs=[pl.BlockSpec((bm, d), lambda i, j: (i, 0)),
                  pl.BlockSpec((d, bn), lambda i, j: (0, col0 // bn + j))],
        out_specs=pl.BlockSpec((bn // LANE, bm, LANE), lambda i, j: (j, i, 0)),
        out_shape=jax.ShapeDtypeStruct((n_cols // LANE, t, LANE), F32),
        compiler_params=_params(("arbitrary", "arbitrary")),
        name=name,
    )(h, w)


def _sgu_kernel(u_ref, v_ref, lg_ref, lb_ref, w_ref, b_ref, o_ref, vs_ref, *, n_groups):
    c = w_ref.shape[1]
    n_feat = n_groups * GROUP_DIM_A
    for r0 in range(0, o_ref.shape[0], c):
        vp = _gelu(v_ref[:, r0:r0 + c, :])
        mu = jnp.sum(jnp.sum(vp, axis=0), axis=-1, keepdims=True) / n_feat
        vc = vp - mu
        var = jnp.sum(jnp.sum(vc * vc, axis=0), axis=-1, keepdims=True) / n_feat
        v = vc * lax.rsqrt(var + EPS) * lg_ref[...] + lb_ref[...]
        for g in range(n_groups):
            if r0 + c == o_ref.shape[0]:
                vs_ref[:, _lanes(g)] = v[g]
            mixed = jnp.dot(w_ref[g], v[g].astype(BF16), preferred_element_type=F32) + b_ref[g]
            o_ref[r0:r0 + c, _lanes(g)] = (_gelu(u_ref[g, r0:r0 + c, :]) * mixed).astype(o_ref.dtype)


def _sgu(z_sl, d_a, ln_g, ln_b, w_mix, b_mix):
    t = z_sl.shape[1]
    n_groups, c, _ = w_mix.shape
    rows = _tile(t, SGU_ROWS_PER_STEP, c)
    slab = pl.BlockSpec((n_groups, 1, LANE), lambda i: (0, 0, 0))
    return pl.pallas_call(
        functools.partial(_sgu_kernel, n_groups=n_groups),
        grid=(t // rows,),
        in_specs=[
            pl.BlockSpec((n_groups, rows, LANE), lambda i: (0, i, 0)),
            pl.BlockSpec((n_groups, rows, LANE), lambda i: (1, i, 0)),
            slab, slab,
            pl.BlockSpec((n_groups, c, c), lambda i: (0, 0, 0)),
            pl.BlockSpec((n_groups, c, LANE), lambda i: (0, 0, 0)),
        ],
        out_specs=[pl.BlockSpec((rows, d_a), lambda i: (i, 0)), pl.BlockSpec((c, d_a), lambda i: (0, 0))],
        out_shape=[jax.ShapeDtypeStruct((t, d_a), BF16), jax.ShapeDtypeStruct((c, d_a), F32)],
        compiler_params=_params(("arbitrary",)),
        name="sgu",
    )(z_sl, z_sl, ln_g.reshape(n_groups, 1, LANE), ln_b.reshape(n_groups, 1, LANE), w_mix, b_mix)


def _t5_bucket(dist, n_buckets):
    n = np.asarray(dist, np.int32)
    safe = np.maximum(n, 1).astype(np.float32)
    large = MAX_EXACT + (np.log(safe / MAX_EXACT) / np.log(np.float32(MAX_DISTANCE / MAX_EXACT))
                         * (n_buckets - MAX_EXACT)).astype(np.int32)
    large = np.minimum(large, n_buckets - 1)
    return np.where(n < MAX_EXACT, n, large).astype(np.int32)


def _bias_lookup(bias_tab, bucket):
    n_buckets = bias_tab.shape[0]
    flat = np.asarray(bucket).reshape(-1)
    onehot = (jnp.asarray(flat)[None, :] == jnp.arange(n_buckets)[:, None]).astype(F32)
    out = jnp.dot(bias_tab.T, onehot, precision=lax.Precision.HIGHEST)
    return out.reshape((bias_tab.shape[1],) + tuple(np.asarray(bucket).shape))


def _merge_by_lse(outs, lses):
    m = functools.reduce(jnp.maximum, lses)
    w = [jnp.exp(l - m) for l in lses]
    num = functools.reduce(lambda x, y: x + y, [wi * oi for wi, oi in zip(w, outs)])
    return num / functools.reduce(lambda x, y: x + y, w)


def _attn_prompt_kernel(*refs, hb, dil, n_other):
    q_ref, kc_ref, vc_ref, bias_ref, band_ref = refs[:5]
    others = refs[5:5 + 2 * n_other]
    n_out = 1 if n_other else 2
    outs = refs[5 + 2 * n_other:5 + 2 * n_other + n_out]
    kp_ref, vp_ref = refs[5 + 2 * n_other + n_out:][:2]
    o_acc, lse_acc = refs[5 + 2 * n_other + n_out + 2:] if n_other else outs
    b = pl.program_id(1)

    @pl.when(b == 0)
    def _():
        kp_ref[...] = jnp.zeros(kp_ref.shape, F32)
        vp_ref[...] = jnp.zeros(vp_ref.shape, F32)

    span = STEPS * dil
    n_pb = q_ref.shape[1] // span
    col = lax.broadcasted_iota(jnp.int32, (STEPS, 2 * STEPS), 1)
    band = band_ref[...] > 0.5
    first = band & ((b > 0) | (col >= STEPS))
    for hh in range(hb):
        for pb in range(n_pb):
            for r in range(dil):
                def rows(block):
                    return pl.ds(block * span + r, STEPS, stride=dil) if dil > 1 else pl.ds(block * span, STEPS)
                k_prev = kp_ref[hh, rows(0), :] if pb == 0 else kc_ref[hh, rows(pb - 1), :]
                v_prev = vp_ref[hh, rows(0), :] if pb == 0 else vc_ref[hh, rows(pb - 1), :]
                q = q_ref[hh, rows(pb), :].astype(BF16)
                kk = jnp.concatenate([k_prev, kc_ref[hh, rows(pb), :]], axis=0).astype(BF16)
                vv = jnp.concatenate([v_prev, vc_ref[hh, rows(pb), :]], axis=0).astype(BF16)
                s = lax.dot_general(q, kk, (((1,), (1,)), ((), ())), preferred_element_type=F32) * ATTN_SCALE
                s = jnp.where(first if pb == 0 else band, s + bias_ref[hh], NEG)
                m = jnp.max(s, axis=-1, keepdims=True)
                e = jnp.exp(s - m)
                den = jnp.sum(e, axis=-1, keepdims=True)
                o_acc[hh, rows(pb), :] = jnp.dot(e.astype(BF16), vv, preferred_element_type=F32) / den
                lse_acc[hh, rows(pb), :] = jnp.broadcast_to(m + jnp.log(den), (STEPS, HEAD_DIM))
    kp_ref[...] = kc_ref[:, (n_pb - 1) * span:, :]
    vp_ref[...] = vc_ref[:, (n_pb - 1) * span:, :]
    if n_other:
        for hh in range(hb):
            merged = _merge_by_lse([r[hh] for r in others[:n_other]] + [o_acc[hh]],
                                   [r[hh] for r in others[n_other:]] + [lse_acc[hh]])
            outs[0][:, _lanes(hh)] = merged.astype(outs[0].dtype)


def _attn_prompt(z_sl, hpg, q_slab, k_slab, v_slab, bias_tab, dil, merge_with=None):
    t = z_sl.shape[1]
    span = STEPS * dil
    assert t % span == 0
    hb = min(hpg, max(1, ATTN_UNITS_PER_STEP // dil))
    assert hpg % hb == 0 and q_slab % hb == 0 and k_slab % hb == 0 and v_slab % hb == 0
    n_pb = math.gcd(t // span, max(1, ATTN_UNITS_PER_STEP // (hb * dil)))
    rows = span * n_pb
    nb = t // rows

    p_idx = np.arange(STEPS)[:, None]
    c_idx = np.arange(2 * STEPS)[None, :]
    steps = p_idx + STEPS - c_idx
    band = ((steps >= 0) & (steps <= STEPS)).astype(np.float32)
    bias = _bias_lookup(bias_tab, _t5_bucket(np.clip(steps, 0, STEPS) * dil, bias_tab.shape[0]))

    def cur(slab):
        return pl.BlockSpec((hb, rows, LANE), lambda hi, b: (slab // hb + hi, b, 0))

    head_major = pl.BlockSpec((hb, rows, LANE), lambda hi, b: (hi, b, 0))
    prev_block = pltpu.VMEM((hb, span, LANE), F32)
    block = pltpu.VMEM((hb, rows, LANE), F32)
    others = [] if merge_with is None else list(merge_with[0]) + list(merge_with[1])
    if merge_with is None:
        out_specs = [head_major, head_major]
        out_shape = [jax.ShapeDtypeStruct((hpg, t, LANE), F32)] * 2
        scratch = [prev_block, prev_block]
    else:
        out_specs = [pl.BlockSpec((rows, hb * LANE), lambda hi, b: (b, hi))]
        out_shape = [jax.ShapeDtypeStruct((t, hpg * LANE), BF16)]
        scratch = [prev_block, prev_block, block, block]
    res = pl.pallas_call(
        functools.partial(_attn_prompt_kernel, hb=hb, dil=dil, n_other=len(others) // 2),
        grid=(hpg // hb, nb),
        in_specs=[cur(q_slab), cur(k_slab), cur(v_slab),
                  pl.BlockSpec((hb, STEPS, 2 * STEPS), lambda hi, b: (hi, 0, 0)),
                  pl.BlockSpec((STEPS, 2 * STEPS), lambda hi, b: (0, 0))] + [head_major] * len(others),
        out_specs=out_specs,
        out_shape=out_shape,
        scratch_shapes=scratch,
        compiler_params=_params(("arbitrary", "arbitrary")),
        name=f"attn_prompt_d{dil}",
    )(z_sl, z_sl, z_sl, bias, jnp.asarray(band), *others)
    return res[0] if merge_with is not None else res


def _attn_sample_kernel(q_ref, kn_ref, vn_ref, ck_ref, cv_ref, bc_ref, mc_ref, bnew_ref, mnew_ref,
                        o_ref, lse_ref, *, hpg, s_len, grouped):
    n_keys = mc_ref.shape[1]
    n_new = hpg * s_len
    valid_c = mc_ref[...] > 0.5

    def cache_head(ref, h):
        if grouped:
            return ref[0, :, pl.ds(h, s_len, stride=hpg), :].reshape(n_keys, HEAD_DIM).astype(BF16)
        return ref[0, pl.ds(h, n_keys, stride=hpg) if hpg > 1 else pl.ds(0, n_keys), :].astype(BF16)

    q_all = q_ref[...].reshape(n_new, HEAD_DIM).astype(BF16)
    s_new = lax.dot_general(q_all, kn_ref[...].reshape(n_new, HEAD_DIM).astype(BF16), (((1,), (1,)), ((), ())),
                            preferred_element_type=F32) * ATTN_SCALE
    key_head = lax.broadcasted_iota(jnp.int32, (s_len, n_new), 1) // s_len
    new_ok = mnew_ref[...] > 0.5

    partial, e_new = [], []
    for h in range(hpg):
        q = q_all[h * s_len:(h + 1) * s_len, :]
        sc = lax.dot_general(q, cache_head(ck_ref, h), (((1,), (1,)), ((), ())),
                             preferred_element_type=F32) * ATTN_SCALE
        sc = jnp.where(valid_c, sc + bc_ref[h], NEG)
        sn = jnp.where(new_ok & (key_head == h), s_new[h * s_len:(h + 1) * s_len, :] + bnew_ref[h], NEG)
        m = jnp.maximum(jnp.max(sc, axis=-1, keepdims=True), jnp.max(sn, axis=-1, keepdims=True))
        ec = jnp.exp(sc - m)
        en = jnp.exp(sn - m)
        den = jnp.sum(ec, axis=-1, keepdims=True) + jnp.sum(en, axis=-1, keepdims=True)
        acc = jnp.dot(ec.astype(BF16), cache_head(cv_ref, h), preferred_element_type=F32)
        partial.append((acc, den, m))
        e_new.append(en)
    o_new = jnp.dot(jnp.concatenate(e_new, axis=0).astype(BF16),
                    vn_ref[...].reshape(n_new, HEAD_DIM).astype(BF16), preferred_element_type=F32)
    for h, (acc, den, m) in enumerate(partial):
        o_ref[h] = (acc + o_new[h * s_len:(h + 1) * s_len, :]) / den
        lse_ref[h] = jnp.broadcast_to(m + jnp.log(den), (s_len, HEAD_DIM))


def _attn_sample(z_sl, s_len, hpg, q_slab, k_slab, v_slab, cache_k, cache_v, bias_tab, dil):
    t = z_sl.shape[1]
    n_seq = t // s_len
    lc = cache_k.shape[1]
    assert q_slab % hpg == 0 and k_slab % hpg == 0 and v_slab % hpg == 0

    j = np.arange(STEPS + 1)
    idx = lc + np.arange(s_len)[:, None] - j[None, :] * dil
    assert idx.min() >= 0
    bucket = _t5_bucket(j * dil, bias_tab.shape[0])
    mask = np.zeros((s_len, lc + s_len), np.float32)
    bsel = np.zeros((s_len, lc + s_len), np.int32)
    for s in range(s_len):
        mask[s, idx[s]] = 1.0
        bsel[s, idx[s]] = bucket

    grouped = dil > s_len and lc % dil == 0 and s_len == SUBLANE
    if grouped:
        pos = (np.arange(lc // dil)[:, None] * dil + np.arange(s_len)[None, :]).reshape(-1)
        assert mask[:, :lc].sum() == mask[:, pos].sum()
        ck = cache_k.reshape(n_seq, lc // dil, dil * hpg, HEAD_DIM)
        cv = cache_v.reshape(n_seq, lc // dil, dil * hpg, HEAD_DIM)
        cache_spec = pl.BlockSpec((1, lc // dil, s_len * hpg, LANE), lambda b: (b, 0, 0, 0))
    else:
        pos = np.arange(lc)
        ck = cache_k.reshape(n_seq, lc * hpg, HEAD_DIM)
        cv = cache_v.reshape(n_seq, lc * hpg, HEAD_DIM)
        cache_spec = pl.BlockSpec((1, lc * hpg, LANE), lambda b: (b, 0, 0))
    n_keys = len(pos)
    bias_c = _bias_lookup(bias_tab, bsel[:, pos])
    bias_new = jnp.tile(_bias_lookup(bias_tab, bsel[:, lc:]), (1, 1, hpg))
    mask_c = jnp.asarray(mask[:, pos])
    mask_new = jnp.asarray(np.tile(mask[:, lc:], (1, hpg)))

    out_spec = pl.BlockSpec((hpg, s_len, LANE), lambda b: (0, b, 0))
    return pl.pallas_call(
        functools.partial(_attn_sample_kernel, hpg=hpg, s_len=s_len, grouped=grouped),
        grid=(n_seq,),
        in_specs=[pl.BlockSpec((hpg, s_len, LANE), lambda b: (q_slab // hpg, b, 0)),
                  pl.BlockSpec((hpg, s_len, LANE), lambda b: (k_slab // hpg, b, 0)),
                  pl.BlockSpec((hpg, s_len, LANE), lambda b: (v_slab // hpg, b, 0)),
                  cache_spec, cache_spec,
                  pl.BlockSpec((hpg, s_len, n_keys), lambda b: (0, 0, 0)),
                  pl.BlockSpec((s_len, n_keys), lambda b: (0, 0)),
                  pl.BlockSpec((hpg, s_len, hpg * s_len), lambda b: (0, 0, 0)),
                  pl.BlockSpec((s_len, hpg * s_len), lambda b: (0, 0))],
        out_specs=[out_spec, out_spec],
        out_shape=[jax.ShapeDtypeStruct((hpg, t, LANE), F32)] * 2,
        compiler_params=_params(("arbitrary",)),
        name=f"attn_sample_d{dil}",
    )(z_sl, z_sl, z_sl, ck, cv, bias_c, mask_c, bias_new, mask_new)


def _merge_kernel(o1, o2, o3, l1, l2, l3, out_ref):
    merged = _merge_by_lse([o1[...], o2[...], o3[...]], [l1[...], l2[...], l3[...]])
    for h in range(merged.shape[0]):
        out_ref[:, _lanes(h)] = merged[h].astype(out_ref.dtype)


def _merge(outs, lses):
    hpg, t, _ = outs[0].shape
    bt = _tile(t, 256, SUBLANE)
    spec = pl.BlockSpec((hpg, bt, LANE), lambda i: (0, i, 0))
    return pl.pallas_call(
        _merge_kernel,
        grid=(t // bt,),
        in_specs=[spec] * 6,
        out_specs=pl.BlockSpec((bt, hpg * LANE), lambda i: (i, 0)),
        out_shape=jax.ShapeDtypeStruct((t, hpg * LANE), BF16),
        compiler_params=_params(("arbitrary",)),
        name="merge_groups",
    )(*outs, *lses)


def _gated_proj_kernel(a_ref, b_ref, wa_ref, wb_ref, ga_ref, gb_ref, o_ref):
    n_slabs = ga_ref.shape[0]
    per_strip = min(n_slabs, MXU_WIDTH // LANE)
    for c0 in range(0, n_slabs, per_strip):
        c1 = min(c0 + per_strip, n_slabs)
        cols = slice(c0 * LANE, c1 * LANE)
        pa = jnp.dot(a_ref[...], wa_ref[:, cols], preferred_element_type=F32)
        pb = jnp.dot(b_ref[...], wb_ref[:, cols], preferred_element_type=F32)
        for c in range(c0, c1):
            gated = ga_ref[c] * pa[:, _lanes(c - c0)] + gb_ref[c] * pb[:, _lanes(c - c0)]
            o_ref[:, _lanes(c)] = gated.astype(o_ref.dtype)


def _gated_proj(o_a, o_b, w_a, w_b, z_sl, bn, gate_a_blk, gate_b_blk):
    t, d_a = o_a.shape
    d_b = o_b.shape[1]
    d = w_a.shape[1]
    bm = _tile(t, MATMUL_TILE, SUBLANE)
    return pl.pallas_call(
        _gated_proj_kernel,
        grid=(t // bm, d // bn),
        in_specs=[pl.BlockSpec((bm, d_a), lambda i, j: (i, 0)),
                  pl.BlockSpec((bm, d_b), lambda i, j: (i, 0)),
                  pl.BlockSpec((d_a, bn), lambda i, j: (0, j)),
                  pl.BlockSpec((d_b, bn), lambda i, j: (0, j)),
                  pl.BlockSpec((bn // LANE, bm, LANE), lambda i, j: (gate_a_blk + j, i, 0)),
                  pl.BlockSpec((bn // LANE, bm, LANE), lambda i, j: (gate_b_blk + j, i, 0))],
        out_specs=pl.BlockSpec((bm, bn), lambda i, j: (i, j)),
        out_shape=jax.ShapeDtypeStruct((t, d), BF16),
        compiler_params=_params(("arbitrary", "arbitrary")),
        name="gated_proj",
    )(o_a, o_b, w_a, w_b, z_sl, z_sl)


def _matmul_kernel(x_ref, w_ref, o_ref, *, nk):
    if nk == 1:
        o_ref[...] = jnp.dot(x_ref[...], w_ref[...], preferred_element_type=F32)
    else:
        @pl.when(pl.program_id(2) == 0)
        def _():
            o_ref[...] = jnp.zeros(o_ref.shape, o_ref.dtype)

        o_ref[...] += jnp.dot(x_ref[...], w_ref[...], preferred_element_type=F32)


def _matmul(x, w, name, k_tile=None):
    t, kd = x.shape
    n = w.shape[1]
    bm = _tile(t, MATMUL_TILE, SUBLANE)
    bn = _tile(n, MATMUL_TILE, LANE)
    bk = kd if k_tile is None else k_tile
    nk = kd // bk
    return pl.pallas_call(
        functools.partial(_matmul_kernel, nk=nk),
        grid=(t // bm, n // bn, nk),
        in_specs=[pl.BlockSpec((bm, bk), lambda i, j, k: (i, k)),
                  pl.BlockSpec((bk, bn), lambda i, j, k: (k, j))],
        out_specs=pl.BlockSpec((bm, bn), lambda i, j, k: (i, j)),
        out_shape=jax.ShapeDtypeStruct((t, n), F32),
        compiler_params=_params(("arbitrary", "arbitrary", "arbitrary")),
        name=name,
    )(x, w)


def _ffn_down_round_kernel(x_ref, w_in, o_ref, w_out, w_buf, acc_ref, *, nk, k_rows):
    bb, i = pl.program_id(0), pl.program_id(1)
    ni = pl.num_programs(1)
    ck, bn = w_in.shape
    bm = x_ref.shape[0]
    slot = (bb + 1) % 2

    def round_chunk():
        last = pl.num_programs(0) - 1
        k_c = jnp.minimum(bb, last - 1) % nk
        i_c = jnp.where(bb == last, ni - 1, i)
        row = (k_c * ni + i_c) * ck + lax.broadcasted_iota(jnp.int32, (ck, bn), 0)
        chunk = jnp.where(row < k_rows, w_in[...], 0.0).astype(BF16)
        w_buf[bb % 2, pl.ds(pl.multiple_of(i * ck, ck), ck), :] = chunk
        w_out[...] = chunk

    pl.when(bb == 0)(round_chunk)

    @pl.when(bb > 0)
    def _():
        round_chunk()
        k = (bb - 1) % nk
        row_split = 2 if bm % (4 * SUBLANE) == 0 else 1
        hm = bm // row_split
        strip = min(bn, MXU_WIDTH)
        for h in range(row_split):
            rows = pl.ds(pl.multiple_of(i * bm + h * hm, hm), hm)
            x = x_ref[h * hm:(h + 1) * hm, :]
            for c0 in range(0, bn, strip):
                cols = slice(c0, c0 + strip)
                prev = jnp.where(k == 0, 0.0, acc_ref[rows, cols])
                total = prev + jnp.dot(x, w_buf[slot, :, cols], preferred_element_type=F32)
                acc_ref[rows, cols] = total
                o_ref[h * hm:(h + 1) * hm, cols] = total


def _ffn_down_round(x, w, k_tile, name):
    t, kp = x.shape
    k_rows, n = w.shape
    bm = _tile(t, MATMUL_TILE, SUBLANE)
    bn = _tile(n, FFN_DOWN_TILE, LANE)
    ni, nj, nk = t // bm, n // bn, kp // k_tile
    assert k_tile % (2 * SUBLANE * ni) == 0 and kp - k_rows < k_tile // ni
    ck = k_tile // ni
    nb = nj * nk

    def mm(bb, i):
        b = jnp.maximum(bb - 1, 0)
        return jnp.where(bb == 0, 0, i), b % nk, b // nk

    def chunk(bb, i):
        b = jnp.minimum(bb, nb - 1)
        return (b % nk) * ni + jnp.where(bb == nb, ni - 1, i), b // nk

    def out_idx(bb, i):
        i_m, k_m, j_m = mm(bb, i)
        return jnp.where(k_m == nk - 1, i_m, 0), j_m

    return pl.pallas_call(
        functools.partial(_ffn_down_round_kernel, nk=nk, k_rows=k_rows),
        grid=(nb + 1, ni),
        in_specs=[pl.BlockSpec((bm, k_tile), lambda bb, i: mm(bb, i)[:2]),
                  pl.BlockSpec((ck, bn), chunk)],
        out_specs=[pl.BlockSpec((bm, bn), out_idx), pl.BlockSpec((ck, bn), chunk)],
        out_shape=[jax.ShapeDtypeStruct((t, n), F32), jax.ShapeDtypeStruct((kp, n), BF16)],
        scratch_shapes=[pltpu.VMEM((2, k_tile, bn), BF16), pltpu.VMEM((t, bn), F32)],
        compiler_params=_params(("arbitrary", "arbitrary")),
        name=name,
    )(x, w)


def _post_mix_kernel(x_ref, y_ref, g1_ref, g2_ref, x1_ref, h2_ref):
    y = y_ref[...]
    x1 = x_ref[...] + (y * _rms_scale(y)) * g1_ref[...]
    x1_ref[...] = x1
    h2_ref[...] = ((x1 * _rms_scale(x1)) * g2_ref[...]).astype(h2_ref.dtype)


def _post_mix(x, y, g_post, g_pre):
    t, d = x.shape
    bt = _tile(t, 256, SUBLANE)
    row = pl.BlockSpec((bt, d), lambda i: (i, 0))
    vec = pl.BlockSpec((1, d), lambda i: (0, 0))
    return pl.pallas_call(
        _post_mix_kernel,
        grid=(t // bt,),
        in_specs=[row, row, vec, vec],
        out_specs=[row, row],
        out_shape=[jax.ShapeDtypeStruct((t, d), F32), jax.ShapeDtypeStruct((t, d), BF16)],
        compiler_params=_params(("arbitrary",)),
        name="post_mix",
    )(x, y, g_post.reshape(1, d), g_pre.reshape(1, d))


def _residual_norm_kernel(x_ref, y_ref, g_ref, o_ref):
    y = y_ref[...]
    o_ref[...] = x_ref[...] + (y * _rms_scale(y)) * g_ref[...]


def _residual_norm(x, y, g):
    t, d = x.shape
    bt = _tile(t, 256, SUBLANE)
    row = pl.BlockSpec((bt, d), lambda i: (i, 0))
    return pl.pallas_call(
        _residual_norm_kernel,
        grid=(t // bt,),
        in_specs=[row, row, pl.BlockSpec((1, d), lambda i: (0, 0))],
        out_specs=row,
        out_shape=jax.ShapeDtypeStruct((t, d), F32),
        compiler_params=_params(("arbitrary",)),
        name="residual_norm",
    )(x, y, g.reshape(1, d))


def _ffn_up_kernel(*refs, conv_w, s_len, d_ff):
    if s_len is None:
        (x_ref, wg_in, wu_in, cw_ref, cb_ref, out_ref, tail_ref, wg_out, wu_out,
         wg_buf, wu_buf, a_buf, u_buf, o_buf) = refs
        hist_refs = None
        jj, i = pl.program_id(0), pl.program_id(1)
        ck = wg_in.shape[0]

        def round_chunks():
            rows_c = pl.ds(pl.multiple_of(i * ck, ck), ck)
            for src, buf, dst in ((wg_in, wg_buf, wg_out), (wu_in, wu_buf, wu_out)):
                chunk = src[...].astype(BF16)
                buf[jj % 2, rows_c, :] = chunk
                dst[...] = chunk

        slot = (jj + 1) % 2
        wg = lambda cols: wg_buf[slot, :, cols]
        wu = lambda cols: wu_buf[slot, :, cols]
        col_blk = jj - 1
        row_split = 2 if out_ref.shape[0] % (4 * SUBLANE) == 0 else 1
    else:
        x_ref, wg_ref, wu_ref, cw_ref, cb_ref, h1_ref, h2_ref, out_ref, tail_ref, a_buf, u_buf, o_buf = refs
        hist_refs = {1: h1_ref, 2: h2_ref}
        i = pl.program_id(1)
        wg = lambda cols: wg_ref[:, cols]
        wu = lambda cols: wu_ref[:, cols]
        col_blk = pl.program_id(0)
        row_split = 1
    bm, bn = out_ref.shape
    hm = bm // row_split
    half = hm // 2
    top = SUBLANE
    n_slabs = bn // LANE
    per_strip = min(n_slabs, MXU_WIDTH // LANE)

    def units():
        @pl.when(i == 0)
        def _():
            a_buf[:, 0:top, :] = jnp.zeros((n_slabs, top, LANE), F32)

        lane = lax.broadcasted_iota(jnp.int32, (1, LANE), 1)
        for h in range(row_split):
            r0 = h * hm
            x = x_ref[r0:r0 + hm, :]
            for c0 in range(0, n_slabs, per_strip):
                c1 = min(c0 + per_strip, n_slabs)
                cols = slice(c0 * LANE, c1 * LANE)
                a = jnp.dot(x, wg(cols), preferred_element_type=F32)
                u = jnp.dot(x, wu(cols), preferred_element_type=F32)
                if s_len is None and h == 0 and c0 == 0:
                    round_chunks()
                if s_len is not None:
                    tail_ref[r0:r0 + hm, cols] = a
                elif h == row_split - 1:
                    tail_ref[:, cols] = a[hm - SUBLANE:, :]
                for c in range(c0, c1):
                    a_buf[c, top + r0:top + r0 + hm, :] = a[:, _lanes(c - c0)]
                    u_buf[c, r0:r0 + hm, :] = u[:, _lanes(c - c0)]
                    in_range = (col_blk * bn + c * LANE + lane) < d_ff
                    for e in range(2):
                        taps = [a_buf[c, pl.ds(top + r0 + e - lag, half, stride=2), :] for lag in range(conv_w)]
                        if hist_refs is not None:
                            s = 2 * (lax.broadcasted_iota(jnp.int32, (half, LANE), 0) % (s_len // 2)) + e
                            for lag in range(1, conv_w):
                                hist = hist_refs[lag][c, pl.ds(r0 + e, half, stride=2), :]
                                taps[lag] = jnp.where(s < lag, hist, taps[lag])
                        acc = cw_ref[0:1, _lanes(c)] * taps[conv_w - 1]
                        for k in range(1, conv_w):
                            acc = acc + cw_ref[k:k + 1, _lanes(c)] * taps[conv_w - 1 - k]
                        acc = cb_ref[:, _lanes(c)] + acc
                        val = _gelu(acc) * u_buf[c, pl.ds(r0 + e, half, stride=2), :]
                        o_buf[c, pl.ds(r0 + e, half, stride=2), :] = jnp.where(in_range, val, 0.0)
                    out_ref[r0:r0 + hm, _lanes(c)] = o_buf[c, r0:r0 + hm, :].astype(out_ref.dtype)
        if s_len is None:
            a_buf[:, 0:top, :] = a_buf[:, bm:bm + top, :]

    if s_len is None:
        pl.when(jj == 0)(round_chunks)
        pl.when(jj > 0)(units)
    else:
        units()


def _ffn_up_prompt(h, w_gate, w_up, conv_w, conv_b, ffp):
    t, d = h.shape
    d_ff = w_gate.shape[1]
    cw = conv_w.shape[0]
    assert cw == 3 and d_ff % LANE == 0
    bn = FFN_TILE
    assert ffp % bn == 0 and ffp - d_ff < bn
    bm = _tile(t, MATMUL_TILE, 4 * SUBLANE)
    ni, nj = t // bm, ffp // bn
    assert d % ni == 0
    ck = d // ni
    n_slabs = bn // LANE

    def fin_i(jj, i):
        return jnp.where(jj == 0, 0, i)

    def fin_j(jj):
        return jnp.maximum(jj - 1, 0)

    def chunk(jj, i):
        return jnp.where(jj == nj, ni - 1, i), jnp.minimum(jj, nj - 1)

    w_in_spec = pl.BlockSpec((ck, bn), chunk)
    wbuf = pltpu.VMEM((2, d, bn), BF16)
    act, tail, wg16, wu16 = pl.pallas_call(
        functools.partial(_ffn_up_kernel, conv_w=cw, s_len=None, d_ff=d_ff),
        grid=(nj + 1, ni),
        in_specs=[pl.BlockSpec((bm, d), lambda jj, i: (fin_i(jj, i), 0)), w_in_spec, w_in_spec,
                  pl.BlockSpec((cw, bn), lambda jj, i: (0, fin_j(jj))),
                  pl.BlockSpec((1, bn), lambda jj, i: (0, fin_j(jj)))],
        out_specs=[pl.BlockSpec((bm, bn), lambda jj, i: (fin_i(jj, i), fin_j(jj))),
                   pl.BlockSpec((SUBLANE, bn), lambda jj, i: (0, fin_j(jj))),
                   w_in_spec, w_in_spec],
        out_shape=[jax.ShapeDtypeStruct((t, ffp), BF16), jax.ShapeDtypeStruct((SUBLANE, d_ff), F32),
                   jax.ShapeDtypeStruct((d, d_ff), BF16), jax.ShapeDtypeStruct((d, d_ff), BF16)],
        scratch_shapes=[wbuf, wbuf,
                        pltpu.VMEM((n_slabs, bm + SUBLANE, LANE), F32), pltpu.VMEM((n_slabs, bm, LANE), F32),
                        pltpu.VMEM((n_slabs, bm, LANE), F32)],
        compiler_params=_params(("arbitrary", "arbitrary")),
        name="ffn_up",
    )(h, w_gate, w_up, conv_w, conv_b.reshape(1, d_ff))
    return act, tail, wg16, wu16


def _ffn_up_sample(h, w_gate, w_up, conv_w, conv_b, ffp, hist, s_len):
    t, d = h.shape
    d_ff = w_gate.shape[1]
    cw = conv_w.shape[0]
    assert cw == 3 and d_ff % LANE == 0 and s_len % 2 == 0
    bn = _tile(ffp, MATMUL_TILE, LANE)
    assert ffp - d_ff < bn
    n_slabs = bn // LANE
    w_spec = pl.BlockSpec((d, bn), lambda j, i: (0, j))
    out_spec = pl.BlockSpec((t, bn), lambda j, i: (i, j))

    def slabs(rows):
        return jnp.transpose(rows.reshape(t, d_ff // LANE, LANE), (1, 0, 2))
    h1 = slabs(jnp.pad(hist[:, 1:2], ((0, 0), (0, s_len - 1), (0, 0))))
    h2 = slabs(jnp.pad(hist, ((0, 0), (0, s_len - hist.shape[1]), (0, 0))))
    hist_spec = pl.BlockSpec((n_slabs, t, LANE), lambda j, i: (j, i, 0))
    return pl.pallas_call(
        functools.partial(_ffn_up_kernel, conv_w=cw, s_len=s_len, d_ff=d_ff),
        grid=(ffp // bn, 1),
        in_specs=[pl.BlockSpec((t, d), lambda j, i: (i, 0)), w_spec, w_spec,
                  pl.BlockSpec((cw, bn), lambda j, i: (0, j)), pl.BlockSpec((1, bn), lambda j, i: (0, j)),
                  hist_spec, hist_spec],
        out_specs=[out_spec, out_spec],
        out_shape=[jax.ShapeDtypeStruct((t, ffp), BF16), jax.ShapeDtypeStruct((t, d_ff), F32)],
        scratch_shapes=[pltpu.VMEM((n_slabs, t + SUBLANE, LANE), F32), pltpu.VMEM((n_slabs, t, LANE), F32),
                        pltpu.VMEM((n_slabs, t, LANE), F32)],
        compiler_params=_params(("arbitrary", "arbitrary")),
        name="ffn_up_decode",
    )(h, w_gate, w_up, conv_w, conv_b.reshape(1, d_ff), h1, h2)


def _layer(x, p, s_len=None, caches=None, conv_hist=None):
    t, d = x.shape
    d_a = p["ln_g"].shape[0]
    hpg = p["hpg"]
    gw = hpg * HEAD_DIM
    bn = p["bn"]
    o1 = 2 * d_a
    d_qkv = 3 * gw

    h = _rmsnorm_cast(x, p["g_pre_mix"])
    w16 = dict(p.get("w16", {}))
    z_parts = []
    for name, col0, n_cols, act in (("in_proj_lin", 0, o1 + 3 * d_qkv, lambda v: v),
                                    ("in_proj_gates", o1 + 3 * d_qkv, 2 * d, _sigmoid)):
        if s_len is None:
            z_part, w16[name] = _in_proj_round(h, p["w_in"], col0, n_cols, act, name)
        else:
            z_part = _in_proj(h, w16[name], _tile(n_cols, 2 * MATMUL_TILE, LANE), 0, n_cols, act, name + "_decode")
        z_parts.append(z_part)
    qkv_sl, gate_sl = z_parts

    if s_len is None:
        w_mix, b_mix = p["w_mix_prompt"], p["b_mix_prompt"]
    else:
        w_mix, b_mix = p["w_mix_sample"], p["b_mix_sample"]
    o_a, v_state = _sgu(qkv_sl, d_a, p["ln_g"], p["ln_b"], w_mix, b_mix)

    outs, lses = [], []
    for gi, (_, dil) in enumerate(DILATION_GROUPS):
        q_slab = o1 // LANE + gi * hpg
        k_slab = (o1 + d_qkv) // LANE + gi * hpg
        v_slab = (o1 + 2 * d_qkv) // LANE + gi * hpg
        bias_tab = p["rel_bias"][:, gi * hpg:(gi + 1) * hpg]
        if s_len is not None:
            o, lse = _attn_sample(qkv_sl, s_len, hpg, q_slab, k_slab, v_slab, caches[2 * gi], caches[2 * gi + 1],
                                  bias_tab, dil)
        elif gi < len(DILATION_GROUPS) - 1:
            o, lse = _attn_prompt(qkv_sl, hpg, q_slab, k_slab, v_slab, bias_tab, dil)
        else:
            o_b = _attn_prompt(qkv_sl, hpg, q_slab, k_slab, v_slab, bias_tab, dil, merge_with=(outs, lses))
            break
        outs.append(o)
        lses.append(lse)
    if s_len is not None:
        o_b = _merge(outs, lses)

    merged = _gated_proj(o_a, o_b, p["w_proj_a"], p["w_proj_b"], gate_sl, bn, 0, d // bn)
    if s_len is None:
        y, w16["out_proj"] = _in_proj_round(merged, p["w_out"], 0, d, lambda v: v, "out_proj", slab_out=False)
    else:
        y = _matmul(merged, w16["out_proj"], "out_proj_decode")
    x1, h2 = _post_mix(x, y, p["g_post_mix"], p["g_pre_ffn"])

    if s_len is None:
        act, a_tail, w16["ffn_gate"], w16["ffn_up"] = _ffn_up_prompt(h2, p["w_gate"], p["w_up"], p["conv_w"],
                                                                     p["conv_b"], p["ffp"])
    else:
        act, a_tail = _ffn_up_sample(h2, w16["ffn_gate"], w16["ffn_up"], p["conv_w"], p["conv_b"], p["ffp"],
                                     conv_hist, s_len)
    if s_len is None:
        f, w16["ffn_down"] = _ffn_down_round(act, p["w_down"], p["ffn_k_tile"], "ffn_down")
    else:
        f = _matmul(act, w16["ffn_down"], "ffn_down_decode", k_tile=p["ffn_k_tile"])
    y_out = _residual_norm(x1, f, p["g_post_ffn"])
    return y_out, qkv_sl, v_state, a_tail, w16


def kernel(x_prompt, x_sample, cache_k_g1, cache_v_g1, cache_k_g2, cache_v_g2, cache_k_g3, cache_v_g3, state_conv, g_pre_mix, w_in, sgu_ln_g, sgu_ln_b, w_spatial, b_spatial, rel_bias, w_proj_a, w_proj_b, w_out, g_post_mix, g_pre_ffn, w_gate, w_up, conv_w, conv_b, w_down, g_post_ffn):
    depth = w_in.shape[0]
    assert depth == 1
    n_prompt, seq, d = x_prompt.shape
    assert n_prompt == 1 and seq % CHUNK == 0
    n_seq, s_len, _ = x_sample.shape
    assert s_len == SUBLANE
    d_a = sgu_ln_g.shape[1]
    n_groups = w_spatial.shape[1]
    n_heads = rel_bias.shape[1]
    hpg = n_heads // len(DILATION_GROUPS)
    gw = hpg * HEAD_DIM
    d_ff = w_gate.shape[2]
    cw = conv_w.shape[1]
    bn = _tile(math.gcd(2 * d_a, gw), MATMUL_TILE, LANE)
    ffp = -(-d_ff // FFN_TILE) * FFN_TILE
    t_s = n_seq * s_len

    tri = np.tril(np.ones((CHUNK, CHUNK), np.float32))
    w_mix_prompt = (w_spatial[0] * tri).astype(BF16)
    b_mix_prompt = jnp.broadcast_to(b_spatial[0][:, :, None], (n_groups, CHUNK, LANE))
    w_small = w_spatial[0][:, :s_len, :s_len] * tri[:s_len, :s_len]
    eye = np.eye(n_seq, dtype=np.float32)
    w_mix_sample = jnp.einsum("ab,gpq->gapbq", eye, w_small).reshape(n_groups, t_s, t_s).astype(BF16)
    b_mix_sample = jnp.broadcast_to(jnp.tile(b_spatial[0][:, :s_len], (1, n_seq))[:, :, None], (n_groups, t_s, LANE))

    p = dict(
        hpg=hpg, bn=bn, ffp=ffp,
        g_pre_mix=g_pre_mix[0], w_in=w_in[0], ln_g=sgu_ln_g[0], ln_b=sgu_ln_b[0],
        w_mix_prompt=w_mix_prompt, b_mix_prompt=b_mix_prompt,
        w_mix_sample=w_mix_sample, b_mix_sample=b_mix_sample,
        rel_bias=rel_bias,
        w_proj_a=w_proj_a[0].astype(BF16), w_proj_b=w_proj_b[0].astype(BF16), w_out=w_out[0],
        g_post_mix=g_post_mix[0], g_pre_ffn=g_pre_ffn[0],
        w_gate=w_gate[0], w_up=w_up[0],
        conv_w=conv_w[0], conv_b=conv_b[0],
        w_down=w_down[0],
        ffn_k_tile=_tile(ffp, 3072, LANE),
        g_post_ffn=g_post_ffn[0],
    )

    caches = tuple(c[0] for c in (cache_k_g1, cache_v_g1, cache_k_g2, cache_v_g2, cache_k_g3, cache_v_g3))

    yp, zp, vp_state, ap_tail, w16 = _layer(x_prompt[0], p)
    ys, zs, vs_state, as_all, _ = _layer(x_sample.reshape(t_s, d), dict(p, w16=w16), s_len=s_len,
                                         caches=caches, conv_hist=state_conv[0])

    d_qkv = 3 * gw
    prompt_kv, sample_kv = [], []
    for gi, (win, _) in enumerate(DILATION_GROUPS):
        keep = min(win, seq)
        for base in (d_qkv, 2 * d_qkv):
            s0 = (2 * d_a + base + gi * gw) // LANE
            pk = jnp.transpose(zp[s0:s0 + hpg, seq - keep:, :], (1, 0, 2))
            prompt_kv.append(pk.reshape(1, 1, keep, hpg, HEAD_DIM))
            sk = jnp.transpose(zs[s0:s0 + hpg], (1, 0, 2))
            sample_kv.append(sk.reshape(1, n_seq, s_len, hpg, HEAD_DIM))
    p_conv = ap_tail[SUBLANE - (cw - 1):].reshape(1, 1, cw - 1, d_ff)
    s_conv = as_all.reshape(n_seq, s_len, d_ff)[:, s_len - (cw - 1):].reshape(1, n_seq, cw - 1, d_ff)
    return (yp.reshape(1, seq, d), ys.reshape(n_seq, s_len, d),
            *prompt_kv, vp_state.reshape(1, 1, CHUNK, d_a), p_conv,
            *sample_kv, vs_state.reshape(1, n_seq, s_len, d_a), s_conv)
```

```python
import functools
import math

import numpy as np
import jax
import jax.numpy as jnp
from jax import lax
from jax.experimental import pallas as pl
from jax.experimental.pallas import tpu as pltpu

F32 = jnp.float32
BF16 = jnp.bfloat16

HEAD_DIM = 128
STEPS = 128
CHUNK = 128
GROUP_DIM_A = 128
DILATION_GROUPS = ((128, 1), (512, 4), (2048, 16))
MAX_EXACT = 16
MAX_DISTANCE = 2048
EPS = 1e-6
NEG = -1e30
ATTN_SCALE = HEAD_DIM ** -0.5

LANE = 128
SUBLANE = 8
VMEM_LIMIT_BYTES = 56 * 1024 * 1024
MATMUL_TILE = 1024
MXU_WIDTH = 256
FFN_TILE = 512
IN_PROJ_ROUND_TILE = 1024
ATTN_UNITS_PER_STEP = 16
SGU_ROWS_PER_STEP = 512


def _tile(dim, target, align):
    best = None
    t = align
    while t <= min(dim, target):
        if dim % t == 0:
            best = t
        t += align
    return best if best is not None else dim


def _params(semantics):
    return pltpu.CompilerParams(dimension_semantics=semantics, vmem_limit_bytes=VMEM_LIMIT_BYTES)


def _gelu(x):
    return 0.5 * x * (1.0 + jnp.tanh(math.sqrt(2.0 / math.pi) * (x + 0.044715 * (x * x * x))))


def _sigmoid(x):
    return 1.0 / (1.0 + jnp.exp(-x))


def _rms_scale(x):
    return lax.rsqrt(jnp.mean(x * x, axis=-1, keepdims=True) + EPS)


def _lanes(c):
    return slice(c * LANE, (c + 1) * LANE)


def _rmsnorm_cast_kernel(x_ref, g_ref, o_ref):
    x = x_ref[...]
    o_ref[...] = ((x * _rms_scale(x)) * g_ref[...]).astype(o_ref.dtype)


def _rmsnorm_cast(x, g):
    t, d = x.shape
    bt = _tile(t, 256, SUBLANE)
    return pl.pallas_call(
        _rmsnorm_cast_kernel,
        grid=(t // bt,),
        in_specs=[pl.BlockSpec((bt, d), lambda i: (i, 0)), pl.BlockSpec((1, d), lambda i: (0, 0))],
        out_specs=pl.BlockSpec((bt, d), lambda i: (i, 0)),
        out_shape=jax.ShapeDtypeStruct((t, d), BF16),
        compiler_params=_params(("arbitrary",)),
        name="rmsnorm_cast",
    )(x, g.reshape(1, d))


def _in_proj_kernel(x_ref, w_ref, o_ref, *, act):
    acc = jnp.dot(x_ref[...], w_ref[...], preferred_element_type=F32)
    for c in range(o_ref.shape[0]):
        o_ref[c] = act(acc[:, _lanes(c)])


def _in_proj_round_kernel(x_ref, w_in, o_ref, w_out, w_buf, *, act, slab_out):
    jj, i = pl.program_id(0), pl.program_id(1)
    ck = w_in.shape[0]
    slot = (jj + 1) % 2
    bm = o_ref.shape[-2]
    n_slabs = w_in.shape[1] // LANE
    row_split = 2 if bm % (4 * SUBLANE) == 0 else 1
    hm = bm // row_split
    per_strip = min(n_slabs, MXU_WIDTH // LANE)

    def round_chunk():
        chunk = w_in[...].astype(BF16)
        w_buf[jj % 2, pl.ds(pl.multiple_of(i * ck, ck), ck), :] = chunk
        w_out[...] = chunk

    pl.when(jj == 0)(round_chunk)

    @pl.when(jj > 0)
    def _():
        round_chunk()
        for h in range(row_split):
            x = x_ref[h * hm:(h + 1) * hm, :]
            for c0 in range(0, n_slabs, per_strip):
                c1 = min(c0 + per_strip, n_slabs)
                acc = jnp.dot(x, w_buf[slot, :, c0 * LANE:c1 * LANE], preferred_element_type=F32)
                for c in range(c0, c1):
                    val = act(acc[:, _lanes(c - c0)])
                    if slab_out:
                        o_ref[c, h * hm:(h + 1) * hm, :] = val
                    else:
                        o_ref[h * hm:(h + 1) * hm, _lanes(c)] = val


def _in_proj_round(h, w, col0, n_cols, act, name, slab_out=True):
    t, d = h.shape
    bn = _tile(math.gcd(n_cols, col0) if col0 else n_cols, IN_PROJ_ROUND_TILE, LANE)
    bm = _tile(t, MATMUL_TILE, SUBLANE)
    ni, nj = t // bm, n_cols // bn
    assert d % ni == 0
    ck = d // ni

    def fin(jj, i):
        return jnp.maximum(jj - 1, 0), jnp.where(jj == 0, 0, i)

    def chunk(jj, i):
        return jnp.where(jj == nj, ni - 1, i), jnp.minimum(jj, nj - 1)

    if slab_out:
        out_spec = pl.BlockSpec((bn // LANE, bm, LANE), lambda jj, i: fin(jj, i) + (0,))
        out_shape = jax.ShapeDtypeStruct((n_cols // LANE, t, LANE), F32)
    else:
        out_spec = pl.BlockSpec((bm, bn), lambda jj, i: fin(jj, i)[::-1])
        out_shape = jax.ShapeDtypeStruct((t, n_cols), F32)
    return pl.pallas_call(
        functools.partial(_in_proj_round_kernel, act=act, slab_out=slab_out),
        grid=(nj + 1, ni),
        in_specs=[pl.BlockSpec((bm, d), lambda jj, i: (fin(jj, i)[1], 0)),
                  pl.BlockSpec((ck, bn), lambda jj, i: (chunk(jj, i)[0], col0 // bn + chunk(jj, i)[1]))],
        out_specs=[out_spec, pl.BlockSpec((ck, bn), chunk)],
        out_shape=[out_shape, jax.ShapeDtypeStruct((d, n_cols), BF16)],
        scratch_shapes=[pltpu.VMEM((2, d, bn), BF16)],
        compiler_params=_params(("arbitrary", "arbitrary")),
        name=name,
    )(h, w)


def _in_proj(h, w, bn, col0, n_cols, act, name):
    t, d = h.shape
    bm = _tile(t, MATMUL_TILE, SUBLANE)
    return pl.pallas_call(
        functools.partial(_in_proj_kernel, act=act),
        grid=(t // bm, n_cols // bn),
        in_specs=[pl.BlockSpec((bm, d), lambda i, j: (i, 0)),
                  pl.BlockSpec((d, bn), lambda i, j: (0, col0 // bn + j))],
        out_specs=pl.BlockSpec((bn // LANE, bm, LANE), lambda i, j: (j, i, 0)),
        out_shape=jax.ShapeDtypeStruct((n_cols // LANE, t, LANE), F32),
        compiler_params=_params(("arbitrary", "arbitrary")),
        name=name,
    )(h, w)


def _sgu_kernel(u_ref, v_ref, lg_ref, lb_ref, w_ref, b_ref, o_ref, vs_ref, *, n_groups):
    c = w_ref.shape[1]
    n_feat = n_groups * GROUP_DIM_A
    for r0 in range(0, o_ref.shape[0], c):
        vp = _gelu(v_ref[:, r0:r0 + c, :])
        mu = jnp.sum(jnp.sum(vp, axis=0), axis=-1, keepdims=True) / n_feat
        vc = vp - mu
        var = jnp.sum(jnp.sum(vc * vc, axis=0), axis=-1, keepdims=True) / n_feat
        v = vc * lax.rsqrt(var + EPS) * lg_ref[...] + lb_ref[...]
        for g in range(n_groups):
            if r0 + c == o_ref.shape[0]:
                vs_ref[:, _lanes(g)] = v[g]
            mixed = jnp.dot(w_ref[g], v[g].astype(BF16), preferred_element_type=F32) + b_ref[g]
            o_ref[r0:r0 + c, _lanes(g)] = (_gelu(u_ref[g, r0:r0 + c, :]) * mixed).astype(o_ref.dtype)


def _sgu(z_sl, d_a, ln_g, ln_b, w_mix, b_mix):
    t = z_sl.shape[1]
    n_groups, c, _ = w_mix.shape
    rows = _tile(t, SGU_ROWS_PER_STEP, c)
    slab = pl.BlockSpec((n_groups, 1, LANE), lambda i: (0, 0, 0))
    return pl.pallas_call(
        functools.partial(_sgu_kernel, n_groups=n_groups),
        grid=(t // rows,),
        in_specs=[
            pl.BlockSpec((n_groups, rows, LANE), lambda i: (0, i, 0)),
            pl.BlockSpec((n_groups, rows, LANE), lambda i: (1, i, 0)),
            slab, slab,
            pl.BlockSpec((n_groups, c, c), lambda i: (0, 0, 0)),
            pl.BlockSpec((n_groups, c, LANE), lambda i: (0, 0, 0)),
        ],
        out_specs=[pl.BlockSpec((rows, d_a), lambda i: (i, 0)), pl.BlockSpec((c, d_a), lambda i: (0, 0))],
        out_shape=[jax.ShapeDtypeStruct((t, d_a), BF16), jax.ShapeDtypeStruct((c, d_a), F32)],
        compiler_params=_params(("arbitrary",)),
        name="sgu",
    )(z_sl, z_sl, ln_g.reshape(n_groups, 1, LANE), ln_b.reshape(n_groups, 1, LANE), w_mix, b_mix)


def _t5_bucket(dist, n_buckets):
    n = np.asarray(dist, np.int32)
    safe = np.maximum(n, 1).astype(np.float32)
    large = MAX_EXACT + (np.log(safe / MAX_EXACT) / np.log(np.float32(MAX_DISTANCE / MAX_EXACT))
                         * (n_buckets - MAX_EXACT)).astype(np.int32)
    large = np.minimum(large, n_buckets - 1)
    return np.where(n < MAX_EXACT, n, large).astype(np.int32)


def _bias_lookup(bias_tab, bucket):
    n_buckets = bias_tab.shape[0]
    flat = np.asarray(bucket).reshape(-1)
    onehot = (jnp.asarray(flat)[None, :] == jnp.arange(n_buckets)[:, None]).astype(F32)
    out = jnp.dot(bias_tab.T, onehot, precision=lax.Precision.HIGHEST)
    return out.reshape((bias_tab.shape[1],) + tuple(np.asarray(bucket).shape))


def _merge_by_lse(outs, lses):
    m = functools.reduce(jnp.maximum, lses)
    w = [jnp.exp(l - m) for l in lses]
    num = functools.reduce(lambda x, y: x + y, [wi * oi for wi, oi in zip(w, outs)])
    return num / functools.reduce(lambda x, y: x + y, w)


def _attn_prompt_kernel(*refs, hb, dil, n_other):
    q_ref, kc_ref, vc_ref, bias_ref, band_ref = refs[:5]
    others = refs[5:5 + 2 * n_other]
    n_out = 1 if n_other else 2
    outs = refs[5 + 2 * n_other:5 + 2 * n_other + n_out]
    kp_ref, vp_ref = refs[5 + 2 * n_other + n_out:][:2]
    o_acc, lse_acc = refs[5 + 2 * n_other + n_out + 2:] if n_other else outs
    b = pl.program_id(1)

    @pl.when(b == 0)
    def _():
        kp_ref[...] = jnp.zeros(kp_ref.shape, F32)
        vp_ref[...] = jnp.zeros(vp_ref.shape, F32)

    span = STEPS * dil
    n_pb = q_ref.shape[1] // span
    col = lax.broadcasted_iota(jnp.int32, (STEPS, 2 * STEPS), 1)
    band = band_ref[...] > 0.5
    first = band & ((b > 0) | (col >= STEPS))
    for hh in range(hb):
        for pb in range(n_pb):
            for r in range(dil):
                def rows(block):
                    return pl.ds(block * span + r, STEPS, stride=dil) if dil > 1 else pl.ds(block * span, STEPS)
                k_prev = kp_ref[hh, rows(0), :] if pb == 0 else kc_ref[hh, rows(pb - 1), :]
                v_prev = vp_ref[hh, rows(0), :] if pb == 0 else vc_ref[hh, rows(pb - 1), :]
                q = q_ref[hh, rows(pb), :].astype(BF16)
                kk = jnp.concatenate([k_prev, kc_ref[hh, rows(pb), :]], axis=0).astype(BF16)
                vv = jnp.concatenate([v_prev, vc_ref[hh, rows(pb), :]], axis=0).astype(BF16)
                s = lax.dot_general(q, kk, (((1,), (1,)), ((), ())), preferred_element_type=F32) * ATTN_SCALE
                s = jnp.where(first if pb == 0 else band, s + bias_ref[hh], NEG)
                m = jnp.max(s, axis=-1, keepdims=True)
                e = jnp.exp(s - m)
                den = jnp.sum(e, axis=-1, keepdims=True)
                o_acc[hh, rows(pb), :] = jnp.dot(e.astype(BF16), vv, preferred_element_type=F32) / den
                lse_acc[hh, rows(pb), :] = jnp.broadcast_to(m + jnp.log(den), (STEPS, HEAD_DIM))
    kp_ref[...] = kc_ref[:, (n_pb - 1) * span:, :]
    vp_ref[...] = vc_ref[:, (n_pb - 1) * span:, :]
    if n_other:
        for hh in range(hb):
            merged = _merge_by_lse([r[hh] for r in others[:n_other]] + [o_acc[hh]],
                                   [r[hh] for r in others[n_other:]] + [lse_acc[hh]])
            outs[0][:, _lanes(hh)] = merged.astype(outs[0].dtype)


def _attn_prompt(z_sl, hpg, q_slab, k_slab, v_slab, bias_tab, dil, merge_with=None):
    t = z_sl.shape[1]
    span = STEPS * dil
    assert t % span == 0
    hb = min(hpg, max(1, ATTN_UNITS_PER_STEP // dil))
    assert hpg % hb == 0 and q_slab % hb == 0 and k_slab % hb == 0 and v_slab % hb == 0
    n_pb = math.gcd(t // span, max(1, ATTN_UNITS_PER_STEP // (hb * dil)))
    rows = span * n_pb
    nb = t // rows

    p_idx = np.arange(STEPS)[:, None]
    c_idx = np.arange(2 * STEPS)[None, :]
    steps = p_idx + STEPS - c_idx
    band = ((steps >= 0) & (steps <= STEPS)).astype(np.float32)
    bias = _bias_lookup(bias_tab, _t5_bucket(np.clip(steps, 0, STEPS) * dil, bias_tab.shape[0]))

    def cur(slab):
        return pl.BlockSpec((hb, rows, LANE), lambda hi, b: (slab // hb + hi, b, 0))

    head_major = pl.BlockSpec((hb, rows, LANE), lambda hi, b: (hi, b, 0))
    prev_block = pltpu.VMEM((hb, span, LANE), F32)
    block = pltpu.VMEM((hb, rows, LANE), F32)
    others = [] if merge_with is None else list(merge_with[0]) + list(merge_with[1])
    if merge_with is None:
        out_specs = [head_major, head_major]
        out_shape = [jax.ShapeDtypeStruct((hpg, t, LANE), F32)] * 2
        scratch = [prev_block, prev_block]
    else:
        out_specs = [pl.BlockSpec((rows, hb * LANE), lambda hi, b: (b, hi))]
        out_shape = [jax.ShapeDtypeStruct((t, hpg * LANE), BF16)]
        scratch = [prev_block, prev_block, block, block]
    res = pl.pallas_call(
        functools.partial(_attn_prompt_kernel, hb=hb, dil=dil, n_other=len(others) // 2),
        grid=(hpg // hb, nb),
        in_specs=[cur(q_slab), cur(k_slab), cur(v_slab),
                  pl.BlockSpec((hb, STEPS, 2 * STEPS), lambda hi, b: (hi, 0, 0)),
                  pl.BlockSpec((STEPS, 2 * STEPS), lambda hi, b: (0, 0))] + [head_major] * len(others),
        out_specs=out_specs,
        out_shape=out_shape,
        scratch_shapes=scratch,
        compiler_params=_params(("arbitrary", "arbitrary")),
        name=f"attn_prompt_d{dil}",
    )(z_sl, z_sl, z_sl, bias, jnp.asarray(band), *others)
    return res[0] if merge_with is not None else res


def _attn_sample_kernel(q_ref, kn_ref, vn_ref, ck_ref, cv_ref, bc_ref, mc_ref, bnew_ref, mnew_ref,
                        o_ref, lse_ref, *, hpg, s_len, grouped):
    n_keys = mc_ref.shape[1]
    n_new = hpg * s_len
    valid_c = mc_ref[...] > 0.5

    def cache_head(ref, h):
        if grouped:
            return ref[0, :, pl.ds(h, s_len, stride=hpg), :].reshape(n_keys, HEAD_DIM).astype(BF16)
        return ref[0, pl.ds(h, n_keys, stride=hpg) if hpg > 1 else pl.ds(0, n_keys), :].astype(BF16)

    q_all = q_ref[...].reshape(n_new, HEAD_DIM).astype(BF16)
    s_new = lax.dot_general(q_all, kn_ref[...].reshape(n_new, HEAD_DIM).astype(BF16), (((1,), (1,)), ((), ())),
                            preferred_element_type=F32) * ATTN_SCALE
    key_head = lax.broadcasted_iota(jnp.int32, (s_len, n_new), 1) // s_len
    new_ok = mnew_ref[...] > 0.5

    partial, e_new = [], []
    for h in range(hpg):
        q = q_all[h * s_len:(h + 1) * s_len, :]
        sc = lax.dot_general(q, cache_head(ck_ref, h), (((1,), (1,)), ((), ())),
                             preferred_element_type=F32) * ATTN_SCALE
        sc = jnp.where(valid_c, sc + bc_ref[h], NEG)
        sn = jnp.where(new_ok & (key_head == h), s_new[h * s_len:(h + 1) * s_len, :] + bnew_ref[h], NEG)
        m = jnp.maximum(jnp.max(sc, axis=-1, keepdims=True), jnp.max(sn, axis=-1, keepdims=True))
        ec = jnp.exp(sc - m)
        en = jnp.exp(sn - m)
        den = jnp.sum(ec, axis=-1, keepdims=True) + jnp.sum(en, axis=-1, keepdims=True)
        acc = jnp.dot(ec.astype(BF16), cache_head(cv_ref, h), preferred_element_type=F32)
        partial.append((acc, den, m))
        e_new.append(en)
    o_new = jnp.dot(jnp.concatenate(e_new, axis=0).astype(BF16),
                    vn_ref[...].reshape(n_new, HEAD_DIM).astype(BF16), preferred_element_type=F32)
    for h, (acc, den, m) in enumerate(partial):
        o_ref[h] = (acc + o_new[h * s_len:(h + 1) * s_len, :]) / den
        lse_ref[h] = jnp.broadcast_to(m + jnp.log(den), (s_len, HEAD_DIM))


def _attn_sample(z_sl, s_len, hpg, q_slab, k_slab, v_slab, cache_k, cache_v, bias_tab, dil):
    t = z_sl.shape[1]
    n_seq = t // s_len
    lc = cache_k.shape[1]
    assert q_slab % hpg == 0 and k_slab % hpg == 0 and v_slab % hpg == 0

    j = np.arange(STEPS + 1)
    idx = lc + np.arange(s_len)[:, None] - j[None, :] * dil
    assert idx.min() >= 0
    bucket = _t5_bucket(j * dil, bias_tab.shape[0])
    mask = np.zeros((s_len, lc + s_len), np.float32)
    bsel = np.zeros((s_len, lc + s_len), np.int32)
    for s in range(s_len):
        mask[s, idx[s]] = 1.0
        bsel[s, idx[s]] = bucket

    grouped = dil > s_len and lc % dil == 0 and s_len == SUBLANE
    if grouped:
        pos = (np.arange(lc // dil)[:, None] * dil + np.arange(s_len)[None, :]).reshape(-1)
        assert mask[:, :lc].sum() == mask[:, pos].sum()
        ck = cache_k.reshape(n_seq, lc // dil, dil * hpg, HEAD_DIM)
        cv = cache_v.reshape(n_seq, lc // dil, dil * hpg, HEAD_DIM)
        cache_spec = pl.BlockSpec((1, lc // dil, s_len * hpg, LANE), lambda b: (b, 0, 0, 0))
    else:
        pos = np.arange(lc)
        ck = cache_k.reshape(n_seq, lc * hpg, HEAD_DIM)
        cv = cache_v.reshape(n_seq, lc * hpg, HEAD_DIM)
        cache_spec = pl.BlockSpec((1, lc * hpg, LANE), lambda b: (b, 0, 0))
    n_keys = len(pos)
    bias_c = _bias_lookup(bias_tab, bsel[:, pos])
    bias_new = jnp.tile(_bias_lookup(bias_tab, bsel[:, lc:]), (1, 1, hpg))
    mask_c = jnp.asarray(mask[:, pos])
    mask_new = jnp.asarray(np.tile(mask[:, lc:], (1, hpg)))

    out_spec = pl.BlockSpec((hpg, s_len, LANE), lambda b: (0, b, 0))
    return pl.pallas_call(
        functools.partial(_attn_sample_kernel, hpg=hpg, s_len=s_len, grouped=grouped),
        grid=(n_seq,),
        in_specs=[pl.BlockSpec((hpg, s_len, LANE), lambda b: (q_slab // hpg, b, 0)),
                  pl.BlockSpec((hpg, s_len, LANE), lambda b: (k_slab // hpg, b, 0)),
                  pl.BlockSpec((hpg, s_len, LANE), lambda b: (v_slab // hpg, b, 0)),
                  cache_spec, cache_spec,
                  pl.BlockSpec((hpg, s_len, n_keys), lambda b: (0, 0, 0)),
                  pl.BlockSpec((s_len, n_keys), lambda b: (0, 0)),
                  pl.BlockSpec((hpg, s_len, hpg * s_len), lambda b: (0, 0, 0)),
                  pl.BlockSpec((s_len, hpg * s_len), lambda b: (0, 0))],
        out_specs=[out_spec, out_spec],
        out_shape=[jax.ShapeDtypeStruct((hpg, t, LANE), F32)] * 2,
        compiler_params=_params(("arbitrary",)),
        name=f"attn_sample_d{dil}",
    )(z_sl, z_sl, z_sl, ck, cv, bias_c, mask_c, bias_new, mask_new)


def _merge_kernel(o1, o2, o3, l1, l2, l3, out_ref):
    merged = _merge_by_lse([o1[...], o2[...], o3[...]], [l1[...], l2[...], l3[...]])
    for h in range(merged.shape[0]):
        out_ref[:, _lanes(h)] = merged[h].astype(out_ref.dtype)


def _merge(outs, lses):
    hpg, t, _ = outs[0].shape
    bt = _tile(t, 256, SUBLANE)
    spec = pl.BlockSpec((hpg, bt, LANE), lambda i: (0, i, 0))
    return pl.pallas_call(
        _merge_kernel,
        grid=(t // bt,),
        in_specs=[spec] * 6,
        out_specs=pl.BlockSpec((bt, hpg * LANE), lambda i: (i, 0)),
        out_shape=jax.ShapeDtypeStruct((t, hpg * LANE), BF16),
        compiler_params=_params(("arbitrary",)),
        name="merge_groups",
    )(*outs, *lses)


def _gated_proj_kernel(a_ref, b_ref, wa_ref, wb_ref, ga_ref, gb_ref, o_ref):
    n_slabs = ga_ref.shape[0]
    per_strip = min(n_slabs, MXU_WIDTH // LANE)
    for c0 in range(0, n_slabs, per_strip):
        c1 = min(c0 + per_strip, n_slabs)
        cols = slice(c0 * LANE, c1 * LANE)
        pa = jnp.dot(a_ref[...], wa_ref[:, cols], preferred_element_type=F32)
        pb = jnp.dot(b_ref[...], wb_ref[:, cols], preferred_element_type=F32)
        for c in range(c0, c1):
            gated = ga_ref[c] * pa[:, _lanes(c - c0)] + gb_ref[c] * pb[:, _lanes(c - c0)]
            o_ref[:, _lanes(c)] = gated.astype(o_ref.dtype)


def _gated_proj(o_a, o_b, w_a, w_b, z_sl, bn, gate_a_blk, gate_b_blk):
    t, d_a = o_a.shape
    d_b = o_b.shape[1]
    d = w_a.shape[1]
    bm = _tile(t, MATMUL_TILE, SUBLANE)
    return pl.pallas_call(
        _gated_proj_kernel,
        grid=(t // bm, d // bn),
        in_specs=[pl.BlockSpec((bm, d_a), lambda i, j: (i, 0)),
                  pl.BlockSpec((bm, d_b), lambda i, j: (i, 0)),
                  pl.BlockSpec((d_a, bn), lambda i, j: (0, j)),
                  pl.BlockSpec((d_b, bn), lambda i, j: (0, j)),
                  pl.BlockSpec((bn // LANE, bm, LANE), lambda i, j: (gate_a_blk + j, i, 0)),
                  pl.BlockSpec((bn // LANE, bm, LANE), lambda i, j: (gate_b_blk + j, i, 0))],
        out_specs=pl.BlockSpec((bm, bn), lambda i, j: (i, j)),
        out_shape=jax.ShapeDtypeStruct((t, d), BF16),
        compiler_params=_params(("arbitrary", "arbitrary")),
        name="gated_proj",
    )(o_a, o_b, w_a, w_b, z_sl, z_sl)


def _matmul_kernel(x_ref, w_ref, o_ref, *, nk):
    if nk == 1:
        o_ref[...] = jnp.dot(x_ref[...], w_ref[...], preferred_element_type=F32)
    else:
        @pl.when(pl.program_id(2) == 0)
        def _():
            o_ref[...] = jnp.zeros(o_ref.shape, o_ref.dtype)

        o_ref[...] += jnp.dot(x_ref[...], w_ref[...], preferred_element_type=F32)


def _matmul(x, w, name, k_tile=None):
    t, kd = x.shape
    n = w.shape[1]
    bm = _tile(t, MATMUL_TILE, SUBLANE)
    bn = _tile(n, MATMUL_TILE, LANE)
    bk = kd if k_tile is None else k_tile
    nk = kd // bk
    return pl.pallas_call(
        functools.partial(_matmul_kernel, nk=nk),
        grid=(t // bm, n // bn, nk),
        in_specs=[pl.BlockSpec((bm, bk), lambda i, j, k: (i, k)),
                  pl.BlockSpec((bk, bn), lambda i, j, k: (k, j))],
        out_specs=pl.BlockSpec((bm, bn), lambda i, j, k: (i, j)),
        out_shape=jax.ShapeDtypeStruct((t, n), F32),
        compiler_params=_params(("arbitrary", "arbitrary", "arbitrary")),
        name=name,
    )(x, w)


def _post_mix_kernel(x_ref, y_ref, g1_ref, g2_ref, x1_ref, h2_ref):
    y = y_ref[...]
    x1 = x_ref[...] + (y * _rms_scale(y)) * g1_ref[...]
    x1_ref[...] = x1
    h2_ref[...] = ((x1 * _rms_scale(x1)) * g2_ref[...]).astype(h2_ref.dtype)


def _post_mix(x, y, g_post, g_pre):
    t, d = x.shape
    bt = _tile(t, 256, SUBLANE)
    row = pl.BlockSpec((bt, d), lambda i: (i, 0))
    vec = pl.BlockSpec((1, d), lambda i: (0, 0))
    return pl.pallas_call(
        _post_mix_kernel,
        grid=(t // bt,),
        in_specs=[row, row, vec, vec],
        out_specs=[row, row],
        out_shape=[jax.ShapeDtypeStruct((t, d), F32), jax.ShapeDtypeStruct((t, d), BF16)],
        compiler_params=_params(("arbitrary",)),
        name="post_mix",
    )(x, y, g_post.reshape(1, d), g_pre.reshape(1, d))


def _residual_norm_kernel(x_ref, y_ref, g_ref, o_ref):
    y = y_ref[...]
    o_ref[...] = x_ref[...] + (y * _rms_scale(y)) * g_ref[...]


def _residual_norm(x, y, g):
    t, d = x.shape
    bt = _tile(t, 256, SUBLANE)
    row = pl.BlockSpec((bt, d), lambda i: (i, 0))
    return pl.pallas_call(
        _residual_norm_kernel,
        grid=(t // bt,),
        in_specs=[row, row, pl.BlockSpec((1, d), lambda i: (0, 0))],
        out_specs=row,
        out_shape=jax.ShapeDtypeStruct((t, d), F32),
        compiler_params=_params(("arbitrary",)),
        name="residual_norm",
    )(x, y, g.reshape(1, d))


def _ffn_up_kernel(*refs, conv_w, s_len, d_ff):
    if s_len is None:
        (x_ref, wg_in, wu_in, wd_in, cw_ref, cb_ref, out_ref, tail_ref, wg_out, wu_out, wd_out,
         wg_buf, wu_buf, a_buf, u_buf, o_buf) = refs
        hist_refs = None
        jj, i = pl.program_id(0), pl.program_id(1)
        ck = wg_in.shape[0]

        def round_chunks():
            rows_c = pl.ds(pl.multiple_of(i * ck, ck), ck)
            for src, buf, dst in ((wg_in, wg_buf, wg_out), (wu_in, wu_buf, wu_out)):
                chunk = src[...].astype(BF16)
                buf[jj % 2, rows_c, :] = chunk
                dst[...] = chunk

        def round_down_rows():
            rc = wd_in.shape[0]
            row = ((jj - 1) * pl.num_programs(1) + i) * rc + lax.broadcasted_iota(jnp.int32, wd_in.shape, 0)
            wd_out[...] = jnp.where(row < d_ff, wd_in[...], 0.0).astype(BF16)

        slot = (jj + 1) % 2
        wg = lambda cols: wg_buf[slot, :, cols]
        wu = lambda cols: wu_buf[slot, :, cols]
        col_blk = jj - 1
        row_split = 2 if out_ref.shape[0] % (4 * SUBLANE) == 0 else 1
    else:
        x_ref, wg_ref, wu_ref, cw_ref, cb_ref, h1_ref, h2_ref, out_ref, tail_ref, a_buf, u_buf, o_buf = refs
        hist_refs = {1: h1_ref, 2: h2_ref}
        i = pl.program_id(1)
        wg = lambda cols: wg_ref[:, cols]
        wu = lambda cols: wu_ref[:, cols]
        col_blk = pl.program_id(0)
        row_split = 1
    bm, bn = out_ref.shape
    hm = bm // row_split
    half = hm // 2
    top = SUBLANE
    n_slabs = bn // LANE
    per_strip = min(n_slabs, MXU_WIDTH // LANE)

    def units():
        @pl.when(i == 0)
        def _():
            a_buf[:, 0:top, :] = jnp.zeros((n_slabs, top, LANE), F32)

        lane = lax.broadcasted_iota(jnp.int32, (1, LANE), 1)
        for h in range(row_split):
            r0 = h * hm
            x = x_ref[r0:r0 + hm, :]
            for c0 in range(0, n_slabs, per_strip):
                c1 = min(c0 + per_strip, n_slabs)
                cols = slice(c0 * LANE, c1 * LANE)
                a = jnp.dot(x, wg(cols), preferred_element_type=F32)
                u = jnp.dot(x, wu(cols), preferred_element_type=F32)
                if s_len is None and h == 0 and c0 == 0:
                    round_chunks()
                    round_down_rows()
                if s_len is not None:
                    tail_ref[r0:r0 + hm, cols] = a
                elif h == row_split - 1:
                    tail_ref[:, cols] = a[hm - SUBLANE:, :]
                for c in range(c0, c1):
                    a_buf[c, top + r0:top + r0 + hm, :] = a[:, _lanes(c - c0)]
                    u_buf[c, r0:r0 + hm, :] = u[:, _lanes(c - c0)]
                    in_range = (col_blk * bn + c * LANE + lane) < d_ff
                    for e in range(2):
                        taps = [a_buf[c, pl.ds(top + r0 + e - lag, half, stride=2), :] for lag in range(conv_w)]
                        if hist_refs is not None:
                            s = 2 * (lax.broadcasted_iota(jnp.int32, (half, LANE), 0) % (s_len // 2)) + e
                            for lag in range(1, conv_w):
                                hist = hist_refs[lag][c, pl.ds(r0 + e, half, stride=2), :]
                                taps[lag] = jnp.where(s < lag, hist, taps[lag])
                        acc = cw_ref[0:1, _lanes(c)] * taps[conv_w - 1]
                        for k in range(1, conv_w):
                            acc = acc + cw_ref[k:k + 1, _lanes(c)] * taps[conv_w - 1 - k]
                        acc = cb_ref[:, _lanes(c)] + acc
                        val = _gelu(acc) * u_buf[c, pl.ds(r0 + e, half, stride=2), :]
                        o_buf[c, pl.ds(r0 + e, half, stride=2), :] = jnp.where(in_range, val, 0.0)
                    out_ref[r0:r0 + hm, _lanes(c)] = o_buf[c, r0:r0 + hm, :].astype(out_ref.dtype)
        if s_len is None:
            a_buf[:, 0:top, :] = a_buf[:, bm:bm + top, :]

    if s_len is None:
        pl.when(jj == 0)(round_chunks)
        pl.when(jj > 0)(units)
    else:
        units()


def _ffn_up_prompt(h, w_gate, w_up, w_down, conv_w, conv_b, ffp):
    t, d = h.shape
    d_ff = w_gate.shape[1]
    cw = conv_w.shape[0]
    assert cw == 3 and d_ff % LANE == 0
    bn = FFN_TILE
    assert ffp % bn == 0 and ffp - d_ff < bn
    bm = _tile(t, MATMUL_TILE, 4 * SUBLANE)
    ni, nj = t // bm, ffp // bn
    assert d % ni == 0
    ck = d // ni
    n_slabs = bn // LANE

    def fin_i(jj, i):
        return jnp.where(jj == 0, 0, i)

    def fin_j(jj):
        return jnp.maximum(jj - 1, 0)

    def chunk(jj, i):
        return jnp.where(jj == nj, ni - 1, i), jnp.minimum(jj, nj - 1)

    n_out = w_down.shape[1]
    assert ffp % (nj * ni) == 0
    rc = ffp // (nj * ni)
    assert rc % (2 * SUBLANE) == 0

    def down_blk(jj, i):
        return fin_j(jj) * ni + fin_i(jj, i)

    last_src = -(-d_ff // rc) - 1
    w_in_spec = pl.BlockSpec((ck, bn), chunk)
    wbuf = pltpu.VMEM((2, d, bn), BF16)
    return pl.pallas_call(
        functools.partial(_ffn_up_kernel, conv_w=cw, s_len=None, d_ff=d_ff),
        grid=(nj + 1, ni),
        in_specs=[pl.BlockSpec((bm, d), lambda jj, i: (fin_i(jj, i), 0)), w_in_spec, w_in_spec,
                  pl.BlockSpec((rc, n_out), lambda jj, i: (jnp.minimum(down_blk(jj, i), last_src), 0)),
                  pl.BlockSpec((cw, bn), lambda jj, i: (0, fin_j(jj))),
                  pl.BlockSpec((1, bn), lambda jj, i: (0, fin_j(jj)))],
        out_specs=[pl.BlockSpec((bm, bn), lambda jj, i: (fin_i(jj, i), fin_j(jj))),
                   pl.BlockSpec((SUBLANE, bn), lambda jj, i: (0, fin_j(jj))),
                   w_in_spec, w_in_spec,
                   pl.BlockSpec((rc, n_out), lambda jj, i: (down_blk(jj, i), 0))],
        out_shape=[jax.ShapeDtypeStruct((t, ffp), BF16), jax.ShapeDtypeStruct((SUBLANE, d_ff), F32),
                   jax.ShapeDtypeStruct((d, d_ff), BF16), jax.ShapeDtypeStruct((d, d_ff), BF16),
                   jax.ShapeDtypeStruct((ffp, n_out), BF16)],
        scratch_shapes=[wbuf, wbuf,
                        pltpu.VMEM((n_slabs, bm + SUBLANE, LANE), F32), pltpu.VMEM((n_slabs, bm, LANE), F32),
                        pltpu.VMEM((n_slabs, bm, LANE), F32)],
        compiler_params=_params(("arbitrary", "arbitrary")),
        name="ffn_up",
    )(h, w_gate, w_up, w_down, conv_w, conv_b.reshape(1, d_ff))


def _ffn_up_sample(h, w_gate, w_up, conv_w, conv_b, ffp, hist, s_len):
    t, d = h.shape
    d_ff = w_gate.shape[1]
    cw = conv_w.shape[0]
    assert cw == 3 and d_ff % LANE == 0 and s_len % 2 == 0
    bn = _tile(ffp, MATMUL_TILE, LANE)
    assert ffp - d_ff < bn
    n_slabs = bn // LANE
    w_spec = pl.BlockSpec((d, bn), lambda j, i: (0, j))
    out_spec = pl.BlockSpec((t, bn), lambda j, i: (i, j))

    def slabs(rows):
        return jnp.transpose(rows.reshape(t, d_ff // LANE, LANE), (1, 0, 2))
    h1 = slabs(jnp.pad(hist[:, 1:2], ((0, 0), (0, s_len - 1), (0, 0))))
    h2 = slabs(jnp.pad(hist, ((0, 0), (0, s_len - hist.shape[1]), (0, 0))))
    hist_spec = pl.BlockSpec((n_slabs, t, LANE), lambda j, i: (j, i, 0))
    return pl.pallas_call(
        functools.partial(_ffn_up_kernel, conv_w=cw, s_len=s_len, d_ff=d_ff),
        grid=(ffp // bn, 1),
        in_specs=[pl.BlockSpec((t, d), lambda j, i: (i, 0)), w_spec, w_spec,
                  pl.BlockSpec((cw, bn), lambda j, i: (0, j)), pl.BlockSpec((1, bn), lambda j, i: (0, j)),
                  hist_spec, hist_spec],
        out_specs=[out_spec, out_spec],
        out_shape=[jax.ShapeDtypeStruct((t, ffp), BF16), jax.ShapeDtypeStruct((t, d_ff), F32)],
        scratch_shapes=[pltpu.VMEM((n_slabs, t + SUBLANE, LANE), F32), pltpu.VMEM((n_slabs, t, LANE), F32),
                        pltpu.VMEM((n_slabs, t, LANE), F32)],
        compiler_params=_params(("arbitrary", "arbitrary")),
        name="ffn_up_decode",
    )(h, w_gate, w_up, conv_w, conv_b.reshape(1, d_ff), h1, h2)


def _layer(x, p, s_len=None, caches=None, conv_hist=None):
    t, d = x.shape
    d_a = p["ln_g"].shape[0]
    hpg = p["hpg"]
    gw = hpg * HEAD_DIM
    bn = p["bn"]
    o1 = 2 * d_a
    d_qkv = 3 * gw

    h = _rmsnorm_cast(x, p["g_pre_mix"])
    w16 = dict(p.get("w16", {}))
    z_parts = []
    for name, col0, n_cols, act in (("in_proj_lin", 0, o1 + 3 * d_qkv, lambda v: v),
                                    ("in_proj_gates", o1 + 3 * d_qkv, 2 * d, _sigmoid)):
        if s_len is None:
            z_part, w16[name] = _in_proj_round(h, p["w_in"], col0, n_cols, act, name)
        else:
            z_part = _in_proj(h, w16[name], _tile(n_cols, 2 * MATMUL_TILE, LANE), 0, n_cols, act, name + "_decode")
        z_parts.append(z_part)
    qkv_sl, gate_sl = z_parts

    if s_len is None:
        w_mix, b_mix = p["w_mix_prompt"], p["b_mix_prompt"]
    else:
        w_mix, b_mix = p["w_mix_sample"], p["b_mix_sample"]
    o_a, v_state = _sgu(qkv_sl, d_a, p["ln_g"], p["ln_b"], w_mix, b_mix)

    outs, lses = [], []
    for gi, (_, dil) in enumerate(DILATION_GROUPS):
        q_slab = o1 // LANE + gi * hpg
        k_slab = (o1 + d_qkv) // LANE + gi * hpg
        v_slab = (o1 + 2 * d_qkv) // LANE + gi * hpg
        bias_tab = p["rel_bias"][:, gi * hpg:(gi + 1) * hpg]
        if s_len is not None:
            o, lse = _attn_sample(qkv_sl, s_len, hpg, q_slab, k_slab, v_slab, caches[2 * gi], caches[2 * gi + 1],
                                  bias_tab, dil)
        elif gi < len(DILATION_GROUPS) - 1:
            o, lse = _attn_prompt(qkv_sl, hpg, q_slab, k_slab, v_slab, bias_tab, dil)
        else:
            o_b = _attn_prompt(qkv_sl, hpg, q_slab, k_slab, v_slab, bias_tab, dil, merge_with=(outs, lses))
            break
        outs.append(o)
        lses.append(lse)
    if s_len is not None:
        o_b = _merge(outs, lses)

    merged = _gated_proj(o_a, o_b, p["w_proj_a"], p["w_proj_b"], gate_sl, bn, 0, d // bn)
    if s_len is None:
        y, w16["out_proj"] = _in_proj_round(merged, p["w_out"], 0, d, lambda v: v, "out_proj", slab_out=False)
    else:
        y = _matmul(merged, w16["out_proj"], "out_proj_decode")
    x1, h2 = _post_mix(x, y, p["g_post_mix"], p["g_pre_ffn"])

    if s_len is None:
        act, a_tail, w16["ffn_gate"], w16["ffn_up"], w16["ffn_down"] = _ffn_up_prompt(
            h2, p["w_gate"], p["w_up"], p["w_down"], p["conv_w"], p["conv_b"], p["ffp"])
    else:
        act, a_tail = _ffn_up_sample(h2, w16["ffn_gate"], w16["ffn_up"], p["conv_w"], p["conv_b"], p["ffp"],
                                     conv_hist, s_len)
    f = _matmul(act, w16["ffn_down"], "ffn_down", k_tile=p["ffn_k_tile"])
    y_out = _residual_norm(x1, f, p["g_post_ffn"])
    return y_out, qkv_sl, v_state, a_tail, w16


def kernel(x_prompt, x_sample, cache_k_g1, cache_v_g1, cache_k_g2, cache_v_g2, cache_k_g3, cache_v_g3, state_conv, g_pre_mix, w_in, sgu_ln_g, sgu_ln_b, w_spatial, b_spatial, rel_bias, w_proj_a, w_proj_b, w_out, g_post_mix, g_pre_ffn, w_gate, w_up, conv_w, conv_b, w_down, g_post_ffn):
    depth = w_in.shape[0]
    assert depth == 1
    n_prompt, seq, d = x_prompt.shape
    assert n_prompt == 1 and seq % CHUNK == 0
    n_seq, s_len, _ = x_sample.shape
    assert s_len == SUBLANE
    d_a = sgu_ln_g.shape[1]
    n_groups = w_spatial.shape[1]
    n_heads = rel_bias.shape[1]
    hpg = n_heads // len(DILATION_GROUPS)
    gw = hpg * HEAD_DIM
    d_ff = w_gate.shape[2]
    cw = conv_w.shape[1]
    bn = _tile(math.gcd(2 * d_a, gw), MATMUL_TILE, LANE)
    ffp = -(-d_ff // FFN_TILE) * FFN_TILE
    t_s = n_seq * s_len

    tri = np.tril(np.ones((CHUNK, CHUNK), np.float32))
    w_mix_prompt = (w_spatial[0] * tri).astype(BF16)
    b_mix_prompt = jnp.broadcast_to(b_spatial[0][:, :, None], (n_groups, CHUNK, LANE))
    w_small = w_spatial[0][:, :s_len, :s_len] * tri[:s_len, :s_len]
    eye = np.eye(n_seq, dtype=np.float32)
    w_mix_sample = jnp.einsum("ab,gpq->gapbq", eye, w_small).reshape(n_groups, t_s, t_s).astype(BF16)
    b_mix_sample = jnp.broadcast_to(jnp.tile(b_spatial[0][:, :s_len], (1, n_seq))[:, :, None], (n_groups, t_s, LANE))

    p = dict(
        hpg=hpg, bn=bn, ffp=ffp,
        g_pre_mix=g_pre_mix[0], w_in=w_in[0], ln_g=sgu_ln_g[0], ln_b=sgu_ln_b[0],
        w_mix_prompt=w_mix_prompt, b_mix_prompt=b_mix_prompt,
        w_mix_sample=w_mix_sample, b_mix_sample=b_mix_sample,
        rel_bias=rel_bias,
        w_proj_a=w_proj_a[0].astype(BF16), w_proj_b=w_proj_b[0].astype(BF16), w_out=w_out[0],
        g_post_mix=g_post_mix[0], g_pre_ffn=g_pre_ffn[0],
        w_gate=w_gate[0], w_up=w_up[0],
        conv_w=conv_w[0], conv_b=conv_b[0],
        w_down=w_down[0],
        ffn_k_tile=_tile(ffp, 3072, LANE),
        g_post_ffn=g_post_ffn[0],
    )

    caches = tuple(c[0] for c in (cache_k_g1, cache_v_g1, cache_k_g2, cache_v_g2, cache_k_g3, cache_v_g3))

    yp, zp, vp_state, ap_tail, w16 = _layer(x_prompt[0], p)
    ys, zs, vs_state, as_all, _ = _layer(x_sample.reshape(t_s, d), dict(p, w16=w16), s_len=s_len,
                                         caches=caches, conv_hist=state_conv[0])

    d_qkv = 3 * gw
    prompt_kv, sample_kv = [], []
    for gi, (win, _) in enumerate(DILATION_GROUPS):
        keep = min(win, seq)
        for base in (d_qkv, 2 * d_qkv):
            s0 = (2 * d_a + base + gi * gw) // LANE
            pk = jnp.transpose(zp[s0:s0 + hpg, seq - keep:, :], (1, 0, 2))
            prompt_kv.append(pk.reshape(1, 1, keep, hpg, HEAD_DIM))
            sk = jnp.transpose(zs[s0:s0 + hpg], (1, 0, 2))
            sample_kv.append(sk.reshape(1, n_seq, s_len, hpg, HEAD_DIM))
    p_conv = ap_tail[SUBLANE - (cw - 1):].reshape(1, 1, cw - 1, d_ff)
    s_conv = as_all.reshape(n_seq, s_len, d_ff)[:, s_len - (cw - 1):].reshape(1, n_seq, cw - 1, d_ff)
    return (yp.reshape(1, seq, d), ys.reshape(n_seq, s_len, d),
            *prompt_kv, vp_state.reshape(1, 1, CHUNK, d_a), p_conv,
            *sample_kv, vs_state.reshape(1, n_seq, s_len, d_a), s_conv)
```

```python
import functools
import math

import numpy as np
import jax
import jax.numpy as jnp
from jax import lax
from jax.experimental import pallas as pl
from jax.experimental.pallas import tpu as pltpu

F32 = jnp.float32
BF16 = jnp.bfloat16

HEAD_DIM = 128
STEPS = 128
CHUNK = 128
GROUP_DIM_A = 128
DILATION_GROUPS = ((128, 1), (512, 4), (2048, 16))
MAX_EXACT = 16
MAX_DISTANCE = 2048
EPS = 1e-6
NEG = -1e30
ATTN_SCALE = HEAD_DIM ** -0.5

LANE = 128
SUBLANE = 8
VMEM_LIMIT_BYTES = 56 * 1024 * 1024
MATMUL_TILE = 1024
MXU_WIDTH = 256
FFN_TILE = 512
IN_PROJ_ROUND_TILE = 1024
ATTN_UNITS_PER_STEP = 16
SGU_ROWS_PER_STEP = 512


def _tile(dim, target, align):
    best = None
    t = align
    while t <= min(dim, target):
        if dim % t == 0:
            best = t
        t += align
    return best if best is not None else dim


def _params(semantics):
    return pltpu.CompilerParams(dimension_semantics=semantics, vmem_limit_bytes=VMEM_LIMIT_BYTES)


def _gelu(x):
    return 0.5 * x * (1.0 + jnp.tanh(math.sqrt(2.0 / math.pi) * (x + 0.044715 * (x * x * x))))


def _sigmoid(x):
    return 1.0 / (1.0 + jnp.exp(-x))


def _rms_scale(x):
    return lax.rsqrt(jnp.mean(x * x, axis=-1, keepdims=True) + EPS)


def _lanes(c):
    return slice(c * LANE, (c + 1) * LANE)


def _rmsnorm_cast_kernel(x_ref, g_ref, o_ref):
    x = x_ref[...]
    o_ref[...] = ((x * _rms_scale(x)) * g_ref[...]).astype(o_ref.dtype)


def _rmsnorm_cast(x, g):
    t, d = x.shape
    bt = _tile(t, 256, SUBLANE)
    return pl.pallas_call(
        _rmsnorm_cast_kernel,
        grid=(t // bt,),
        in_specs=[pl.BlockSpec((bt, d), lambda i: (i, 0)), pl.BlockSpec((1, d), lambda i: (0, 0))],
        out_specs=pl.BlockSpec((bt, d), lambda i: (i, 0)),
        out_shape=jax.ShapeDtypeStruct((t, d), BF16),
        compiler_params=_params(("arbitrary",)),
        name="rmsnorm_cast",
    )(x, g.reshape(1, d))


def _in_proj_kernel(x_ref, w_ref, o_ref, *, act):
    acc = jnp.dot(x_ref[...], w_ref[...], preferred_element_type=F32)
    for c in range(o_ref.shape[0]):
        o_ref[c] = act(acc[:, _lanes(c)])


def _in_proj_round_kernel(x_ref, w_in, o_ref, w_out, w_buf, *, act, slab_out):
    jj, i = pl.program_id(0), pl.program_id(1)
    ck = w_in.shape[0]
    slot = (jj + 1) % 2
    bm = o_ref.shape[-2]
    n_slabs = w_in.shape[1] // LANE
    row_split = 2 if bm % (4 * SUBLANE) == 0 else 1
    hm = bm // row_split
    per_strip = min(n_slabs, MXU_WIDTH // LANE)

    def round_chunk():
        chunk = w_in[...].astype(BF16)
        w_buf[jj % 2, pl.ds(pl.multiple_of(i * ck, ck), ck), :] = chunk
        w_out[...] = chunk

    pl.when(jj == 0)(round_chunk)

    @pl.when(jj > 0)
    def _():
        round_chunk()
        for h in range(row_split):
            x = x_ref[h * hm:(h + 1) * hm, :]
            for c0 in range(0, n_slabs, per_strip):
                c1 = min(c0 + per_strip, n_slabs)
                acc = jnp.dot(x, w_buf[slot, :, c0 * LANE:c1 * LANE], preferred_element_type=F32)
                for c in range(c0, c1):
                    val = act(acc[:, _lanes(c - c0)])
                    if slab_out:
                        o_ref[c, h * hm:(h + 1) * hm, :] = val
                    else:
                        o_ref[h * hm:(h + 1) * hm, _lanes(c)] = val


def _in_proj_round(h, w, col0, n_cols, act, name, slab_out=True):
    t, d = h.shape
    bn = _tile(math.gcd(n_cols, col0) if col0 else n_cols, IN_PROJ_ROUND_TILE, LANE)
    bm = _tile(t, MATMUL_TILE, SUBLANE)
    ni, nj = t // bm, n_cols // bn
    assert d % ni == 0
    ck = d // ni

    def fin(jj, i):
        return jnp.maximum(jj - 1, 0), jnp.where(jj == 0, 0, i)

    def chunk(jj, i):
        return jnp.where(jj == nj, ni - 1, i), jnp.minimum(jj, nj - 1)

    if slab_out:
        out_spec = pl.BlockSpec((bn // LANE, bm, LANE), lambda jj, i: fin(jj, i) + (0,))
        out_shape = jax.ShapeDtypeStruct((n_cols // LANE, t, LANE), F32)
    else:
        out_spec = pl.BlockSpec((bm, bn), lambda jj, i: fin(jj, i)[::-1])
        out_shape = jax.ShapeDtypeStruct((t, n_cols), F32)
    return pl.pallas_call(
        functools.partial(_in_proj_round_kernel, act=act, slab_out=slab_out),
        grid=(nj + 1, ni),
        in_specs=[pl.BlockSpec((bm, d), lambda jj, i: (fin(jj, i)[1], 0)),
                  pl.BlockSpec((ck, bn), lambda jj, i: (chunk(jj, i)[0], col0 // bn + chunk(jj, i)[1]))],
        out_specs=[out_spec, pl.BlockSpec((ck, bn), chunk)],
        out_shape=[out_shape, jax.ShapeDtypeStruct((d, n_cols), BF16)],
        scratch_shapes=[pltpu.VMEM((2, d, bn), BF16)],
        compiler_params=_params(("arbitrary", "arbitrary")),
        name=name,
    )(h, w)


def _in_proj(h, w, bn, col0, n_cols, act, name):
    t, d = h.shape
    bm = _tile(t, MATMUL_TILE, SUBLANE)
    return pl.pallas_call(
        functools.partial(_in_proj_kernel, act=act),
        grid=(t // bm, n_cols // bn),
        in_specs=[pl.BlockSpec((bm, d), lambda i, j: (i, 0)),
                  pl.BlockSpec((d, bn), lambda i, j: (0, col0 // bn + j))],
        out_specs=pl.BlockSpec((bn // LANE, bm, LANE), lambda i, j: (j, i, 0)),
        out_shape=jax.ShapeDtypeStruct((n_cols // LANE, t, LANE), F32),
        compiler_params=_params(("arbitrary", "arbitrary")),
        name=name,
    )(h, w)


def _sgu_kernel(u_ref, v_ref, lg_ref, lb_ref, w_ref, b_ref, o_ref, vs_ref, *, n_groups):
    c = w_ref.shape[1]
    n_feat = n_groups * GROUP_DIM_A
    for r0 in range(0, o_ref.shape[0], c):
        vp = _gelu(v_ref[:, r0:r0 + c, :])
        mu = jnp.sum(jnp.sum(vp, axis=0), axis=-1, keepdims=True) / n_feat
        vc = vp - mu
        var = jnp.sum(jnp.sum(vc * vc, axis=0), axis=-1, keepdims=True) / n_feat
        v = vc * lax.rsqrt(var + EPS) * lg_ref[...] + lb_ref[...]
        for g in range(n_groups):
            if r0 + c == o_ref.shape[0]:
                vs_ref[:, _lanes(g)] = v[g]
            mixed = jnp.dot(w_ref[g], v[g].astype(BF16), preferred_element_type=F32) + b_ref[g]
            o_ref[r0:r0 + c, _lanes(g)] = (_gelu(u_ref[g, r0:r0 + c, :]) * mixed).astype(o_ref.dtype)


def _sgu(z_sl, d_a, ln_g, ln_b, w_mix, b_mix):
    t = z_sl.shape[1]
    n_groups, c, _ = w_mix.shape
    rows = _tile(t, SGU_ROWS_PER_STEP, c)
    slab = pl.BlockSpec((n_groups, 1, LANE), lambda i: (0, 0, 0))
    return pl.pallas_call(
        functools.partial(_sgu_kernel, n_groups=n_groups),
        grid=(t // rows,),
        in_specs=[
            pl.BlockSpec((n_groups, rows, LANE), lambda i: (0, i, 0)),
            pl.BlockSpec((n_groups, rows, LANE), lambda i: (1, i, 0)),
            slab, slab,
            pl.BlockSpec((n_groups, c, c), lambda i: (0, 0, 0)),
            pl.BlockSpec((n_groups, c, LANE), lambda i: (0, 0, 0)),
        ],
        out_specs=[pl.BlockSpec((rows, d_a), lambda i: (i, 0)), pl.BlockSpec((c, d_a), lambda i: (0, 0))],
        out_shape=[jax.ShapeDtypeStruct((t, d_a), BF16), jax.ShapeDtypeStruct((c, d_a), F32)],
        compiler_params=_params(("arbitrary",)),
        name="sgu",
    )(z_sl, z_sl, ln_g.reshape(n_groups, 1, LANE), ln_b.reshape(n_groups, 1, LANE), w_mix, b_mix)


def _t5_bucket(dist, n_buckets):
    n = np.asarray(dist, np.int32)
    safe = np.maximum(n, 1).astype(np.float32)
    large = MAX_EXACT + (np.log(safe / MAX_EXACT) / np.log(np.float32(MAX_DISTANCE / MAX_EXACT))
                         * (n_buckets - MAX_EXACT)).astype(np.int32)
    large = np.minimum(large, n_buckets - 1)
    return np.where(n < MAX_EXACT, n, large).astype(np.int32)


def _bias_lookup(bias_tab, bucket):
    n_buckets = bias_tab.shape[0]
    flat = np.asarray(bucket).reshape(-1)
    onehot = (jnp.asarray(flat)[None, :] == jnp.arange(n_buckets)[:, None]).astype(F32)
    out = jnp.dot(bias_tab.T, onehot, precision=lax.Precision.HIGHEST)
    return out.reshape((bias_tab.shape[1],) + tuple(np.asarray(bucket).shape))


def _merge_by_lse(outs, lses):
    m = functools.reduce(jnp.maximum, lses)
    w = [jnp.exp(l - m) for l in lses]
    num = functools.reduce(lambda x, y: x + y, [wi * oi for wi, oi in zip(w, outs)])
    return num / functools.reduce(lambda x, y: x + y, w)


def _attn_prompt_kernel(*refs, hb, dil, n_other):
    q_ref, kc_ref, vc_ref, bias_ref, band_ref = refs[:5]
    others = refs[5:5 + 2 * n_other]
    n_out = 1 if n_other else 2
    outs = refs[5 + 2 * n_other:5 + 2 * n_other + n_out]
    kp_ref, vp_ref = refs[5 + 2 * n_other + n_out:][:2]
    o_acc, lse_acc = refs[5 + 2 * n_other + n_out + 2:] if n_other else outs
    b = pl.program_id(1)

    @pl.when(b == 0)
    def _():
        kp_ref[...] = jnp.zeros(kp_ref.shape, F32)
        vp_ref[...] = jnp.zeros(vp_ref.shape, F32)

    span = STEPS * dil
    n_pb = q_ref.shape[1] // span
    col = lax.broadcasted_iota(jnp.int32, (STEPS, 2 * STEPS), 1)
    band = band_ref[...] > 0.5
    first = band & ((b > 0) | (col >= STEPS))
    for hh in range(hb):
        for pb in range(n_pb):
            for r in range(dil):
                def rows(block):
                    return pl.ds(block * span + r, STEPS, stride=dil) if dil > 1 else pl.ds(block * span, STEPS)
                k_prev = kp_ref[hh, rows(0), :] if pb == 0 else kc_ref[hh, rows(pb - 1), :]
                v_prev = vp_ref[hh, rows(0), :] if pb == 0 else vc_ref[hh, rows(pb - 1), :]
                q = q_ref[hh, rows(pb), :].astype(BF16)
                kk = jnp.concatenate([k_prev, kc_ref[hh, rows(pb), :]], axis=0).astype(BF16)
                vv = jnp.concatenate([v_prev, vc_ref[hh, rows(pb), :]], axis=0).astype(BF16)
                s = lax.dot_general(q, kk, (((1,), (1,)), ((), ())), preferred_element_type=F32) * ATTN_SCALE
                s = jnp.where(first if pb == 0 else band, s + bias_ref[hh], NEG)
                m = jnp.max(s, axis=-1, keepdims=True)
                e = jnp.exp(s - m)
                den = jnp.sum(e, axis=-1, keepdims=True)
                o_acc[hh, rows(pb), :] = jnp.dot(e.astype(BF16), vv, preferred_element_type=F32) / den
                lse_acc[hh, rows(pb), :] = jnp.broadcast_to(m + jnp.log(den), (STEPS, HEAD_DIM))
    kp_ref[...] = kc_ref[:, (n_pb - 1) * span:, :]
    vp_ref[...] = vc_ref[:, (n_pb - 1) * span:, :]
    if n_other:
        for hh in range(hb):
            merged = _merge_by_lse([r[hh] for r in others[:n_other]] + [o_acc[hh]],
                                   [r[hh] for r in others[n_other:]] + [lse_acc[hh]])
            outs[0][:, _lanes(hh)] = merged.astype(outs[0].dtype)


def _attn_prompt(z_sl, hpg, q_slab, k_slab, v_slab, bias_tab, dil, merge_with=None):
    t = z_sl.shape[1]
    span = STEPS * dil
    assert t % span == 0
    hb = min(hpg, max(1, ATTN_UNITS_PER_STEP // dil))
    assert hpg % hb == 0 and q_slab % hb == 0 and k_slab % hb == 0 and v_slab % hb == 0
    n_pb = math.gcd(t // span, max(1, ATTN_UNITS_PER_STEP // (hb * dil)))
    rows = span * n_pb
    nb = t // rows

    p_idx = np.arange(STEPS)[:, None]
    c_idx = np.arange(2 * STEPS)[None, :]
    steps = p_idx + STEPS - c_idx
    band = ((steps >= 0) & (steps <= STEPS)).astype(np.float32)
    bias = _bias_lookup(bias_tab, _t5_bucket(np.clip(steps, 0, STEPS) * dil, bias_tab.shape[0]))

    def cur(slab):
        return pl.BlockSpec((hb, rows, LANE), lambda hi, b: (slab // hb + hi, b, 0))

    head_major = pl.BlockSpec((hb, rows, LANE), lambda hi, b: (hi, b, 0))
    prev_block = pltpu.VMEM((hb, span, LANE), F32)
    block = pltpu.VMEM((hb, rows, LANE), F32)
    others = [] if merge_with is None else list(merge_with[0]) + list(merge_with[1])
    if merge_with is None:
        out_specs = [head_major, head_major]
        out_shape = [jax.ShapeDtypeStruct((hpg, t, LANE), F32)] * 2
        scratch = [prev_block, prev_block]
    else:
        out_specs = [pl.BlockSpec((rows, hb * LANE), lambda hi, b: (b, hi))]
        out_shape = [jax.ShapeDtypeStruct((t, hpg * LANE), BF16)]
        scratch = [prev_block, prev_block, block, block]
    res = pl.pallas_call(
        functools.partial(_attn_prompt_kernel, hb=hb, dil=dil, n_other=len(others) // 2),
        grid=(hpg // hb, nb),
        in_specs=[cur(q_slab), cur(k_slab), cur(v_slab),
                  pl.BlockSpec((hb, STEPS, 2 * STEPS), lambda hi, b: (hi, 0, 0)),
                  pl.BlockSpec((STEPS, 2 * STEPS), lambda hi, b: (0, 0))] + [head_major] * len(others),
        out_specs=out_specs,
        out_shape=out_shape,
        scratch_shapes=scratch,
        compiler_params=_params(("arbitrary", "arbitrary")),
        name=f"attn_prompt_d{dil}",
    )(z_sl, z_sl, z_sl, bias, jnp.asarray(band), *others)
    return res[0] if merge_with is not None else res


def _attn_sample_kernel(q_ref, kn_ref, vn_ref, ck_ref, cv_ref, bc_ref, mc_ref, bnew_ref, mnew_ref,
                        o_ref, lse_ref, *, hpg, s_len, grouped):
    n_keys = mc_ref.shape[1]
    n_new = hpg * s_len
    valid_c = mc_ref[...] > 0.5

    def cache_head(ref, h):
        if grouped:
            return ref[0, :, pl.ds(h, s_len, stride=hpg), :].reshape(n_keys, HEAD_DIM).astype(BF16)
        return ref[0, pl.ds(h, n_keys, stride=hpg) if hpg > 1 else pl.ds(0, n_keys), :].astype(BF16)

    q_all = q_ref[...].reshape(n_new, HEAD_DIM).astype(BF16)
    s_new = lax.dot_general(q_all, kn_ref[...].reshape(n_new, HEAD_DIM).astype(BF16), (((1,), (1,)), ((), ())),
                            preferred_element_type=F32) * ATTN_SCALE
    key_head = lax.broadcasted_iota(jnp.int32, (s_len, n_new), 1) // s_len
    new_ok = mnew_ref[...] > 0.5

    partial, e_new = [], []
    for h in range(hpg):
        q = q_all[h * s_len:(h + 1) * s_len, :]
        sc = lax.dot_general(q, cache_head(ck_ref, h), (((1,), (1,)), ((), ())),
                             preferred_element_type=F32) * ATTN_SCALE
        sc = jnp.where(valid_c, sc + bc_ref[h], NEG)
        sn = jnp.where(new_ok & (key_head == h), s_new[h * s_len:(h + 1) * s_len, :] + bnew_ref[h], NEG)
        m = jnp.maximum(jnp.max(sc, axis=-1, keepdims=True), jnp.max(sn, axis=-1, keepdims=True))
        ec = jnp.exp(sc - m)
        en = jnp.exp(sn - m)
        den = jnp.sum(ec, axis=-1, keepdims=True) + jnp.sum(en, axis=-1, keepdims=True)
        acc = jnp.dot(ec.astype(BF16), cache_head(cv_ref, h), preferred_element_type=F32)
        partial.append((acc, den, m))
        e_new.append(en)
    o_new = jnp.dot(jnp.concatenate(e_new, axis=0).astype(BF16),
                    vn_ref[...].reshape(n_new, HEAD_DIM).astype(BF16), preferred_element_type=F32)
    for h, (acc, den, m) in enumerate(partial):
        o_ref[h] = (acc + o_new[h * s_len:(h + 1) * s_len, :]) / den
        lse_ref[h] = jnp.broadcast_to(m + jnp.log(den), (s_len, HEAD_DIM))


def _attn_sample(z_sl, s_len, hpg, q_slab, k_slab, v_slab, cache_k, cache_v, bias_tab, dil):
    t = z_sl.shape[1]
    n_seq = t // s_len
    lc = cache_k.shape[1]
    assert q_slab % hpg == 0 and k_slab % hpg == 0 and v_slab % hpg == 0

    j = np.arange(STEPS + 1)
    idx = lc + np.arange(s_len)[:, None] - j[None, :] * dil
    assert idx.min() >= 0
    bucket = _t5_bucket(j * dil, bias_tab.shape[0])
    mask = np.zeros((s_len, lc + s_len), np.float32)
    bsel = np.zeros((s_len, lc + s_len), np.int32)
    for s in range(s_len):
        mask[s, idx[s]] = 1.0
        bsel[s, idx[s]] = bucket

    grouped = dil > s_len and lc % dil == 0 and s_len == SUBLANE
    if grouped:
        pos = (np.arange(lc // dil)[:, None] * dil + np.arange(s_len)[None, :]).reshape(-1)
        assert mask[:, :lc].sum() == mask[:, pos].sum()
        ck = cache_k.reshape(n_seq, lc // dil, dil * hpg, HEAD_DIM)
        cv = cache_v.reshape(n_seq, lc // dil, dil * hpg, HEAD_DIM)
        cache_spec = pl.BlockSpec((1, lc // dil, s_len * hpg, LANE), lambda b: (b, 0, 0, 0))
    else:
        pos = np.arange(lc)
        ck = cache_k.reshape(n_seq, lc * hpg, HEAD_DIM)
        cv = cache_v.reshape(n_seq, lc * hpg, HEAD_DIM)
        cache_spec = pl.BlockSpec((1, lc * hpg, LANE), lambda b: (b, 0, 0))
    n_keys = len(pos)
    bias_c = _bias_lookup(bias_tab, bsel[:, pos])
    bias_new = jnp.tile(_bias_lookup(bias_tab, bsel[:, lc:]), (1, 1, hpg))
    mask_c = jnp.asarray(mask[:, pos])
    mask_new = jnp.asarray(np.tile(mask[:, lc:], (1, hpg)))

    out_spec = pl.BlockSpec((hpg, s_len, LANE), lambda b: (0, b, 0))
    return pl.pallas_call(
        functools.partial(_attn_sample_kernel, hpg=hpg, s_len=s_len, grouped=grouped),
        grid=(n_seq,),
        in_specs=[pl.BlockSpec((hpg, s_len, LANE), lambda b: (q_slab // hpg, b, 0)),
                  pl.BlockSpec((hpg, s_len, LANE), lambda b: (k_slab // hpg, b, 0)),
                  pl.BlockSpec((hpg, s_len, LANE), lambda b: (v_slab // hpg, b, 0)),
                  cache_spec, cache_spec,
                  pl.BlockSpec((hpg, s_len, n_keys), lambda b: (0, 0, 0)),
                  pl.BlockSpec((s_len, n_keys), lambda b: (0, 0)),
                  pl.BlockSpec((hpg, s_len, hpg * s_len), lambda b: (0, 0, 0)),
                  pl.BlockSpec((s_len, hpg * s_len), lambda b: (0, 0))],
        out_specs=[out_spec, out_spec],
        out_shape=[jax.ShapeDtypeStruct((hpg, t, LANE), F32)] * 2,
        compiler_params=_params(("arbitrary",)),
        name=f"attn_sample_d{dil}",
    )(z_sl, z_sl, z_sl, ck, cv, bias_c, mask_c, bias_new, mask_new)


def _merge_kernel(o1, o2, o3, l1, l2, l3, out_ref):
    merged = _merge_by_lse([o1[...], o2[...], o3[...]], [l1[...], l2[...], l3[...]])
    for h in range(merged.shape[0]):
        out_ref[:, _lanes(h)] = merged[h].astype(out_ref.dtype)


def _merge(outs, lses):
    hpg, t, _ = outs[0].shape
    bt = _tile(t, 256, SUBLANE)
    spec = pl.BlockSpec((hpg, bt, LANE), lambda i: (0, i, 0))
    return pl.pallas_call(
        _merge_kernel,
        grid=(t // bt,),
        in_specs=[spec] * 6,
        out_specs=pl.BlockSpec((bt, hpg * LANE), lambda i: (i, 0)),
        out_shape=jax.ShapeDtypeStruct((t, hpg * LANE), BF16),
        compiler_params=_params(("arbitrary",)),
        name="merge_groups",
    )(*outs, *lses)


def _gated_proj_kernel(a_ref, b_ref, wa_ref, wb_ref, ga_ref, gb_ref, o_ref):
    n_slabs = ga_ref.shape[0]
    per_strip = min(n_slabs, MXU_WIDTH // LANE)
    for c0 in range(0, n_slabs, per_strip):
        c1 = min(c0 + per_strip, n_slabs)
        cols = slice(c0 * LANE, c1 * LANE)
        pa = jnp.dot(a_ref[...], wa_ref[:, cols], preferred_element_type=F32)
        pb = jnp.dot(b_ref[...], wb_ref[:, cols], preferred_element_type=F32)
        for c in range(c0, c1):
            gated = ga_ref[c] * pa[:, _lanes(c - c0)] + gb_ref[c] * pb[:, _lanes(c - c0)]
            o_ref[:, _lanes(c)] = gated.astype(o_ref.dtype)


def _gated_proj(o_a, o_b, w_a, w_b, z_sl, bn, gate_a_blk, gate_b_blk):
    t, d_a = o_a.shape
    d_b = o_b.shape[1]
    d = w_a.shape[1]
    bm = _tile(t, MATMUL_TILE, SUBLANE)
    return pl.pallas_call(
        _gated_proj_kernel,
        grid=(t // bm, d // bn),
        in_specs=[pl.BlockSpec((bm, d_a), lambda i, j: (i, 0)),
                  pl.BlockSpec((bm, d_b), lambda i, j: (i, 0)),
                  pl.BlockSpec((d_a, bn), lambda i, j: (0, j)),
                  pl.BlockSpec((d_b, bn), lambda i, j: (0, j)),
                  pl.BlockSpec((bn // LANE, bm, LANE), lambda i, j: (gate_a_blk + j, i, 0)),
                  pl.BlockSpec((bn // LANE, bm, LANE), lambda i, j: (gate_b_blk + j, i, 0))],
        out_specs=pl.BlockSpec((bm, bn), lambda i, j: (i, j)),
        out_shape=jax.ShapeDtypeStruct((t, d), BF16),
        compiler_params=_params(("arbitrary", "arbitrary")),
        name="gated_proj",
    )(o_a, o_b, w_a, w_b, z_sl, z_sl)


def _matmul_kernel(x_ref, w_ref, o_ref, *, nk):
    if nk == 1:
        o_ref[...] = jnp.dot(x_ref[...], w_ref[...], preferred_element_type=F32)
    else:
        @pl.when(pl.program_id(2) == 0)
        def _():
            o_ref[...] = jnp.zeros(o_ref.shape, o_ref.dtype)

        o_ref[...] += jnp.dot(x_ref[...], w_ref[...], preferred_element_type=F32)


def _matmul(x, w, name, k_tile=None):
    t, kd = x.shape
    n = w.shape[1]
    bm = _tile(t, MATMUL_TILE, SUBLANE)
    bn = _tile(n, MATMUL_TILE, LANE)
    bk = kd if k_tile is None else k_tile
    nk = kd // bk
    return pl.pallas_call(
        functools.partial(_matmul_kernel, nk=nk),
        grid=(t // bm, n // bn, nk),
        in_specs=[pl.BlockSpec((bm, bk), lambda i, j, k: (i, k)),
                  pl.BlockSpec((bk, bn), lambda i, j, k: (k, j))],
        out_specs=pl.BlockSpec((bm, bn), lambda i, j, k: (i, j)),
        out_shape=jax.ShapeDtypeStruct((t, n), F32),
        compiler_params=_params(("arbitrary", "arbitrary", "arbitrary")),
        name=name,
    )(x, w)


def _post_mix_kernel(x_ref, y_ref, g1_ref, g2_ref, x1_ref, h2_ref):
    y = y_ref[...]
    x1 = x_ref[...] + (y * _rms_scale(y)) * g1_ref[...]
    x1_ref[...] = x1
    h2_ref[...] = ((x1 * _rms_scale(x1)) * g2_ref[...]).astype(h2_ref.dtype)


def _post_mix(x, y, g_post, g_pre):
    t, d = x.shape
    bt = _tile(t, 256, SUBLANE)
    row = pl.BlockSpec((bt, d), lambda i: (i, 0))
    vec = pl.BlockSpec((1, d), lambda i: (0, 0))
    return pl.pallas_call(
        _post_mix_kernel,
        grid=(t // bt,),
        in_specs=[row, row, vec, vec],
        out_specs=[row, row],
        out_shape=[jax.ShapeDtypeStruct((t, d), F32), jax.ShapeDtypeStruct((t, d), BF16)],
        compiler_params=_params(("arbitrary",)),
        name="post_mix",
    )(x, y, g_post.reshape(1, d), g_pre.reshape(1, d))


def _residual_norm_kernel(x_ref, y_ref, g_ref, o_ref):
    y = y_ref[...]
    o_ref[...] = x_ref[...] + (y * _rms_scale(y)) * g_ref[...]


def _residual_norm(x, y, g):
    t, d = x.shape
    bt = _tile(t, 256, SUBLANE)
    row = pl.BlockSpec((bt, d), lambda i: (i, 0))
    return pl.pallas_call(
        _residual_norm_kernel,
        grid=(t // bt,),
        in_specs=[row, row, pl.BlockSpec((1, d), lambda i: (0, 0))],
        out_specs=row,
        out_shape=jax.ShapeDtypeStruct((t, d), F32),
        compiler_params=_params(("arbitrary",)),
        name="residual_norm",
    )(x, y, g.reshape(1, d))


def _ffn_up_kernel(*refs, conv_w, s_len, d_ff):
    if s_len is None:
        (x_ref, wg_in, wu_in, wd_in, cw_ref, cb_ref, out_ref, tail_ref, wg_out, wu_out, wd_out,
         wg_buf, wu_buf, a_buf, u_buf, o_buf) = refs
        hist_refs = None
        jj, i = pl.program_id(0), pl.program_id(1)
        ck = wg_in.shape[0]

        def round_chunks():
            rows_c = pl.ds(pl.multiple_of(i * ck, ck), ck)
            for src, buf, dst in ((wg_in, wg_buf, wg_out), (wu_in, wu_buf, wu_out)):
                chunk = src[...].astype(BF16)
                buf[jj % 2, rows_c, :] = chunk
                dst[...] = chunk

        def round_down_rows():
            rc = wd_in.shape[0]
            row = ((jj - 1) * pl.num_programs(1) + i) * rc + lax.broadcasted_iota(jnp.int32, wd_in.shape, 0)
            wd_out[...] = jnp.where(row < d_ff, wd_in[...], 0.0).astype(BF16)

        slot = (jj + 1) % 2
        wg = lambda cols: wg_buf[slot, :, cols]
        wu = lambda cols: wu_buf[slot, :, cols]
        col_blk = jj - 1
        row_split = 2 if out_ref.shape[0] % (4 * SUBLANE) == 0 else 1
    else:
        x_ref, wg_ref, wu_ref, cw_ref, cb_ref, h1_ref, h2_ref, out_ref, tail_ref, a_buf, u_buf, o_buf = refs
        hist_refs = {1: h1_ref, 2: h2_ref}
        i = pl.program_id(1)
        wg = lambda cols: wg_ref[:, cols]
        wu = lambda cols: wu_ref[:, cols]
        col_blk = pl.program_id(0)
        row_split = 1
    bm, bn = out_ref.shape
    hm = bm // row_split
    half = hm // 2
    top = SUBLANE
    n_slabs = bn // LANE
    per_strip = min(n_slabs, MXU_WIDTH // LANE)

    def units():
        @pl.when(i == 0)
        def _():
            a_buf[:, 0:top, :] = jnp.zeros((n_slabs, top, LANE), F32)

        lane = lax.broadcasted_iota(jnp.int32, (1, LANE), 1)
        for h in range(row_split):
            r0 = h * hm
            x = x_ref[r0:r0 + hm, :]
            for c0 in range(0, n_slabs, per_strip):
                c1 = min(c0 + per_strip, n_slabs)
                cols = slice(c0 * LANE, c1 * LANE)
                a = jnp.dot(x, wg(cols), preferred_element_type=F32)
                u = jnp.dot(x, wu(cols), preferred_element_type=F32)
                if s_len is None and h == 0 and c0 == 0:
                    round_chunks()
                    round_down_rows()
                if s_len is not None:
                    tail_ref[r0:r0 + hm, cols] = a
                elif h == row_split - 1:
                    tail_ref[:, cols] = a[hm - SUBLANE:, :]
                for c in range(c0, c1):
                    a_buf[c, top + r0:top + r0 + hm, :] = a[:, _lanes(c - c0)]
                    u_buf[c, r0:r0 + hm, :] = u[:, _lanes(c - c0)]
                    in_range = (col_blk * bn + c * LANE + lane) < d_ff
                    for e in range(2):
                        taps = [a_buf[c, pl.ds(top + r0 + e - lag, half, stride=2), :] for lag in range(conv_w)]
                        if hist_refs is not None:
                            s = 2 * (lax.broadcasted_iota(jnp.int32, (half, LANE), 0) % (s_len // 2)) + e
                            for lag in range(1, conv_w):
                                hist = hist_refs[lag][c, pl.ds(r0 + e, half, stride=2), :]
                                taps[lag] = jnp.where(s < lag, hist, taps[lag])
                        acc = cw_ref[0:1, _lanes(c)] * taps[conv_w - 1]
                        for k in range(1, conv_w):
                            acc = acc + cw_ref[k:k + 1, _lanes(c)] * taps[conv_w - 1 - k]
                        acc = cb_ref[:, _lanes(c)] + acc
                        val = _gelu(acc) * u_buf[c, pl.ds(r0 + e, half, stride=2), :]
                        o_buf[c, pl.ds(r0 + e, half, stride=2), :] = jnp.where(in_range, val, 0.0)
                    out_ref[r0:r0 + hm, _lanes(c)] = o_buf[c, r0:r0 + hm, :].astype(out_ref.dtype)
        if s_len is None:
            a_buf[:, 0:top, :] = a_buf[:, bm:bm + top, :]

    if s_len is None:
        pl.when(jj == 0)(round_chunks)
        pl.when(jj > 0)(units)
    else:
        units()


def _ffn_up_prompt(h, w_gate, w_up, w_down, conv_w, conv_b, ffp):
    t, d = h.shape
    d_ff = w_gate.shape[1]
    cw = conv_w.shape[0]
    assert cw == 3 and d_ff % LANE == 0
    bn = FFN_TILE
    assert ffp % bn == 0 and ffp - d_ff < bn
    bm = _tile(t, MATMUL_TILE, 4 * SUBLANE)
    ni, nj = t // bm, ffp // bn
    assert d % ni == 0
    ck = d // ni
    n_slabs = bn // LANE

    def fin_i(jj, i):
        return jnp.where(jj == 0, 0, i)

    def fin_j(jj):
        return jnp.maximum(jj - 1, 0)

    def chunk(jj, i):
        return jnp.where(jj == nj, ni - 1, i), jnp.minimum(jj, nj - 1)

    n_out = w_down.shape[1]
    assert ffp % (nj * ni) == 0
    rc = ffp // (nj * ni)
    assert rc % (2 * SUBLANE) == 0

    def down_blk(jj, i):
        return fin_j(jj) * ni + fin_i(jj, i)

    last_src = -(-d_ff // rc) - 1
    w_in_spec = pl.BlockSpec((ck, bn), chunk)
    wbuf = pltpu.VMEM((2, d, bn), BF16)
    return pl.pallas_call(
        functools.partial(_ffn_up_kernel, conv_w=cw, s_len=None, d_ff=d_ff),
        grid=(nj + 1, ni),
        in_specs=[pl.BlockSpec((bm, d), lambda jj, i: (fin_i(jj, i), 0)), w_in_spec, w_in_spec,
                  pl.BlockSpec((rc, n_out), lambda jj, i: (jnp.minimum(down_blk(jj, i), last_src), 0)),
                  pl.BlockSpec((cw, bn), lambda jj, i: (0, fin_j(jj))),
                  pl.BlockSpec((1, bn), lambda jj, i: (0, fin_j(jj)))],
        out_specs=[pl.BlockSpec((bm, bn), lambda jj, i: (fin_i(jj, i), fin_j(jj))),
                   pl.BlockSpec((SUBLANE, bn), lambda jj, i: (0, fin_j(jj))),
                   w_in_spec, w_in_spec,
                   pl.BlockSpec((rc, n_out), lambda jj, i: (down_blk(jj, i), 0))],
        out_shape=[jax.ShapeDtypeStruct((t, ffp), BF16), jax.ShapeDtypeStruct((SUBLANE, d_ff), F32),
                   jax.ShapeDtypeStruct((d, d_ff), BF16), jax.ShapeDtypeStruct((d, d_ff), BF16),
                   jax.ShapeDtypeStruct((ffp, n_out), BF16)],
        scratch_shapes=[wbuf, wbuf,
                        pltpu.VMEM((n_slabs, bm + SUBLANE, LANE), F32), pltpu.VMEM((n_slabs, bm, LANE), F32),
                        pltpu.VMEM((n_slabs, bm, LANE), F32)],
        compiler_params=_params(("arbitrary", "arbitrary")),
        name="ffn_up",
    )(h, w_gate, w_up, w_down, conv_w, conv_b.reshape(1, d_ff))


def _ffn_up_sample(h, w_gate, w_up, conv_w, conv_b, ffp, hist, s_len):
    t, d = h.shape
    d_ff = w_gate.shape[1]
    cw = conv_w.shape[0]
    assert cw == 3 and d_ff % LANE == 0 and s_len % 2 == 0
    bn = _tile(ffp, MATMUL_TILE, LANE)
    assert ffp - d_ff < bn
    n_slabs = bn // LANE
    w_spec = pl.BlockSpec((d, bn), lambda j, i: (0, j))
    out_spec = pl.BlockSpec((t, bn), lambda j, i: (i, j))

    def slabs(rows):
        return jnp.transpose(rows.reshape(t, d_ff // LANE, LANE), (1, 0, 2))
    h1 = slabs(jnp.pad(hist[:, 1:2], ((0, 0), (0, s_len - 1), (0, 0))))
    h2 = slabs(jnp.pad(hist, ((0, 0), (0, s_len - hist.shape[1]), (0, 0))))
    hist_spec = pl.BlockSpec((n_slabs, t, LANE), lambda j, i: (j, i, 0))
    return pl.pallas_call(
        functools.partial(_ffn_up_kernel, conv_w=cw, s_len=s_len, d_ff=d_ff),
        grid=(ffp // bn, 1),
        in_specs=[pl.BlockSpec((t, d), lambda j, i: (i, 0)), w_spec, w_spec,
                  pl.BlockSpec((cw, bn), lambda j, i: (0, j)), pl.BlockSpec((1, bn), lambda j, i: (0, j)),
                  hist_spec, hist_spec],
        out_specs=[out_spec, out_spec],
        out_shape=[jax.ShapeDtypeStruct((t, ffp), BF16), jax.ShapeDtypeStruct((t, d_ff), F32)],
        scratch_shapes=[pltpu.VMEM((n_slabs, t + SUBLANE, LANE), F32), pltpu.VMEM((n_slabs, t, LANE), F32),
                        pltpu.VMEM((n_slabs, t, LANE), F32)],
        compiler_params=_params(("arbitrary", "arbitrary")),
        name="ffn_up_decode",
    )(h, w_gate, w_up, conv_w, conv_b.reshape(1, d_ff), h1, h2)


def _kv_state_rows_kernel(k_ref, v_ref, ko_ref, vo_ref, *, hpg):
    n = k_ref.shape[1]
    for src, dst in ((k_ref, ko_ref), (v_ref, vo_ref)):
        for h in range(hpg):
            dst[pl.ds(h, n, stride=hpg) if hpg > 1 else pl.ds(0, n), :] = src[h]


def _kv_state_rows(z_sl, k_slab, v_slab, hpg, row0, n_rows):
    br = _tile(n_rows, 512, SUBLANE)
    assert row0 % br == 0 and k_slab % hpg == 0 and v_slab % hpg == 0
    out_spec = pl.BlockSpec((br * hpg, LANE), lambda i: (i, 0))
    return pl.pallas_call(
        functools.partial(_kv_state_rows_kernel, hpg=hpg),
        grid=(n_rows // br,),
        in_specs=[pl.BlockSpec((hpg, br, LANE), lambda i: (k_slab // hpg, row0 // br + i, 0)),
                  pl.BlockSpec((hpg, br, LANE), lambda i: (v_slab // hpg, row0 // br + i, 0))],
        out_specs=[out_spec, out_spec],
        out_shape=[jax.ShapeDtypeStruct((n_rows * hpg, LANE), F32)] * 2,
        compiler_params=_params(("arbitrary",)),
        name="kv_state_rows",
    )(z_sl, z_sl)


def _layer(x, p, s_len=None, caches=None, conv_hist=None):
    t, d = x.shape
    d_a = p["ln_g"].shape[0]
    hpg = p["hpg"]
    gw = hpg * HEAD_DIM
    bn = p["bn"]
    o1 = 2 * d_a
    d_qkv = 3 * gw

    h = _rmsnorm_cast(x, p["g_pre_mix"])
    w16 = dict(p.get("w16", {}))
    z_parts = []
    for name, col0, n_cols, act in (("in_proj_lin", 0, o1 + 3 * d_qkv, lambda v: v),
                                    ("in_proj_gates", o1 + 3 * d_qkv, 2 * d, _sigmoid)):
        if s_len is None:
            z_part, w16[name] = _in_proj_round(h, p["w_in"], col0, n_cols, act, name)
        else:
            z_part = _in_proj(h, w16[name], _tile(n_cols, 2 * MATMUL_TILE, LANE), 0, n_cols, act, name + "_decode")
        z_parts.append(z_part)
    qkv_sl, gate_sl = z_parts

    if s_len is None:
        w_mix, b_mix = p["w_mix_prompt"], p["b_mix_prompt"]
    else:
        w_mix, b_mix = p["w_mix_sample"], p["b_mix_sample"]
    o_a, v_state = _sgu(qkv_sl, d_a, p["ln_g"], p["ln_b"], w_mix, b_mix)

    outs, lses = [], []
    for gi, (_, dil) in enumerate(DILATION_GROUPS):
        q_slab = o1 // LANE + gi * hpg
        k_slab = (o1 + d_qkv) // LANE + gi * hpg
        v_slab = (o1 + 2 * d_qkv) // LANE + gi * hpg
        bias_tab = p["rel_bias"][:, gi * hpg:(gi + 1) * hpg]
        if s_len is not None:
            o, lse = _attn_sample(qkv_sl, s_len, hpg, q_slab, k_slab, v_slab, caches[2 * gi], caches[2 * gi + 1],
                                  bias_tab, dil)
        elif gi < len(DILATION_GROUPS) - 1:
            o, lse = _attn_prompt(qkv_sl, hpg, q_slab, k_slab, v_slab, bias_tab, dil)
        else:
            o_b = _attn_prompt(qkv_sl, hpg, q_slab, k_slab, v_slab, bias_tab, dil, merge_with=(outs, lses))
            break
        outs.append(o)
        lses.append(lse)
    if s_len is not None:
        o_b = _merge(outs, lses)

    merged = _gated_proj(o_a, o_b, p["w_proj_a"], p["w_proj_b"], gate_sl, bn, 0, d // bn)
    if s_len is None:
        y, w16["out_proj"] = _in_proj_round(merged, p["w_out"], 0, d, lambda v: v, "out_proj", slab_out=False)
    else:
        y = _matmul(merged, w16["out_proj"], "out_proj_decode")
    x1, h2 = _post_mix(x, y, p["g_post_mix"], p["g_pre_ffn"])

    if s_len is None:
        act, a_tail, w16["ffn_gate"], w16["ffn_up"], w16["ffn_down"] = _ffn_up_prompt(
            h2, p["w_gate"], p["w_up"], p["w_down"], p["conv_w"], p["conv_b"], p["ffp"])
    else:
        act, a_tail = _ffn_up_sample(h2, w16["ffn_gate"], w16["ffn_up"], p["conv_w"], p["conv_b"], p["ffp"],
                                     conv_hist, s_len)
    f = _matmul(act, w16["ffn_down"], "ffn_down", k_tile=p["ffn_k_tile"])
    y_out = _residual_norm(x1, f, p["g_post_ffn"])
    return y_out, qkv_sl, v_state, a_tail, w16


def kernel(x_prompt, x_sample, cache_k_g1, cache_v_g1, cache_k_g2, cache_v_g2, cache_k_g3, cache_v_g3, state_conv, g_pre_mix, w_in, sgu_ln_g, sgu_ln_b, w_spatial, b_spatial, rel_bias, w_proj_a, w_proj_b, w_out, g_post_mix, g_pre_ffn, w_gate, w_up, conv_w, conv_b, w_down, g_post_ffn):
    depth = w_in.shape[0]
    assert depth == 1
    n_prompt, seq, d = x_prompt.shape
    assert n_prompt == 1 and seq % CHUNK == 0
    n_seq, s_len, _ = x_sample.shape
    assert s_len == SUBLANE
    d_a = sgu_ln_g.shape[1]
    n_groups = w_spatial.shape[1]
    n_heads = rel_bias.shape[1]
    hpg = n_heads // len(DILATION_GROUPS)
    gw = hpg * HEAD_DIM
    d_ff = w_gate.shape[2]
    cw = conv_w.shape[1]
    bn = _tile(math.gcd(2 * d_a, gw), MATMUL_TILE, LANE)
    ffp = -(-d_ff // FFN_TILE) * FFN_TILE
    t_s = n_seq * s_len

    tri = np.tril(np.ones((CHUNK, CHUNK), np.float32))
    w_mix_prompt = (w_spatial[0] * tri).astype(BF16)
    b_mix_prompt = jnp.broadcast_to(b_spatial[0][:, :, None], (n_groups, CHUNK, LANE))
    w_small = w_spatial[0][:, :s_len, :s_len] * tri[:s_len, :s_len]
    eye = np.eye(n_seq, dtype=np.float32)
    w_mix_sample = jnp.einsum("ab,gpq->gapbq", eye, w_small).reshape(n_groups, t_s, t_s).astype(BF16)
    b_mix_sample = jnp.broadcast_to(jnp.tile(b_spatial[0][:, :s_len], (1, n_seq))[:, :, None], (n_groups, t_s, LANE))

    p = dict(
        hpg=hpg, bn=bn, ffp=ffp,
        g_pre_mix=g_pre_mix[0], w_in=w_in[0], ln_g=sgu_ln_g[0], ln_b=sgu_ln_b[0],
        w_mix_prompt=w_mix_prompt, b_mix_prompt=b_mix_prompt,
        w_mix_sample=w_mix_sample, b_mix_sample=b_mix_sample,
        rel_bias=rel_bias,
        w_proj_a=w_proj_a[0].astype(BF16), w_proj_b=w_proj_b[0].astype(BF16), w_out=w_out[0],
        g_post_mix=g_post_mix[0], g_pre_ffn=g_pre_ffn[0],
        w_gate=w_gate[0], w_up=w_up[0],
        conv_w=conv_w[0], conv_b=conv_b[0],
        w_down=w_down[0],
        ffn_k_tile=_tile(ffp, 3072, LANE),
        g_post_ffn=g_post_ffn[0],
    )

    caches = tuple(c[0] for c in (cache_k_g1, cache_v_g1, cache_k_g2, cache_v_g2, cache_k_g3, cache_v_g3))

    yp, zp, vp_state, ap_tail, w16 = _layer(x_prompt[0], p)
    ys, zs, vs_state, as_all, _ = _layer(x_sample.reshape(t_s, d), dict(p, w16=w16), s_len=s_len,
                                         caches=caches, conv_hist=state_conv[0])

    d_qkv = 3 * gw
    prompt_kv, sample_kv = [], []
    for gi, (win, _) in enumerate(DILATION_GROUPS):
        keep = min(win, seq)
        k_slab = (2 * d_a + d_qkv + gi * gw) // LANE
        v_slab = (2 * d_a + 2 * d_qkv + gi * gw) // LANE
        prompt_kv += [s.reshape(1, 1, keep, hpg, HEAD_DIM)
                      for s in _kv_state_rows(zp, k_slab, v_slab, hpg, seq - keep, keep)]
        sample_kv += [s.reshape(1, n_seq, s_len, hpg, HEAD_DIM)
                      for s in _kv_state_rows(zs, k_slab, v_slab, hpg, 0, t_s)]
    p_conv = ap_tail[SUBLANE - (cw - 1):].reshape(1, 1, cw - 1, d_ff)
    s_conv = as_all.reshape(n_seq, s_len, d_ff)[:, s_len - (cw - 1):].reshape(1, n_seq, cw - 1, d_ff)
    return (yp.reshape(1, seq, d), ys.reshape(n_seq, s_len, d),
            *prompt_kv, vp_state.reshape(1, 1, CHUNK, d_a), p_conv,
            *sample_kv, vs_state.reshape(1, n_seq, s_len, d_a), s_conv)
```

```python
import functools
import math

import numpy as np
import jax
import jax.numpy as jnp
from jax import lax
from jax.experimental import pallas as pl
from jax.experimental.pallas import tpu as pltpu

F32 = jnp.float32
BF16 = jnp.bfloat16

HEAD_DIM = 128
STEPS = 128
CHUNK = 128
GROUP_DIM_A = 128
DILATION_GROUPS = ((128, 1), (512, 4), (2048, 16))
MAX_EXACT = 16
MAX_DISTANCE = 2048
EPS = 1e-6
NEG = -1e30
ATTN_SCALE = HEAD_DIM ** -0.5

LANE = 128
SUBLANE = 8
VMEM_LIMIT_BYTES = 56 * 1024 * 1024
MATMUL_TILE = 1024
MXU_WIDTH = 256
FFN_TILE = 512
IN_PROJ_ROUND_TILE = 1024
ATTN_UNITS_PER_STEP = 16
SGU_ROWS_PER_STEP = 512


def _tile(dim, target, align):
    best = None
    t = align
    while t <= min(dim, target):
        if dim % t == 0:
            best = t
        t += align
    return best if best is not None else dim


def _params(semantics):
    return pltpu.CompilerParams(dimension_semantics=semantics, vmem_limit_bytes=VMEM_LIMIT_BYTES)


def _gelu(x):
    return 0.5 * x * (1.0 + jnp.tanh(math.sqrt(2.0 / math.pi) * (x + 0.044715 * (x * x * x))))


def _sigmoid(x):
    return 1.0 / (1.0 + jnp.exp(-x))


def _rms_scale(x):
    return lax.rsqrt(jnp.mean(x * x, axis=-1, keepdims=True) + EPS)


def _lanes(c):
    return slice(c * LANE, (c + 1) * LANE)


def _rmsnorm_cast_kernel(x_ref, g_ref, o_ref):
    x = x_ref[...]
    o_ref[...] = ((x * _rms_scale(x)) * g_ref[...]).astype(o_ref.dtype)


def _rmsnorm_cast(x, g):
    t, d = x.shape
    bt = _tile(t, 256, SUBLANE)
    return pl.pallas_call(
        _rmsnorm_cast_kernel,
        grid=(t // bt,),
        in_specs=[pl.BlockSpec((bt, d), lambda i: (i, 0)), pl.BlockSpec((1, d), lambda i: (0, 0))],
        out_specs=pl.BlockSpec((bt, d), lambda i: (i, 0)),
        out_shape=jax.ShapeDtypeStruct((t, d), BF16),
        compiler_params=_params(("arbitrary",)),
        name="rmsnorm_cast",
    )(x, g.reshape(1, d))


def _in_proj_kernel(x_ref, w_ref, o_ref, *, act):
    acc = jnp.dot(x_ref[...], w_ref[...], preferred_element_type=F32)
    for c in range(o_ref.shape[0]):
        o_ref[c] = act(acc[:, _lanes(c)])


def _in_proj_round_kernel(x_ref, w_in, o_ref, w_out, w_buf, *, act, slab_out):
    jj, i = pl.program_id(0), pl.program_id(1)
    ck = w_in.shape[0]
    slot = (jj + 1) % 2
    bm = o_ref.shape[-2]
    n_slabs = w_in.shape[1] // LANE
    row_split = 2 if bm % (4 * SUBLANE) == 0 else 1
    hm = bm // row_split
    per_strip = min(n_slabs, MXU_WIDTH // LANE)

    def round_chunk():
        chunk = w_in[...].astype(BF16)
        w_buf[jj % 2, pl.ds(pl.multiple_of(i * ck, ck), ck), :] = chunk
        w_out[...] = chunk

    pl.when(jj == 0)(round_chunk)

    @pl.when(jj > 0)
    def _():
        round_chunk()
        for h in range(row_split):
            x = x_ref[h * hm:(h + 1) * hm, :]
            for c0 in range(0, n_slabs, per_strip):
                c1 = min(c0 + per_strip, n_slabs)
                acc = jnp.dot(x, w_buf[slot, :, c0 * LANE:c1 * LANE], preferred_element_type=F32)
                for c in range(c0, c1):
                    val = act(acc[:, _lanes(c - c0)])
                    if slab_out:
                        o_ref[c, h * hm:(h + 1) * hm, :] = val
                    else:
                        o_ref[h * hm:(h + 1) * hm, _lanes(c)] = val


def _in_proj_round(h, w, col0, n_cols, act, name, slab_out=True):
    t, d = h.shape
    bn = _tile(math.gcd(n_cols, col0) if col0 else n_cols, IN_PROJ_ROUND_TILE, LANE)
    bm = _tile(t, MATMUL_TILE, SUBLANE)
    ni, nj = t // bm, n_cols // bn
    assert d % ni == 0
    ck = d // ni

    def fin(jj, i):
        return jnp.maximum(jj - 1, 0), jnp.where(jj == 0, 0, i)

    def chunk(jj, i):
        return jnp.where(jj == nj, ni - 1, i), jnp.minimum(jj, nj - 1)

    if slab_out:
        out_spec = pl.BlockSpec((bn // LANE, bm, LANE), lambda jj, i: fin(jj, i) + (0,))
        out_shape = jax.ShapeDtypeStruct((n_cols // LANE, t, LANE), F32)
    else:
        out_spec = pl.BlockSpec((bm, bn), lambda jj, i: fin(jj, i)[::-1])
        out_shape = jax.ShapeDtypeStruct((t, n_cols), F32)
    return pl.pallas_call(
        functools.partial(_in_proj_round_kernel, act=act, slab_out=slab_out),
        grid=(nj + 1, ni),
        in_specs=[pl.BlockSpec((bm, d), lambda jj, i: (fin(jj, i)[1], 0)),
                  pl.BlockSpec((ck, bn), lambda jj, i: (chunk(jj, i)[0], col0 // bn + chunk(jj, i)[1]))],
        out_specs=[out_spec, pl.BlockSpec((ck, bn), chunk)],
        out_shape=[out_shape, jax.ShapeDtypeStruct((d, n_cols), BF16)],
        scratch_shapes=[pltpu.VMEM((2, d, bn), BF16)],
        compiler_params=_params(("arbitrary", "arbitrary")),
        name=name,
    )(h, w)


def _in_proj(h, w, bn, col0, n_cols, act, name):
    t, d = h.shape
    bm = _tile(t, MATMUL_TILE, SUBLANE)
    return pl.pallas_call(
        functools.partial(_in_proj_kernel, act=act),
        grid=(t // bm, n_cols // bn),
        in_specs=[pl.BlockSpec((bm, d), lambda i, j: (i, 0)),
                  pl.BlockSpec((d, bn), lambda i, j: (0, col0 // bn + j))],
        out_specs=pl.BlockSpec((bn // LANE, bm, LANE), lambda i, j: (j, i, 0)),
        out_shape=jax.ShapeDtypeStruct((n_cols // LANE, t, LANE), F32),
        compiler_params=_params(("arbitrary", "arbitrary")),
        name=name,
    )(h, w)


def _sgu_kernel(u_ref, v_ref, lg_ref, lb_ref, w_ref, b_ref, o_ref, vs_ref, *, n_groups):
    c = w_ref.shape[1]
    n_feat = n_groups * GROUP_DIM_A
    for r0 in range(0, o_ref.shape[0], c):
        vp = _gelu(v_ref[:, r0:r0 + c, :])
        mu = jnp.sum(jnp.sum(vp, axis=0), axis=-1, keepdims=True) / n_feat
        vc = vp - mu
        var = jnp.sum(jnp.sum(vc * vc, axis=0), axis=-1, keepdims=True) / n_feat
        v = vc * lax.rsqrt(var + EPS) * lg_ref[...] + lb_ref[...]
        for g in range(n_groups):
            if r0 + c == o_ref.shape[0]:
                vs_ref[:, _lanes(g)] = v[g]
            mixed = jnp.dot(w_ref[g], v[g].astype(BF16), preferred_element_type=F32) + b_ref[g]
            o_ref[r0:r0 + c, _lanes(g)] = (_gelu(u_ref[g, r0:r0 + c, :]) * mixed).astype(o_ref.dtype)


def _sgu(z_sl, d_a, ln_g, ln_b, w_mix, b_mix):
    t = z_sl.shape[1]
    n_groups, c, _ = w_mix.shape
    rows = _tile(t, SGU_ROWS_PER_STEP, c)
    slab = pl.BlockSpec((n_groups, 1, LANE), lambda i: (0, 0, 0))
    return pl.pallas_call(
        functools.partial(_sgu_kernel, n_groups=n_groups),
        grid=(t // rows,),
        in_specs=[
            pl.BlockSpec((n_groups, rows, LANE), lambda i: (0, i, 0)),
            pl.BlockSpec((n_groups, rows, LANE), lambda i: (1, i, 0)),
            slab, slab,
            pl.BlockSpec((n_groups, c, c), lambda i: (0, 0, 0)),
            pl.BlockSpec((n_groups, c, LANE), lambda i: (0, 0, 0)),
        ],
        out_specs=[pl.BlockSpec((rows, d_a), lambda i: (i, 0)), pl.BlockSpec((c, d_a), lambda i: (0, 0))],
        out_shape=[jax.ShapeDtypeStruct((t, d_a), BF16), jax.ShapeDtypeStruct((c, d_a), F32)],
        compiler_params=_params(("arbitrary",)),
        name="sgu",
    )(z_sl, z_sl, ln_g.reshape(n_groups, 1, LANE), ln_b.reshape(n_groups, 1, LANE), w_mix, b_mix)


def _t5_bucket(dist, n_buckets):
    n = np.asarray(dist, np.int32)
    safe = np.maximum(n, 1).astype(np.float32)
    large = MAX_EXACT + (np.log(safe / MAX_EXACT) / np.log(np.float32(MAX_DISTANCE / MAX_EXACT))
                         * (n_buckets - MAX_EXACT)).astype(np.int32)
    large = np.minimum(large, n_buckets - 1)
    return np.where(n < MAX_EXACT, n, large).astype(np.int32)


def _bias_lookup(bias_tab, bucket):
    n_buckets = bias_tab.shape[0]
    flat = np.asarray(bucket).reshape(-1)
    onehot = (jnp.asarray(flat)[None, :] == jnp.arange(n_buckets)[:, None]).astype(F32)
    out = jnp.dot(bias_tab.T, onehot, precision=lax.Precision.HIGHEST)
    return out.reshape((bias_tab.shape[1],) + tuple(np.asarray(bucket).shape))


def _merge_by_lse(outs, lses):
    m = functools.reduce(jnp.maximum, lses)
    w = [jnp.exp(l - m) for l in lses]
    num = functools.reduce(lambda x, y: x + y, [wi * oi for wi, oi in zip(w, outs)])
    return num / functools.reduce(lambda x, y: x + y, w)


def _attn_prompt_kernel(*refs, hb, dil, n_other):
    q_ref, kc_ref, vc_ref, bias_ref, band_ref = refs[:5]
    others = refs[5:5 + 2 * n_other]
    n_out = 1 if n_other else 2
    outs = refs[5 + 2 * n_other:5 + 2 * n_other + n_out]
    kp_ref, vp_ref = refs[5 + 2 * n_other + n_out:][:2]
    o_acc, lse_acc = refs[5 + 2 * n_other + n_out + 2:] if n_other else outs
    b = pl.program_id(1)

    @pl.when(b == 0)
    def _():
        kp_ref[...] = jnp.zeros(kp_ref.shape, F32)
        vp_ref[...] = jnp.zeros(vp_ref.shape, F32)

    span = STEPS * dil
    n_pb = q_ref.shape[1] // span
    col = lax.broadcasted_iota(jnp.int32, (STEPS, 2 * STEPS), 1)
    band = band_ref[...] > 0.5
    first = band & ((b > 0) | (col >= STEPS))
    for hh in range(hb):
        for pb in range(n_pb):
            for r in range(dil):
                def rows(block):
                    return pl.ds(block * span + r, STEPS, stride=dil) if dil > 1 else pl.ds(block * span, STEPS)
                k_prev = kp_ref[hh, rows(0), :] if pb == 0 else kc_ref[hh, rows(pb - 1), :]
                v_prev = vp_ref[hh, rows(0), :] if pb == 0 else vc_ref[hh, rows(pb - 1), :]
                q = q_ref[hh, rows(pb), :].astype(BF16)
                kk = jnp.concatenate([k_prev, kc_ref[hh, rows(pb), :]], axis=0).astype(BF16)
                vv = jnp.concatenate([v_prev, vc_ref[hh, rows(pb), :]], axis=0).astype(BF16)
                s = lax.dot_general(q, kk, (((1,), (1,)), ((), ())), preferred_element_type=F32) * ATTN_SCALE
                s = jnp.where(first if pb == 0 else band, s + bias_ref[hh], NEG)
                m = jnp.max(s, axis=-1, keepdims=True)
                e = jnp.exp(s - m)
                den = jnp.sum(e, axis=-1, keepdims=True)
                o_acc[hh, rows(pb), :] = jnp.dot(e.astype(BF16), vv, preferred_element_type=F32) / den
                lse_acc[hh, rows(pb), :] = jnp.broadcast_to(m + jnp.log(den), (STEPS, HEAD_DIM))
    kp_ref[...] = kc_ref[:, (n_pb - 1) * span:, :]
    vp_ref[...] = vc_ref[:, (n_pb - 1) * span:, :]
    if n_other:
        for hh in range(hb):
            merged = _merge_by_lse([r[hh] for r in others[:n_other]] + [o_acc[hh]],
                                   [r[hh] for r in others[n_other:]] + [lse_acc[hh]])
            outs[0][:, _lanes(hh)] = merged.astype(outs[0].dtype)


def _attn_prompt(z_sl, hpg, q_slab, k_slab, v_slab, bias_tab, dil, merge_with=None):
    t = z_sl.shape[1]
    span = STEPS * dil
    assert t % span == 0
    hb = min(hpg, max(1, ATTN_UNITS_PER_STEP // dil))
    assert hpg % hb == 0 and q_slab % hb == 0 and k_slab % hb == 0 and v_slab % hb == 0
    n_pb = math.gcd(t // span, max(1, ATTN_UNITS_PER_STEP // (hb * dil)))
    rows = span * n_pb
    nb = t // rows

    p_idx = np.arange(STEPS)[:, None]
    c_idx = np.arange(2 * STEPS)[None, :]
    steps = p_idx + STEPS - c_idx
    band = ((steps >= 0) & (steps <= STEPS)).astype(np.float32)
    bias = _bias_lookup(bias_tab, _t5_bucket(np.clip(steps, 0, STEPS) * dil, bias_tab.shape[0]))

    def cur(slab):
        return pl.BlockSpec((hb, rows, LANE), lambda hi, b: (slab // hb + hi, b, 0))

    head_major = pl.BlockSpec((hb, rows, LANE), lambda hi, b: (hi, b, 0))
    prev_block = pltpu.VMEM((hb, span, LANE), F32)
    block = pltpu.VMEM((hb, rows, LANE), F32)
    others = [] if merge_with is None else list(merge_with[0]) + list(merge_with[1])
    if merge_with is None:
        out_specs = [head_major, head_major]
        out_shape = [jax.ShapeDtypeStruct((hpg, t, LANE), F32)] * 2
        scratch = [prev_block, prev_block]
    else:
        out_specs = [pl.BlockSpec((rows, hb * LANE), lambda hi, b: (b, hi))]
        out_shape = [jax.ShapeDtypeStruct((t, hpg * LANE), BF16)]
        scratch = [prev_block, prev_block, block, block]
    res = pl.pallas_call(
        functools.partial(_attn_prompt_kernel, hb=hb, dil=dil, n_other=len(others) // 2),
        grid=(hpg // hb, nb),
        in_specs=[cur(q_slab), cur(k_slab), cur(v_slab),
                  pl.BlockSpec((hb, STEPS, 2 * STEPS), lambda hi, b: (hi, 0, 0)),
                  pl.BlockSpec((STEPS, 2 * STEPS), lambda hi, b: (0, 0))] + [head_major] * len(others),
        out_specs=out_specs,
        out_shape=out_shape,
        scratch_shapes=scratch,
        compiler_params=_params(("arbitrary", "arbitrary")),
        name=f"attn_prompt_d{dil}",
    )(z_sl, z_sl, z_sl, bias, jnp.asarray(band), *others)
    return res[0] if merge_with is not None else res


def _attn_sample_kernel(q_ref, kn_ref, vn_ref, ck_ref, cv_ref, bc_ref, mc_ref, bnew_ref, mnew_ref,
                        o_ref, lse_ref, *, hpg, s_len, grouped):
    n_keys = mc_ref.shape[1]
    n_new = hpg * s_len
    valid_c = mc_ref[...] > 0.5

    def cache_head(ref, h):
        if grouped:
            return ref[0, :, pl.ds(h, s_len, stride=hpg), :].reshape(n_keys, HEAD_DIM).astype(BF16)
        return ref[0, pl.ds(h, n_keys, stride=hpg) if hpg > 1 else pl.ds(0, n_keys), :].astype(BF16)

    q_all = q_ref[...].reshape(n_new, HEAD_DIM).astype(BF16)
    s_new = lax.dot_general(q_all, kn_ref[...].reshape(n_new, HEAD_DIM).astype(BF16), (((1,), (1,)), ((), ())),
                            preferred_element_type=F32) * ATTN_SCALE
    key_head = lax.broadcasted_iota(jnp.int32, (s_len, n_new), 1) // s_len
    new_ok = mnew_ref[...] > 0.5

    partial, e_new = [], []
    for h in range(hpg):
        q = q_all[h * s_len:(h + 1) * s_len, :]
        sc = lax.dot_general(q, cache_head(ck_ref, h), (((1,), (1,)), ((), ())),
                             preferred_element_type=F32) * ATTN_SCALE
        sc = jnp.where(valid_c, sc + bc_ref[h], NEG)
        sn = jnp.where(new_ok & (key_head == h), s_new[h * s_len:(h + 1) * s_len, :] + bnew_ref[h], NEG)
        m = jnp.maximum(jnp.max(sc, axis=-1, keepdims=True), jnp.max(sn, axis=-1, keepdims=True))
        ec = jnp.exp(sc - m)
        en = jnp.exp(sn - m)
        den = jnp.sum(ec, axis=-1, keepdims=True) + jnp.sum(en, axis=-1, keepdims=True)
        acc = jnp.dot(ec.astype(BF16), cache_head(cv_ref, h), preferred_element_type=F32)
        partial.append((acc, den, m))
        e_new.append(en)
    o_new = jnp.dot(jnp.concatenate(e_new, axis=0).astype(BF16),
                    vn_ref[...].reshape(n_new, HEAD_DIM).astype(BF16), preferred_element_type=F32)
    for h, (acc, den, m) in enumerate(partial):
        o_ref[h] = (acc + o_new[h * s_len:(h + 1) * s_len, :]) / den
        lse_ref[h] = jnp.broadcast_to(m + jnp.log(den), (s_len, HEAD_DIM))


def _attn_sample(z_sl, s_len, hpg, q_slab, k_slab, v_slab, cache_k, cache_v, bias_tab, dil):
    t = z_sl.shape[1]
    n_seq = t // s_len
    lc = cache_k.shape[1]
    assert q_slab % hpg == 0 and k_slab % hpg == 0 and v_slab % hpg == 0

    j = np.arange(STEPS + 1)
    idx = lc + np.arange(s_len)[:, None] - j[None, :] * dil
    assert idx.min() >= 0
    bucket = _t5_bucket(j * dil, bias_tab.shape[0])
    mask = np.zeros((s_len, lc + s_len), np.float32)
    bsel = np.zeros((s_len, lc + s_len), np.int32)
    for s in range(s_len):
        mask[s, idx[s]] = 1.0
        bsel[s, idx[s]] = bucket

    grouped = dil > s_len and lc % dil == 0 and s_len == SUBLANE
    if grouped:
        pos = (np.arange(lc // dil)[:, None] * dil + np.arange(s_len)[None, :]).reshape(-1)
        assert mask[:, :lc].sum() == mask[:, pos].sum()
        ck = cache_k.reshape(n_seq, lc // dil, dil * hpg, HEAD_DIM)
        cv = cache_v.reshape(n_seq, lc // dil, dil * hpg, HEAD_DIM)
        cache_spec = pl.BlockSpec((1, lc // dil, s_len * hpg, LANE), lambda b: (b, 0, 0, 0))
    else:
        pos = np.arange(lc)
        ck = cache_k.reshape(n_seq, lc * hpg, HEAD_DIM)
        cv = cache_v.reshape(n_seq, lc * hpg, HEAD_DIM)
        cache_spec = pl.BlockSpec((1, lc * hpg, LANE), lambda b: (b, 0, 0))
    n_keys = len(pos)
    bias_c = _bias_lookup(bias_tab, bsel[:, pos])
    bias_new = jnp.tile(_bias_lookup(bias_tab, bsel[:, lc:]), (1, 1, hpg))
    mask_c = jnp.asarray(mask[:, pos])
    mask_new = jnp.asarray(np.tile(mask[:, lc:], (1, hpg)))

    out_spec = pl.BlockSpec((hpg, s_len, LANE), lambda b: (0, b, 0))
    return pl.pallas_call(
        functools.partial(_attn_sample_kernel, hpg=hpg, s_len=s_len, grouped=grouped),
        grid=(n_seq,),
        in_specs=[pl.BlockSpec((hpg, s_len, LANE), lambda b: (q_slab // hpg, b, 0)),
                  pl.BlockSpec((hpg, s_len, LANE), lambda b: (k_slab // hpg, b, 0)),
                  pl.BlockSpec((hpg, s_len, LANE), lambda b: (v_slab // hpg, b, 0)),
                  cache_spec, cache_spec,
                  pl.BlockSpec((hpg, s_len, n_keys), lambda b: (0, 0, 0)),
                  pl.BlockSpec((s_len, n_keys), lambda b: (0, 0)),
                  pl.BlockSpec((hpg, s_len, hpg * s_len), lambda b: (0, 0, 0)),
                  pl.BlockSpec((s_len, hpg * s_len), lambda b: (0, 0))],
        out_specs=[out_spec, out_spec],
        out_shape=[jax.ShapeDtypeStruct((hpg, t, LANE), F32)] * 2,
        compiler_params=_params(("arbitrary",)),
        name=f"attn_sample_d{dil}",
    )(z_sl, z_sl, z_sl, ck, cv, bias_c, mask_c, bias_new, mask_new)


def _merge_kernel(o1, o2, o3, l1, l2, l3, out_ref):
    merged = _merge_by_lse([o1[...], o2[...], o3[...]], [l1[...], l2[...], l3[...]])
    for h in range(merged.shape[0]):
        out_ref[:, _lanes(h)] = merged[h].astype(out_ref.dtype)


def _merge(outs, lses):
    hpg, t, _ = outs[0].shape
    bt = _tile(t, 256, SUBLANE)
    spec = pl.BlockSpec((hpg, bt, LANE), lambda i: (0, i, 0))
    return pl.pallas_call(
        _merge_kernel,
        grid=(t // bt,),
        in_specs=[spec] * 6,
        out_specs=pl.BlockSpec((bt, hpg * LANE), lambda i: (i, 0)),
        out_shape=jax.ShapeDtypeStruct((t, hpg * LANE), BF16),
        compiler_params=_params(("arbitrary",)),
        name="merge_groups",
    )(*outs, *lses)


def _gated_proj_kernel(a_ref, b_ref, wa_ref, wb_ref, ga_ref, gb_ref, o_ref):
    n_slabs = ga_ref.shape[0]
    per_strip = min(n_slabs, MXU_WIDTH // LANE)
    for c0 in range(0, n_slabs, per_strip):
        c1 = min(c0 + per_strip, n_slabs)
        cols = slice(c0 * LANE, c1 * LANE)
        pa = jnp.dot(a_ref[...], wa_ref[:, cols], preferred_element_type=F32)
        pb = jnp.dot(b_ref[...], wb_ref[:, cols], preferred_element_type=F32)
        for c in range(c0, c1):
            gated = ga_ref[c] * pa[:, _lanes(c - c0)] + gb_ref[c] * pb[:, _lanes(c - c0)]
            o_ref[:, _lanes(c)] = gated.astype(o_ref.dtype)


def _gated_proj(o_a, o_b, w_a, w_b, z_sl, bn, gate_a_blk, gate_b_blk):
    t, d_a = o_a.shape
    d_b = o_b.shape[1]
    d = w_a.shape[1]
    bm = _tile(t, MATMUL_TILE, SUBLANE)
    return pl.pallas_call(
        _gated_proj_kernel,
        grid=(t // bm, d // bn),
        in_specs=[pl.BlockSpec((bm, d_a), lambda i, j: (i, 0)),
                  pl.BlockSpec((bm, d_b), lambda i, j: (i, 0)),
                  pl.BlockSpec((d_a, bn), lambda i, j: (0, j)),
                  pl.BlockSpec((d_b, bn), lambda i, j: (0, j)),
                  pl.BlockSpec((bn // LANE, bm, LANE), lambda i, j: (gate_a_blk + j, i, 0)),
                  pl.BlockSpec((bn // LANE, bm, LANE), lambda i, j: (gate_b_blk + j, i, 0))],
        out_specs=pl.BlockSpec((bm, bn), lambda i, j: (i, j)),
        out_shape=jax.ShapeDtypeStruct((t, d), BF16),
        compiler_params=_params(("arbitrary", "arbitrary")),
        name="gated_proj",
    )(o_a, o_b, w_a, w_b, z_sl, z_sl)


def _matmul_kernel(x_ref, w_ref, o_ref, *, nk):
    if nk == 1:
        o_ref[...] = jnp.dot(x_ref[...], w_ref[...], preferred_element_type=F32)
    else:
        @pl.when(pl.program_id(2) == 0)
        def _():
            o_ref[...] = jnp.zeros(o_ref.shape, o_ref.dtype)

        o_ref[...] += jnp.dot(x_ref[...], w_ref[...], preferred_element_type=F32)


def _matmul(x, w, name, k_tile=None):
    t, kd = x.shape
    n = w.shape[1]
    bm = _tile(t, MATMUL_TILE, SUBLANE)
    bn = _tile(n, MATMUL_TILE, LANE)
    bk = kd if k_tile is None else k_tile
    nk = kd // bk
    return pl.pallas_call(
        functools.partial(_matmul_kernel, nk=nk),
        grid=(t // bm, n // bn, nk),
        in_specs=[pl.BlockSpec((bm, bk), lambda i, j, k: (i, k)),
                  pl.BlockSpec((bk, bn), lambda i, j, k: (k, j))],
        out_specs=pl.BlockSpec((bm, bn), lambda i, j, k: (i, j)),
        out_shape=jax.ShapeDtypeStruct((t, n), F32),
        compiler_params=_params(("arbitrary", "arbitrary", "arbitrary")),
        name=name,
    )(x, w)


def _post_mix_kernel(x_ref, y_ref, g1_ref, g2_ref, x1_ref, h2_ref):
    y = y_ref[...]
    x1 = x_ref[...] + (y * _rms_scale(y)) * g1_ref[...]
    x1_ref[...] = x1
    h2_ref[...] = ((x1 * _rms_scale(x1)) * g2_ref[...]).astype(h2_ref.dtype)


def _post_mix(x, y, g_post, g_pre):
    t, d = x.shape
    bt = _tile(t, 256, SUBLANE)
    row = pl.BlockSpec((bt, d), lambda i: (i, 0))
    vec = pl.BlockSpec((1, d), lambda i: (0, 0))
    return pl.pallas_call(
        _post_mix_kernel,
        grid=(t // bt,),
        in_specs=[row, row, vec, vec],
        out_specs=[row, row],
        out_shape=[jax.ShapeDtypeStruct((t, d), F32), jax.ShapeDtypeStruct((t, d), BF16)],
        compiler_params=_params(("arbitrary",)),
        name="post_mix",
    )(x, y, g_post.reshape(1, d), g_pre.reshape(1, d))


def _residual_norm_kernel(x_ref, y_ref, g_ref, o_ref):
    y = y_ref[...]
    o_ref[...] = x_ref[...] + (y * _rms_scale(y)) * g_ref[...]


def _residual_norm(x, y, g):
    t, d = x.shape
    bt = _tile(t, 256, SUBLANE)
    row = pl.BlockSpec((bt, d), lambda i: (i, 0))
    return pl.pallas_call(
        _residual_norm_kernel,
        grid=(t // bt,),
        in_specs=[row, row, pl.BlockSpec((1, d), lambda i: (0, 0))],
        out_specs=row,
        out_shape=jax.ShapeDtypeStruct((t, d), F32),
        compiler_params=_params(("arbitrary",)),
        name="residual_norm",
    )(x, y, g.reshape(1, d))


def _ffn_up_kernel(*refs, conv_w, s_len, d_ff):
    if s_len is None:
        (x_ref, wg_in, wu_in, wd_in, cw_ref, cb_ref, out_ref, tail_ref, wg_out, wu_out, wd_out,
         wg_buf, wu_buf, a_buf, u_buf, o_buf) = refs
        hist_refs = None
        jj, i = pl.program_id(0), pl.program_id(1)
        ck = wg_in.shape[0]

        def round_chunks():
            rows_c = pl.ds(pl.multiple_of(i * ck, ck), ck)
            for src, buf, dst in ((wg_in, wg_buf, wg_out), (wu_in, wu_buf, wu_out)):
                chunk = src[...].astype(BF16)
                buf[jj % 2, rows_c, :] = chunk
                dst[...] = chunk

        def round_down_rows():
            rc = wd_in.shape[0]
            row = ((jj - 1) * pl.num_programs(1) + i) * rc + lax.broadcasted_iota(jnp.int32, wd_in.shape, 0)
            wd_out[...] = jnp.where(row < d_ff, wd_in[...], 0.0).astype(BF16)

        slot = (jj + 1) % 2
        wg = lambda cols: wg_buf[slot, :, cols]
        wu = lambda cols: wu_buf[slot, :, cols]
        col_blk = jj - 1
        row_split = 2 if out_ref.shape[0] % (4 * SUBLANE) == 0 else 1
    else:
        x_ref, wg_ref, wu_ref, cw_ref, cb_ref, h1_ref, h2_ref, out_ref, tail_ref, a_buf, u_buf, o_buf = refs
        hist_refs = {1: h1_ref, 2: h2_ref}
        i = pl.program_id(1)
        wg = lambda cols: wg_ref[:, cols]
        wu = lambda cols: wu_ref[:, cols]
        col_blk = pl.program_id(0)
        row_split = 1
    bm, bn = out_ref.shape
    hm = bm // row_split
    half = hm // 2
    top = SUBLANE
    n_slabs = bn // LANE
    per_strip = min(n_slabs, MXU_WIDTH // LANE)

    def units():
        @pl.when(i == 0)
        def _():
            a_buf[:, 0:top, :] = jnp.zeros((n_slabs, top, LANE), F32)

        lane = lax.broadcasted_iota(jnp.int32, (1, LANE), 1)
        for h in range(row_split):
            r0 = h * hm
            x = x_ref[r0:r0 + hm, :]
            for c0 in range(0, n_slabs, per_strip):
                c1 = min(c0 + per_strip, n_slabs)
                cols = slice(c0 * LANE, c1 * LANE)
                a = jnp.dot(x, wg(cols), preferred_element_type=F32)
                u = jnp.dot(x, wu(cols), preferred_element_type=F32)
                if s_len is None and h == 0 and c0 == 0:
                    round_chunks()
                    round_down_rows()
                if s_len is not None:
                    tail_ref[r0:r0 + hm, cols] = a
                elif h == row_split - 1:
                    tail_ref[:, cols] = a[hm - SUBLANE:, :]
                for c in range(c0, c1):
                    a_buf[c, top + r0:top + r0 + hm, :] = a[:, _lanes(c - c0)]
                    u_buf[c, r0:r0 + hm, :] = u[:, _lanes(c - c0)]
                    in_range = (col_blk * bn + c * LANE + lane) < d_ff
                    for e in range(2):
                        taps = [a_buf[c, pl.ds(top + r0 + e - lag, half, stride=2), :] for lag in range(conv_w)]
                        if hist_refs is not None:
                            s = 2 * (lax.broadcasted_iota(jnp.int32, (half, LANE), 0) % (s_len // 2)) + e
                            for lag in range(1, conv_w):
                                hist = hist_refs[lag][c, pl.ds(r0 + e, half, stride=2), :]
                                taps[lag] = jnp.where(s < lag, hist, taps[lag])
                        acc = cw_ref[0:1, _lanes(c)] * taps[conv_w - 1]
                        for k in range(1, conv_w):
                            acc = acc + cw_ref[k:k + 1, _lanes(c)] * taps[conv_w - 1 - k]
                        acc = cb_ref[:, _lanes(c)] + acc
                        val = _gelu(acc) * u_buf[c, pl.ds(r0 + e, half, stride=2), :]
                        o_buf[c, pl.ds(r0 + e, half, stride=2), :] = jnp.where(in_range, val, 0.0)
                    out_ref[r0:r0 + hm, _lanes(c)] = o_buf[c, r0:r0 + hm, :].astype(out_ref.dtype)
        if s_len is None:
            a_buf[:, 0:top, :] = a_buf[:, bm:bm + top, :]

    if s_len is None:
        pl.when(jj == 0)(round_chunks)
        pl.when(jj > 0)(units)
    else:
        units()


def _ffn_up_prompt(h, w_gate, w_up, w_down, conv_w, conv_b, ffp):
    t, d = h.shape
    d_ff = w_gate.shape[1]
    cw = conv_w.shape[0]
    assert cw == 3 and d_ff % LANE == 0
    bn = FFN_TILE
    assert ffp % bn == 0 and ffp - d_ff < bn
    bm = _tile(t, MATMUL_TILE, 4 * SUBLANE)
    ni, nj = t // bm, ffp // bn
    assert d % ni == 0
    ck = d // ni
    n_slabs = bn // LANE

    def fin_i(jj, i):
        return jnp.where(jj == 0, 0, i)

    def fin_j(jj):
        return jnp.maximum(jj - 1, 0)

    def chunk(jj, i):
        return jnp.where(jj == nj, ni - 1, i), jnp.minimum(jj, nj - 1)

    n_out = w_down.shape[1]
    assert ffp % (nj * ni) == 0
    rc = ffp // (nj * ni)
    assert rc % (2 * SUBLANE) == 0

    def down_blk(jj, i):
        return fin_j(jj) * ni + fin_i(jj, i)

    last_src = -(-d_ff // rc) - 1
    w_in_spec = pl.BlockSpec((ck, bn), chunk)
    wbuf = pltpu.VMEM((2, d, bn), BF16)
    return pl.pallas_call(
        functools.partial(_ffn_up_kernel, conv_w=cw, s_len=None, d_ff=d_ff),
        grid=(nj + 1, ni),
        in_specs=[pl.BlockSpec((bm, d), lambda jj, i: (fin_i(jj, i), 0)), w_in_spec, w_in_spec,
                  pl.BlockSpec((rc, n_out), lambda jj, i: (jnp.minimum(down_blk(jj, i), last_src), 0)),
                  pl.BlockSpec((cw, bn), lambda jj, i: (0, fin_j(jj))),
                  pl.BlockSpec((1, bn), lambda jj, i: (0, fin_j(jj)))],
        out_specs=[pl.BlockSpec((bm, bn), lambda jj, i: (fin_i(jj, i), fin_j(jj))),
                   pl.BlockSpec((SUBLANE, bn), lambda jj, i: (0, fin_j(jj))),
                   w_in_spec, w_in_spec,
                   pl.BlockSpec((rc, n_out), lambda jj, i: (down_blk(jj, i), 0))],
        out_shape=[jax.ShapeDtypeStruct((t, ffp), BF16), jax.ShapeDtypeStruct((SUBLANE, d_ff), F32),
                   jax.ShapeDtypeStruct((d, d_ff), BF16), jax.ShapeDtypeStruct((d, d_ff), BF16),
                   jax.ShapeDtypeStruct((ffp, n_out), BF16)],
        scratch_shapes=[wbuf, wbuf,
                        pltpu.VMEM((n_slabs, bm + SUBLANE, LANE), F32), pltpu.VMEM((n_slabs, bm, LANE), F32),
                        pltpu.VMEM((n_slabs, bm, LANE), F32)],
        compiler_params=_params(("arbitrary", "arbitrary")),
        name="ffn_up",
    )(h, w_gate, w_up, w_down, conv_w, conv_b.reshape(1, d_ff))


def _ffn_up_sample(h, w_gate, w_up, conv_w, conv_b, ffp, hist, s_len):
    t, d = h.shape
    d_ff = w_gate.shape[1]
    cw = conv_w.shape[0]
    assert cw == 3 and d_ff % LANE == 0 and s_len % 2 == 0
    bn = _tile(ffp, MATMUL_TILE, LANE)
    assert ffp - d_ff < bn
    n_slabs = bn // LANE
    w_spec = pl.BlockSpec((d, bn), lambda j, i: (0, j))
    out_spec = pl.BlockSpec((t, bn), lambda j, i: (i, j))

    def slabs(rows):
        n_seq, r, _ = rows.shape
        sl = jnp.transpose(rows.reshape(n_seq, r, d_ff // LANE, LANE), (2, 0, 1, 3))
        return jnp.pad(sl, ((0, 0), (0, 0), (0, s_len - r), (0, 0))).reshape(d_ff // LANE, t, LANE)
    h1 = slabs(hist[:, 1:2])
    h2 = slabs(hist)
    hist_spec = pl.BlockSpec((n_slabs, t, LANE), lambda j, i: (j, i, 0))
    return pl.pallas_call(
        functools.partial(_ffn_up_kernel, conv_w=cw, s_len=s_len, d_ff=d_ff),
        grid=(ffp // bn, 1),
        in_specs=[pl.BlockSpec((t, d), lambda j, i: (i, 0)), w_spec, w_spec,
                  pl.BlockSpec((cw, bn), lambda j, i: (0, j)), pl.BlockSpec((1, bn), lambda j, i: (0, j)),
                  hist_spec, hist_spec],
        out_specs=[out_spec, out_spec],
        out_shape=[jax.ShapeDtypeStruct((t, ffp), BF16), jax.ShapeDtypeStruct((t, d_ff), F32)],
        scratch_shapes=[pltpu.VMEM((n_slabs, t + SUBLANE, LANE), F32), pltpu.VMEM((n_slabs, t, LANE), F32),
                        pltpu.VMEM((n_slabs, t, LANE), F32)],
        compiler_params=_params(("arbitrary", "arbitrary")),
        name="ffn_up_decode",
    )(h, w_gate, w_up, conv_w, conv_b.reshape(1, d_ff), h1, h2)


def _kv_state_rows_kernel(k_ref, v_ref, ko_ref, vo_ref, *, hpg):
    n = k_ref.shape[1]
    for src, dst in ((k_ref, ko_ref), (v_ref, vo_ref)):
        for h in range(hpg):
            dst[pl.ds(h, n, stride=hpg) if hpg > 1 else pl.ds(0, n), :] = src[h]


def _kv_state_rows(z_sl, k_slab, v_slab, hpg, row0, n_rows):
    br = _tile(n_rows, 512, SUBLANE)
    assert row0 % br == 0 and k_slab % hpg == 0 and v_slab % hpg == 0
    out_spec = pl.BlockSpec((br * hpg, LANE), lambda i: (i, 0))
    return pl.pallas_call(
        functools.partial(_kv_state_rows_kernel, hpg=hpg),
        grid=(n_rows // br,),
        in_specs=[pl.BlockSpec((hpg, br, LANE), lambda i: (k_slab // hpg, row0 // br + i, 0)),
                  pl.BlockSpec((hpg, br, LANE), lambda i: (v_slab // hpg, row0 // br + i, 0))],
        out_specs=[out_spec, out_spec],
        out_shape=[jax.ShapeDtypeStruct((n_rows * hpg, LANE), F32)] * 2,
        compiler_params=_params(("arbitrary",)),
        name="kv_state_rows",
    )(z_sl, z_sl)


def _layer(x, p, s_len=None, caches=None, conv_hist=None):
    t, d = x.shape
    d_a = p["ln_g"].shape[0]
    hpg = p["hpg"]
    gw = hpg * HEAD_DIM
    bn = p["bn"]
    o1 = 2 * d_a
    d_qkv = 3 * gw

    h = _rmsnorm_cast(x, p["g_pre_mix"])
    w16 = dict(p.get("w16", {}))
    z_parts = []
    for name, col0, n_cols, act in (("in_proj_lin", 0, o1 + 3 * d_qkv, lambda v: v),
                                    ("in_proj_gates", o1 + 3 * d_qkv, 2 * d, _sigmoid)):
        if s_len is None:
            z_part, w16[name] = _in_proj_round(h, p["w_in"], col0, n_cols, act, name)
        else:
            z_part = _in_proj(h, w16[name], _tile(n_cols, 2 * MATMUL_TILE, LANE), 0, n_cols, act, name + "_decode")
        z_parts.append(z_part)
    qkv_sl, gate_sl = z_parts

    if s_len is None:
        w_mix, b_mix = p["w_mix_prompt"], p["b_mix_prompt"]
    else:
        w_mix, b_mix = p["w_mix_sample"], p["b_mix_sample"]
    o_a, v_state = _sgu(qkv_sl, d_a, p["ln_g"], p["ln_b"], w_mix, b_mix)

    outs, lses = [], []
    for gi, (_, dil) in enumerate(DILATION_GROUPS):
        q_slab = o1 // LANE + gi * hpg
        k_slab = (o1 + d_qkv) // LANE + gi * hpg
        v_slab = (o1 + 2 * d_qkv) // LANE + gi * hpg
        bias_tab = p["rel_bias"][:, gi * hpg:(gi + 1) * hpg]
        if s_len is not None:
            o, lse = _attn_sample(qkv_sl, s_len, hpg, q_slab, k_slab, v_slab, caches[2 * gi], caches[2 * gi + 1],
                                  bias_tab, dil)
        elif gi < len(DILATION_GROUPS) - 1:
            o, lse = _attn_prompt(qkv_sl, hpg, q_slab, k_slab, v_slab, bias_tab, dil)
        else:
            o_b = _attn_prompt(qkv_sl, hpg, q_slab, k_slab, v_slab, bias_tab, dil, merge_with=(outs, lses))
            break
        outs.append(o)
        lses.append(lse)
    if s_len is not None:
        o_b = _merge(outs, lses)

    merged = _gated_proj(o_a, o_b, p["w_proj_a"], p["w_proj_b"], gate_sl, bn, 0, d // bn)
    if s_len is None:
        y, w16["out_proj"] = _in_proj_round(merged, p["w_out"], 0, d, lambda v: v, "out_proj", slab_out=False)
    else:
        y = _matmul(merged, w16["out_proj"], "out_proj_decode")
    x1, h2 = _post_mix(x, y, p["g_post_mix"], p["g_pre_ffn"])

    if s_len is None:
        act, a_tail, w16["ffn_gate"], w16["ffn_up"], w16["ffn_down"] = _ffn_up_prompt(
            h2, p["w_gate"], p["w_up"], p["w_down"], p["conv_w"], p["conv_b"], p["ffp"])
    else:
        act, a_tail = _ffn_up_sample(h2, w16["ffn_gate"], w16["ffn_up"], p["conv_w"], p["conv_b"], p["ffp"],
                                     conv_hist, s_len)
    f = _matmul(act, w16["ffn_down"], "ffn_down", k_tile=p["ffn_k_tile"])
    y_out = _residual_norm(x1, f, p["g_post_ffn"])
    return y_out, qkv_sl, v_state, a_tail, w16


def kernel(x_prompt, x_sample, cache_k_g1, cache_v_g1, cache_k_g2, cache_v_g2, cache_k_g3, cache_v_g3, state_conv, g_pre_mix, w_in, sgu_ln_g, sgu_ln_b, w_spatial, b_spatial, rel_bias, w_proj_a, w_proj_b, w_out, g_post_mix, g_pre_ffn, w_gate, w_up, conv_w, conv_b, w_down, g_post_ffn):
    depth = w_in.shape[0]
    assert depth == 1
    n_prompt, seq, d = x_prompt.shape
    assert n_prompt == 1 and seq % CHUNK == 0
    n_seq, s_len, _ = x_sample.shape
    assert s_len == SUBLANE
    d_a = sgu_ln_g.shape[1]
    n_groups = w_spatial.shape[1]
    n_heads = rel_bias.shape[1]
    hpg = n_heads // len(DILATION_GROUPS)
    gw = hpg * HEAD_DIM
    d_ff = w_gate.shape[2]
    cw = conv_w.shape[1]
    bn = _tile(math.gcd(2 * d_a, gw), MATMUL_TILE, LANE)
    ffp = -(-d_ff // FFN_TILE) * FFN_TILE
    t_s = n_seq * s_len

    tri = np.tril(np.ones((CHUNK, CHUNK), np.float32))
    w_mix_prompt = (w_spatial[0] * tri).astype(BF16)
    b_mix_prompt = jnp.broadcast_to(b_spatial[0][:, :, None], (n_groups, CHUNK, LANE))
    w_small = w_spatial[0][:, :s_len, :s_len] * tri[:s_len, :s_len]
    same_seq = np.kron(np.eye(n_seq, dtype=np.float32), np.ones((s_len, s_len), np.float32))
    w_mix_sample = (jnp.tile(w_small, (1, n_seq, n_seq)) * same_seq).astype(BF16)
    b_mix_sample = jnp.broadcast_to(jnp.tile(b_spatial[0][:, :s_len], (1, n_seq))[:, :, None], (n_groups, t_s, LANE))

    p = dict(
        hpg=hpg, bn=bn, ffp=ffp,
        g_pre_mix=g_pre_mix[0], w_in=w_in[0], ln_g=sgu_ln_g[0], ln_b=sgu_ln_b[0],
        w_mix_prompt=w_mix_prompt, b_mix_prompt=b_mix_prompt,
        w_mix_sample=w_mix_sample, b_mix_sample=b_mix_sample,
        rel_bias=rel_bias,
        w_proj_a=w_proj_a[0].astype(BF16), w_proj_b=w_proj_b[0].astype(BF16), w_out=w_out[0],
        g_post_mix=g_post_mix[0], g_pre_ffn=g_pre_ffn[0],
        w_gate=w_gate[0], w_up=w_up[0],
        conv_w=conv_w[0], conv_b=conv_b[0],
        w_down=w_down[0],
        ffn_k_tile=_tile(ffp, 3072, LANE),
        g_post_ffn=g_post_ffn[0],
    )

    caches = tuple(c[0] for c in (cache_k_g1, cache_v_g1, cache_k_g2, cache_v_g2, cache_k_g3, cache_v_g3))

    yp, zp, vp_state, ap_tail, w16 = _layer(x_prompt[0], p)
    ys, zs, vs_state, as_all, _ = _layer(x_sample.reshape(t_s, d), dict(p, w16=w16), s_len=s_len,
                                         caches=caches, conv_hist=state_conv[0])

    d_qkv = 3 * gw
    prompt_kv, sample_kv = [], []
    for gi, (win, _) in enumerate(DILATION_GROUPS):
        keep = min(win, seq)
        k_slab = (2 * d_a + d_qkv + gi * gw) // LANE
        v_slab = (2 * d_a + 2 * d_qkv + gi * gw) // LANE
        prompt_kv += [s.reshape(1, 1, keep, hpg, HEAD_DIM)
                      for s in _kv_state_rows(zp, k_slab, v_slab, hpg, seq - keep, keep)]
        sample_kv += [s.reshape(1, n_seq, s_len, hpg, HEAD_DIM)
                      for s in _kv_state_rows(zs, k_slab, v_slab, hpg, 0, t_s)]
    p_conv = ap_tail[SUBLANE - (cw - 1):].reshape(1, 1, cw - 1, d_ff)
    s_conv = as_all.reshape(n_seq, s_len, d_ff)[:, s_len - (cw - 1):].reshape(1, n_seq, cw - 1, d_ff)
    return (yp.reshape(1, seq, d), ys.reshape(n_seq, s_len, d),
            *prompt_kv, vp_state.reshape(1, 1, CHUNK, d_a), p_conv,
            *sample_kv, vs_state.reshape(1, n_seq, s_len, d_a), s_conv)
```

```python
import functools
import math

import numpy as np
import jax
import jax.numpy as jnp
from jax import lax
from jax.experimental import pallas as pl
from jax.experimental.pallas import tpu as pltpu

F32 = jnp.float32
BF16 = jnp.bfloat16

HEAD_DIM = 128
STEPS = 128
CHUNK = 128
GROUP_DIM_A = 128
DILATION_GROUPS = ((128, 1), (512, 4), (2048, 16))
MAX_EXACT = 16
MAX_DISTANCE = 2048
EPS = 1e-6
NEG = -1e30
ATTN_SCALE = HEAD_DIM ** -0.5

LANE = 128
SUBLANE = 8
VMEM_LIMIT_BYTES = 56 * 1024 * 1024
MATMUL_TILE = 1024
MXU_WIDTH = 256
FFN_TILE = 512
IN_PROJ_ROUND_TILE = 1024
ATTN_UNITS_PER_STEP = 16
SGU_ROWS_PER_STEP = 512
ELEMENTWISE_ROWS = 256
KV_STATE_ROWS = 512
FFN_DOWN_K_TILE = 3072
DECODE_COL_TILE = 2048


def _tile(dim, target, align):
    best = None
    t = align
    while t <= min(dim, target):
        if dim % t == 0:
            best = t
        t += align
    return best if best is not None else dim


def _params(semantics):
    return pltpu.CompilerParams(dimension_semantics=semantics, vmem_limit_bytes=VMEM_LIMIT_BYTES)


def _gelu(x):
    return 0.5 * x * (1.0 + jnp.tanh(math.sqrt(2.0 / math.pi) * (x + 0.044715 * (x * x * x))))


def _sigmoid(x):
    return 1.0 / (1.0 + jnp.exp(-x))


def _rms_scale(x):
    return lax.rsqrt(jnp.mean(x * x, axis=-1, keepdims=True) + EPS)


def _lanes(c):
    return slice(c * LANE, (c + 1) * LANE)


def _rmsnorm_cast_kernel(x_ref, g_ref, o_ref):
    x = x_ref[...]
    o_ref[...] = ((x * _rms_scale(x)) * g_ref[...]).astype(o_ref.dtype)


def _rmsnorm_cast(x, g):
    t, d = x.shape
    bt = _tile(t, ELEMENTWISE_ROWS, SUBLANE)
    return pl.pallas_call(
        _rmsnorm_cast_kernel,
        grid=(t // bt,),
        in_specs=[pl.BlockSpec((bt, d), lambda i: (i, 0)), pl.BlockSpec((1, d), lambda i: (0, 0))],
        out_specs=pl.BlockSpec((bt, d), lambda i: (i, 0)),
        out_shape=jax.ShapeDtypeStruct((t, d), BF16),
        compiler_params=_params(("arbitrary",)),
        name="rmsnorm_cast",
    )(x, g.reshape(1, d))


def _in_proj_kernel(x_ref, w_ref, o_ref, *, act):
    acc = jnp.dot(x_ref[...], w_ref[...], preferred_element_type=F32)
    for c in range(o_ref.shape[0]):
        o_ref[c] = act(acc[:, _lanes(c)])


def _in_proj_round_kernel(x_ref, w_in, o_ref, w_out, w_buf, *, act, slab_out):
    jj, i = pl.program_id(0), pl.program_id(1)
    ck = w_in.shape[0]
    slot = (jj + 1) % 2
    bm = o_ref.shape[-2]
    n_slabs = w_in.shape[1] // LANE
    row_split = 2 if bm % (4 * SUBLANE) == 0 else 1
    hm = bm // row_split
    per_strip = min(n_slabs, MXU_WIDTH // LANE)

    def round_chunk():
        chunk = w_in[...].astype(BF16)
        w_buf[jj % 2, pl.ds(pl.multiple_of(i * ck, ck), ck), :] = chunk
        w_out[...] = chunk

    pl.when(jj == 0)(round_chunk)

    @pl.when(jj > 0)
    def _():
        round_chunk()
        for h in range(row_split):
            x = x_ref[h * hm:(h + 1) * hm, :]
            for c0 in range(0, n_slabs, per_strip):
                c1 = min(c0 + per_strip, n_slabs)
                acc = jnp.dot(x, w_buf[slot, :, c0 * LANE:c1 * LANE], preferred_element_type=F32)
                for c in range(c0, c1):
                    val = act(acc[:, _lanes(c - c0)])
                    if slab_out:
                        o_ref[c, h * hm:(h + 1) * hm, :] = val
                    else:
                        o_ref[h * hm:(h + 1) * hm, _lanes(c)] = val


def _in_proj_round(h, w, col0, n_cols, act, name, slab_out=True):
    t, d = h.shape
    bn = _tile(math.gcd(n_cols, col0) if col0 else n_cols, IN_PROJ_ROUND_TILE, LANE)
    bm = _tile(t, MATMUL_TILE, SUBLANE)
    ni, nj = t // bm, n_cols // bn
    assert d % ni == 0
    ck = d // ni

    def fin(jj, i):
        return jnp.maximum(jj - 1, 0), jnp.where(jj == 0, 0, i)

    def chunk(jj, i):
        return jnp.where(jj == nj, ni - 1, i), jnp.minimum(jj, nj - 1)

    if slab_out:
        out_spec = pl.BlockSpec((bn // LANE, bm, LANE), lambda jj, i: fin(jj, i) + (0,))
        out_shape = jax.ShapeDtypeStruct((n_cols // LANE, t, LANE), F32)
    else:
        out_spec = pl.BlockSpec((bm, bn), lambda jj, i: fin(jj, i)[::-1])
        out_shape = jax.ShapeDtypeStruct((t, n_cols), F32)
    return pl.pallas_call(
        functools.partial(_in_proj_round_kernel, act=act, slab_out=slab_out),
        grid=(nj + 1, ni),
        in_specs=[pl.BlockSpec((bm, d), lambda jj, i: (fin(jj, i)[1], 0)),
                  pl.BlockSpec((ck, bn), lambda jj, i: (chunk(jj, i)[0], col0 // bn + chunk(jj, i)[1]))],
        out_specs=[out_spec, pl.BlockSpec((ck, bn), chunk)],
        out_shape=[out_shape, jax.ShapeDtypeStruct((d, n_cols), BF16)],
        scratch_shapes=[pltpu.VMEM((2, d, bn), BF16)],
        compiler_params=_params(("arbitrary", "arbitrary")),
        name=name,
    )(h, w)


def _in_proj(h, w, bn, col0, n_cols, act, name):
    t, d = h.shape
    bm = _tile(t, MATMUL_TILE, SUBLANE)
    return pl.pallas_call(
        functools.partial(_in_proj_kernel, act=act),
        grid=(t // bm, n_cols // bn),
        in_specs=[pl.BlockSpec((bm, d), lambda i, j: (i, 0)),
                  pl.BlockSpec((d, bn), lambda i, j: (0, col0 // bn + j))],
        out_specs=pl.BlockSpec((bn // LANE, bm, LANE), lambda i, j: (j, i, 0)),
        out_shape=jax.ShapeDtypeStruct((n_cols // LANE, t, LANE), F32),
        compiler_params=_params(("arbitrary", "arbitrary")),
        name=name,
    )(h, w)


def _sgu_kernel(u_ref, v_ref, lg_ref, lb_ref, w_ref, b_ref, o_ref, vs_ref, *, n_groups):
    c = w_ref.shape[1]
    n_feat = n_groups * GROUP_DIM_A
    for r0 in range(0, o_ref.shape[0], c):
        vp = _gelu(v_ref[:, r0:r0 + c, :])
        mu = jnp.sum(jnp.sum(vp, axis=0), axis=-1, keepdims=True) / n_feat
        vc = vp - mu
        var = jnp.sum(jnp.sum(vc * vc, axis=0), axis=-1, keepdims=True) / n_feat
        v = vc * lax.rsqrt(var + EPS) * lg_ref[...] + lb_ref[...]
        for g in range(n_groups):
            if r0 + c == o_ref.shape[0]:
                vs_ref[:, _lanes(g)] = v[g]
            mixed = jnp.dot(w_ref[g], v[g].astype(BF16), preferred_element_type=F32) + b_ref[g]
            o_ref[r0:r0 + c, _lanes(g)] = (_gelu(u_ref[g, r0:r0 + c, :]) * mixed).astype(o_ref.dtype)


def _sgu(z_sl, d_a, ln_g, ln_b, w_mix, b_mix):
    t = z_sl.shape[1]
    n_groups, c, _ = w_mix.shape
    rows = _tile(t, SGU_ROWS_PER_STEP, c)
    slab = pl.BlockSpec((n_groups, 1, LANE), lambda i: (0, 0, 0))
    return pl.pallas_call(
        functools.partial(_sgu_kernel, n_groups=n_groups),
        grid=(t // rows,),
        in_specs=[
            pl.BlockSpec((n_groups, rows, LANE), lambda i: (0, i, 0)),
            pl.BlockSpec((n_groups, rows, LANE), lambda i: (1, i, 0)),
            slab, slab,
            pl.BlockSpec((n_groups, c, c), lambda i: (0, 0, 0)),
            pl.BlockSpec((n_groups, c, LANE), lambda i: (0, 0, 0)),
        ],
        out_specs=[pl.BlockSpec((rows, d_a), lambda i: (i, 0)), pl.BlockSpec((c, d_a), lambda i: (0, 0))],
        out_shape=[jax.ShapeDtypeStruct((t, d_a), BF16), jax.ShapeDtypeStruct((c, d_a), F32)],
        compiler_params=_params(("arbitrary",)),
        name="sgu",
    )(z_sl, z_sl, ln_g.reshape(n_groups, 1, LANE), ln_b.reshape(n_groups, 1, LANE), w_mix, b_mix)


def _t5_bucket(dist, n_buckets):
    n = np.asarray(dist, np.int32)
    safe = np.maximum(n, 1).astype(np.float32)
    large = MAX_EXACT + (np.log(safe / MAX_EXACT) / np.log(np.float32(MAX_DISTANCE / MAX_EXACT))
                         * (n_buckets - MAX_EXACT)).astype(np.int32)
    large = np.minimum(large, n_buckets - 1)
    return np.where(n < MAX_EXACT, n, large).astype(np.int32)


def _bias_lookup(bias_tab, bucket):
    n_buckets = bias_tab.shape[0]
    flat = np.asarray(bucket).reshape(-1)
    onehot = (jnp.asarray(flat)[None, :] == jnp.arange(n_buckets)[:, None]).astype(F32)
    out = jnp.dot(bias_tab.T, onehot, precision=lax.Precision.HIGHEST)
    return out.reshape((bias_tab.shape[1],) + tuple(np.asarray(bucket).shape))


def _merge_by_lse(outs, lses):
    m = functools.reduce(jnp.maximum, lses)
    w = [jnp.exp(l - m) for l in lses]
    num = functools.reduce(lambda x, y: x + y, [wi * oi for wi, oi in zip(w, outs)])
    return num / functools.reduce(lambda x, y: x + y, w)


def _attn_prompt_kernel(*refs, hb, dil, n_other):
    q_ref, kc_ref, vc_ref, bias_ref, band_ref = refs[:5]
    others = refs[5:5 + 2 * n_other]
    n_out = 1 if n_other else 2
    outs = refs[5 + 2 * n_other:5 + 2 * n_other + n_out]
    kp_ref, vp_ref = refs[5 + 2 * n_other + n_out:][:2]
    o_acc, lse_acc = refs[5 + 2 * n_other + n_out + 2:] if n_other else outs
    b = pl.program_id(1)

    @pl.when(b == 0)
    def _():
        kp_ref[...] = jnp.zeros(kp_ref.shape, F32)
        vp_ref[...] = jnp.zeros(vp_ref.shape, F32)

    span = STEPS * dil
    n_pb = q_ref.shape[1] // span
    col = lax.broadcasted_iota(jnp.int32, (STEPS, 2 * STEPS), 1)
    band = band_ref[...] > 0.5
    first = band & ((b > 0) | (col >= STEPS))
    for hh in range(hb):
        for pb in range(n_pb):
            for r in range(dil):
                def rows(block):
                    return pl.ds(block * span + r, STEPS, stride=dil) if dil > 1 else pl.ds(block * span, STEPS)
                k_prev = kp_ref[hh, rows(0), :] if pb == 0 else kc_ref[hh, rows(pb - 1), :]
                v_prev = vp_ref[hh, rows(0), :] if pb == 0 else vc_ref[hh, rows(pb - 1), :]
                q = q_ref[hh, rows(pb), :].astype(BF16)
                kk = jnp.concatenate([k_prev, kc_ref[hh, rows(pb), :]], axis=0).astype(BF16)
                vv = jnp.concatenate([v_prev, vc_ref[hh, rows(pb), :]], axis=0).astype(BF16)
                s = lax.dot_general(q, kk, (((1,), (1,)), ((), ())), preferred_element_type=F32) * ATTN_SCALE
                s = jnp.where(first if pb == 0 else band, s + bias_ref[hh], NEG)
                m = jnp.max(s, axis=-1, keepdims=True)
                e = jnp.exp(s - m)
                den = jnp.sum(e, axis=-1, keepdims=True)
                o_acc[hh, rows(pb), :] = jnp.dot(e.astype(BF16), vv, preferred_element_type=F32) / den
                lse_acc[hh, rows(pb), :] = jnp.broadcast_to(m + jnp.log(den), (STEPS, HEAD_DIM))
    kp_ref[...] = kc_ref[:, (n_pb - 1) * span:, :]
    vp_ref[...] = vc_ref[:, (n_pb - 1) * span:, :]
    if n_other:
        for hh in range(hb):
            merged = _merge_by_lse([r[hh] for r in others[:n_other]] + [o_acc[hh]],
                                   [r[hh] for r in others[n_other:]] + [lse_acc[hh]])
            outs[0][:, _lanes(hh)] = merged.astype(outs[0].dtype)


def _attn_prompt(z_sl, hpg, q_slab, k_slab, v_slab, bias_tab, dil, merge_with=None):
    t = z_sl.shape[1]
    span = STEPS * dil
    assert t % span == 0
    hb = min(hpg, max(1, ATTN_UNITS_PER_STEP // dil))
    assert hpg % hb == 0 and q_slab % hb == 0 and k_slab % hb == 0 and v_slab % hb == 0
    n_pb = math.gcd(t // span, max(1, ATTN_UNITS_PER_STEP // (hb * dil)))
    rows = span * n_pb
    nb = t // rows

    p_idx = np.arange(STEPS)[:, None]
    c_idx = np.arange(2 * STEPS)[None, :]
    steps = p_idx + STEPS - c_idx
    band = ((steps >= 0) & (steps <= STEPS)).astype(np.float32)
    bias = _bias_lookup(bias_tab, _t5_bucket(np.clip(steps, 0, STEPS) * dil, bias_tab.shape[0]))

    def cur(slab):
        return pl.BlockSpec((hb, rows, LANE), lambda hi, b: (slab // hb + hi, b, 0))

    head_major = pl.BlockSpec((hb, rows, LANE), lambda hi, b: (hi, b, 0))
    prev_block = pltpu.VMEM((hb, span, LANE), F32)
    block = pltpu.VMEM((hb, rows, LANE), F32)
    others = [] if merge_with is None else list(merge_with[0]) + list(merge_with[1])
    if merge_with is None:
        out_specs = [head_major, head_major]
        out_shape = [jax.ShapeDtypeStruct((hpg, t, LANE), F32)] * 2
        scratch = [prev_block, prev_block]
    else:
        out_specs = [pl.BlockSpec((rows, hb * LANE), lambda hi, b: (b, hi))]
        out_shape = [jax.ShapeDtypeStruct((t, hpg * LANE), BF16)]
        scratch = [prev_block, prev_block, block, block]
    res = pl.pallas_call(
        functools.partial(_attn_prompt_kernel, hb=hb, dil=dil, n_other=len(others) // 2),
        grid=(hpg // hb, nb),
        in_specs=[cur(q_slab), cur(k_slab), cur(v_slab),
                  pl.BlockSpec((hb, STEPS, 2 * STEPS), lambda hi, b: (hi, 0, 0)),
                  pl.BlockSpec((STEPS, 2 * STEPS), lambda hi, b: (0, 0))] + [head_major] * len(others),
        out_specs=out_specs,
        out_shape=out_shape,
        scratch_shapes=scratch,
        compiler_params=_params(("arbitrary", "arbitrary")),
        name=f"attn_prompt_d{dil}",
    )(z_sl, z_sl, z_sl, bias, jnp.asarray(band), *others)
    return res[0] if merge_with is not None else res


def _attn_sample_kernel(q_ref, kn_ref, vn_ref, ck_ref, cv_ref, bc_ref, mc_ref, bnew_ref, mnew_ref,
                        o_ref, lse_ref, *, hpg, s_len, grouped):
    n_keys = mc_ref.shape[1]
    n_new = hpg * s_len
    valid_c = mc_ref[...] > 0.5

    def cache_head(ref, h):
        if grouped:
            return ref[0, :, pl.ds(h, s_len, stride=hpg), :].reshape(n_keys, HEAD_DIM).astype(BF16)
        return ref[0, pl.ds(h, n_keys, stride=hpg) if hpg > 1 else pl.ds(0, n_keys), :].astype(BF16)

    q_all = q_ref[...].reshape(n_new, HEAD_DIM).astype(BF16)
    s_new = lax.dot_general(q_all, kn_ref[...].reshape(n_new, HEAD_DIM).astype(BF16), (((1,), (1,)), ((), ())),
                            preferred_element_type=F32) * ATTN_SCALE
    key_head = lax.broadcasted_iota(jnp.int32, (s_len, n_new), 1) // s_len
    new_ok = mnew_ref[...] > 0.5

    partial, e_new = [], []
    for h in range(hpg):
        q = q_all[h * s_len:(h + 1) * s_len, :]
        sc = lax.dot_general(q, cache_head(ck_ref, h), (((1,), (1,)), ((), ())),
                             preferred_element_type=F32) * ATTN_SCALE
        sc = jnp.where(valid_c, sc + bc_ref[h], NEG)
        sn = jnp.where(new_ok & (key_head == h), s_new[h * s_len:(h + 1) * s_len, :] + bnew_ref[h], NEG)
        m = jnp.maximum(jnp.max(sc, axis=-1, keepdims=True), jnp.max(sn, axis=-1, keepdims=True))
        ec = jnp.exp(sc - m)
        en = jnp.exp(sn - m)
        den = jnp.sum(ec, axis=-1, keepdims=True) + jnp.sum(en, axis=-1, keepdims=True)
        acc = jnp.dot(ec.astype(BF16), cache_head(cv_ref, h), preferred_element_type=F32)
        partial.append((acc, den, m))
        e_new.append(en)
    o_new = jnp.dot(jnp.concatenate(e_new, axis=0).astype(BF16),
                    vn_ref[...].reshape(n_new, HEAD_DIM).astype(BF16), preferred_element_type=F32)
    for h, (acc, den, m) in enumerate(partial):
        o_ref[h] = (acc + o_new[h * s_len:(h + 1) * s_len, :]) / den
        lse_ref[h] = jnp.broadcast_to(m + jnp.log(den), (s_len, HEAD_DIM))


def _attn_sample(z_sl, s_len, hpg, q_slab, k_slab, v_slab, cache_k, cache_v, bias_tab, dil):
    t = z_sl.shape[1]
    n_seq = t // s_len
    lc = cache_k.shape[1]
    assert q_slab % hpg == 0 and k_slab % hpg == 0 and v_slab % hpg == 0

    j = np.arange(STEPS + 1)
    idx = lc + np.arange(s_len)[:, None] - j[None, :] * dil
    assert idx.min() >= 0
    bucket = _t5_bucket(j * dil, bias_tab.shape[0])
    mask = np.zeros((s_len, lc + s_len), np.float32)
    bsel = np.zeros((s_len, lc + s_len), np.int32)
    for s in range(s_len):
        mask[s, idx[s]] = 1.0
        bsel[s, idx[s]] = bucket

    grouped = dil > s_len and lc % dil == 0 and s_len == SUBLANE
    if grouped:
        pos = (np.arange(lc // dil)[:, None] * dil + np.arange(s_len)[None, :]).reshape(-1)
        assert mask[:, :lc].sum() == mask[:, pos].sum()
        ck = cache_k.reshape(n_seq, lc // dil, dil * hpg, HEAD_DIM)
        cv = cache_v.reshape(n_seq, lc // dil, dil * hpg, HEAD_DIM)
        cache_spec = pl.BlockSpec((1, lc // dil, s_len * hpg, LANE), lambda b: (b, 0, 0, 0))
    else:
        pos = np.arange(lc)
        ck = cache_k.reshape(n_seq, lc * hpg, HEAD_DIM)
        cv = cache_v.reshape(n_seq, lc * hpg, HEAD_DIM)
        cache_spec = pl.BlockSpec((1, lc * hpg, LANE), lambda b: (b, 0, 0))
    n_keys = len(pos)
    bias_c = _bias_lookup(bias_tab, bsel[:, pos])
    bias_new = jnp.tile(_bias_lookup(bias_tab, bsel[:, lc:]), (1, 1, hpg))
    mask_c = jnp.asarray(mask[:, pos])
    mask_new = jnp.asarray(np.tile(mask[:, lc:], (1, hpg)))

    out_spec = pl.BlockSpec((hpg, s_len, LANE), lambda b: (0, b, 0))
    return pl.pallas_call(
        functools.partial(_attn_sample_kernel, hpg=hpg, s_len=s_len, grouped=grouped),
        grid=(n_seq,),
        in_specs=[pl.BlockSpec((hpg, s_len, LANE), lambda b: (q_slab // hpg, b, 0)),
                  pl.BlockSpec((hpg, s_len, LANE), lambda b: (k_slab // hpg, b, 0)),
                  pl.BlockSpec((hpg, s_len, LANE), lambda b: (v_slab // hpg, b, 0)),
                  cache_spec, cache_spec,
                  pl.BlockSpec((hpg, s_len, n_keys), lambda b: (0, 0, 0)),
                  pl.BlockSpec((s_len, n_keys), lambda b: (0, 0)),
                  pl.BlockSpec((hpg, s_len, hpg * s_len), lambda b: (0, 0, 0)),
                  pl.BlockSpec((s_len, hpg * s_len), lambda b: (0, 0))],
        out_specs=[out_spec, out_spec],
        out_shape=[jax.ShapeDtypeStruct((hpg, t, LANE), F32)] * 2,
        compiler_params=_params(("arbitrary",)),
        name=f"attn_sample_d{dil}",
    )(z_sl, z_sl, z_sl, ck, cv, bias_c, mask_c, bias_new, mask_new)


def _merge_kernel(o1, o2, o3, l1, l2, l3, out_ref):
    merged = _merge_by_lse([o1[...], o2[...], o3[...]], [l1[...], l2[...], l3[...]])
    for h in range(merged.shape[0]):
        out_ref[:, _lanes(h)] = merged[h].astype(out_ref.dtype)


def _merge(outs, lses):
    hpg, t, _ = outs[0].shape
    bt = _tile(t, ELEMENTWISE_ROWS, SUBLANE)
    spec = pl.BlockSpec((hpg, bt, LANE), lambda i: (0, i, 0))
    return pl.pallas_call(
        _merge_kernel,
        grid=(t // bt,),
        in_specs=[spec] * 6,
        out_specs=pl.BlockSpec((bt, hpg * LANE), lambda i: (i, 0)),
        out_shape=jax.ShapeDtypeStruct((t, hpg * LANE), BF16),
        compiler_params=_params(("arbitrary",)),
        name="merge_groups",
    )(*outs, *lses)


def _gated_proj_kernel(a_ref, b_ref, wa_ref, wb_ref, ga_ref, gb_ref, o_ref):
    n_slabs = ga_ref.shape[0]
    per_strip = min(n_slabs, MXU_WIDTH // LANE)
    for c0 in range(0, n_slabs, per_strip):
        c1 = min(c0 + per_strip, n_slabs)
        cols = slice(c0 * LANE, c1 * LANE)
        pa = jnp.dot(a_ref[...], wa_ref[:, cols], preferred_element_type=F32)
        pb = jnp.dot(b_ref[...], wb_ref[:, cols], preferred_element_type=F32)
        for c in range(c0, c1):
            gated = ga_ref[c] * pa[:, _lanes(c - c0)] + gb_ref[c] * pb[:, _lanes(c - c0)]
            o_ref[:, _lanes(c)] = gated.astype(o_ref.dtype)


def _gated_proj(o_a, o_b, w_a, w_b, z_sl, bn, gate_a_blk, gate_b_blk):
    t, d_a = o_a.shape
    d_b = o_b.shape[1]
    d = w_a.shape[1]
    bm = _tile(t, MATMUL_TILE, SUBLANE)
    return pl.pallas_call(
        _gated_proj_kernel,
        grid=(t // bm, d // bn),
        in_specs=[pl.BlockSpec((bm, d_a), lambda i, j: (i, 0)),
                  pl.BlockSpec((bm, d_b), lambda i, j: (i, 0)),
                  pl.BlockSpec((d_a, bn), lambda i, j: (0, j)),
                  pl.BlockSpec((d_b, bn), lambda i, j: (0, j)),
                  pl.BlockSpec((bn // LANE, bm, LANE), lambda i, j: (gate_a_blk + j, i, 0)),
                  pl.BlockSpec((bn // LANE, bm, LANE), lambda i, j: (gate_b_blk + j, i, 0))],
        out_specs=pl.BlockSpec((bm, bn), lambda i, j: (i, j)),
        out_shape=jax.ShapeDtypeStruct((t, d), BF16),
        compiler_params=_params(("arbitrary", "arbitrary")),
        name="gated_proj",
    )(o_a, o_b, w_a, w_b, z_sl, z_sl)


def _matmul_kernel(x_ref, w_ref, o_ref, *, nk):
    if nk == 1:
        o_ref[...] = jnp.dot(x_ref[...], w_ref[...], preferred_element_type=F32)
    else:
        @pl.when(pl.program_id(2) == 0)
        def _():
            o_ref[...] = jnp.zeros(o_ref.shape, o_ref.dtype)

        o_ref[...] += jnp.dot(x_ref[...], w_ref[...], preferred_element_type=F32)


def _matmul(x, w, name, k_tile=None):
    t, kd = x.shape
    n = w.shape[1]
    bm = _tile(t, MATMUL_TILE, SUBLANE)
    bn = _tile(n, MATMUL_TILE, LANE)
    bk = kd if k_tile is None else k_tile
    nk = kd // bk
    return pl.pallas_call(
        functools.partial(_matmul_kernel, nk=nk),
        grid=(t // bm, n // bn, nk),
        in_specs=[pl.BlockSpec((bm, bk), lambda i, j, k: (i, k)),
                  pl.BlockSpec((bk, bn), lambda i, j, k: (k, j))],
        out_specs=pl.BlockSpec((bm, bn), lambda i, j, k: (i, j)),
        out_shape=jax.ShapeDtypeStruct((t, n), F32),
        compiler_params=_params(("arbitrary", "arbitrary", "arbitrary")),
        name=name,
    )(x, w)


def _post_mix_kernel(x_ref, y_ref, g1_ref, g2_ref, x1_ref, h2_ref):
    y = y_ref[...]
    x1 = x_ref[...] + (y * _rms_scale(y)) * g1_ref[...]
    x1_ref[...] = x1
    h2_ref[...] = ((x1 * _rms_scale(x1)) * g2_ref[...]).astype(h2_ref.dtype)


def _post_mix(x, y, g_post, g_pre):
    t, d = x.shape
    bt = _tile(t, ELEMENTWISE_ROWS, SUBLANE)
    row = pl.BlockSpec((bt, d), lambda i: (i, 0))
    vec = pl.BlockSpec((1, d), lambda i: (0, 0))
    return pl.pallas_call(
        _post_mix_kernel,
        grid=(t // bt,),
        in_specs=[row, row, vec, vec],
        out_specs=[row, row],
        out_shape=[jax.ShapeDtypeStruct((t, d), F32), jax.ShapeDtypeStruct((t, d), BF16)],
        compiler_params=_params(("arbitrary",)),
        name="post_mix",
    )(x, y, g_post.reshape(1, d), g_pre.reshape(1, d))


def _residual_norm_kernel(x_ref, y_ref, g_ref, o_ref):
    y = y_ref[...]
    o_ref[...] = x_ref[...] + (y * _rms_scale(y)) * g_ref[...]


def _residual_norm(x, y, g):
    t, d = x.shape
    bt = _tile(t, ELEMENTWISE_ROWS, SUBLANE)
    row = pl.BlockSpec((bt, d), lambda i: (i, 0))
    return pl.pallas_call(
        _residual_norm_kernel,
        grid=(t // bt,),
        in_specs=[row, row, pl.BlockSpec((1, d), lambda i: (0, 0))],
        out_specs=row,
        out_shape=jax.ShapeDtypeStruct((t, d), F32),
        compiler_params=_params(("arbitrary",)),
        name="residual_norm",
    )(x, y, g.reshape(1, d))


def _ffn_up_kernel(*refs, conv_w, s_len, d_ff):
    if s_len is None:
        (x_ref, wg_in, wu_in, wd_in, cw_ref, cb_ref, out_ref, tail_ref, wg_out, wu_out, wd_out,
         wg_buf, wu_buf, a_buf, u_buf, o_buf) = refs
        hist_refs = None
        jj, i = pl.program_id(0), pl.program_id(1)
        ck = wg_in.shape[0]

        def round_chunks():
            rows_c = pl.ds(pl.multiple_of(i * ck, ck), ck)
            for src, buf, dst in ((wg_in, wg_buf, wg_out), (wu_in, wu_buf, wu_out)):
                chunk = src[...].astype(BF16)
                buf[jj % 2, rows_c, :] = chunk
                dst[...] = chunk

        def round_down_rows():
            rc = wd_in.shape[0]
            row = ((jj - 1) * pl.num_programs(1) + i) * rc + lax.broadcasted_iota(jnp.int32, wd_in.shape, 0)
            wd_out[...] = jnp.where(row < d_ff, wd_in[...], 0.0).astype(BF16)

        slot = (jj + 1) % 2
        wg = lambda cols: wg_buf[slot, :, cols]
        wu = lambda cols: wu_buf[slot, :, cols]
        col_blk = jj - 1
        row_split = 2 if out_ref.shape[0] % (4 * SUBLANE) == 0 else 1
    else:
        x_ref, wg_ref, wu_ref, cw_ref, cb_ref, h1_ref, h2_ref, out_ref, tail_ref, a_buf, u_buf, o_buf = refs
        hist_refs = {1: h1_ref, 2: h2_ref}
        i = pl.program_id(1)
        wg = lambda cols: wg_ref[:, cols]
        wu = lambda cols: wu_ref[:, cols]
        col_blk = pl.program_id(0)
        row_split = 1
    bm, bn = out_ref.shape
    hm = bm // row_split
    half = hm // 2
    top = SUBLANE
    n_slabs = bn // LANE
    per_strip = min(n_slabs, MXU_WIDTH // LANE)

    def units():
        @pl.when(i == 0)
        def _():
            a_buf[:, 0:top, :] = jnp.zeros((n_slabs, top, LANE), F32)

        lane = lax.broadcasted_iota(jnp.int32, (1, LANE), 1)
        for h in range(row_split):
            r0 = h * hm
            x = x_ref[r0:r0 + hm, :]
            for c0 in range(0, n_slabs, per_strip):
                c1 = min(c0 + per_strip, n_slabs)
                cols = slice(c0 * LANE, c1 * LANE)
                a = jnp.dot(x, wg(cols), preferred_element_type=F32)
                u = jnp.dot(x, wu(cols), preferred_element_type=F32)
                if s_len is None and h == 0 and c0 == 0:
                    round_chunks()
                    round_down_rows()
                if s_len is not None:
                    tail_ref[r0:r0 + hm, cols] = a
                elif h == row_split - 1:
                    tail_ref[:, cols] = a[hm - SUBLANE:, :]
                for c in range(c0, c1):
                    a_buf[c, top + r0:top + r0 + hm, :] = a[:, _lanes(c - c0)]
                    u_buf[c, r0:r0 + hm, :] = u[:, _lanes(c - c0)]
                    in_range = (col_blk * bn + c * LANE + lane) < d_ff
                    for e in range(2):
                        taps = [a_buf[c, pl.ds(top + r0 + e - lag, half, stride=2), :] for lag in range(conv_w)]
                        if hist_refs is not None:
                            s = 2 * (lax.broadcasted_iota(jnp.int32, (half, LANE), 0) % (s_len // 2)) + e
                            for lag in range(1, conv_w):
                                hist = hist_refs[lag][c, pl.ds(r0 + e, half, stride=2), :]
                                taps[lag] = jnp.where(s < lag, hist, taps[lag])
                        acc = cw_ref[0:1, _lanes(c)] * taps[conv_w - 1]
                        for k in range(1, conv_w):
                            acc = acc + cw_ref[k:k + 1, _lanes(c)] * taps[conv_w - 1 - k]
                        acc = cb_ref[:, _lanes(c)] + acc
                        val = _gelu(acc) * u_buf[c, pl.ds(r0 + e, half, stride=2), :]
                        o_buf[c, pl.ds(r0 + e, half, stride=2), :] = jnp.where(in_range, val, 0.0)
                    out_ref[r0:r0 + hm, _lanes(c)] = o_buf[c, r0:r0 + hm, :].astype(out_ref.dtype)
        if s_len is None:
            a_buf[:, 0:top, :] = a_buf[:, bm:bm + top, :]

    if s_len is None:
        pl.when(jj == 0)(round_chunks)
        pl.when(jj > 0)(units)
    else:
        units()


def _ffn_up_prompt(h, w_gate, w_up, w_down, conv_w, conv_b, ffp):
    t, d = h.shape
    d_ff = w_gate.shape[1]
    cw = conv_w.shape[0]
    assert cw == 3 and d_ff % LANE == 0
    bn = FFN_TILE
    assert ffp % bn == 0 and ffp - d_ff < bn
    bm = _tile(t, MATMUL_TILE, 4 * SUBLANE)
    ni, nj = t // bm, ffp // bn
    assert d % ni == 0
    ck = d // ni
    n_slabs = bn // LANE

    def fin_i(jj, i):
        return jnp.where(jj == 0, 0, i)

    def fin_j(jj):
        return jnp.maximum(jj - 1, 0)

    def chunk(jj, i):
        return jnp.where(jj == nj, ni - 1, i), jnp.minimum(jj, nj - 1)

    n_out = w_down.shape[1]
    assert ffp % (nj * ni) == 0
    rc = ffp // (nj * ni)
    assert rc % (2 * SUBLANE) == 0

    def down_blk(jj, i):
        return fin_j(jj) * ni + fin_i(jj, i)

    last_src = -(-d_ff // rc) - 1
    w_in_spec = pl.BlockSpec((ck, bn), chunk)
    wbuf = pltpu.VMEM((2, d, bn), BF16)
    return pl.pallas_call(
        functools.partial(_ffn_up_kernel, conv_w=cw, s_len=None, d_ff=d_ff),
        grid=(nj + 1, ni),
        in_specs=[pl.BlockSpec((bm, d), lambda jj, i: (fin_i(jj, i), 0)), w_in_spec, w_in_spec,
                  pl.BlockSpec((rc, n_out), lambda jj, i: (jnp.minimum(down_blk(jj, i), last_src), 0)),
                  pl.BlockSpec((cw, bn), lambda jj, i: (0, fin_j(jj))),
                  pl.BlockSpec((1, bn), lambda jj, i: (0, fin_j(jj)))],
        out_specs=[pl.BlockSpec((bm, bn), lambda jj, i: (fin_i(jj, i), fin_j(jj))),
                   pl.BlockSpec((SUBLANE, bn), lambda jj, i: (0, fin_j(jj))),
                   w_in_spec, w_in_spec,
                   pl.BlockSpec((rc, n_out), lambda jj, i: (down_blk(jj, i), 0))],
        out_shape=[jax.ShapeDtypeStruct((t, ffp), BF16), jax.ShapeDtypeStruct((SUBLANE, d_ff), F32),
                   jax.ShapeDtypeStruct((d, d_ff), BF16), jax.ShapeDtypeStruct((d, d_ff), BF16),
                   jax.ShapeDtypeStruct((ffp, n_out), BF16)],
        scratch_shapes=[wbuf, wbuf,
                        pltpu.VMEM((n_slabs, bm + SUBLANE, LANE), F32), pltpu.VMEM((n_slabs, bm, LANE), F32),
                        pltpu.VMEM((n_slabs, bm, LANE), F32)],
        compiler_params=_params(("arbitrary", "arbitrary")),
        name="ffn_up",
    )(h, w_gate, w_up, w_down, conv_w, conv_b.reshape(1, d_ff))


def _ffn_up_sample(h, w_gate, w_up, conv_w, conv_b, ffp, hist, s_len):
    t, d = h.shape
    d_ff = w_gate.shape[1]
    cw = conv_w.shape[0]
    assert cw == 3 and d_ff % LANE == 0 and s_len % 2 == 0
    bn = _tile(ffp, MATMUL_TILE, LANE)
    assert ffp - d_ff < bn
    n_slabs = bn // LANE
    w_spec = pl.BlockSpec((d, bn), lambda j, i: (0, j))
    out_spec = pl.BlockSpec((t, bn), lambda j, i: (i, j))

    def slabs(rows):
        n_seq, r, _ = rows.shape
        sl = jnp.transpose(rows.reshape(n_seq, r, d_ff // LANE, LANE), (2, 0, 1, 3))
        return jnp.pad(sl, ((0, 0), (0, 0), (0, s_len - r), (0, 0))).reshape(d_ff // LANE, t, LANE)
    h1 = slabs(hist[:, 1:2])
    h2 = slabs(hist)
    hist_spec = pl.BlockSpec((n_slabs, t, LANE), lambda j, i: (j, i, 0))
    return pl.pallas_call(
        functools.partial(_ffn_up_kernel, conv_w=cw, s_len=s_len, d_ff=d_ff),
        grid=(ffp // bn, 1),
        in_specs=[pl.BlockSpec((t, d), lambda j, i: (i, 0)), w_spec, w_spec,
                  pl.BlockSpec((cw, bn), lambda j, i: (0, j)), pl.BlockSpec((1, bn), lambda j, i: (0, j)),
                  hist_spec, hist_spec],
        out_specs=[out_spec, out_spec],
        out_shape=[jax.ShapeDtypeStruct((t, ffp), BF16), jax.ShapeDtypeStruct((t, d_ff), F32)],
        scratch_shapes=[pltpu.VMEM((n_slabs, t + SUBLANE, LANE), F32), pltpu.VMEM((n_slabs, t, LANE), F32),
                        pltpu.VMEM((n_slabs, t, LANE), F32)],
        compiler_params=_params(("arbitrary", "arbitrary")),
        name="ffn_up_decode",
    )(h, w_gate, w_up, conv_w, conv_b.reshape(1, d_ff), h1, h2)


def _kv_state_rows_kernel(k_ref, v_ref, ko_ref, vo_ref, *, hpg):
    n = k_ref.shape[1]
    for src, dst in ((k_ref, ko_ref), (v_ref, vo_ref)):
        for h in range(hpg):
            dst[pl.ds(h, n, stride=hpg) if hpg > 1 else pl.ds(0, n), :] = src[h]


def _kv_state_rows(z_sl, k_slab, v_slab, hpg, row0, n_rows):
    br = _tile(n_rows, KV_STATE_ROWS, SUBLANE)
    assert row0 % br == 0 and k_slab % hpg == 0 and v_slab % hpg == 0
    out_spec = pl.BlockSpec((br * hpg, LANE), lambda i: (i, 0))
    return pl.pallas_call(
        functools.partial(_kv_state_rows_kernel, hpg=hpg),
        grid=(n_rows // br,),
        in_specs=[pl.BlockSpec((hpg, br, LANE), lambda i: (k_slab // hpg, row0 // br + i, 0)),
                  pl.BlockSpec((hpg, br, LANE), lambda i: (v_slab // hpg, row0 // br + i, 0))],
        out_specs=[out_spec, out_spec],
        out_shape=[jax.ShapeDtypeStruct((n_rows * hpg, LANE), F32)] * 2,
        compiler_params=_params(("arbitrary",)),
        name="kv_state_rows",
    )(z_sl, z_sl)


def _layer(x, p, s_len=None, caches=None, conv_hist=None):
    t, d = x.shape
    d_a = p["ln_g"].shape[0]
    hpg = p["hpg"]
    gw = hpg * HEAD_DIM
    bn = p["bn"]
    o1 = 2 * d_a
    d_qkv = 3 * gw

    h = _rmsnorm_cast(x, p["g_pre_mix"])
    w16 = dict(p.get("w16", {}))
    z_parts = []
    for name, col0, n_cols, act in (("in_proj_lin", 0, o1 + 3 * d_qkv, lambda v: v),
                                    ("in_proj_gates", o1 + 3 * d_qkv, 2 * d, _sigmoid)):
        if s_len is None:
            z_part, w16[name] = _in_proj_round(h, p["w_in"], col0, n_cols, act, name)
        else:
            z_part = _in_proj(h, w16[name], _tile(n_cols, DECODE_COL_TILE, LANE), 0, n_cols, act, name + "_decode")
        z_parts.append(z_part)
    qkv_sl, gate_sl = z_parts

    if s_len is None:
        w_mix, b_mix = p["w_mix_prompt"], p["b_mix_prompt"]
    else:
        w_mix, b_mix = p["w_mix_sample"], p["b_mix_sample"]
    o_a, v_state = _sgu(qkv_sl, d_a, p["ln_g"], p["ln_b"], w_mix, b_mix)

    outs, lses = [], []
    for gi, (_, dil) in enumerate(DILATION_GROUPS):
        q_slab = o1 // LANE + gi * hpg
        k_slab = (o1 + d_qkv) // LANE + gi * hpg
        v_slab = (o1 + 2 * d_qkv) // LANE + gi * hpg
        bias_tab = p["rel_bias"][:, gi * hpg:(gi + 1) * hpg]
        if s_len is not None:
            o, lse = _attn_sample(qkv_sl, s_len, hpg, q_slab, k_slab, v_slab, caches[2 * gi], caches[2 * gi + 1],
                                  bias_tab, dil)
        elif gi < len(DILATION_GROUPS) - 1:
            o, lse = _attn_prompt(qkv_sl, hpg, q_slab, k_slab, v_slab, bias_tab, dil)
        else:
            o_b = _attn_prompt(qkv_sl, hpg, q_slab, k_slab, v_slab, bias_tab, dil, merge_with=(outs, lses))
            break
        outs.append(o)
        lses.append(lse)
    if s_len is not None:
        o_b = _merge(outs, lses)

    merged = _gated_proj(o_a, o_b, p["w_proj_a"], p["w_proj_b"], gate_sl, bn, 0, d // bn)
    if s_len is None:
        y, w16["out_proj"] = _in_proj_round(merged, p["w_out"], 0, d, lambda v: v, "out_proj", slab_out=False)
    else:
        y = _matmul(merged, w16["out_proj"], "out_proj_decode")
    x1, h2 = _post_mix(x, y, p["g_post_mix"], p["g_pre_ffn"])

    if s_len is None:
        act, a_tail, w16["ffn_gate"], w16["ffn_up"], w16["ffn_down"] = _ffn_up_prompt(
            h2, p["w_gate"], p["w_up"], p["w_down"], p["conv_w"], p["conv_b"], p["ffp"])
    else:
        act, a_tail = _ffn_up_sample(h2, w16["ffn_gate"], w16["ffn_up"], p["conv_w"], p["conv_b"], p["ffp"],
                                     conv_hist, s_len)
    f = _matmul(act, w16["ffn_down"], "ffn_down", k_tile=p["ffn_k_tile"])
    y_out = _residual_norm(x1, f, p["g_post_ffn"])
    return y_out, qkv_sl, v_state, a_tail, w16


def kernel(x_prompt, x_sample, cache_k_g1, cache_v_g1, cache_k_g2, cache_v_g2, cache_k_g3, cache_v_g3, state_conv, g_pre_mix, w_in, sgu_ln_g, sgu_ln_b, w_spatial, b_spatial, rel_bias, w_proj_a, w_proj_b, w_out, g_post_mix, g_pre_ffn, w_gate, w_up, conv_w, conv_b, w_down, g_post_ffn):
    depth = w_in.shape[0]
    assert depth == 1
    n_prompt, seq, d = x_prompt.shape
    assert n_prompt == 1 and seq % CHUNK == 0
    n_seq, s_len, _ = x_sample.shape
    assert s_len == SUBLANE
    d_a = sgu_ln_g.shape[1]
    n_groups = w_spatial.shape[1]
    n_heads = rel_bias.shape[1]
    hpg = n_heads // len(DILATION_GROUPS)
    gw = hpg * HEAD_DIM
    d_ff = w_gate.shape[2]
    cw = conv_w.shape[1]
    bn = _tile(math.gcd(2 * d_a, gw), MATMUL_TILE, LANE)
    ffp = -(-d_ff // FFN_TILE) * FFN_TILE
    t_s = n_seq * s_len

    tri = np.tril(np.ones((CHUNK, CHUNK), np.float32))
    w_mix_prompt = (w_spatial[0] * tri).astype(BF16)
    b_mix_prompt = jnp.broadcast_to(b_spatial[0][:, :, None], (n_groups, CHUNK, LANE))
    w_small = w_spatial[0][:, :s_len, :s_len] * tri[:s_len, :s_len]
    same_seq = np.kron(np.eye(n_seq, dtype=np.float32), np.ones((s_len, s_len), np.float32))
    w_mix_sample = (jnp.tile(w_small, (1, n_seq, n_seq)) * same_seq).astype(BF16)
    b_mix_sample = jnp.broadcast_to(jnp.tile(b_spatial[0][:, :s_len], (1, n_seq))[:, :, None], (n_groups, t_s, LANE))

    p = dict(
        hpg=hpg, bn=bn, ffp=ffp,
        g_pre_mix=g_pre_mix[0], w_in=w_in[0], ln_g=sgu_ln_g[0], ln_b=sgu_ln_b[0],
        w_mix_prompt=w_mix_prompt, b_mix_prompt=b_mix_prompt,
        w_mix_sample=w_mix_sample, b_mix_sample=b_mix_sample,
        rel_bias=rel_bias,
        w_proj_a=w_proj_a[0].astype(BF16), w_proj_b=w_proj_b[0].astype(BF16), w_out=w_out[0],
        g_post_mix=g_post_mix[0], g_pre_ffn=g_pre_ffn[0],
        w_gate=w_gate[0], w_up=w_up[0],
        conv_w=conv_w[0], conv_b=conv_b[0],
        w_down=w_down[0],
        ffn_k_tile=_tile(ffp, FFN_DOWN_K_TILE, LANE),
        g_post_ffn=g_post_ffn[0],
    )

    caches = tuple(c[0] for c in (cache_k_g1, cache_v_g1, cache_k_g2, cache_v_g2, cache_k_g3, cache_v_g3))

    yp, zp, vp_state, ap_tail, w16 = _layer(x_prompt[0], p)
    ys, zs, vs_state, as_all, _ = _layer(x_sample.reshape(t_s, d), dict(p, w16=w16), s_len=s_len,
                                         caches=caches, conv_hist=state_conv[0])

    d_qkv = 3 * gw
    prompt_kv, sample_kv = [], []
    for gi, (win, _) in enumerate(DILATION_GROUPS):
        keep = min(win, seq)
        k_slab = (2 * d_a + d_qkv + gi * gw) // LANE
        v_slab = (2 * d_a + 2 * d_qkv + gi * gw) // LANE
        prompt_kv += [s.reshape(1, 1, keep, hpg, HEAD_DIM)
                      for s in _kv_state_rows(zp, k_slab, v_slab, hpg, seq - keep, keep)]
        sample_kv += [s.reshape(1, n_seq, s_len, hpg, HEAD_DIM)
                      for s in _kv_state_rows(zs, k_slab, v_slab, hpg, 0, t_s)]
    p_conv = ap_tail[SUBLANE - (cw - 1):].reshape(1, 1, cw - 1, d_ff)
    s_conv = as_all.reshape(n_seq, s_len, d_ff)[:, s_len - (cw - 1):].reshape(1, n_seq, cw - 1, d_ff)
    return (yp.reshape(1, seq, d), ys.reshape(n_seq, s_len, d),
            *prompt_kv, vp_state.reshape(1, 1, CHUNK, d_a), p_conv,
            *sample_kv, vs_state.reshape(1, n_seq, s_len, d_a), s_conv)
```

```python
import functools
import math

import numpy as np
import jax
import jax.numpy as jnp
from jax import lax
from jax.experimental import pallas as pl
from jax.experimental.pallas import tpu as pltpu

F32 = jnp.float32
BF16 = jnp.bfloat16

HEAD_DIM = 128
STEPS = 128
CHUNK = 128
GROUP_DIM_A = 128
DILATION_GROUPS = ((128, 1), (512, 4), (2048, 16))
MAX_EXACT = 16
MAX_DISTANCE = 2048
EPS = 1e-6
NEG = -1e30
ATTN_SCALE = HEAD_DIM ** -0.5

LANE = 128
SUBLANE = 8
VMEM_LIMIT_BYTES = 56 * 1024 * 1024
MATMUL_TILE = 1024
MXU_WIDTH = 256
FFN_TILE = 512
IN_PROJ_ROUND_TILE = 1024
ATTN_UNITS_PER_STEP = 16
SGU_ROWS_PER_STEP = 512
ELEMENTWISE_ROWS = 256
KV_STATE_ROWS = 512
FFN_DOWN_K_TILE = 3072
DECODE_COL_TILE = 2048


def _tile(dim, target, align):
    best = None
    t = align
    while t <= min(dim, target):
        if dim % t == 0:
            best = t
        t += align
    return best if best is not None else dim


def _params(semantics):
    return pltpu.CompilerParams(dimension_semantics=semantics, vmem_limit_bytes=VMEM_LIMIT_BYTES)


def _gelu(x):
    return 0.5 * x * (1.0 + jnp.tanh(math.sqrt(2.0 / math.pi) * (x + 0.044715 * (x * x * x))))


def _sigmoid(x):
    return 1.0 / (1.0 + jnp.exp(-x))


def _rms_scale(x):
    return lax.rsqrt(jnp.mean(x * x, axis=-1, keepdims=True) + EPS)


def _lanes(c):
    return slice(c * LANE, (c + 1) * LANE)


def _rmsnorm_cast_kernel(x_ref, g_ref, o_ref):
    x = x_ref[...]
    o_ref[...] = ((x * _rms_scale(x)) * g_ref[...]).astype(o_ref.dtype)


def _rmsnorm_cast(x, g):
    t, d = x.shape
    bt = _tile(t, ELEMENTWISE_ROWS, SUBLANE)
    return pl.pallas_call(
        _rmsnorm_cast_kernel,
        grid=(t // bt,),
        in_specs=[pl.BlockSpec((bt, d), lambda i: (i, 0)), pl.BlockSpec((1, d), lambda i: (0, 0))],
        out_specs=pl.BlockSpec((bt, d), lambda i: (i, 0)),
        out_shape=jax.ShapeDtypeStruct((t, d), BF16),
        compiler_params=_params(("arbitrary",)),
        name="rmsnorm_cast",
    )(x, g.reshape(1, d))


def _in_proj_kernel(x_ref, w_ref, o_ref, *, act):
    acc = jnp.dot(x_ref[...], w_ref[...], preferred_element_type=F32)
    for c in range(o_ref.shape[0]):
        o_ref[c] = act(acc[:, _lanes(c)])


def _in_proj_round_kernel(x_ref, w_in, o_ref, w_out, w_buf, *, act, slab_out):
    jj, i = pl.program_id(0), pl.program_id(1)
    ck = w_in.shape[0]
    slot = (jj + 1) % 2
    bm = o_ref.shape[-2]
    n_slabs = w_in.shape[1] // LANE
    row_split = 2 if bm % (4 * SUBLANE) == 0 else 1
    hm = bm // row_split
    per_strip = min(n_slabs, MXU_WIDTH // LANE)

    def round_chunk():
        chunk = w_in[...].astype(BF16)
        w_buf[jj % 2, pl.ds(pl.multiple_of(i * ck, ck), ck), :] = chunk
        w_out[...] = chunk

    pl.when(jj == 0)(round_chunk)

    @pl.when(jj > 0)
    def _():
        round_chunk()
        for h in range(row_split):
            x = x_ref[h * hm:(h + 1) * hm, :]
            for c0 in range(0, n_slabs, per_strip):
                c1 = min(c0 + per_strip, n_slabs)
                acc = jnp.dot(x, w_buf[slot, :, c0 * LANE:c1 * LANE], preferred_element_type=F32)
                for c in range(c0, c1):
                    val = act(acc[:, _lanes(c - c0)])
                    if slab_out:
                        o_ref[c, h * hm:(h + 1) * hm, :] = val
                    else:
                        o_ref[h * hm:(h + 1) * hm, _lanes(c)] = val


def _in_proj_round(h, w, col0, n_cols, act, name, slab_out=True):
    t, d = h.shape
    bn = _tile(math.gcd(n_cols, col0) if col0 else n_cols, IN_PROJ_ROUND_TILE, LANE)
    bm = _tile(t, MATMUL_TILE, SUBLANE)
    ni, nj = t // bm, n_cols // bn
    assert d % ni == 0
    ck = d // ni

    def fin(jj, i):
        return jnp.maximum(jj - 1, 0), jnp.where(jj == 0, 0, i)

    def chunk(jj, i):
        return jnp.where(jj == nj, ni - 1, i), jnp.minimum(jj, nj - 1)

    if slab_out:
        out_spec = pl.BlockSpec((bn // LANE, bm, LANE), lambda jj, i: fin(jj, i) + (0,))
        out_shape = jax.ShapeDtypeStruct((n_cols // LANE, t, LANE), F32)
    else:
        out_spec = pl.BlockSpec((bm, bn), lambda jj, i: fin(jj, i)[::-1])
        out_shape = jax.ShapeDtypeStruct((t, n_cols), F32)
    return pl.pallas_call(
        functools.partial(_in_proj_round_kernel, act=act, slab_out=slab_out),
        grid=(nj + 1, ni),
        in_specs=[pl.BlockSpec((bm, d), lambda jj, i: (fin(jj, i)[1], 0)),
                  pl.BlockSpec((ck, bn), lambda jj, i: (chunk(jj, i)[0], col0 // bn + chunk(jj, i)[1]))],
        out_specs=[out_spec, pl.BlockSpec((ck, bn), chunk)],
        out_shape=[out_shape, jax.ShapeDtypeStruct((d, n_cols), BF16)],
        scratch_shapes=[pltpu.VMEM((2, d, bn), BF16)],
        compiler_params=_params(("arbitrary", "arbitrary")),
        name=name,
    )(h, w)


def _in_proj(h, w, bn, col0, n_cols, act, name):
    t, d = h.shape
    bm = _tile(t, MATMUL_TILE, SUBLANE)
    return pl.pallas_call(
        functools.partial(_in_proj_kernel, act=act),
        grid=(t // bm, n_cols // bn),
        in_specs=[pl.BlockSpec((bm, d), lambda i, j: (i, 0)),
                  pl.BlockSpec((d, bn), lambda i, j: (0, col0 // bn + j))],
        out_specs=pl.BlockSpec((bn // LANE, bm, LANE), lambda i, j: (j, i, 0)),
        out_shape=jax.ShapeDtypeStruct((n_cols // LANE, t, LANE), F32),
        compiler_params=_params(("arbitrary", "arbitrary")),
        name=name,
    )(h, w)


def _sgu_kernel(u_ref, v_ref, lg_ref, lb_ref, w_ref, b_ref, o_ref, vs_ref, *, n_groups):
    c = w_ref.shape[1]
    n_feat = n_groups * GROUP_DIM_A
    for r0 in range(0, o_ref.shape[0], c):
        vp = _gelu(v_ref[:, r0:r0 + c, :])
        mu = jnp.sum(jnp.sum(vp, axis=0), axis=-1, keepdims=True) / n_feat
        vc = vp - mu
        var = jnp.sum(jnp.sum(vc * vc, axis=0), axis=-1, keepdims=True) / n_feat
        v = vc * lax.rsqrt(var + EPS) * lg_ref[...] + lb_ref[...]
        for g in range(n_groups):
            if r0 + c == o_ref.shape[0]:
                vs_ref[:, _lanes(g)] = v[g]
            mixed = jnp.dot(w_ref[g], v[g].astype(BF16), preferred_element_type=F32) + b_ref[g]
            o_ref[r0:r0 + c, _lanes(g)] = (_gelu(u_ref[g, r0:r0 + c, :]) * mixed).astype(o_ref.dtype)


def _sgu(z_sl, d_a, ln_g, ln_b, w_mix, b_mix):
    t = z_sl.shape[1]
    n_groups, c, _ = w_mix.shape
    rows = _tile(t, SGU_ROWS_PER_STEP, c)
    slab = pl.BlockSpec((n_groups, 1, LANE), lambda i: (0, 0, 0))
    return pl.pallas_call(
        functools.partial(_sgu_kernel, n_groups=n_groups),
        grid=(t // rows,),
        in_specs=[
            pl.BlockSpec((n_groups, rows, LANE), lambda i: (0, i, 0)),
            pl.BlockSpec((n_groups, rows, LANE), lambda i: (1, i, 0)),
            slab, slab,
            pl.BlockSpec((n_groups, c, c), lambda i: (0, 0, 0)),
            pl.BlockSpec((n_groups, c, LANE), lambda i: (0, 0, 0)),
        ],
        out_specs=[pl.BlockSpec((rows, d_a), lambda i: (i, 0)), pl.BlockSpec((c, d_a), lambda i: (0, 0))],
        out_shape=[jax.ShapeDtypeStruct((t, d_a), BF16), jax.ShapeDtypeStruct((c, d_a), F32)],
        compiler_params=_params(("arbitrary",)),
        name="sgu",
    )(z_sl, z_sl, ln_g.reshape(n_groups, 1, LANE), ln_b.reshape(n_groups, 1, LANE), w_mix, b_mix)


def _t5_bucket(dist, n_buckets):
    n = np.asarray(dist, np.int32)
    safe = np.maximum(n, 1).astype(np.float32)
    large = MAX_EXACT + (np.log(safe / MAX_EXACT) / np.log(np.float32(MAX_DISTANCE / MAX_EXACT))
                         * (n_buckets - MAX_EXACT)).astype(np.int32)
    large = np.minimum(large, n_buckets - 1)
    return np.where(n < MAX_EXACT, n, large).astype(np.int32)


def _bias_lookup(bias_tab, bucket):
    n_buckets = bias_tab.shape[0]
    flat = np.asarray(bucket).reshape(-1)
    onehot = (jnp.asarray(flat)[None, :] == jnp.arange(n_buckets)[:, None]).astype(F32)
    out = jnp.dot(bias_tab.T, onehot, precision=lax.Precision.HIGHEST)
    return out.reshape((bias_tab.shape[1],) + tuple(np.asarray(bucket).shape))


def _merge_by_lse(outs, lses):
    m = functools.reduce(jnp.maximum, lses)
    w = [jnp.exp(l - m) for l in lses]
    num = functools.reduce(lambda x, y: x + y, [wi * oi for wi, oi in zip(w, outs)])
    return num / functools.reduce(lambda x, y: x + y, w)


def _attn_prompt_kernel(*refs, hb, dil, n_other):
    q_ref, kc_ref, vc_ref, bias_ref, band_ref = refs[:5]
    others = refs[5:5 + 2 * n_other]
    n_out = 1 if n_other else 2
    outs = refs[5 + 2 * n_other:5 + 2 * n_other + n_out]
    kp_ref, vp_ref = refs[5 + 2 * n_other + n_out:][:2]
    o_acc, lse_acc = refs[5 + 2 * n_other + n_out + 2:] if n_other else outs
    b = pl.program_id(1)

    @pl.when(b == 0)
    def _():
        kp_ref[...] = jnp.zeros(kp_ref.shape, F32)
        vp_ref[...] = jnp.zeros(vp_ref.shape, F32)

    span = STEPS * dil
    n_pb = q_ref.shape[1] // span
    col = lax.broadcasted_iota(jnp.int32, (STEPS, 2 * STEPS), 1)
    band = band_ref[...] > 0.5
    first = band & ((b > 0) | (col >= STEPS))
    for hh in range(hb):
        for pb in range(n_pb):
            for r in range(dil):
                def rows(block):
                    return pl.ds(block * span + r, STEPS, stride=dil) if dil > 1 else pl.ds(block * span, STEPS)
                k_prev = kp_ref[hh, rows(0), :] if pb == 0 else kc_ref[hh, rows(pb - 1), :]
                v_prev = vp_ref[hh, rows(0), :] if pb == 0 else vc_ref[hh, rows(pb - 1), :]
                q = q_ref[hh, rows(pb), :].astype(BF16)
                kk = jnp.concatenate([k_prev, kc_ref[hh, rows(pb), :]], axis=0).astype(BF16)
                vv = jnp.concatenate([v_prev, vc_ref[hh, rows(pb), :]], axis=0).astype(BF16)
                s = lax.dot_general(q, kk, (((1,), (1,)), ((), ())), preferred_element_type=F32) * ATTN_SCALE
                s = jnp.where(first if pb == 0 else band, s + bias_ref[hh], NEG)
                m = jnp.max(s, axis=-1, keepdims=True)
                e = jnp.exp(s - m)
                den = jnp.sum(e, axis=-1, keepdims=True)
                o_acc[hh, rows(pb), :] = jnp.dot(e.astype(BF16), vv, preferred_element_type=F32) / den
                lse_acc[hh, rows(pb), :] = jnp.broadcast_to(m + jnp.log(den), (STEPS, HEAD_DIM))
    kp_ref[...] = kc_ref[:, (n_pb - 1) * span:, :]
    vp_ref[...] = vc_ref[:, (n_pb - 1) * span:, :]
    if n_other:
        for hh in range(hb):
            merged = _merge_by_lse([r[hh] for r in others[:n_other]] + [o_acc[hh]],
                                   [r[hh] for r in others[n_other:]] + [lse_acc[hh]])
            outs[0][:, _lanes(hh)] = merged.astype(outs[0].dtype)


def _attn_prompt(z_sl, hpg, q_slab, k_slab, v_slab, bias_tab, dil, merge_with=None):
    t = z_sl.shape[1]
    span = STEPS * dil
    assert t % span == 0
    hb = min(hpg, max(1, ATTN_UNITS_PER_STEP // dil))
    assert hpg % hb == 0 and q_slab % hb == 0 and k_slab % hb == 0 and v_slab % hb == 0
    n_pb = math.gcd(t // span, max(1, ATTN_UNITS_PER_STEP // (hb * dil)))
    rows = span * n_pb
    nb = t // rows

    p_idx = np.arange(STEPS)[:, None]
    c_idx = np.arange(2 * STEPS)[None, :]
    steps = p_idx + STEPS - c_idx
    band = ((steps >= 0) & (steps <= STEPS)).astype(np.float32)
    bias = _bias_lookup(bias_tab, _t5_bucket(np.clip(steps, 0, STEPS) * dil, bias_tab.shape[0]))

    def cur(slab):
        return pl.BlockSpec((hb, rows, LANE), lambda hi, b: (slab // hb + hi, b, 0))

    head_major = pl.BlockSpec((hb, rows, LANE), lambda hi, b: (hi, b, 0))
    prev_block = pltpu.VMEM((hb, span, LANE), F32)
    block = pltpu.VMEM((hb, rows, LANE), F32)
    others = [] if merge_with is None else list(merge_with[0]) + list(merge_with[1])
    if merge_with is None:
        out_specs = [head_major, head_major]
        out_shape = [jax.ShapeDtypeStruct((hpg, t, LANE), F32)] * 2
        scratch = [prev_block, prev_block]
    else:
        out_specs = [pl.BlockSpec((rows, hb * LANE), lambda hi, b: (b, hi))]
        out_shape = [jax.ShapeDtypeStruct((t, hpg * LANE), BF16)]
        scratch = [prev_block, prev_block, block, block]
    res = pl.pallas_call(
        functools.partial(_attn_prompt_kernel, hb=hb, dil=dil, n_other=len(others) // 2),
        grid=(hpg // hb, nb),
        in_specs=[cur(q_slab), cur(k_slab), cur(v_slab),
                  pl.BlockSpec((hb, STEPS, 2 * STEPS), lambda hi, b: (hi, 0, 0)),
                  pl.BlockSpec((STEPS, 2 * STEPS), lambda hi, b: (0, 0))] + [head_major] * len(others),
        out_specs=out_specs,
        out_shape=out_shape,
        scratch_shapes=scratch,
        compiler_params=_params(("arbitrary", "arbitrary")),
        name=f"attn_prompt_d{dil}",
    )(z_sl, z_sl, z_sl, bias, jnp.asarray(band), *others)
    return res[0] if merge_with is not None else res


def _attn_sample_kernel(q_ref, kn_ref, vn_ref, ck_ref, cv_ref, bc_ref, mc_ref, bnew_ref, mnew_ref,
                        o_ref, lse_ref, *, hpg, s_len, grouped):
    n_keys = mc_ref.shape[1]
    n_new = hpg * s_len
    valid_c = mc_ref[...] > 0.5

    def cache_head(ref, h):
        if grouped:
            return ref[0, :, pl.ds(h, s_len, stride=hpg), :].reshape(n_keys, HEAD_DIM).astype(BF16)
        return ref[0, pl.ds(h, n_keys, stride=hpg) if hpg > 1 else pl.ds(0, n_keys), :].astype(BF16)

    q_all = q_ref[...].reshape(n_new, HEAD_DIM).astype(BF16)
    s_new = lax.dot_general(q_all, kn_ref[...].reshape(n_new, HEAD_DIM).astype(BF16), (((1,), (1,)), ((), ())),
                            preferred_element_type=F32) * ATTN_SCALE
    key_head = lax.broadcasted_iota(jnp.int32, (s_len, n_new), 1) // s_len
    new_ok = mnew_ref[...] > 0.5

    partial, e_new = [], []
    for h in range(hpg):
        q = q_all[h * s_len:(h + 1) * s_len, :]
        sc = lax.dot_general(q, cache_head(ck_ref, h), (((1,), (1,)), ((), ())),
                             preferred_element_type=F32) * ATTN_SCALE
        sc = jnp.where(valid_c, sc + bc_ref[h], NEG)
        sn = jnp.where(new_ok & (key_head == h), s_new[h * s_len:(h + 1) * s_len, :] + bnew_ref[h], NEG)
        m = jnp.maximum(jnp.max(sc, axis=-1, keepdims=True), jnp.max(sn, axis=-1, keepdims=True))
        ec = jnp.exp(sc - m)
        en = jnp.exp(sn - m)
        den = jnp.sum(ec, axis=-1, keepdims=True) + jnp.sum(en, axis=-1, keepdims=True)
        acc = jnp.dot(ec.astype(BF16), cache_head(cv_ref, h), preferred_element_type=F32)
        partial.append((acc, den, m))
        e_new.append(en)
    o_new = jnp.dot(jnp.concatenate(e_new, axis=0).astype(BF16),
                    vn_ref[...].reshape(n_new, HEAD_DIM).astype(BF16), preferred_element_type=F32)
    for h, (acc, den, m) in enumerate(partial):
        o_ref[h] = (acc + o_new[h * s_len:(h + 1) * s_len, :]) / den
        lse_ref[h] = jnp.broadcast_to(m + jnp.log(den), (s_len, HEAD_DIM))


def _attn_sample(z_sl, s_len, hpg, q_slab, k_slab, v_slab, cache_k, cache_v, bias_tab, dil):
    t = z_sl.shape[1]
    n_seq = t // s_len
    lc = cache_k.shape[1]
    assert q_slab % hpg == 0 and k_slab % hpg == 0 and v_slab % hpg == 0

    j = np.arange(STEPS + 1)
    idx = lc + np.arange(s_len)[:, None] - j[None, :] * dil
    assert idx.min() >= 0
    bucket = _t5_bucket(j * dil, bias_tab.shape[0])
    mask = np.zeros((s_len, lc + s_len), np.float32)
    bsel = np.zeros((s_len, lc + s_len), np.int32)
    for s in range(s_len):
        mask[s, idx[s]] = 1.0
        bsel[s, idx[s]] = bucket

    grouped = dil > s_len and lc % dil == 0 and s_len == SUBLANE
    if grouped:
        pos = (np.arange(lc // dil)[:, None] * dil + np.arange(s_len)[None, :]).reshape(-1)
        assert mask[:, :lc].sum() == mask[:, pos].sum()
        ck = cache_k.reshape(n_seq, lc // dil, dil * hpg, HEAD_DIM)
        cv = cache_v.reshape(n_seq, lc // dil, dil * hpg, HEAD_DIM)
        cache_spec = pl.BlockSpec((1, lc // dil, s_len * hpg, LANE), lambda b: (b, 0, 0, 0))
    else:
        pos = np.arange(lc)
        ck = cache_k.reshape(n_seq, lc * hpg, HEAD_DIM)
        cv = cache_v.reshape(n_seq, lc * hpg, HEAD_DIM)
        cache_spec = pl.BlockSpec((1, lc * hpg, LANE), lambda b: (b, 0, 0))
    n_keys = len(pos)
    bias_c = _bias_lookup(bias_tab, bsel[:, pos])
    bias_new = jnp.tile(_bias_lookup(bias_tab, bsel[:, lc:]), (1, 1, hpg))
    mask_c = jnp.asarray(mask[:, pos])
    mask_new = jnp.asarray(np.tile(mask[:, lc:], (1, hpg)))

    out_spec = pl.BlockSpec((hpg, s_len, LANE), lambda b: (0, b, 0))
    return pl.pallas_call(
        functools.partial(_attn_sample_kernel, hpg=hpg, s_len=s_len, grouped=grouped),
        grid=(n_seq,),
        in_specs=[pl.BlockSpec((hpg, s_len, LANE), lambda b: (q_slab // hpg, b, 0)),
                  pl.BlockSpec((hpg, s_len, LANE), lambda b: (k_slab // hpg, b, 0)),
                  pl.BlockSpec((hpg, s_len, LANE), lambda b: (v_slab // hpg, b, 0)),
                  cache_spec, cache_spec,
                  pl.BlockSpec((hpg, s_len, n_keys), lambda b: (0, 0, 0)),
                  pl.BlockSpec((s_len, n_keys), lambda b: (0, 0)),
                  pl.BlockSpec((hpg, s_len, hpg * s_len), lambda b: (0, 0, 0)),
                  pl.BlockSpec((s_len, hpg * s_len), lambda b: (0, 0))],
        out_specs=[out_spec, out_spec],
        out_shape=[jax.ShapeDtypeStruct((hpg, t, LANE), F32)] * 2,
        compiler_params=_params(("arbitrary",)),
        name=f"attn_sample_d{dil}",
    )(z_sl, z_sl, z_sl, ck, cv, bias_c, mask_c, bias_new, mask_new)


def _merge_kernel(o1, o2, o3, l1, l2, l3, out_ref):
    merged = _merge_by_lse([o1[...], o2[...], o3[...]], [l1[...], l2[...], l3[...]])
    for h in range(merged.shape[0]):
        out_ref[:, _lanes(h)] = merged[h].astype(out_ref.dtype)


def _merge(outs, lses):
    hpg, t, _ = outs[0].shape
    bt = _tile(t, ELEMENTWISE_ROWS, SUBLANE)
    spec = pl.BlockSpec((hpg, bt, LANE), lambda i: (0, i, 0))
    return pl.pallas_call(
        _merge_kernel,
        grid=(t // bt,),
        in_specs=[spec] * 6,
        out_specs=pl.BlockSpec((bt, hpg * LANE), lambda i: (i, 0)),
        out_shape=jax.ShapeDtypeStruct((t, hpg * LANE), BF16),
        compiler_params=_params(("arbitrary",)),
        name="merge_groups",
    )(*outs, *lses)


def _gated_proj_kernel(a_ref, b_ref, wa_ref, wb_ref, ga_ref, gb_ref, o_ref):
    n_slabs = ga_ref.shape[0]
    per_strip = min(n_slabs, MXU_WIDTH // LANE)
    for c0 in range(0, n_slabs, per_strip):
        c1 = min(c0 + per_strip, n_slabs)
        cols = slice(c0 * LANE, c1 * LANE)
        pa = jnp.dot(a_ref[...], wa_ref[:, cols], preferred_element_type=F32)
        pb = jnp.dot(b_ref[...], wb_ref[:, cols], preferred_element_type=F32)
        for c in range(c0, c1):
            gated = _sigmoid(ga_ref[c]) * pa[:, _lanes(c - c0)] + _sigmoid(gb_ref[c]) * pb[:, _lanes(c - c0)]
            o_ref[:, _lanes(c)] = gated.astype(o_ref.dtype)


def _gated_proj(o_a, o_b, w_a, w_b, z_sl, bn, gate_a_blk, gate_b_blk):
    t, d_a = o_a.shape
    d_b = o_b.shape[1]
    d = w_a.shape[1]
    bm = _tile(t, MATMUL_TILE, SUBLANE)
    return pl.pallas_call(
        _gated_proj_kernel,
        grid=(t // bm, d // bn),
        in_specs=[pl.BlockSpec((bm, d_a), lambda i, j: (i, 0)),
                  pl.BlockSpec((bm, d_b), lambda i, j: (i, 0)),
                  pl.BlockSpec((d_a, bn), lambda i, j: (0, j)),
                  pl.BlockSpec((d_b, bn), lambda i, j: (0, j)),
                  pl.BlockSpec((bn // LANE, bm, LANE), lambda i, j: (gate_a_blk + j, i, 0)),
                  pl.BlockSpec((bn // LANE, bm, LANE), lambda i, j: (gate_b_blk + j, i, 0))],
        out_specs=pl.BlockSpec((bm, bn), lambda i, j: (i, j)),
        out_shape=jax.ShapeDtypeStruct((t, d), BF16),
        compiler_params=_params(("arbitrary", "arbitrary")),
        name="gated_proj",
    )(o_a, o_b, w_a, w_b, z_sl, z_sl)


def _matmul_kernel(x_ref, w_ref, o_ref, *, nk):
    if nk == 1:
        o_ref[...] = jnp.dot(x_ref[...], w_ref[...], preferred_element_type=F32)
    else:
        @pl.when(pl.program_id(2) == 0)
        def _():
            o_ref[...] = jnp.zeros(o_ref.shape, o_ref.dtype)

        o_ref[...] += jnp.dot(x_ref[...], w_ref[...], preferred_element_type=F32)


def _matmul(x, w, name, k_tile=None):
    t, kd = x.shape
    n = w.shape[1]
    bm = _tile(t, MATMUL_TILE, SUBLANE)
    bn = _tile(n, MATMUL_TILE, LANE)
    bk = kd if k_tile is None else k_tile
    nk = kd // bk
    return pl.pallas_call(
        functools.partial(_matmul_kernel, nk=nk),
        grid=(t // bm, n // bn, nk),
        in_specs=[pl.BlockSpec((bm, bk), lambda i, j, k: (i, k)),
                  pl.BlockSpec((bk, bn), lambda i, j, k: (k, j))],
        out_specs=pl.BlockSpec((bm, bn), lambda i, j, k: (i, j)),
        out_shape=jax.ShapeDtypeStruct((t, n), F32),
        compiler_params=_params(("arbitrary", "arbitrary", "arbitrary")),
        name=name,
    )(x, w)


def _post_mix_kernel(x_ref, y_ref, g1_ref, g2_ref, x1_ref, h2_ref):
    y = y_ref[...]
    x1 = x_ref[...] + (y * _rms_scale(y)) * g1_ref[...]
    x1_ref[...] = x1
    h2_ref[...] = ((x1 * _rms_scale(x1)) * g2_ref[...]).astype(h2_ref.dtype)


def _post_mix(x, y, g_post, g_pre):
    t, d = x.shape
    bt = _tile(t, ELEMENTWISE_ROWS, SUBLANE)
    row = pl.BlockSpec((bt, d), lambda i: (i, 0))
    vec = pl.BlockSpec((1, d), lambda i: (0, 0))
    return pl.pallas_call(
        _post_mix_kernel,
        grid=(t // bt,),
        in_specs=[row, row, vec, vec],
        out_specs=[row, row],
        out_shape=[jax.ShapeDtypeStruct((t, d), F32), jax.ShapeDtypeStruct((t, d), BF16)],
        compiler_params=_params(("arbitrary",)),
        name="post_mix",
    )(x, y, g_post.reshape(1, d), g_pre.reshape(1, d))


def _residual_norm_kernel(x_ref, y_ref, g_ref, o_ref):
    y = y_ref[...]
    o_ref[...] = x_ref[...] + (y * _rms_scale(y)) * g_ref[...]


def _residual_norm(x, y, g):
    t, d = x.shape
    bt = _tile(t, ELEMENTWISE_ROWS, SUBLANE)
    row = pl.BlockSpec((bt, d), lambda i: (i, 0))
    return pl.pallas_call(
        _residual_norm_kernel,
        grid=(t // bt,),
        in_specs=[row, row, pl.BlockSpec((1, d), lambda i: (0, 0))],
        out_specs=row,
        out_shape=jax.ShapeDtypeStruct((t, d), F32),
        compiler_params=_params(("arbitrary",)),
        name="residual_norm",
    )(x, y, g.reshape(1, d))


def _ffn_up_kernel(*refs, conv_w, s_len, d_ff):
    if s_len is None:
        (x_ref, wg_in, wu_in, wd_in, cw_ref, cb_ref, out_ref, tail_ref, wg_out, wu_out, wd_out,
         wg_buf, wu_buf, a_buf, u_buf, o_buf) = refs
        hist_refs = None
        jj, i = pl.program_id(0), pl.program_id(1)
        ck = wg_in.shape[0]

        def round_chunks():
            rows_c = pl.ds(pl.multiple_of(i * ck, ck), ck)
            for src, buf, dst in ((wg_in, wg_buf, wg_out), (wu_in, wu_buf, wu_out)):
                chunk = src[...].astype(BF16)
                buf[jj % 2, rows_c, :] = chunk
                dst[...] = chunk

        def round_down_rows():
            rc = wd_in.shape[0]
            row = ((jj - 1) * pl.num_programs(1) + i) * rc + lax.broadcasted_iota(jnp.int32, wd_in.shape, 0)
            wd_out[...] = jnp.where(row < d_ff, wd_in[...], 0.0).astype(BF16)

        slot = (jj + 1) % 2
        wg = lambda cols: wg_buf[slot, :, cols]
        wu = lambda cols: wu_buf[slot, :, cols]
        col_blk = jj - 1
        row_split = 2 if out_ref.shape[0] % (4 * SUBLANE) == 0 else 1
    else:
        x_ref, wg_ref, wu_ref, cw_ref, cb_ref, h1_ref, h2_ref, out_ref, tail_ref, a_buf, u_buf, o_buf = refs
        hist_refs = {1: h1_ref, 2: h2_ref}
        i = pl.program_id(1)
        wg = lambda cols: wg_ref[:, cols]
        wu = lambda cols: wu_ref[:, cols]
        col_blk = pl.program_id(0)
        row_split = 1
    bm, bn = out_ref.shape
    hm = bm // row_split
    half = hm // 2
    top = SUBLANE
    n_slabs = bn // LANE
    per_strip = min(n_slabs, MXU_WIDTH // LANE)

    def units():
        @pl.when(i == 0)
        def _():
            a_buf[:, 0:top, :] = jnp.zeros((n_slabs, top, LANE), F32)

        lane = lax.broadcasted_iota(jnp.int32, (1, LANE), 1)
        for h in range(row_split):
            r0 = h * hm
            x = x_ref[r0:r0 + hm, :]
            for c0 in range(0, n_slabs, per_strip):
                c1 = min(c0 + per_strip, n_slabs)
                cols = slice(c0 * LANE, c1 * LANE)
                a = jnp.dot(x, wg(cols), preferred_element_type=F32)
                u = jnp.dot(x, wu(cols), preferred_element_type=F32)
                if s_len is None and h == 0 and c0 == 0:
                    round_chunks()
                    round_down_rows()
                if s_len is not None:
                    tail_ref[r0:r0 + hm, cols] = a
                elif h == row_split - 1:
                    tail_ref[:, cols] = a[hm - SUBLANE:, :]
                for c in range(c0, c1):
                    a_buf[c, top + r0:top + r0 + hm, :] = a[:, _lanes(c - c0)]
                    u_buf[c, r0:r0 + hm, :] = u[:, _lanes(c - c0)]
                    in_range = (col_blk * bn + c * LANE + lane) < d_ff
                    for e in range(2):
                        taps = [a_buf[c, pl.ds(top + r0 + e - lag, half, stride=2), :] for lag in range(conv_w)]
                        if hist_refs is not None:
                            s = 2 * (lax.broadcasted_iota(jnp.int32, (half, LANE), 0) % (s_len // 2)) + e
                            for lag in range(1, conv_w):
                                hist = hist_refs[lag][c, pl.ds(r0 + e, half, stride=2), :]
                                taps[lag] = jnp.where(s < lag, hist, taps[lag])
                        acc = cw_ref[0:1, _lanes(c)] * taps[conv_w - 1]
                        for k in range(1, conv_w):
                            acc = acc + cw_ref[k:k + 1, _lanes(c)] * taps[conv_w - 1 - k]
                        acc = cb_ref[:, _lanes(c)] + acc
                        val = _gelu(acc) * u_buf[c, pl.ds(r0 + e, half, stride=2), :]
                        o_buf[c, pl.ds(r0 + e, half, stride=2), :] = jnp.where(in_range, val, 0.0)
                    out_ref[r0:r0 + hm, _lanes(c)] = o_buf[c, r0:r0 + hm, :].astype(out_ref.dtype)
        if s_len is None:
            a_buf[:, 0:top, :] = a_buf[:, bm:bm + top, :]

    if s_len is None:
        pl.when(jj == 0)(round_chunks)
        pl.when(jj > 0)(units)
    else:
        units()


def _ffn_up_prompt(h, w_gate, w_up, w_down, conv_w, conv_b, ffp):
    t, d = h.shape
    d_ff = w_gate.shape[1]
    cw = conv_w.shape[0]
    assert cw == 3 and d_ff % LANE == 0
    bn = FFN_TILE
    assert ffp % bn == 0 and ffp - d_ff < bn
    bm = _tile(t, MATMUL_TILE, 4 * SUBLANE)
    ni, nj = t // bm, ffp // bn
    assert d % ni == 0
    ck = d // ni
    n_slabs = bn // LANE

    def fin_i(jj, i):
        return jnp.where(jj == 0, 0, i)

    def fin_j(jj):
        return jnp.maximum(jj - 1, 0)

    def chunk(jj, i):
        return jnp.where(jj == nj, ni - 1, i), jnp.minimum(jj, nj - 1)

    n_out = w_down.shape[1]
    assert ffp % (nj * ni) == 0
    rc = ffp // (nj * ni)
    assert rc % (2 * SUBLANE) == 0

    def down_blk(jj, i):
        return fin_j(jj) * ni + fin_i(jj, i)

    last_src = -(-d_ff // rc) - 1
    w_in_spec = pl.BlockSpec((ck, bn), chunk)
    wbuf = pltpu.VMEM((2, d, bn), BF16)
    return pl.pallas_call(
        functools.partial(_ffn_up_kernel, conv_w=cw, s_len=None, d_ff=d_ff),
        grid=(nj + 1, ni),
        in_specs=[pl.BlockSpec((bm, d), lambda jj, i: (fin_i(jj, i), 0)), w_in_spec, w_in_spec,
                  pl.BlockSpec((rc, n_out), lambda jj, i: (jnp.minimum(down_blk(jj, i), last_src), 0)),
                  pl.BlockSpec((cw, bn), lambda jj, i: (0, fin_j(jj))),
                  pl.BlockSpec((1, bn), lambda jj, i: (0, fin_j(jj)))],
        out_specs=[pl.BlockSpec((bm, bn), lambda jj, i: (fin_i(jj, i), fin_j(jj))),
                   pl.BlockSpec((SUBLANE, bn), lambda jj, i: (0, fin_j(jj))),
                   w_in_spec, w_in_spec,
                   pl.BlockSpec((rc, n_out), lambda jj, i: (down_blk(jj, i), 0))],
        out_shape=[jax.ShapeDtypeStruct((t, ffp), BF16), jax.ShapeDtypeStruct((SUBLANE, d_ff), F32),
                   jax.ShapeDtypeStruct((d, d_ff), BF16), jax.ShapeDtypeStruct((d, d_ff), BF16),
                   jax.ShapeDtypeStruct((ffp, n_out), BF16)],
        scratch_shapes=[wbuf, wbuf,
                        pltpu.VMEM((n_slabs, bm + SUBLANE, LANE), F32), pltpu.VMEM((n_slabs, bm, LANE), F32),
                        pltpu.VMEM((n_slabs, bm, LANE), F32)],
        compiler_params=_params(("arbitrary", "arbitrary")),
        name="ffn_up",
    )(h, w_gate, w_up, w_down, conv_w, conv_b.reshape(1, d_ff))


def _ffn_up_sample(h, w_gate, w_up, conv_w, conv_b, ffp, hist, s_len):
    t, d = h.shape
    d_ff = w_gate.shape[1]
    cw = conv_w.shape[0]
    assert cw == 3 and d_ff % LANE == 0 and s_len % 2 == 0
    bn = _tile(ffp, MATMUL_TILE, LANE)
    assert ffp - d_ff < bn
    n_slabs = bn // LANE
    w_spec = pl.BlockSpec((d, bn), lambda j, i: (0, j))
    out_spec = pl.BlockSpec((t, bn), lambda j, i: (i, j))

    def slabs(rows):
        n_seq, r, _ = rows.shape
        sl = jnp.transpose(rows.reshape(n_seq, r, d_ff // LANE, LANE), (2, 0, 1, 3))
        return jnp.pad(sl, ((0, 0), (0, 0), (0, s_len - r), (0, 0))).reshape(d_ff // LANE, t, LANE)
    h1 = slabs(hist[:, 1:2])
    h2 = slabs(hist)
    hist_spec = pl.BlockSpec((n_slabs, t, LANE), lambda j, i: (j, i, 0))
    return pl.pallas_call(
        functools.partial(_ffn_up_kernel, conv_w=cw, s_len=s_len, d_ff=d_ff),
        grid=(ffp // bn, 1),
        in_specs=[pl.BlockSpec((t, d), lambda j, i: (i, 0)), w_spec, w_spec,
                  pl.BlockSpec((cw, bn), lambda j, i: (0, j)), pl.BlockSpec((1, bn), lambda j, i: (0, j)),
                  hist_spec, hist_spec],
        out_specs=[out_spec, out_spec],
        out_shape=[jax.ShapeDtypeStruct((t, ffp), BF16), jax.ShapeDtypeStruct((t, d_ff), F32)],
        scratch_shapes=[pltpu.VMEM((n_slabs, t + SUBLANE, LANE), F32), pltpu.VMEM((n_slabs, t, LANE), F32),
                        pltpu.VMEM((n_slabs, t, LANE), F32)],
        compiler_params=_params(("arbitrary", "arbitrary")),
        name="ffn_up_decode",
    )(h, w_gate, w_up, conv_w, conv_b.reshape(1, d_ff), h1, h2)


def _kv_state_rows_kernel(k_ref, v_ref, ko_ref, vo_ref, *, hpg):
    n = k_ref.shape[1]
    for src, dst in ((k_ref, ko_ref), (v_ref, vo_ref)):
        for h in range(hpg):
            dst[pl.ds(h, n, stride=hpg) if hpg > 1 else pl.ds(0, n), :] = src[h]


def _kv_state_rows(z_sl, k_slab, v_slab, hpg, row0, n_rows):
    br = _tile(n_rows, KV_STATE_ROWS, SUBLANE)
    assert row0 % br == 0 and k_slab % hpg == 0 and v_slab % hpg == 0
    out_spec = pl.BlockSpec((br * hpg, LANE), lambda i: (i, 0))
    return pl.pallas_call(
        functools.partial(_kv_state_rows_kernel, hpg=hpg),
        grid=(n_rows // br,),
        in_specs=[pl.BlockSpec((hpg, br, LANE), lambda i: (k_slab // hpg, row0 // br + i, 0)),
                  pl.BlockSpec((hpg, br, LANE), lambda i: (v_slab // hpg, row0 // br + i, 0))],
        out_specs=[out_spec, out_spec],
        out_shape=[jax.ShapeDtypeStruct((n_rows * hpg, LANE), F32)] * 2,
        compiler_params=_params(("arbitrary",)),
        name="kv_state_rows",
    )(z_sl, z_sl)


def _layer(x, p, s_len=None, caches=None, conv_hist=None):
    t, d = x.shape
    d_a = p["ln_g"].shape[0]
    hpg = p["hpg"]
    gw = hpg * HEAD_DIM
    bn = p["bn"]
    o1 = 2 * d_a
    d_qkv = 3 * gw

    h = _rmsnorm_cast(x, p["g_pre_mix"])
    w16 = dict(p.get("w16", {}))
    d_in = o1 + 3 * d_qkv + 2 * d
    if s_len is None:
        qkv_sl, w16["in_proj"] = _in_proj_round(h, p["w_in"], 0, d_in, lambda v: v, "in_proj")
    else:
        qkv_sl = _in_proj(h, w16["in_proj"], _tile(d_in, DECODE_COL_TILE, LANE), 0, d_in, lambda v: v,
                          "in_proj_decode")

    if s_len is None:
        w_mix, b_mix = p["w_mix_prompt"], p["b_mix_prompt"]
    else:
        w_mix, b_mix = p["w_mix_sample"], p["b_mix_sample"]
    o_a, v_state = _sgu(qkv_sl, d_a, p["ln_g"], p["ln_b"], w_mix, b_mix)

    outs, lses = [], []
    for gi, (_, dil) in enumerate(DILATION_GROUPS):
        q_slab = o1 // LANE + gi * hpg
        k_slab = (o1 + d_qkv) // LANE + gi * hpg
        v_slab = (o1 + 2 * d_qkv) // LANE + gi * hpg
        bias_tab = p["rel_bias"][:, gi * hpg:(gi + 1) * hpg]
        if s_len is not None:
            o, lse = _attn_sample(qkv_sl, s_len, hpg, q_slab, k_slab, v_slab, caches[2 * gi], caches[2 * gi + 1],
                                  bias_tab, dil)
        elif gi < len(DILATION_GROUPS) - 1:
            o, lse = _attn_prompt(qkv_sl, hpg, q_slab, k_slab, v_slab, bias_tab, dil)
        else:
            o_b = _attn_prompt(qkv_sl, hpg, q_slab, k_slab, v_slab, bias_tab, dil, merge_with=(outs, lses))
            break
        outs.append(o)
        lses.append(lse)
    if s_len is not None:
        o_b = _merge(outs, lses)

    gate_a_blk = (o1 + 3 * d_qkv) // bn
    merged = _gated_proj(o_a, o_b, p["w_proj_a"], p["w_proj_b"], qkv_sl, bn, gate_a_blk, gate_a_blk + d // bn)
    if s_len is None:
        y, w16["out_proj"] = _in_proj_round(merged, p["w_out"], 0, d, lambda v: v, "out_proj", slab_out=False)
    else:
        y = _matmul(merged, w16["out_proj"], "out_proj_decode")
    x1, h2 = _post_mix(x, y, p["g_post_mix"], p["g_pre_ffn"])

    if s_len is None:
        act, a_tail, w16["ffn_gate"], w16["ffn_up"], w16["ffn_down"] = _ffn_up_prompt(
            h2, p["w_gate"], p["w_up"], p["w_down"], p["conv_w"], p["conv_b"], p["ffp"])
    else:
        act, a_tail = _ffn_up_sample(h2, w16["ffn_gate"], w16["ffn_up"], p["conv_w"], p["conv_b"], p["ffp"],
                                     conv_hist, s_len)
    f = _matmul(act, w16["ffn_down"], "ffn_down", k_tile=p["ffn_k_tile"])
    y_out = _residual_norm(x1, f, p["g_post_ffn"])
    return y_out, qkv_sl, v_state, a_tail, w16


def kernel(x_prompt, x_sample, cache_k_g1, cache_v_g1, cache_k_g2, cache_v_g2, cache_k_g3, cache_v_g3, state_conv, g_pre_mix, w_in, sgu_ln_g, sgu_ln_b, w_spatial, b_spatial, rel_bias, w_proj_a, w_proj_b, w_out, g_post_mix, g_pre_ffn, w_gate, w_up, conv_w, conv_b, w_down, g_post_ffn):
    depth = w_in.shape[0]
    assert depth == 1
    n_prompt, seq, d = x_prompt.shape
    assert n_prompt == 1 and seq % CHUNK == 0
    n_seq, s_len, _ = x_sample.shape
    assert s_len == SUBLANE
    d_a = sgu_ln_g.shape[1]
    n_groups = w_spatial.shape[1]
    n_heads = rel_bias.shape[1]
    hpg = n_heads // len(DILATION_GROUPS)
    gw = hpg * HEAD_DIM
    d_ff = w_gate.shape[2]
    cw = conv_w.shape[1]
    bn = _tile(math.gcd(2 * d_a, gw), MATMUL_TILE, LANE)
    ffp = -(-d_ff // FFN_TILE) * FFN_TILE
    t_s = n_seq * s_len

    tri = np.tril(np.ones((CHUNK, CHUNK), np.float32))
    w_mix_prompt = (w_spatial[0] * tri).astype(BF16)
    b_mix_prompt = jnp.broadcast_to(b_spatial[0][:, :, None], (n_groups, CHUNK, LANE))
    w_small = w_spatial[0][:, :s_len, :s_len] * tri[:s_len, :s_len]
    same_seq = np.kron(np.eye(n_seq, dtype=np.float32), np.ones((s_len, s_len), np.float32))
    w_mix_sample = (jnp.tile(w_small, (1, n_seq, n_seq)) * same_seq).astype(BF16)
    b_mix_sample = jnp.broadcast_to(jnp.tile(b_spatial[0][:, :s_len], (1, n_seq))[:, :, None], (n_groups, t_s, LANE))

    p = dict(
        hpg=hpg, bn=bn, ffp=ffp,
        g_pre_mix=g_pre_mix[0], w_in=w_in[0], ln_g=sgu_ln_g[0], ln_b=sgu_ln_b[0],
        w_mix_prompt=w_mix_prompt, b_mix_prompt=b_mix_prompt,
        w_mix_sample=w_mix_sample, b_mix_sample=b_mix_sample,
        rel_bias=rel_bias,
        w_proj_a=w_proj_a[0].astype(BF16), w_proj_b=w_proj_b[0].astype(BF16), w_out=w_out[0],
        g_post_mix=g_post_mix[0], g_pre_ffn=g_pre_ffn[0],
        w_gate=w_gate[0], w_up=w_up[0],
        conv_w=conv_w[0], conv_b=conv_b[0],
        w_down=w_down[0],
        ffn_k_tile=_tile(ffp, FFN_DOWN_K_TILE, LANE),
        g_post_ffn=g_post_ffn[0],
    )

    caches = tuple(c[0] for c in (cache_k_g1, cache_v_g1, cache_k_g2, cache_v_g2, cache_k_g3, cache_v_g3))

    yp, zp, vp_state, ap_tail, w16 = _layer(x_prompt[0], p)
    ys, zs, vs_state, as_all, _ = _layer(x_sample.reshape(t_s, d), dict(p, w16=w16), s_len=s_len,
                                         caches=caches, conv_hist=state_conv[0])

    d_qkv = 3 * gw
    prompt_kv, sample_kv = [], []
    for gi, (win, _) in enumerate(DILATION_GROUPS):
        keep = min(win, seq)
        k_slab = (2 * d_a + d_qkv + gi * gw) // LANE
        v_slab = (2 * d_a + 2 * d_qkv + gi * gw) // LANE
        prompt_kv += [s.reshape(1, 1, keep, hpg, HEAD_DIM)
                      for s in _kv_state_rows(zp, k_slab, v_slab, hpg, seq - keep, keep)]
        sample_kv += [s.reshape(1, n_seq, s_len, hpg, HEAD_DIM)
                      for s in _kv_state_rows(zs, k_slab, v_slab, hpg, 0, t_s)]
    p_conv = ap_tail[SUBLANE - (cw - 1):].reshape(1, 1, cw - 1, d_ff)
    s_conv = as_all.reshape(n_seq, s_len, d_ff)[:, s_len - (cw - 1):].reshape(1, n_seq, cw - 1, d_ff)
    return (yp.reshape(1, seq, d), ys.reshape(n_seq, s_len, d),
            *prompt_kv, vp_state.reshape(1, 1, CHUNK, d_a), p_conv,
            *sample_kv, vs_state.reshape(1, n_seq, s_len, d_a), s_conv)
```

```python
import functools
import math

import numpy as np
import jax
import jax.numpy as jnp
from jax import lax
from jax.experimental import pallas as pl
from jax.experimental.pallas import tpu as pltpu

F32 = jnp.float32
BF16 = jnp.bfloat16

HEAD_DIM = 128
STEPS = 128
CHUNK = 128
GROUP_DIM_A = 128
DILATION_GROUPS = ((128, 1), (512, 4), (2048, 16))
MAX_EXACT = 16
MAX_DISTANCE = 2048
EPS = 1e-6
NEG = -1e30
ATTN_SCALE = HEAD_DIM ** -0.5

LANE = 128
SUBLANE = 8
VMEM_LIMIT_BYTES = 56 * 1024 * 1024
MATMUL_TILE = 1024
MXU_WIDTH = 256
FFN_TILE = 512
IN_PROJ_ROUND_TILE = 1024
ATTN_UNITS_PER_STEP = 16
SGU_ROWS_PER_STEP = 512
ELEMENTWISE_ROWS = 256
KV_STATE_ROWS = 512
FFN_DOWN_K_TILE = 3072
DECODE_COL_TILE = 2048


def _tile(dim, target, align):
    best = None
    t = align
    while t <= min(dim, target):
        if dim % t == 0:
            best = t
        t += align
    return best if best is not None else dim


def _params(semantics):
    return pltpu.CompilerParams(dimension_semantics=semantics, vmem_limit_bytes=VMEM_LIMIT_BYTES)


def _gelu(x):
    return 0.5 * x * (1.0 + jnp.tanh(math.sqrt(2.0 / math.pi) * (x + 0.044715 * (x * x * x))))


def _sigmoid(x):
    return 1.0 / (1.0 + jnp.exp(-x))


def _rms_scale(x):
    return lax.rsqrt(jnp.mean(x * x, axis=-1, keepdims=True) + EPS)


def _lanes(c):
    return slice(c * LANE, (c + 1) * LANE)


def _rmsnorm_cast_kernel(x_ref, g_ref, o_ref):
    x = x_ref[...]
    o_ref[...] = ((x * _rms_scale(x)) * g_ref[...]).astype(o_ref.dtype)


def _rmsnorm_cast(x, g):
    t, d = x.shape
    bt = _tile(t, ELEMENTWISE_ROWS, SUBLANE)
    return pl.pallas_call(
        _rmsnorm_cast_kernel,
        grid=(t // bt,),
        in_specs=[pl.BlockSpec((bt, d), lambda i: (i, 0)), pl.BlockSpec((1, d), lambda i: (0, 0))],
        out_specs=pl.BlockSpec((bt, d), lambda i: (i, 0)),
        out_shape=jax.ShapeDtypeStruct((t, d), BF16),
        compiler_params=_params(("arbitrary",)),
        name="rmsnorm_cast",
    )(x, g.reshape(1, d))


def _in_proj_kernel(x_ref, w_ref, o_ref):
    acc = jnp.dot(x_ref[...], w_ref[...], preferred_element_type=F32)
    for c in range(o_ref.shape[0]):
        o_ref[c] = acc[:, _lanes(c)]


def _in_proj_round_kernel(x_ref, w_in, o_ref, w_out, w_buf, *, slab_out):
    jj, i = pl.program_id(0), pl.program_id(1)
    ck = w_in.shape[0]
    slot = (jj + 1) % 2
    bm = o_ref.shape[-2]
    n_slabs = w_in.shape[1] // LANE
    row_split = 2 if bm % (4 * SUBLANE) == 0 else 1
    hm = bm // row_split
    per_strip = min(n_slabs, MXU_WIDTH // LANE)

    def round_chunk():
        chunk = w_in[...].astype(BF16)
        w_buf[jj % 2, pl.ds(pl.multiple_of(i * ck, ck), ck), :] = chunk
        w_out[...] = chunk

    pl.when(jj == 0)(round_chunk)

    @pl.when(jj > 0)
    def _():
        round_chunk()
        for h in range(row_split):
            x = x_ref[h * hm:(h + 1) * hm, :]
            for c0 in range(0, n_slabs, per_strip):
                c1 = min(c0 + per_strip, n_slabs)
                acc = jnp.dot(x, w_buf[slot, :, c0 * LANE:c1 * LANE], preferred_element_type=F32)
                for c in range(c0, c1):
                    val = acc[:, _lanes(c - c0)]
                    if slab_out:
                        o_ref[c, h * hm:(h + 1) * hm, :] = val
                    else:
                        o_ref[h * hm:(h + 1) * hm, _lanes(c)] = val


def _in_proj_round(h, w, col0, n_cols, name, slab_out=True):
    t, d = h.shape
    bn = _tile(math.gcd(n_cols, col0) if col0 else n_cols, IN_PROJ_ROUND_TILE, LANE)
    bm = _tile(t, MATMUL_TILE, SUBLANE)
    ni, nj = t // bm, n_cols // bn
    assert d % ni == 0
    ck = d // ni

    def fin(jj, i):
        return jnp.maximum(jj - 1, 0), jnp.where(jj == 0, 0, i)

    def chunk(jj, i):
        return jnp.where(jj == nj, ni - 1, i), jnp.minimum(jj, nj - 1)

    if slab_out:
        out_spec = pl.BlockSpec((bn // LANE, bm, LANE), lambda jj, i: fin(jj, i) + (0,))
        out_shape = jax.ShapeDtypeStruct((n_cols // LANE, t, LANE), F32)
    else:
        out_spec = pl.BlockSpec((bm, bn), lambda jj, i: fin(jj, i)[::-1])
        out_shape = jax.ShapeDtypeStruct((t, n_cols), F32)
    return pl.pallas_call(
        functools.partial(_in_proj_round_kernel, slab_out=slab_out),
        grid=(nj + 1, ni),
        in_specs=[pl.BlockSpec((bm, d), lambda jj, i: (fin(jj, i)[1], 0)),
                  pl.BlockSpec((ck, bn), lambda jj, i: (chunk(jj, i)[0], col0 // bn + chunk(jj, i)[1]))],
        out_specs=[out_spec, pl.BlockSpec((ck, bn), chunk)],
        out_shape=[out_shape, jax.ShapeDtypeStruct((d, n_cols), BF16)],
        scratch_shapes=[pltpu.VMEM((2, d, bn), BF16)],
        compiler_params=_params(("arbitrary", "arbitrary")),
        name=name,
    )(h, w)


def _in_proj(h, w, bn, col0, n_cols, name):
    t, d = h.shape
    bm = _tile(t, MATMUL_TILE, SUBLANE)
    return pl.pallas_call(
        _in_proj_kernel,
        grid=(t // bm, n_cols // bn),
        in_specs=[pl.BlockSpec((bm, d), lambda i, j: (i, 0)),
                  pl.BlockSpec((d, bn), lambda i, j: (0, col0 // bn + j))],
        out_specs=pl.BlockSpec((bn // LANE, bm, LANE), lambda i, j: (j, i, 0)),
        out_shape=jax.ShapeDtypeStruct((n_cols // LANE, t, LANE), F32),
        compiler_params=_params(("arbitrary", "arbitrary")),
        name=name,
    )(h, w)


def _sgu_kernel(u_ref, v_ref, lg_ref, lb_ref, w_ref, b_ref, o_ref, vs_ref, *, n_groups):
    c = w_ref.shape[1]
    n_feat = n_groups * GROUP_DIM_A
    for r0 in range(0, o_ref.shape[0], c):
        vp = _gelu(v_ref[:, r0:r0 + c, :])
        mu = jnp.sum(jnp.sum(vp, axis=0), axis=-1, keepdims=True) / n_feat
        vc = vp - mu
        var = jnp.sum(jnp.sum(vc * vc, axis=0), axis=-1, keepdims=True) / n_feat
        v = vc * lax.rsqrt(var + EPS) * lg_ref[...] + lb_ref[...]
        for g in range(n_groups):
            if r0 + c == o_ref.shape[0]:
                vs_ref[:, _lanes(g)] = v[g]
            mixed = jnp.dot(w_ref[g], v[g].astype(BF16), preferred_element_type=F32) + b_ref[g]
            o_ref[r0:r0 + c, _lanes(g)] = (_gelu(u_ref[g, r0:r0 + c, :]) * mixed).astype(o_ref.dtype)


def _sgu(z_sl, d_a, ln_g, ln_b, w_mix, b_mix):
    t = z_sl.shape[1]
    n_groups, c, _ = w_mix.shape
    rows = _tile(t, SGU_ROWS_PER_STEP, c)
    slab = pl.BlockSpec((n_groups, 1, LANE), lambda i: (0, 0, 0))
    return pl.pallas_call(
        functools.partial(_sgu_kernel, n_groups=n_groups),
        grid=(t // rows,),
        in_specs=[
            pl.BlockSpec((n_groups, rows, LANE), lambda i: (0, i, 0)),
            pl.BlockSpec((n_groups, rows, LANE), lambda i: (1, i, 0)),
            slab, slab,
            pl.BlockSpec((n_groups, c, c), lambda i: (0, 0, 0)),
            pl.BlockSpec((n_groups, c, LANE), lambda i: (0, 0, 0)),
        ],
        out_specs=[pl.BlockSpec((rows, d_a), lambda i: (i, 0)), pl.BlockSpec((c, d_a), lambda i: (0, 0))],
        out_shape=[jax.ShapeDtypeStruct((t, d_a), BF16), jax.ShapeDtypeStruct((c, d_a), F32)],
        compiler_params=_params(("arbitrary",)),
        name="sgu",
    )(z_sl, z_sl, ln_g.reshape(n_groups, 1, LANE), ln_b.reshape(n_groups, 1, LANE), w_mix, b_mix)


def _t5_bucket(dist, n_buckets):
    n = np.asarray(dist, np.int32)
    safe = np.maximum(n, 1).astype(np.float32)
    large = MAX_EXACT + (np.log(safe / MAX_EXACT) / np.log(np.float32(MAX_DISTANCE / MAX_EXACT))
                         * (n_buckets - MAX_EXACT)).astype(np.int32)
    large = np.minimum(large, n_buckets - 1)
    return np.where(n < MAX_EXACT, n, large).astype(np.int32)


def _bias_lookup(bias_tab, bucket):
    n_buckets = bias_tab.shape[0]
    flat = np.asarray(bucket).reshape(-1)
    onehot = (jnp.asarray(flat)[None, :] == jnp.arange(n_buckets)[:, None]).astype(F32)
    out = jnp.dot(bias_tab.T, onehot, precision=lax.Precision.HIGHEST)
    return out.reshape((bias_tab.shape[1],) + tuple(np.asarray(bucket).shape))


def _merge_by_lse(outs, lses):
    m = functools.reduce(jnp.maximum, lses)
    w = [jnp.exp(l - m) for l in lses]
    num = functools.reduce(lambda x, y: x + y, [wi * oi for wi, oi in zip(w, outs)])
    return num / functools.reduce(lambda x, y: x + y, w)


def _attn_prompt_kernel(*refs, hb, dil, n_other):
    q_ref, kc_ref, vc_ref, bias_ref, band_ref = refs[:5]
    others = refs[5:5 + 2 * n_other]
    n_out = 1 if n_other else 2
    outs = refs[5 + 2 * n_other:5 + 2 * n_other + n_out]
    kp_ref, vp_ref = refs[5 + 2 * n_other + n_out:][:2]
    o_acc, lse_acc = refs[5 + 2 * n_other + n_out + 2:] if n_other else outs
    b = pl.program_id(1)

    @pl.when(b == 0)
    def _():
        kp_ref[...] = jnp.zeros(kp_ref.shape, F32)
        vp_ref[...] = jnp.zeros(vp_ref.shape, F32)

    span = STEPS * dil
    n_pb = q_ref.shape[1] // span
    col = lax.broadcasted_iota(jnp.int32, (STEPS, 2 * STEPS), 1)
    band = band_ref[...] > 0.5
    first = band & ((b > 0) | (col >= STEPS))
    for hh in range(hb):
        for pb in range(n_pb):
            for r in range(dil):
                def rows(block):
                    return pl.ds(block * span + r, STEPS, stride=dil) if dil > 1 else pl.ds(block * span, STEPS)
                k_prev = kp_ref[hh, rows(0), :] if pb == 0 else kc_ref[hh, rows(pb - 1), :]
                v_prev = vp_ref[hh, rows(0), :] if pb == 0 else vc_ref[hh, rows(pb - 1), :]
                q = q_ref[hh, rows(pb), :].astype(BF16)
                kk = jnp.concatenate([k_prev, kc_ref[hh, rows(pb), :]], axis=0).astype(BF16)
                vv = jnp.concatenate([v_prev, vc_ref[hh, rows(pb), :]], axis=0).astype(BF16)
                s = lax.dot_general(q, kk, (((1,), (1,)), ((), ())), preferred_element_type=F32) * ATTN_SCALE
                s = jnp.where(first if pb == 0 else band, s + bias_ref[hh], NEG)
                m = jnp.max(s, axis=-1, keepdims=True)
                e = jnp.exp(s - m)
                den = jnp.sum(e, axis=-1, keepdims=True)
                o_acc[hh, rows(pb), :] = jnp.dot(e.astype(BF16), vv, preferred_element_type=F32) / den
                lse_acc[hh, rows(pb), :] = jnp.broadcast_to(m + jnp.log(den), (STEPS, HEAD_DIM))
    kp_ref[...] = kc_ref[:, (n_pb - 1) * span:, :]
    vp_ref[...] = vc_ref[:, (n_pb - 1) * span:, :]
    if n_other:
        for hh in range(hb):
            merged = _merge_by_lse([r[hh] for r in others[:n_other]] + [o_acc[hh]],
                                   [r[hh] for r in others[n_other:]] + [lse_acc[hh]])
            outs[0][:, _lanes(hh)] = merged.astype(outs[0].dtype)


def _attn_prompt(z_sl, hpg, q_slab, k_slab, v_slab, bias_tab, dil, merge_with=None):
    t = z_sl.shape[1]
    span = STEPS * dil
    assert t % span == 0
    hb = min(hpg, max(1, ATTN_UNITS_PER_STEP // dil))
    assert hpg % hb == 0 and q_slab % hb == 0 and k_slab % hb == 0 and v_slab % hb == 0
    n_pb = math.gcd(t // span, max(1, ATTN_UNITS_PER_STEP // (hb * dil)))
    rows = span * n_pb
    nb = t // rows

    p_idx = np.arange(STEPS)[:, None]
    c_idx = np.arange(2 * STEPS)[None, :]
    steps = p_idx + STEPS - c_idx
    band = ((steps >= 0) & (steps <= STEPS)).astype(np.float32)
    bias = _bias_lookup(bias_tab, _t5_bucket(np.clip(steps, 0, STEPS) * dil, bias_tab.shape[0]))

    def cur(slab):
        return pl.BlockSpec((hb, rows, LANE), lambda hi, b: (slab // hb + hi, b, 0))

    head_major = pl.BlockSpec((hb, rows, LANE), lambda hi, b: (hi, b, 0))
    prev_block = pltpu.VMEM((hb, span, LANE), F32)
    block = pltpu.VMEM((hb, rows, LANE), F32)
    others = [] if merge_with is None else list(merge_with[0]) + list(merge_with[1])
    if merge_with is None:
        out_specs = [head_major, head_major]
        out_shape = [jax.ShapeDtypeStruct((hpg, t, LANE), F32)] * 2
        scratch = [prev_block, prev_block]
    else:
        out_specs = [pl.BlockSpec((rows, hb * LANE), lambda hi, b: (b, hi))]
        out_shape = [jax.ShapeDtypeStruct((t, hpg * LANE), BF16)]
        scratch = [prev_block, prev_block, block, block]
    res = pl.pallas_call(
        functools.partial(_attn_prompt_kernel, hb=hb, dil=dil, n_other=len(others) // 2),
        grid=(hpg // hb, nb),
        in_specs=[cur(q_slab), cur(k_slab), cur(v_slab),
                  pl.BlockSpec((hb, STEPS, 2 * STEPS), lambda hi, b: (hi, 0, 0)),
                  pl.BlockSpec((STEPS, 2 * STEPS), lambda hi, b: (0, 0))] + [head_major] * len(others),
        out_specs=out_specs,
        out_shape=out_shape,
        scratch_shapes=scratch,
        compiler_params=_params(("arbitrary", "arbitrary")),
        name=f"attn_prompt_d{dil}",
    )(z_sl, z_sl, z_sl, bias, jnp.asarray(band), *others)
    return res[0] if merge_with is not None else res


def _attn_sample_kernel(q_ref, kn_ref, vn_ref, ck_ref, cv_ref, bc_ref, mc_ref, bnew_ref, mnew_ref,
                        o_ref, lse_ref, *, hpg, s_len, grouped):
    n_keys = mc_ref.shape[1]
    n_new = hpg * s_len
    valid_c = mc_ref[...] > 0.5

    def cache_head(ref, h):
        if grouped:
            return ref[0, :, pl.ds(h, s_len, stride=hpg), :].reshape(n_keys, HEAD_DIM).astype(BF16)
        return ref[0, pl.ds(h, n_keys, stride=hpg) if hpg > 1 else pl.ds(0, n_keys), :].astype(BF16)

    q_all = q_ref[...].reshape(n_new, HEAD_DIM).astype(BF16)
    s_new = lax.dot_general(q_all, kn_ref[...].reshape(n_new, HEAD_DIM).astype(BF16), (((1,), (1,)), ((), ())),
                            preferred_element_type=F32) * ATTN_SCALE
    key_head = lax.broadcasted_iota(jnp.int32, (s_len, n_new), 1) // s_len
    new_ok = mnew_ref[...] > 0.5

    partial, e_new = [], []
    for h in range(hpg):
        q = q_all[h * s_len:(h + 1) * s_len, :]
        sc = lax.dot_general(q, cache_head(ck_ref, h), (((1,), (1,)), ((), ())),
                             preferred_element_type=F32) * ATTN_SCALE
        sc = jnp.where(valid_c, sc + bc_ref[h], NEG)
        sn = jnp.where(new_ok & (key_head == h), s_new[h * s_len:(h + 1) * s_len, :] + bnew_ref[h], NEG)
        m = jnp.maximum(jnp.max(sc, axis=-1, keepdims=True), jnp.max(sn, axis=-1, keepdims=True))
        ec = jnp.exp(sc - m)
        en = jnp.exp(sn - m)
        den = jnp.sum(ec, axis=-1, keepdims=True) + jnp.sum(en, axis=-1, keepdims=True)
        acc = jnp.dot(ec.astype(BF16), cache_head(cv_ref, h), preferred_element_type=F32)
        partial.append((acc, den, m))
        e_new.append(en)
    o_new = jnp.dot(jnp.concatenate(e_new, axis=0).astype(BF16),
                    vn_ref[...].reshape(n_new, HEAD_DIM).astype(BF16), preferred_element_type=F32)
    for h, (acc, den, m) in enumerate(partial):
        o_ref[h] = (acc + o_new[h * s_len:(h + 1) * s_len, :]) / den
        lse_ref[h] = jnp.broadcast_to(m + jnp.log(den), (s_len, HEAD_DIM))


def _attn_sample(z_sl, s_len, hpg, q_slab, k_slab, v_slab, cache_k, cache_v, bias_tab, dil):
    t = z_sl.shape[1]
    n_seq = t // s_len
    lc = cache_k.shape[1]
    assert q_slab % hpg == 0 and k_slab % hpg == 0 and v_slab % hpg == 0

    j = np.arange(STEPS + 1)
    idx = lc + np.arange(s_len)[:, None] - j[None, :] * dil
    assert idx.min() >= 0
    bucket = _t5_bucket(j * dil, bias_tab.shape[0])
    mask = np.zeros((s_len, lc + s_len), np.float32)
    bsel = np.zeros((s_len, lc + s_len), np.int32)
    for s in range(s_len):
        mask[s, idx[s]] = 1.0
        bsel[s, idx[s]] = bucket

    grouped = dil > s_len and lc % dil == 0 and s_len == SUBLANE
    if grouped:
        pos = (np.arange(lc // dil)[:, None] * dil + np.arange(s_len)[None, :]).reshape(-1)
        assert mask[:, :lc].sum() == mask[:, pos].sum()
        ck = cache_k.reshape(n_seq, lc // dil, dil * hpg, HEAD_DIM)
        cv = cache_v.reshape(n_seq, lc // dil, dil * hpg, HEAD_DIM)
        cache_spec = pl.BlockSpec((1, lc // dil, s_len * hpg, LANE), lambda b: (b, 0, 0, 0))
    else:
        pos = np.arange(lc)
        ck = cache_k.reshape(n_seq, lc * hpg, HEAD_DIM)
        cv = cache_v.reshape(n_seq, lc * hpg, HEAD_DIM)
        cache_spec = pl.BlockSpec((1, lc * hpg, LANE), lambda b: (b, 0, 0))
    n_keys = len(pos)
    bias_c = _bias_lookup(bias_tab, bsel[:, pos])
    bias_new = jnp.tile(_bias_lookup(bias_tab, bsel[:, lc:]), (1, 1, hpg))
    mask_c = jnp.asarray(mask[:, pos])
    mask_new = jnp.asarray(np.tile(mask[:, lc:], (1, hpg)))

    out_spec = pl.BlockSpec((hpg, s_len, LANE), lambda b: (0, b, 0))
    return pl.pallas_call(
        functools.partial(_attn_sample_kernel, hpg=hpg, s_len=s_len, grouped=grouped),
        grid=(n_seq,),
        in_specs=[pl.BlockSpec((hpg, s_len, LANE), lambda b: (q_slab // hpg, b, 0)),
                  pl.BlockSpec((hpg, s_len, LANE), lambda b: (k_slab // hpg, b, 0)),
                  pl.BlockSpec((hpg, s_len, LANE), lambda b: (v_slab // hpg, b, 0)),
                  cache_spec, cache_spec,
                  pl.BlockSpec((hpg, s_len, n_keys), lambda b: (0, 0, 0)),
                  pl.BlockSpec((s_len, n_keys), lambda b: (0, 0)),
                  pl.BlockSpec((hpg, s_len, hpg * s_len), lambda b: (0, 0, 0)),
                  pl.BlockSpec((s_len, hpg * s_len), lambda b: (0, 0))],
        out_specs=[out_spec, out_spec],
        out_shape=[jax.ShapeDtypeStruct((hpg, t, LANE), F32)] * 2,
        compiler_params=_params(("arbitrary",)),
        name=f"attn_sample_d{dil}",
    )(z_sl, z_sl, z_sl, ck, cv, bias_c, mask_c, bias_new, mask_new)


def _merge_kernel(o1, o2, o3, l1, l2, l3, out_ref):
    merged = _merge_by_lse([o1[...], o2[...], o3[...]], [l1[...], l2[...], l3[...]])
    for h in range(merged.shape[0]):
        out_ref[:, _lanes(h)] = merged[h].astype(out_ref.dtype)


def _merge(outs, lses):
    hpg, t, _ = outs[0].shape
    bt = _tile(t, ELEMENTWISE_ROWS, SUBLANE)
    spec = pl.BlockSpec((hpg, bt, LANE), lambda i: (0, i, 0))
    return pl.pallas_call(
        _merge_kernel,
        grid=(t // bt,),
        in_specs=[spec] * 6,
        out_specs=pl.BlockSpec((bt, hpg * LANE), lambda i: (i, 0)),
        out_shape=jax.ShapeDtypeStruct((t, hpg * LANE), BF16),
        compiler_params=_params(("arbitrary",)),
        name="merge_groups",
    )(*outs, *lses)


def _gated_proj_kernel(a_ref, b_ref, wa_ref, wb_ref, ga_ref, gb_ref, o_ref):
    n_slabs = ga_ref.shape[0]
    per_strip = min(n_slabs, MXU_WIDTH // LANE)
    for c0 in range(0, n_slabs, per_strip):
        c1 = min(c0 + per_strip, n_slabs)
        cols = slice(c0 * LANE, c1 * LANE)
        pa = jnp.dot(a_ref[...], wa_ref[:, cols], preferred_element_type=F32)
        pb = jnp.dot(b_ref[...], wb_ref[:, cols], preferred_element_type=F32)
        for c in range(c0, c1):
            gated = _sigmoid(ga_ref[c]) * pa[:, _lanes(c - c0)] + _sigmoid(gb_ref[c]) * pb[:, _lanes(c - c0)]
            o_ref[:, _lanes(c)] = gated.astype(o_ref.dtype)


def _gated_proj(o_a, o_b, w_a, w_b, z_sl, bn, gate_a_blk, gate_b_blk):
    t, d_a = o_a.shape
    d_b = o_b.shape[1]
    d = w_a.shape[1]
    bm = _tile(t, MATMUL_TILE, SUBLANE)
    return pl.pallas_call(
        _gated_proj_kernel,
        grid=(t // bm, d // bn),
        in_specs=[pl.BlockSpec((bm, d_a), lambda i, j: (i, 0)),
                  pl.BlockSpec((bm, d_b), lambda i, j: (i, 0)),
                  pl.BlockSpec((d_a, bn), lambda i, j: (0, j)),
                  pl.BlockSpec((d_b, bn), lambda i, j: (0, j)),
                  pl.BlockSpec((bn // LANE, bm, LANE), lambda i, j: (gate_a_blk + j, i, 0)),
                  pl.BlockSpec((bn // LANE, bm, LANE), lambda i, j: (gate_b_blk + j, i, 0))],
        out_specs=pl.BlockSpec((bm, bn), lambda i, j: (i, j)),
        out_shape=jax.ShapeDtypeStruct((t, d), BF16),
        compiler_params=_params(("arbitrary", "arbitrary")),
        name="gated_proj",
    )(o_a, o_b, w_a, w_b, z_sl, z_sl)


def _matmul_kernel(x_ref, w_ref, o_ref, *, nk):
    if nk == 1:
        o_ref[...] = jnp.dot(x_ref[...], w_ref[...], preferred_element_type=F32)
    else:
        @pl.when(pl.program_id(2) == 0)
        def _():
            o_ref[...] = jnp.zeros(o_ref.shape, o_ref.dtype)

        o_ref[...] += jnp.dot(x_ref[...], w_ref[...], preferred_element_type=F32)


def _matmul(x, w, name, k_tile=None):
    t, kd = x.shape
    n = w.shape[1]
    bm = _tile(t, MATMUL_TILE, SUBLANE)
    bn = _tile(n, MATMUL_TILE, LANE)
    bk = kd if k_tile is None else k_tile
    nk = kd // bk
    return pl.pallas_call(
        functools.partial(_matmul_kernel, nk=nk),
        grid=(t // bm, n // bn, nk),
        in_specs=[pl.BlockSpec((bm, bk), lambda i, j, k: (i, k)),
                  pl.BlockSpec((bk, bn), lambda i, j, k: (k, j))],
        out_specs=pl.BlockSpec((bm, bn), lambda i, j, k: (i, j)),
        out_shape=jax.ShapeDtypeStruct((t, n), F32),
        compiler_params=_params(("arbitrary", "arbitrary", "arbitrary")),
        name=name,
    )(x, w)


def _post_mix_kernel(x_ref, y_ref, g1_ref, g2_ref, x1_ref, h2_ref):
    y = y_ref[...]
    x1 = x_ref[...] + (y * _rms_scale(y)) * g1_ref[...]
    x1_ref[...] = x1
    h2_ref[...] = ((x1 * _rms_scale(x1)) * g2_ref[...]).astype(h2_ref.dtype)


def _post_mix(x, y, g_post, g_pre):
    t, d = x.shape
    bt = _tile(t, ELEMENTWISE_ROWS, SUBLANE)
    row = pl.BlockSpec((bt, d), lambda i: (i, 0))
    vec = pl.BlockSpec((1, d), lambda i: (0, 0))
    return pl.pallas_call(
        _post_mix_kernel,
        grid=(t // bt,),
        in_specs=[row, row, vec, vec],
        out_specs=[row, row],
        out_shape=[jax.ShapeDtypeStruct((t, d), F32), jax.ShapeDtypeStruct((t, d), BF16)],
        compiler_params=_params(("arbitrary",)),
        name="post_mix",
    )(x, y, g_post.reshape(1, d), g_pre.reshape(1, d))


def _residual_norm_kernel(x_ref, y_ref, g_ref, o_ref):
    y = y_ref[...]
    o_ref[...] = x_ref[...] + (y * _rms_scale(y)) * g_ref[...]


def _residual_norm(x, y, g):
    t, d = x.shape
    bt = _tile(t, ELEMENTWISE_ROWS, SUBLANE)
    row = pl.BlockSpec((bt, d), lambda i: (i, 0))
    return pl.pallas_call(
        _residual_norm_kernel,
        grid=(t // bt,),
        in_specs=[row, row, pl.BlockSpec((1, d), lambda i: (0, 0))],
        out_specs=row,
        out_shape=jax.ShapeDtypeStruct((t, d), F32),
        compiler_params=_params(("arbitrary",)),
        name="residual_norm",
    )(x, y, g.reshape(1, d))


def _ffn_up_kernel(*refs, conv_w, s_len, d_ff):
    if s_len is None:
        (x_ref, wg_in, wu_in, wd_in, cw_ref, cb_ref, out_ref, tail_ref, wg_out, wu_out, wd_out,
         wg_buf, wu_buf, a_buf, u_buf, o_buf) = refs
        hist_refs = None
        jj, i = pl.program_id(0), pl.program_id(1)
        ck = wg_in.shape[0]

        def round_chunks():
            rows_c = pl.ds(pl.multiple_of(i * ck, ck), ck)
            for src, buf, dst in ((wg_in, wg_buf, wg_out), (wu_in, wu_buf, wu_out)):
                chunk = src[...].astype(BF16)
                buf[jj % 2, rows_c, :] = chunk
                dst[...] = chunk

        def round_down_rows():
            rc = wd_in.shape[0]
            row = ((jj - 1) * pl.num_programs(1) + i) * rc + lax.broadcasted_iota(jnp.int32, wd_in.shape, 0)
            wd_out[...] = jnp.where(row < d_ff, wd_in[...], 0.0).astype(BF16)

        slot = (jj + 1) % 2
        wg = lambda cols: wg_buf[slot, :, cols]
        wu = lambda cols: wu_buf[slot, :, cols]
        col_blk = jj - 1
        row_split = 2 if out_ref.shape[0] % (4 * SUBLANE) == 0 else 1
    else:
        x_ref, wg_ref, wu_ref, cw_ref, cb_ref, h1_ref, h2_ref, out_ref, tail_ref, a_buf, u_buf, o_buf = refs
        hist_refs = {1: h1_ref, 2: h2_ref}
        i = pl.program_id(1)
        wg = lambda cols: wg_ref[:, cols]
        wu = lambda cols: wu_ref[:, cols]
        col_blk = pl.program_id(0)
        row_split = 1
    bm, bn = out_ref.shape
    hm = bm // row_split
    half = hm // 2
    top = SUBLANE
    n_slabs = bn // LANE
    per_strip = min(n_slabs, MXU_WIDTH // LANE)

    def units():
        @pl.when(i == 0)
        def _():
            a_buf[:, 0:top, :] = jnp.zeros((n_slabs, top, LANE), F32)

        lane = lax.broadcasted_iota(jnp.int32, (1, LANE), 1)
        for h in range(row_split):
            r0 = h * hm
            x = x_ref[r0:r0 + hm, :]
            for c0 in range(0, n_slabs, per_strip):
                c1 = min(c0 + per_strip, n_slabs)
                cols = slice(c0 * LANE, c1 * LANE)
                a = jnp.dot(x, wg(cols), preferred_element_type=F32)
                u = jnp.dot(x, wu(cols), preferred_element_type=F32)
                if s_len is None and h == 0 and c0 == 0:
                    round_chunks()
                    round_down_rows()
                if s_len is not None:
                    tail_ref[r0:r0 + hm, cols] = a
                elif h == row_split - 1:
                    tail_ref[:, cols] = a[hm - SUBLANE:, :]
                for c in range(c0, c1):
                    a_buf[c, top + r0:top + r0 + hm, :] = a[:, _lanes(c - c0)]
                    u_buf[c, r0:r0 + hm, :] = u[:, _lanes(c - c0)]
                    in_range = (col_blk * bn + c * LANE + lane) < d_ff
                    for e in range(2):
                        taps = [a_buf[c, pl.ds(top + r0 + e - lag, half, stride=2), :] for lag in range(conv_w)]
                        if hist_refs is not None:
                            s = 2 * (lax.broadcasted_iota(jnp.int32, (half, LANE), 0) % (s_len // 2)) + e
                            for lag in range(1, conv_w):
                                hist = hist_refs[lag][c, pl.ds(r0 + e, half, stride=2), :]
                                taps[lag] = jnp.where(s < lag, hist, taps[lag])
                        acc = cw_ref[0:1, _lanes(c)] * taps[conv_w - 1]
                        for k in range(1, conv_w):
                            acc = acc + cw_ref[k:k + 1, _lanes(c)] * taps[conv_w - 1 - k]
                        acc = cb_ref[:, _lanes(c)] + acc
                        val = _gelu(acc) * u_buf[c, pl.ds(r0 + e, half, stride=2), :]
                        o_buf[c, pl.ds(r0 + e, half, stride=2), :] = jnp.where(in_range, val, 0.0)
                    out_ref[r0:r0 + hm, _lanes(c)] = o_buf[c, r0:r0 + hm, :].astype(out_ref.dtype)
        if s_len is None:
            a_buf[:, 0:top, :] = a_buf[:, bm:bm + top, :]

    if s_len is None:
        pl.when(jj == 0)(round_chunks)
        pl.when(jj > 0)(units)
    else:
        units()


def _ffn_up_prompt(h, w_gate, w_up, w_down, conv_w, conv_b, ffp):
    t, d = h.shape
    d_ff = w_gate.shape[1]
    cw = conv_w.shape[0]
    assert cw == 3 and d_ff % LANE == 0
    bn = FFN_TILE
    assert ffp % bn == 0 and ffp - d_ff < bn
    bm = _tile(t, MATMUL_TILE, 4 * SUBLANE)
    ni, nj = t // bm, ffp // bn
    assert d % ni == 0
    ck = d // ni
    n_slabs = bn // LANE

    def fin_i(jj, i):
        return jnp.where(jj == 0, 0, i)

    def fin_j(jj):
        return jnp.maximum(jj - 1, 0)

    def chunk(jj, i):
        return jnp.where(jj == nj, ni - 1, i), jnp.minimum(jj, nj - 1)

    n_out = w_down.shape[1]
    assert ffp % (nj * ni) == 0
    rc = ffp // (nj * ni)
    assert rc % (2 * SUBLANE) == 0

    def down_blk(jj, i):
        return fin_j(jj) * ni + fin_i(jj, i)

    last_src = -(-d_ff // rc) - 1
    w_in_spec = pl.BlockSpec((ck, bn), chunk)
    wbuf = pltpu.VMEM((2, d, bn), BF16)
    return pl.pallas_call(
        functools.partial(_ffn_up_kernel, conv_w=cw, s_len=None, d_ff=d_ff),
        grid=(nj + 1, ni),
        in_specs=[pl.BlockSpec((bm, d), lambda jj, i: (fin_i(jj, i), 0)), w_in_spec, w_in_spec,
                  pl.BlockSpec((rc, n_out), lambda jj, i: (jnp.minimum(down_blk(jj, i), last_src), 0)),
                  pl.BlockSpec((cw, bn), lambda jj, i: (0, fin_j(jj))),
                  pl.BlockSpec((1, bn), lambda jj, i: (0, fin_j(jj)))],
        out_specs=[pl.BlockSpec((bm, bn), lambda jj, i: (fin_i(jj, i), fin_j(jj))),
                   pl.BlockSpec((SUBLANE, bn), lambda jj, i: (0, fin_j(jj))),
                   w_in_spec, w_in_spec,
                   pl.BlockSpec((rc, n_out), lambda jj, i: (down_blk(jj, i), 0))],
        out_shape=[jax.ShapeDtypeStruct((t, ffp), BF16), jax.ShapeDtypeStruct((SUBLANE, d_ff), F32),
                   jax.ShapeDtypeStruct((d, d_ff), BF16), jax.ShapeDtypeStruct((d, d_ff), BF16),
                   jax.ShapeDtypeStruct((ffp, n_out), BF16)],
        scratch_shapes=[wbuf, wbuf,
                        pltpu.VMEM((n_slabs, bm + SUBLANE, LANE), F32), pltpu.VMEM((n_slabs, bm, LANE), F32),
                        pltpu.VMEM((n_slabs, bm, LANE), F32)],
        compiler_params=_params(("arbitrary", "arbitrary")),
        name="ffn_up",
    )(h, w_gate, w_up, w_down, conv_w, conv_b.reshape(1, d_ff))


def _ffn_up_sample(h, w_gate, w_up, conv_w, conv_b, ffp, hist, s_len):
    t, d = h.shape
    d_ff = w_gate.shape[1]
    cw = conv_w.shape[0]
    assert cw == 3 and d_ff % LANE == 0 and s_len % 2 == 0
    bn = _tile(ffp, MATMUL_TILE, LANE)
    assert ffp - d_ff < bn
    n_slabs = bn // LANE
    w_spec = pl.BlockSpec((d, bn), lambda j, i: (0, j))
    out_spec = pl.BlockSpec((t, bn), lambda j, i: (i, j))

    def slabs(rows):
        n_seq, r, _ = rows.shape
        sl = jnp.transpose(rows.reshape(n_seq, r, d_ff // LANE, LANE), (2, 0, 1, 3))
        return jnp.pad(sl, ((0, 0), (0, 0), (0, s_len - r), (0, 0))).reshape(d_ff // LANE, t, LANE)
    h1 = slabs(hist[:, 1:2])
    h2 = slabs(hist)
    hist_spec = pl.BlockSpec((n_slabs, t, LANE), lambda j, i: (j, i, 0))
    return pl.pallas_call(
        functools.partial(_ffn_up_kernel, conv_w=cw, s_len=s_len, d_ff=d_ff),
        grid=(ffp // bn, 1),
        in_specs=[pl.BlockSpec((t, d), lambda j, i: (i, 0)), w_spec, w_spec,
                  pl.BlockSpec((cw, bn), lambda j, i: (0, j)), pl.BlockSpec((1, bn), lambda j, i: (0, j)),
                  hist_spec, hist_spec],
        out_specs=[out_spec, out_spec],
        out_shape=[jax.ShapeDtypeStruct((t, ffp), BF16), jax.ShapeDtypeStruct((t, d_ff), F32)],
        scratch_shapes=[pltpu.VMEM((n_slabs, t + SUBLANE, LANE), F32), pltpu.VMEM((n_slabs, t, LANE), F32),
                        pltpu.VMEM((n_slabs, t, LANE), F32)],
        compiler_params=_params(("arbitrary", "arbitrary")),
        name="ffn_up_decode",
    )(h, w_gate, w_up, conv_w, conv_b.reshape(1, d_ff), h1, h2)


def _kv_state_rows_kernel(k_ref, v_ref, ko_ref, vo_ref, *, hpg):
    n = k_ref.shape[1]
    for src, dst in ((k_ref, ko_ref), (v_ref, vo_ref)):
        for h in range(hpg):
            dst[pl.ds(h, n, stride=hpg) if hpg > 1 else pl.ds(0, n), :] = src[h]


def _kv_state_rows(z_sl, k_slab, v_slab, hpg, row0, n_rows):
    br = _tile(n_rows, KV_STATE_ROWS, SUBLANE)
    assert row0 % br == 0 and k_slab % hpg == 0 and v_slab % hpg == 0
    out_spec = pl.BlockSpec((br * hpg, LANE), lambda i: (i, 0))
    return pl.pallas_call(
        functools.partial(_kv_state_rows_kernel, hpg=hpg),
        grid=(n_rows // br,),
        in_specs=[pl.BlockSpec((hpg, br, LANE), lambda i: (k_slab // hpg, row0 // br + i, 0)),
                  pl.BlockSpec((hpg, br, LANE), lambda i: (v_slab // hpg, row0 // br + i, 0))],
        out_specs=[out_spec, out_spec],
        out_shape=[jax.ShapeDtypeStruct((n_rows * hpg, LANE), F32)] * 2,
        compiler_params=_params(("arbitrary",)),
        name="kv_state_rows",
    )(z_sl, z_sl)


def _layer(x, p, s_len=None, caches=None, conv_hist=None):
    t, d = x.shape
    d_a = p["ln_g"].shape[0]
    hpg = p["hpg"]
    gw = hpg * HEAD_DIM
    bn = p["bn"]
    o1 = 2 * d_a
    d_qkv = 3 * gw

    h = _rmsnorm_cast(x, p["g_pre_mix"])
    w16 = dict(p.get("w16", {}))
    d_in = o1 + 3 * d_qkv + 2 * d
    if s_len is None:
        qkv_sl, w16["in_proj"] = _in_proj_round(h, p["w_in"], 0, d_in, "in_proj")
    else:
        qkv_sl = _in_proj(h, w16["in_proj"], _tile(d_in, DECODE_COL_TILE, LANE), 0, d_in, "in_proj_decode")

    if s_len is None:
        w_mix, b_mix = p["w_mix_prompt"], p["b_mix_prompt"]
    else:
        w_mix, b_mix = p["w_mix_sample"], p["b_mix_sample"]
    o_a, v_state = _sgu(qkv_sl, d_a, p["ln_g"], p["ln_b"], w_mix, b_mix)

    outs, lses = [], []
    for gi, (_, dil) in enumerate(DILATION_GROUPS):
        q_slab = o1 // LANE + gi * hpg
        k_slab = (o1 + d_qkv) // LANE + gi * hpg
        v_slab = (o1 + 2 * d_qkv) // LANE + gi * hpg
        bias_tab = p["rel_bias"][:, gi * hpg:(gi + 1) * hpg]
        if s_len is not None:
            o, lse = _attn_sample(qkv_sl, s_len, hpg, q_slab, k_slab, v_slab, caches[2 * gi], caches[2 * gi + 1],
                                  bias_tab, dil)
        elif gi < len(DILATION_GROUPS) - 1:
            o, lse = _attn_prompt(qkv_sl, hpg, q_slab, k_slab, v_slab, bias_tab, dil)
        else:
            o_b = _attn_prompt(qkv_sl, hpg, q_slab, k_slab, v_slab, bias_tab, dil, merge_with=(outs, lses))
            break
        outs.append(o)
        lses.append(lse)
    if s_len is not None:
        o_b = _merge(outs, lses)

    gate_a_blk = (o1 + 3 * d_qkv) // bn
    merged = _gated_proj(o_a, o_b, p["w_proj_a"], p["w_proj_b"], qkv_sl, bn, gate_a_blk, gate_a_blk + d // bn)
    if s_len is None:
        y, w16["out_proj"] = _in_proj_round(merged, p["w_out"], 0, d, "out_proj", slab_out=False)
    else:
        y = _matmul(merged, w16["out_proj"], "out_proj_decode")
    x1, h2 = _post_mix(x, y, p["g_post_mix"], p["g_pre_ffn"])

    if s_len is None:
        act, a_tail, w16["ffn_gate"], w16["ffn_up"], w16["ffn_down"] = _ffn_up_prompt(
            h2, p["w_gate"], p["w_up"], p["w_down"], p["conv_w"], p["conv_b"], p["ffp"])
    else:
        act, a_tail = _ffn_up_sample(h2, w16["ffn_gate"], w16["ffn_up"], p["conv_w"], p["conv_b"], p["ffp"],
                                     conv_hist, s_len)
    f = _matmul(act, w16["ffn_down"], "ffn_down", k_tile=p["ffn_k_tile"])
    y_out = _residual_norm(x1, f, p["g_post_ffn"])
    return y_out, qkv_sl, v_state, a_tail, w16


def kernel(x_prompt, x_sample, cache_k_g1, cache_v_g1, cache_k_g2, cache_v_g2, cache_k_g3, cache_v_g3, state_conv, g_pre_mix, w_in, sgu_ln_g, sgu_ln_b, w_spatial, b_spatial, rel_bias, w_proj_a, w_proj_b, w_out, g_post_mix, g_pre_ffn, w_gate, w_up, conv_w, conv_b, w_down, g_post_ffn):
    depth = w_in.shape[0]
    assert depth == 1
    n_prompt, seq, d = x_prompt.shape
    assert n_prompt == 1 and seq % CHUNK == 0
    n_seq, s_len, _ = x_sample.shape
    assert s_len == SUBLANE
    d_a = sgu_ln_g.shape[1]
    n_groups = w_spatial.shape[1]
    n_heads = rel_bias.shape[1]
    hpg = n_heads // len(DILATION_GROUPS)
    gw = hpg * HEAD_DIM
    d_ff = w_gate.shape[2]
    cw = conv_w.shape[1]
    bn = _tile(math.gcd(2 * d_a, gw), MATMUL_TILE, LANE)
    ffp = -(-d_ff // FFN_TILE) * FFN_TILE
    t_s = n_seq * s_len

    tri = np.tril(np.ones((CHUNK, CHUNK), np.float32))
    w_mix_prompt = (w_spatial[0] * tri).astype(BF16)
    b_mix_prompt = jnp.broadcast_to(b_spatial[0][:, :, None], (n_groups, CHUNK, LANE))
    w_small = w_spatial[0][:, :s_len, :s_len] * tri[:s_len, :s_len]
    same_seq = np.kron(np.eye(n_seq, dtype=np.float32), np.ones((s_len, s_len), np.float32))
    w_mix_sample = (jnp.tile(w_small, (1, n_seq, n_seq)) * same_seq).astype(BF16)
    b_mix_sample = jnp.broadcast_to(jnp.tile(b_spatial[0][:, :s_len], (1, n_seq))[:, :, None], (n_groups, t_s, LANE))

    p = dict(
        hpg=hpg, bn=bn, ffp=ffp,
        g_pre_mix=g_pre_mix[0], w_in=w_in[0], ln_g=sgu_ln_g[0], ln_b=sgu_ln_b[0],
        w_mix_prompt=w_mix_prompt, b_mix_prompt=b_mix_prompt,
        w_mix_sample=w_mix_sample, b_mix_sample=b_mix_sample,
        rel_bias=rel_bias,
        w_proj_a=w_proj_a[0].astype(BF16), w_proj_b=w_proj_b[0].astype(BF16), w_out=w_out[0],
        g_post_mix=g_post_mix[0], g_pre_ffn=g_pre_ffn[0],
        w_gate=w_gate[0], w_up=w_up[0],
        conv_w=conv_w[0], conv_b=conv_b[0],
        w_down=w_down[0],
        ffn_k_tile=_tile(ffp, FFN_DOWN_K_TILE, LANE),
        g_post_ffn=g_post_ffn[0],
    )

    caches = tuple(c[0] for c in (cache_k_g1, cache_v_g1, cache_k_g2, cache_v_g2, cache_k_g3, cache_v_g3))

    yp, zp, vp_state, ap_tail, w16 = _layer(x_prompt[0], p)
    ys, zs, vs_state, as_all, _ = _layer(x_sample.reshape(t_s, d), dict(p, w16=w16), s_len=s_len,
                                         caches=caches, conv_hist=state_conv[0])

    d_qkv = 3 * gw
    prompt_kv, sample_kv = [], []
    for gi, (win, _) in enumerate(DILATION_GROUPS):
        keep = min(win, seq)
        k_slab = (2 * d_a + d_qkv + gi * gw) // LANE
        v_slab = (2 * d_a + 2 * d_qkv + gi * gw) // LANE
        prompt_kv += [s.reshape(1, 1, keep, hpg, HEAD_DIM)
                      for s in _kv_state_rows(zp, k_slab, v_slab, hpg, seq - keep, keep)]
        sample_kv += [s.reshape(1, n_seq, s_len, hpg, HEAD_DIM)
                      for s in _kv_state_rows(zs, k_slab, v_slab, hpg, 0, t_s)]
    p_conv = ap_tail[SUBLANE - (cw - 1):].reshape(1, 1, cw - 1, d_ff)
    s_conv = as_all.reshape(n_seq, s_len, d_ff)[:, s_len - (cw - 1):].reshape(1, n_seq, cw - 1, d_ff)
    return (yp.reshape(1, seq, d), ys.reshape(n_seq, s_len, d),
            *prompt_kv, vp_state.reshape(1, 1, CHUNK, d_a), p_conv,
            *sample_kv, vs_state.reshape(1, n_seq, s_len, d_a), s_conv)
```

```python
import functools
import math

import numpy as np
import jax
import jax.numpy as jnp
from jax import lax
from jax.experimental import pallas as pl
from jax.experimental.pallas import tpu as pltpu

F32 = jnp.float32
BF16 = jnp.bfloat16

HEAD_DIM = 128
STEPS = 128
CHUNK = 128
GROUP_DIM_A = 128
DILATION_GROUPS = ((128, 1), (512, 4), (2048, 16))
MAX_EXACT = 16
MAX_DISTANCE = 2048
EPS = 1e-6
NEG = -1e30
ATTN_SCALE = HEAD_DIM ** -0.5

LANE = 128
SUBLANE = 8
VMEM_LIMIT_BYTES = 56 * 1024 * 1024
MATMUL_TILE = 1024
MXU_WIDTH = 256
FFN_TILE = 512
IN_PROJ_ROUND_TILE = 1024
ATTN_UNITS_PER_STEP = 16
SGU_ROWS_PER_STEP = 512
NORM_COL_CHUNK = 512
ELEMENTWISE_ROWS = 256
KV_STATE_ROWS = 512
FFN_DOWN_K_TILE = 3072
DECODE_COL_TILE = 2048


def _tile(dim, target, align):
    best = None
    t = align
    while t <= min(dim, target):
        if dim % t == 0:
            best = t
        t += align
    return best if best is not None else dim


def _params(semantics):
    return pltpu.CompilerParams(dimension_semantics=semantics, vmem_limit_bytes=VMEM_LIMIT_BYTES)


def _gelu(x):
    return 0.5 * x * (1.0 + jnp.tanh(math.sqrt(2.0 / math.pi) * (x + 0.044715 * (x * x * x))))


def _sigmoid(x):
    return 1.0 / (1.0 + jnp.exp(-x))


def _rms_scale(x):
    return lax.rsqrt(jnp.mean(x * x, axis=-1, keepdims=True) + EPS)


def _col_chunks(d):
    return [slice(c, min(c + NORM_COL_CHUNK, d)) for c in range(0, d, NORM_COL_CHUNK)]


def _rms_scale_ref(ref):
    d = ref.shape[-1]
    ss = None
    for cols in _col_chunks(d):
        x = ref[:, cols].astype(F32)
        part = jnp.sum(x * x, axis=-1, keepdims=True)
        ss = part if ss is None else ss + part
    return lax.rsqrt(ss / d + EPS)


def _lanes(c):
    return slice(c * LANE, (c + 1) * LANE)


def _rmsnorm_cast_kernel(x_ref, g_ref, o_ref):
    scale = _rms_scale_ref(x_ref)
    for cols in _col_chunks(x_ref.shape[-1]):
        o_ref[:, cols] = ((x_ref[:, cols] * scale) * g_ref[:, cols]).astype(o_ref.dtype)


def _rmsnorm_cast(x, g):
    t, d = x.shape
    bt = _tile(t, ELEMENTWISE_ROWS, SUBLANE)
    return pl.pallas_call(
        _rmsnorm_cast_kernel,
        grid=(t // bt,),
        in_specs=[pl.BlockSpec((bt, d), lambda i: (i, 0)), pl.BlockSpec((1, d), lambda i: (0, 0))],
        out_specs=pl.BlockSpec((bt, d), lambda i: (i, 0)),
        out_shape=jax.ShapeDtypeStruct((t, d), BF16),
        compiler_params=_params(("arbitrary",)),
        name="rmsnorm_cast",
    )(x, g.reshape(1, d))


def _in_proj_kernel(x_ref, w_ref, o_ref):
    acc = jnp.dot(x_ref[...], w_ref[...], preferred_element_type=F32)
    for c in range(o_ref.shape[0]):
        o_ref[c] = acc[:, _lanes(c)]


def _in_proj_round_kernel(x_ref, w_in, o_ref, w_out, w_buf, *, slab_out):
    jj, i = pl.program_id(0), pl.program_id(1)
    ck = w_in.shape[0]
    slot = (jj + 1) % 2
    bm = o_ref.shape[-2]
    n_slabs = w_in.shape[1] // LANE
    row_split = 2 if bm % (4 * SUBLANE) == 0 else 1
    hm = bm // row_split
    per_strip = min(n_slabs, MXU_WIDTH // LANE)

    def round_chunk():
        chunk = w_in[...].astype(BF16)
        w_buf[jj % 2, pl.ds(pl.multiple_of(i * ck, ck), ck), :] = chunk
        w_out[...] = chunk

    pl.when(jj == 0)(round_chunk)

    @pl.when(jj > 0)
    def _():
        round_chunk()
        for h in range(row_split):
            x = x_ref[h * hm:(h + 1) * hm, :]
            for c0 in range(0, n_slabs, per_strip):
                c1 = min(c0 + per_strip, n_slabs)
                acc = jnp.dot(x, w_buf[slot, :, c0 * LANE:c1 * LANE], preferred_element_type=F32)
                for c in range(c0, c1):
                    val = acc[:, _lanes(c - c0)]
                    if slab_out:
                        o_ref[c, h * hm:(h + 1) * hm, :] = val
                    else:
                        o_ref[h * hm:(h + 1) * hm, _lanes(c)] = val


def _in_proj_round(h, w, col0, n_cols, name, slab_out=True):
    t, d = h.shape
    bn = _tile(math.gcd(n_cols, col0) if col0 else n_cols, IN_PROJ_ROUND_TILE, LANE)
    bm = _tile(t, MATMUL_TILE, SUBLANE)
    ni, nj = t // bm, n_cols // bn
    assert d % ni == 0
    ck = d // ni

    def fin(jj, i):
        return jnp.maximum(jj - 1, 0), jnp.where(jj == 0, 0, i)

    def chunk(jj, i):
        return jnp.where(jj == nj, ni - 1, i), jnp.minimum(jj, nj - 1)

    if slab_out:
        out_spec = pl.BlockSpec((bn // LANE, bm, LANE), lambda jj, i: fin(jj, i) + (0,))
        out_shape = jax.ShapeDtypeStruct((n_cols // LANE, t, LANE), F32)
    else:
        out_spec = pl.BlockSpec((bm, bn), lambda jj, i: fin(jj, i)[::-1])
        out_shape = jax.ShapeDtypeStruct((t, n_cols), F32)
    return pl.pallas_call(
        functools.partial(_in_proj_round_kernel, slab_out=slab_out),
        grid=(nj + 1, ni),
        in_specs=[pl.BlockSpec((bm, d), lambda jj, i: (fin(jj, i)[1], 0)),
                  pl.BlockSpec((ck, bn), lambda jj, i: (chunk(jj, i)[0], col0 // bn + chunk(jj, i)[1]))],
        out_specs=[out_spec, pl.BlockSpec((ck, bn), chunk)],
        out_shape=[out_shape, jax.ShapeDtypeStruct((d, n_cols), BF16)],
        scratch_shapes=[pltpu.VMEM((2, d, bn), BF16)],
        compiler_params=_params(("arbitrary", "arbitrary")),
        name=name,
    )(h, w)


def _in_proj(h, w, bn, col0, n_cols, name):
    t, d = h.shape
    bm = _tile(t, MATMUL_TILE, SUBLANE)
    return pl.pallas_call(
        _in_proj_kernel,
        grid=(t // bm, n_cols // bn),
        in_specs=[pl.BlockSpec((bm, d), lambda i, j: (i, 0)),
                  pl.BlockSpec((d, bn), lambda i, j: (0, col0 // bn + j))],
        out_specs=pl.BlockSpec((bn // LANE, bm, LANE), lambda i, j: (j, i, 0)),
        out_shape=jax.ShapeDtypeStruct((n_cols // LANE, t, LANE), F32),
        compiler_params=_params(("arbitrary", "arbitrary")),
        name=name,
    )(h, w)


def _sgu_kernel(u_ref, v_ref, lg_ref, lb_ref, w_ref, b_ref, o_ref, vs_ref, *, n_groups):
    c = w_ref.shape[1]
    n_feat = n_groups * GROUP_DIM_A
    for r0 in range(0, o_ref.shape[0], c):
        vp = _gelu(v_ref[:, r0:r0 + c, :])
        mu = jnp.sum(jnp.sum(vp, axis=0), axis=-1, keepdims=True) / n_feat
        vc = vp - mu
        var = jnp.sum(jnp.sum(vc * vc, axis=0), axis=-1, keepdims=True) / n_feat
        v = vc * lax.rsqrt(var + EPS) * lg_ref[...] + lb_ref[...]
        for g in range(n_groups):
            if r0 + c == o_ref.shape[0]:
                vs_ref[:, _lanes(g)] = v[g]
            mixed = jnp.dot(w_ref[g], v[g].astype(BF16), preferred_element_type=F32) + b_ref[g]
            o_ref[r0:r0 + c, _lanes(g)] = (_gelu(u_ref[g, r0:r0 + c, :]) * mixed).astype(o_ref.dtype)


def _sgu(z_sl, d_a, ln_g, ln_b, w_mix, b_mix):
    t = z_sl.shape[1]
    n_groups, c, _ = w_mix.shape
    rows = _tile(t, SGU_ROWS_PER_STEP, c)
    slab = pl.BlockSpec((n_groups, 1, LANE), lambda i: (0, 0, 0))
    return pl.pallas_call(
        functools.partial(_sgu_kernel, n_groups=n_groups),
        grid=(t // rows,),
        in_specs=[
            pl.BlockSpec((n_groups, rows, LANE), lambda i: (0, i, 0)),
            pl.BlockSpec((n_groups, rows, LANE), lambda i: (1, i, 0)),
            slab, slab,
            pl.BlockSpec((n_groups, c, c), lambda i: (0, 0, 0)),
            pl.BlockSpec((n_groups, c, LANE), lambda i: (0, 0, 0)),
        ],
        out_specs=[pl.BlockSpec((rows, d_a), lambda i: (i, 0)), pl.BlockSpec((c, d_a), lambda i: (0, 0))],
        out_shape=[jax.ShapeDtypeStruct((t, d_a), BF16), jax.ShapeDtypeStruct((c, d_a), F32)],
        compiler_params=_params(("arbitrary",)),
        name="sgu",
    )(z_sl, z_sl, ln_g.reshape(n_groups, 1, LANE), ln_b.reshape(n_groups, 1, LANE), w_mix, b_mix)


def _t5_bucket(dist, n_buckets):
    n = np.asarray(dist, np.int32)
    safe = np.maximum(n, 1).astype(np.float32)
    large = MAX_EXACT + (np.log(safe / MAX_EXACT) / np.log(np.float32(MAX_DISTANCE / MAX_EXACT))
                         * (n_buckets - MAX_EXACT)).astype(np.int32)
    large = np.minimum(large, n_buckets - 1)
    return np.where(n < MAX_EXACT, n, large).astype(np.int32)


def _bias_lookup(bias_tab, bucket):
    n_buckets = bias_tab.shape[0]
    flat = np.asarray(bucket).reshape(-1)
    onehot = (jnp.asarray(flat)[None, :] == jnp.arange(n_buckets)[:, None]).astype(F32)
    out = jnp.dot(bias_tab.T, onehot, precision=lax.Precision.HIGHEST)
    return out.reshape((bias_tab.shape[1],) + tuple(np.asarray(bucket).shape))


def _merge_by_lse(outs, lses):
    m = functools.reduce(jnp.maximum, lses)
    w = [jnp.exp(l - m) for l in lses]
    num = functools.reduce(lambda x, y: x + y, [wi * oi for wi, oi in zip(w, outs)])
    return num / functools.reduce(lambda x, y: x + y, w)


def _attn_prompt_kernel(*refs, hb, dil, n_other):
    q_ref, kc_ref, vc_ref, bias_ref, band_ref = refs[:5]
    others = refs[5:5 + 2 * n_other]
    n_out = 1 if n_other else 2
    outs = refs[5 + 2 * n_other:5 + 2 * n_other + n_out]
    kp_ref, vp_ref = refs[5 + 2 * n_other + n_out:][:2]
    o_acc, lse_acc = refs[5 + 2 * n_other + n_out + 2:] if n_other else outs
    b = pl.program_id(1)

    @pl.when(b == 0)
    def _():
        kp_ref[...] = jnp.zeros(kp_ref.shape, F32)
        vp_ref[...] = jnp.zeros(vp_ref.shape, F32)

    span = STEPS * dil
    n_pb = q_ref.shape[1] // span
    col = lax.broadcasted_iota(jnp.int32, (STEPS, 2 * STEPS), 1)
    band = band_ref[...] > 0.5
    first = band & ((b > 0) | (col >= STEPS))
    for hh in range(hb):
        for pb in range(n_pb):
            for r in range(dil):
                def rows(block):
                    return pl.ds(block * span + r, STEPS, stride=dil) if dil > 1 else pl.ds(block * span, STEPS)
                k_prev = kp_ref[hh, rows(0), :] if pb == 0 else kc_ref[hh, rows(pb - 1), :]
                v_prev = vp_ref[hh, rows(0), :] if pb == 0 else vc_ref[hh, rows(pb - 1), :]
                q = q_ref[hh, rows(pb), :].astype(BF16)
                kk = jnp.concatenate([k_prev, kc_ref[hh, rows(pb), :]], axis=0).astype(BF16)
                vv = jnp.concatenate([v_prev, vc_ref[hh, rows(pb), :]], axis=0).astype(BF16)
                s = lax.dot_general(q, kk, (((1,), (1,)), ((), ())), preferred_element_type=F32) * ATTN_SCALE
                s = jnp.where(first if pb == 0 else band, s + bias_ref[hh], NEG)
                m = jnp.max(s, axis=-1, keepdims=True)
                e = jnp.exp(s - m)
                den = jnp.sum(e, axis=-1, keepdims=True)
                o_acc[hh, rows(pb), :] = jnp.dot(e.astype(BF16), vv, preferred_element_type=F32) / den
                lse_acc[hh, rows(pb), :] = jnp.broadcast_to(m + jnp.log(den), (STEPS, HEAD_DIM))
    kp_ref[...] = kc_ref[:, (n_pb - 1) * span:, :]
    vp_ref[...] = vc_ref[:, (n_pb - 1) * span:, :]
    if n_other:
        for hh in range(hb):
            merged = _merge_by_lse([r[hh] for r in others[:n_other]] + [o_acc[hh]],
                                   [r[hh] for r in others[n_other:]] + [lse_acc[hh]])
            outs[0][:, _lanes(hh)] = merged.astype(outs[0].dtype)


def _attn_prompt(z_sl, hpg, q_slab, k_slab, v_slab, bias_tab, dil, merge_with=None):
    t = z_sl.shape[1]
    span = STEPS * dil
    assert t % span == 0
    hb = min(hpg, max(1, ATTN_UNITS_PER_STEP // dil))
    assert hpg % hb == 0 and q_slab % hb == 0 and k_slab % hb == 0 and v_slab % hb == 0
    n_pb = math.gcd(t // span, max(1, ATTN_UNITS_PER_STEP // (hb * dil)))
    rows = span * n_pb
    nb = t // rows

    p_idx = np.arange(STEPS)[:, None]
    c_idx = np.arange(2 * STEPS)[None, :]
    steps = p_idx + STEPS - c_idx
    band = ((steps >= 0) & (steps <= STEPS)).astype(np.float32)
    bias = _bias_lookup(bias_tab, _t5_bucket(np.clip(steps, 0, STEPS) * dil, bias_tab.shape[0]))

    def cur(slab):
        return pl.BlockSpec((hb, rows, LANE), lambda hi, b: (slab // hb + hi, b, 0))

    head_major = pl.BlockSpec((hb, rows, LANE), lambda hi, b: (hi, b, 0))
    prev_block = pltpu.VMEM((hb, span, LANE), F32)
    block = pltpu.VMEM((hb, rows, LANE), F32)
    others = [] if merge_with is None else list(merge_with[0]) + list(merge_with[1])
    if merge_with is None:
        out_specs = [head_major, head_major]
        out_shape = [jax.ShapeDtypeStruct((hpg, t, LANE), F32)] * 2
        scratch = [prev_block, prev_block]
    else:
        out_specs = [pl.BlockSpec((rows, hb * LANE), lambda hi, b: (b, hi))]
        out_shape = [jax.ShapeDtypeStruct((t, hpg * LANE), BF16)]
        scratch = [prev_block, prev_block, block, block]
    res = pl.pallas_call(
        functools.partial(_attn_prompt_kernel, hb=hb, dil=dil, n_other=len(others) // 2),
        grid=(hpg // hb, nb),
        in_specs=[cur(q_slab), cur(k_slab), cur(v_slab),
                  pl.BlockSpec((hb, STEPS, 2 * STEPS), lambda hi, b: (hi, 0, 0)),
                  pl.BlockSpec((STEPS, 2 * STEPS), lambda hi, b: (0, 0))] + [head_major] * len(others),
        out_specs=out_specs,
        out_shape=out_shape,
        scratch_shapes=scratch,
        compiler_params=_params(("arbitrary", "arbitrary")),
        name=f"attn_prompt_d{dil}",
    )(z_sl, z_sl, z_sl, bias, jnp.asarray(band), *others)
    return res[0] if merge_with is not None else res


def _attn_sample_kernel(q_ref, kn_ref, vn_ref, ck_ref, cv_ref, bc_ref, mc_ref, bnew_ref, mnew_ref,
                        o_ref, lse_ref, *, hpg, s_len, grouped):
    n_keys = mc_ref.shape[1]
    n_new = hpg * s_len
    valid_c = mc_ref[...] > 0.5

    def cache_head(ref, h):
        if grouped:
            return ref[0, :, pl.ds(h, s_len, stride=hpg), :].reshape(n_keys, HEAD_DIM).astype(BF16)
        return ref[0, pl.ds(h, n_keys, stride=hpg) if hpg > 1 else pl.ds(0, n_keys), :].astype(BF16)

    q_all = q_ref[...].reshape(n_new, HEAD_DIM).astype(BF16)
    s_new = lax.dot_general(q_all, kn_ref[...].reshape(n_new, HEAD_DIM).astype(BF16), (((1,), (1,)), ((), ())),
                            preferred_element_type=F32) * ATTN_SCALE
    key_head = lax.broadcasted_iota(jnp.int32, (s_len, n_new), 1) // s_len
    new_ok = mnew_ref[...] > 0.5

    partial, e_new = [], []
    for h in range(hpg):
        q = q_all[h * s_len:(h + 1) * s_len, :]
        sc = lax.dot_general(q, cache_head(ck_ref, h), (((1,), (1,)), ((), ())),
                             preferred_element_type=F32) * ATTN_SCALE
        sc = jnp.where(valid_c, sc + bc_ref[h], NEG)
        sn = jnp.where(new_ok & (key_head == h), s_new[h * s_len:(h + 1) * s_len, :] + bnew_ref[h], NEG)
        m = jnp.maximum(jnp.max(sc, axis=-1, keepdims=True), jnp.max(sn, axis=-1, keepdims=True))
        ec = jnp.exp(sc - m)
        en = jnp.exp(sn - m)
        den = jnp.sum(ec, axis=-1, keepdims=True) + jnp.sum(en, axis=-1, keepdims=True)
        acc = jnp.dot(ec.astype(BF16), cache_head(cv_ref, h), preferred_element_type=F32)
        partial.append((acc, den, m))
        e_new.append(en)
    o_new = jnp.dot(jnp.concatenate(e_new, axis=0).astype(BF16),
                    vn_ref[...].reshape(n_new, HEAD_DIM).astype(BF16), preferred_element_type=F32)
    for h, (acc, den, m) in enumerate(partial):
        o_ref[h] = (acc + o_new[h * s_len:(h + 1) * s_len, :]) / den
        lse_ref[h] = jnp.broadcast_to(m + jnp.log(den), (s_len, HEAD_DIM))


def _attn_sample(z_sl, s_len, hpg, q_slab, k_slab, v_slab, cache_k, cache_v, bias_tab, dil):
    t = z_sl.shape[1]
    n_seq = t // s_len
    lc = cache_k.shape[1]
    assert q_slab % hpg == 0 and k_slab % hpg == 0 and v_slab % hpg == 0

    j = np.arange(STEPS + 1)
    idx = lc + np.arange(s_len)[:, None] - j[None, :] * dil
    assert idx.min() >= 0
    bucket = _t5_bucket(j * dil, bias_tab.shape[0])
    mask = np.zeros((s_len, lc + s_len), np.float32)
    bsel = np.zeros((s_len, lc + s_len), np.int32)
    for s in range(s_len):
        mask[s, idx[s]] = 1.0
        bsel[s, idx[s]] = bucket

    grouped = dil > s_len and lc % dil == 0 and s_len == SUBLANE
    if grouped:
        pos = (np.arange(lc // dil)[:, None] * dil + np.arange(s_len)[None, :]).reshape(-1)
        assert mask[:, :lc].sum() == mask[:, pos].sum()
        ck = cache_k.reshape(n_seq, lc // dil, dil * hpg, HEAD_DIM)
        cv = cache_v.reshape(n_seq, lc // dil, dil * hpg, HEAD_DIM)
        cache_spec = pl.BlockSpec((1, lc // dil, s_len * hpg, LANE), lambda b: (b, 0, 0, 0))
    else:
        pos = np.arange(lc)
        ck = cache_k.reshape(n_seq, lc * hpg, HEAD_DIM)
        cv = cache_v.reshape(n_seq, lc * hpg, HEAD_DIM)
        cache_spec = pl.BlockSpec((1, lc * hpg, LANE), lambda b: (b, 0, 0))
    n_keys = len(pos)
    bias_c = _bias_lookup(bias_tab, bsel[:, pos])
    bias_new = jnp.tile(_bias_lookup(bias_tab, bsel[:, lc:]), (1, 1, hpg))
    mask_c = jnp.asarray(mask[:, pos])
    mask_new = jnp.asarray(np.tile(mask[:, lc:], (1, hpg)))

    out_spec = pl.BlockSpec((hpg, s_len, LANE), lambda b: (0, b, 0))
    return pl.pallas_call(
        functools.partial(_attn_sample_kernel, hpg=hpg, s_len=s_len, grouped=grouped),
        grid=(n_seq,),
        in_specs=[pl.BlockSpec((hpg, s_len, LANE), lambda b: (q_slab // hpg, b, 0)),
                  pl.BlockSpec((hpg, s_len, LANE), lambda b: (k_slab // hpg, b, 0)),
                  pl.BlockSpec((hpg, s_len, LANE), lambda b: (v_slab // hpg, b, 0)),
                  cache_spec, cache_spec,
                  pl.BlockSpec((hpg, s_len, n_keys), lambda b: (0, 0, 0)),
                  pl.BlockSpec((s_len, n_keys), lambda b: (0, 0)),
                  pl.BlockSpec((hpg, s_len, hpg * s_len), lambda b: (0, 0, 0)),
                  pl.BlockSpec((s_len, hpg * s_len), lambda b: (0, 0))],
        out_specs=[out_spec, out_spec],
        out_shape=[jax.ShapeDtypeStruct((hpg, t, LANE), F32)] * 2,
        compiler_params=_params(("arbitrary",)),
        name=f"attn_sample_d{dil}",
    )(z_sl, z_sl, z_sl, ck, cv, bias_c, mask_c, bias_new, mask_new)


def _merge_kernel(o1, o2, o3, l1, l2, l3, out_ref):
    merged = _merge_by_lse([o1[...], o2[...], o3[...]], [l1[...], l2[...], l3[...]])
    for h in range(merged.shape[0]):
        out_ref[:, _lanes(h)] = merged[h].astype(out_ref.dtype)


def _merge(outs, lses):
    hpg, t, _ = outs[0].shape
    bt = _tile(t, ELEMENTWISE_ROWS, SUBLANE)
    spec = pl.BlockSpec((hpg, bt, LANE), lambda i: (0, i, 0))
    return pl.pallas_call(
        _merge_kernel,
        grid=(t // bt,),
        in_specs=[spec] * 6,
        out_specs=pl.BlockSpec((bt, hpg * LANE), lambda i: (i, 0)),
        out_shape=jax.ShapeDtypeStruct((t, hpg * LANE), BF16),
        compiler_params=_params(("arbitrary",)),
        name="merge_groups",
    )(*outs, *lses)


def _gated_proj_kernel(a_ref, b_ref, wa_ref, wb_ref, ga_ref, gb_ref, o_ref):
    n_slabs = ga_ref.shape[0]
    per_strip = min(n_slabs, MXU_WIDTH // LANE)
    for c0 in range(0, n_slabs, per_strip):
        c1 = min(c0 + per_strip, n_slabs)
        cols = slice(c0 * LANE, c1 * LANE)
        pa = jnp.dot(a_ref[...], wa_ref[:, cols], preferred_element_type=F32)
        pb = jnp.dot(b_ref[...], wb_ref[:, cols], preferred_element_type=F32)
        for c in range(c0, c1):
            gated = _sigmoid(ga_ref[c]) * pa[:, _lanes(c - c0)] + _sigmoid(gb_ref[c]) * pb[:, _lanes(c - c0)]
            o_ref[:, _lanes(c)] = gated.astype(o_ref.dtype)


def _gated_proj(o_a, o_b, w_a, w_b, z_sl, bn, gate_a_blk, gate_b_blk):
    t, d_a = o_a.shape
    d_b = o_b.shape[1]
    d = w_a.shape[1]
    bm = _tile(t, MATMUL_TILE, SUBLANE)
    return pl.pallas_call(
        _gated_proj_kernel,
        grid=(t // bm, d // bn),
        in_specs=[pl.BlockSpec((bm, d_a), lambda i, j: (i, 0)),
                  pl.BlockSpec((bm, d_b), lambda i, j: (i, 0)),
                  pl.BlockSpec((d_a, bn), lambda i, j: (0, j)),
                  pl.BlockSpec((d_b, bn), lambda i, j: (0, j)),
                  pl.BlockSpec((bn // LANE, bm, LANE), lambda i, j: (gate_a_blk + j, i, 0)),
                  pl.BlockSpec((bn // LANE, bm, LANE), lambda i, j: (gate_b_blk + j, i, 0))],
        out_specs=pl.BlockSpec((bm, bn), lambda i, j: (i, j)),
        out_shape=jax.ShapeDtypeStruct((t, d), BF16),
        compiler_params=_params(("arbitrary", "arbitrary")),
        name="gated_proj",
    )(o_a, o_b, w_a, w_b, z_sl, z_sl)


def _matmul_kernel(x_ref, w_ref, o_ref, *, nk):
    if nk == 1:
        o_ref[...] = jnp.dot(x_ref[...], w_ref[...], preferred_element_type=F32)
    else:
        @pl.when(pl.program_id(2) == 0)
        def _():
            o_ref[...] = jnp.zeros(o_ref.shape, o_ref.dtype)

        o_ref[...] += jnp.dot(x_ref[...], w_ref[...], preferred_element_type=F32)


def _matmul(x, w, name, k_tile=None):
    t, kd = x.shape
    n = w.shape[1]
    bm = _tile(t, MATMUL_TILE, SUBLANE)
    bn = _tile(n, MATMUL_TILE, LANE)
    bk = kd if k_tile is None else k_tile
    nk = kd // bk
    return pl.pallas_call(
        functools.partial(_matmul_kernel, nk=nk),
        grid=(t // bm, n // bn, nk),
        in_specs=[pl.BlockSpec((bm, bk), lambda i, j, k: (i, k)),
                  pl.BlockSpec((bk, bn), lambda i, j, k: (k, j))],
        out_specs=pl.BlockSpec((bm, bn), lambda i, j, k: (i, j)),
        out_shape=jax.ShapeDtypeStruct((t, n), F32),
        compiler_params=_params(("arbitrary", "arbitrary", "arbitrary")),
        name=name,
    )(x, w)


def _post_mix_kernel(x_ref, y_ref, g1_ref, g2_ref, x1_ref, h2_ref):
    chunks = _col_chunks(x_ref.shape[-1])
    scale_y = _rms_scale_ref(y_ref)
    for cols in chunks:
        x1_ref[:, cols] = x_ref[:, cols] + (y_ref[:, cols] * scale_y) * g1_ref[:, cols]
    scale_x1 = _rms_scale_ref(x1_ref)
    for cols in chunks:
        h2_ref[:, cols] = ((x1_ref[:, cols] * scale_x1) * g2_ref[:, cols]).astype(h2_ref.dtype)


def _post_mix(x, y, g_post, g_pre):
    t, d = x.shape
    bt = _tile(t, ELEMENTWISE_ROWS, SUBLANE)
    row = pl.BlockSpec((bt, d), lambda i: (i, 0))
    vec = pl.BlockSpec((1, d), lambda i: (0, 0))
    return pl.pallas_call(
        _post_mix_kernel,
        grid=(t // bt,),
        in_specs=[row, row, vec, vec],
        out_specs=[row, row],
        out_shape=[jax.ShapeDtypeStruct((t, d), F32), jax.ShapeDtypeStruct((t, d), BF16)],
        compiler_params=_params(("arbitrary",)),
        name="post_mix",
    )(x, y, g_post.reshape(1, d), g_pre.reshape(1, d))


def _residual_norm_kernel(x_ref, y_ref, g_ref, o_ref):
    scale = _rms_scale_ref(y_ref)
    for cols in _col_chunks(x_ref.shape[-1]):
        o_ref[:, cols] = x_ref[:, cols] + (y_ref[:, cols] * scale) * g_ref[:, cols]


def _residual_norm(x, y, g):
    t, d = x.shape
    bt = _tile(t, ELEMENTWISE_ROWS, SUBLANE)
    row = pl.BlockSpec((bt, d), lambda i: (i, 0))
    return pl.pallas_call(
        _residual_norm_kernel,
        grid=(t // bt,),
        in_specs=[row, row, pl.BlockSpec((1, d), lambda i: (0, 0))],
        out_specs=row,
        out_shape=jax.ShapeDtypeStruct((t, d), F32),
        compiler_params=_params(("arbitrary",)),
        name="residual_norm",
    )(x, y, g.reshape(1, d))


def _ffn_up_kernel(*refs, conv_w, s_len, d_ff):
    if s_len is None:
        (x_ref, wg_in, wu_in, wd_in, cw_ref, cb_ref, out_ref, tail_ref, wg_out, wu_out, wd_out,
         wg_buf, wu_buf, a_buf, u_buf, o_buf) = refs
        hist_refs = None
        jj, i = pl.program_id(0), pl.program_id(1)
        ck = wg_in.shape[0]

        def round_chunks():
            rows_c = pl.ds(pl.multiple_of(i * ck, ck), ck)
            for src, buf, dst in ((wg_in, wg_buf, wg_out), (wu_in, wu_buf, wu_out)):
                chunk = src[...].astype(BF16)
                buf[jj % 2, rows_c, :] = chunk
                dst[...] = chunk

        def round_down_rows():
            rc = wd_in.shape[0]
            row = ((jj - 1) * pl.num_programs(1) + i) * rc + lax.broadcasted_iota(jnp.int32, wd_in.shape, 0)
            wd_out[...] = jnp.where(row < d_ff, wd_in[...], 0.0).astype(BF16)

        slot = (jj + 1) % 2
        wg = lambda cols: wg_buf[slot, :, cols]
        wu = lambda cols: wu_buf[slot, :, cols]
        col_blk = jj - 1
        row_split = 2 if out_ref.shape[0] % (4 * SUBLANE) == 0 else 1
    else:
        x_ref, wg_ref, wu_ref, cw_ref, cb_ref, h1_ref, h2_ref, out_ref, tail_ref, a_buf, u_buf, o_buf = refs
        hist_refs = {1: h1_ref, 2: h2_ref}
        i = pl.program_id(1)
        wg = lambda cols: wg_ref[:, cols]
        wu = lambda cols: wu_ref[:, cols]
        col_blk = pl.program_id(0)
        row_split = 1
    bm, bn = out_ref.shape
    hm = bm // row_split
    half = hm // 2
    top = SUBLANE
    n_slabs = bn // LANE
    per_strip = min(n_slabs, MXU_WIDTH // LANE)

    def units():
        @pl.when(i == 0)
        def _():
            a_buf[:, 0:top, :] = jnp.zeros((n_slabs, top, LANE), F32)

        lane = lax.broadcasted_iota(jnp.int32, (1, LANE), 1)
        for h in range(row_split):
            r0 = h * hm
            x = x_ref[r0:r0 + hm, :]
            for c0 in range(0, n_slabs, per_strip):
                c1 = min(c0 + per_strip, n_slabs)
                cols = slice(c0 * LANE, c1 * LANE)
                a = jnp.dot(x, wg(cols), preferred_element_type=F32)
                u = jnp.dot(x, wu(cols), preferred_element_type=F32)
                if s_len is None and h == 0 and c0 == 0:
                    round_chunks()
                    round_down_rows()
                if s_len is not None:
                    tail_ref[r0:r0 + hm, cols] = a
                elif h == row_split - 1:
                    tail_ref[:, cols] = a[hm - SUBLANE:, :]
                for c in range(c0, c1):
                    a_buf[c, top + r0:top + r0 + hm, :] = a[:, _lanes(c - c0)]
                    u_buf[c, r0:r0 + hm, :] = u[:, _lanes(c - c0)]
                    in_range = (col_blk * bn + c * LANE + lane) < d_ff
                    for e in range(2):
                        taps = [a_buf[c, pl.ds(top + r0 + e - lag, half, stride=2), :] for lag in range(conv_w)]
                        if hist_refs is not None:
                            s = 2 * (lax.broadcasted_iota(jnp.int32, (half, LANE), 0) % (s_len // 2)) + e
                            for lag in range(1, conv_w):
                                hist = hist_refs[lag][c, pl.ds(r0 + e, half, stride=2), :]
                                taps[lag] = jnp.where(s < lag, hist, taps[lag])
                        acc = cw_ref[0:1, _lanes(c)] * taps[conv_w - 1]
                        for k in range(1, conv_w):
                            acc = acc + cw_ref[k:k + 1, _lanes(c)] * taps[conv_w - 1 - k]
                        acc = cb_ref[:, _lanes(c)] + acc
                        val = _gelu(acc) * u_buf[c, pl.ds(r0 + e, half, stride=2), :]
                        o_buf[c, pl.ds(r0 + e, half, stride=2), :] = jnp.where(in_range, val, 0.0)
                    out_ref[r0:r0 + hm, _lanes(c)] = o_buf[c, r0:r0 + hm, :].astype(out_ref.dtype)
        if s_len is None:
            a_buf[:, 0:top, :] = a_buf[:, bm:bm + top, :]

    if s_len is None:
        pl.when(jj == 0)(round_chunks)
        pl.when(jj > 0)(units)
    else:
        units()


def _ffn_up_prompt(h, w_gate, w_up, w_down, conv_w, conv_b, ffp):
    t, d = h.shape
    d_ff = w_gate.shape[1]
    cw = conv_w.shape[0]
    assert cw == 3 and d_ff % LANE == 0
    bn = FFN_TILE
    assert ffp % bn == 0 and ffp - d_ff < bn
    bm = _tile(t, MATMUL_TILE, 4 * SUBLANE)
    ni, nj = t // bm, ffp // bn
    assert d % ni == 0
    ck = d // ni
    n_slabs = bn // LANE

    def fin_i(jj, i):
        return jnp.where(jj == 0, 0, i)

    def fin_j(jj):
        return jnp.maximum(jj - 1, 0)

    def chunk(jj, i):
        return jnp.where(jj == nj, ni - 1, i), jnp.minimum(jj, nj - 1)

    n_out = w_down.shape[1]
    assert ffp % (nj * ni) == 0
    rc = ffp // (nj * ni)
    assert rc % (2 * SUBLANE) == 0

    def down_blk(jj, i):
        return fin_j(jj) * ni + fin_i(jj, i)

    last_src = -(-d_ff // rc) - 1
    w_in_spec = pl.BlockSpec((ck, bn), chunk)
    wbuf = pltpu.VMEM((2, d, bn), BF16)
    return pl.pallas_call(
        functools.partial(_ffn_up_kernel, conv_w=cw, s_len=None, d_ff=d_ff),
        grid=(nj + 1, ni),
        in_specs=[pl.BlockSpec((bm, d), lambda jj, i: (fin_i(jj, i), 0)), w_in_spec, w_in_spec,
                  pl.BlockSpec((rc, n_out), lambda jj, i: (jnp.minimum(down_blk(jj, i), last_src), 0)),
                  pl.BlockSpec((cw, bn), lambda jj, i: (0, fin_j(jj))),
                  pl.BlockSpec((1, bn), lambda jj, i: (0, fin_j(jj)))],
        out_specs=[pl.BlockSpec((bm, bn), lambda jj, i: (fin_i(jj, i), fin_j(jj))),
                   pl.BlockSpec((SUBLANE, bn), lambda jj, i: (0, fin_j(jj))),
                   w_in_spec, w_in_spec,
                   pl.BlockSpec((rc, n_out), lambda jj, i: (down_blk(jj, i), 0))],
        out_shape=[jax.ShapeDtypeStruct((t, ffp), BF16), jax.ShapeDtypeStruct((SUBLANE, d_ff), F32),
                   jax.ShapeDtypeStruct((d, d_ff), BF16), jax.ShapeDtypeStruct((d, d_ff), BF16),
                   jax.ShapeDtypeStruct((ffp, n_out), BF16)],
        scratch_shapes=[wbuf, wbuf,
                        pltpu.VMEM((n_slabs, bm + SUBLANE, LANE), F32), pltpu.VMEM((n_slabs, bm, LANE), F32),
                        pltpu.VMEM((n_slabs, bm, LANE), F32)],
        compiler_params=_params(("arbitrary", "arbitrary")),
        name="ffn_up",
    )(h, w_gate, w_up, w_down, conv_w, conv_b.reshape(1, d_ff))


def _ffn_up_sample(h, w_gate, w_up, conv_w, conv_b, ffp, hist, s_len):
    t, d = h.shape
    d_ff = w_gate.shape[1]
    cw = conv_w.shape[0]
    assert cw == 3 and d_ff % LANE == 0 and s_len % 2 == 0
    bn = _tile(ffp, MATMUL_TILE, LANE)
    assert ffp - d_ff < bn
    n_slabs = bn // LANE
    w_spec = pl.BlockSpec((d, bn), lambda j, i: (0, j))
    out_spec = pl.BlockSpec((t, bn), lambda j, i: (i, j))

    def slabs(rows):
        n_seq, r, _ = rows.shape
        sl = jnp.transpose(rows.reshape(n_seq, r, d_ff // LANE, LANE), (2, 0, 1, 3))
        return jnp.pad(sl, ((0, 0), (0, 0), (0, s_len - r), (0, 0))).reshape(d_ff // LANE, t, LANE)
    h1 = slabs(hist[:, 1:2])
    h2 = slabs(hist)
    hist_spec = pl.BlockSpec((n_slabs, t, LANE), lambda j, i: (j, i, 0))
    return pl.pallas_call(
        functools.partial(_ffn_up_kernel, conv_w=cw, s_len=s_len, d_ff=d_ff),
        grid=(ffp // bn, 1),
        in_specs=[pl.BlockSpec((t, d), lambda j, i: (i, 0)), w_spec, w_spec,
                  pl.BlockSpec((cw, bn), lambda j, i: (0, j)), pl.BlockSpec((1, bn), lambda j, i: (0, j)),
                  hist_spec, hist_spec],
        out_specs=[out_spec, out_spec],
        out_shape=[jax.ShapeDtypeStruct((t, ffp), BF16), jax.ShapeDtypeStruct((t, d_ff), F32)],
        scratch_shapes=[pltpu.VMEM((n_slabs, t + SUBLANE, LANE), F32), pltpu.VMEM((n_slabs, t, LANE), F32),
                        pltpu.VMEM((n_slabs, t, LANE), F32)],
        compiler_params=_params(("arbitrary", "arbitrary")),
        name="ffn_up_decode",
    )(h, w_gate, w_up, conv_w, conv_b.reshape(1, d_ff), h1, h2)


def _kv_state_rows_kernel(k_ref, v_ref, ko_ref, vo_ref, *, hpg):
    n = k_ref.shape[1]
    for src, dst in ((k_ref, ko_ref), (v_ref, vo_ref)):
        for h in range(hpg):
            dst[pl.ds(h, n, stride=hpg) if hpg > 1 else pl.ds(0, n), :] = src[h]


def _kv_state_rows(z_sl, k_slab, v_slab, hpg, row0, n_rows):
    br = _tile(n_rows, KV_STATE_ROWS, SUBLANE)
    assert row0 % br == 0 and k_slab % hpg == 0 and v_slab % hpg == 0
    out_spec = pl.BlockSpec((br * hpg, LANE), lambda i: (i, 0))
    return pl.pallas_call(
        functools.partial(_kv_state_rows_kernel, hpg=hpg),
        grid=(n_rows // br,),
        in_specs=[pl.BlockSpec((hpg, br, LANE), lambda i: (k_slab // hpg, row0 // br + i, 0)),
                  pl.BlockSpec((hpg, br, LANE), lambda i: (v_slab // hpg, row0 // br + i, 0))],
        out_specs=[out_spec, out_spec],
        out_shape=[jax.ShapeDtypeStruct((n_rows * hpg, LANE), F32)] * 2,
        compiler_params=_params(("arbitrary",)),
        name="kv_state_rows",
    )(z_sl, z_sl)


def _layer(x, p, s_len=None, caches=None, conv_hist=None):
    t, d = x.shape
    d_a = p["ln_g"].shape[0]
    hpg = p["hpg"]
    gw = hpg * HEAD_DIM
    bn = p["bn"]
    o1 = 2 * d_a
    d_qkv = 3 * gw

    h = _rmsnorm_cast(x, p["g_pre_mix"])
    w16 = dict(p.get("w16", {}))
    d_in = o1 + 3 * d_qkv + 2 * d
    if s_len is None:
        qkv_sl, w16["in_proj"] = _in_proj_round(h, p["w_in"], 0, d_in, "in_proj")
    else:
        qkv_sl = _in_proj(h, w16["in_proj"], _tile(d_in, DECODE_COL_TILE, LANE), 0, d_in, "in_proj_decode")

    if s_len is None:
        w_mix, b_mix = p["w_mix_prompt"], p["b_mix_prompt"]
    else:
        w_mix, b_mix = p["w_mix_sample"], p["b_mix_sample"]
    o_a, v_state = _sgu(qkv_sl, d_a, p["ln_g"], p["ln_b"], w_mix, b_mix)

    outs, lses = [], []
    for gi, (_, dil) in enumerate(DILATION_GROUPS):
        q_slab = o1 // LANE + gi * hpg
        k_slab = (o1 + d_qkv) // LANE + gi * hpg
        v_slab = (o1 + 2 * d_qkv) // LANE + gi * hpg
        bias_tab = p["rel_bias"][:, gi * hpg:(gi + 1) * hpg]
        if s_len is not None:
            o, lse = _attn_sample(qkv_sl, s_len, hpg, q_slab, k_slab, v_slab, caches[2 * gi], caches[2 * gi + 1],
                                  bias_tab, dil)
        elif gi < len(DILATION_GROUPS) - 1:
            o, lse = _attn_prompt(qkv_sl, hpg, q_slab, k_slab, v_slab, bias_tab, dil)
        else:
            o_b = _attn_prompt(qkv_sl, hpg, q_slab, k_slab, v_slab, bias_tab, dil, merge_with=(outs, lses))
            break
        outs.append(o)
        lses.append(lse)
    if s_len is not None:
        o_b = _merge(outs, lses)

    gate_a_blk = (o1 + 3 * d_qkv) // bn
    merged = _gated_proj(o_a, o_b, p["w_proj_a"], p["w_proj_b"], qkv_sl, bn, gate_a_blk, gate_a_blk + d // bn)
    if s_len is None:
        y, w16["out_proj"] = _in_proj_round(merged, p["w_out"], 0, d, "out_proj", slab_out=False)
    else:
        y = _matmul(merged, w16["out_proj"], "out_proj_decode")
    x1, h2 = _post_mix(x, y, p["g_post_mix"], p["g_pre_ffn"])

    if s_len is None:
        act, a_tail, w16["ffn_gate"], w16["ffn_up"], w16["ffn_down"] = _ffn_up_prompt(
            h2, p["w_gate"], p["w_up"], p["w_down"], p["conv_w"], p["conv_b"], p["ffp"])
    else:
        act, a_tail = _ffn_up_sample(h2, w16["ffn_gate"], w16["ffn_up"], p["conv_w"], p["conv_b"], p["ffp"],
                                     conv_hist, s_len)
    f = _matmul(act, w16["ffn_down"], "ffn_down", k_tile=p["ffn_k_tile"])
    y_out = _residual_norm(x1, f, p["g_post_ffn"])
    return y_out, qkv_sl, v_state, a_tail, w16


def kernel(x_prompt, x_sample, cache_k_g1, cache_v_g1, cache_k_g2, cache_v_g2, cache_k_g3, cache_v_g3, state_conv, g_pre_mix, w_in, sgu_ln_g, sgu_ln_b, w_spatial, b_spatial, rel_bias, w_proj_a, w_proj_b, w_out, g_post_mix, g_pre_ffn, w_gate, w_up, conv_w, conv_b, w_down, g_post_ffn):
    depth = w_in.shape[0]
    assert depth == 1
    n_prompt, seq, d = x_prompt.shape
    assert n_prompt == 1 and seq % CHUNK == 0
    n_seq, s_len, _ = x_sample.shape
    assert s_len == SUBLANE
    d_a = sgu_ln_g.shape[1]
    n_groups = w_spatial.shape[1]
    n_heads = rel_bias.shape[1]
    hpg = n_heads // len(DILATION_GROUPS)
    gw = hpg * HEAD_DIM
    d_ff = w_gate.shape[2]
    cw = conv_w.shape[1]
    bn = _tile(math.gcd(2 * d_a, gw), MATMUL_TILE, LANE)
    ffp = -(-d_ff // FFN_TILE) * FFN_TILE
    t_s = n_seq * s_len

    tri = np.tril(np.ones((CHUNK, CHUNK), np.float32))
    w_mix_prompt = (w_spatial[0] * tri).astype(BF16)
    b_mix_prompt = jnp.broadcast_to(b_spatial[0][:, :, None], (n_groups, CHUNK, LANE))
    w_small = w_spatial[0][:, :s_len, :s_len] * tri[:s_len, :s_len]
    same_seq = np.kron(np.eye(n_seq, dtype=np.float32), np.ones((s_len, s_len), np.float32))
    w_mix_sample = (jnp.tile(w_small, (1, n_seq, n_seq)) * same_seq).astype(BF16)
    b_mix_sample = jnp.broadcast_to(jnp.tile(b_spatial[0][:, :s_len], (1, n_seq))[:, :, None], (n_groups, t_s, LANE))

    p = dict(
        hpg=hpg, bn=bn, ffp=ffp,
        g_pre_mix=g_pre_mix[0], w_in=w_in[0], ln_g=sgu_ln_g[0], ln_b=sgu_ln_b[0],
        w_mix_prompt=w_mix_prompt, b_mix_prompt=b_mix_prompt,
        w_mix_sample=w_mix_sample, b_mix_sample=b_mix_sample,
        rel_bias=rel_bias,
        w_proj_a=w_proj_a[0].astype(BF16), w_proj_b=w_proj_b[0].astype(BF16), w_out=w_out[0],
        g_post_mix=g_post_mix[0], g_pre_ffn=g_pre_ffn[0],
        w_gate=w_gate[0], w_up=w_up[0],
        conv_w=conv_w[0], conv_b=conv_b[0],
        w_down=w_down[0],
        ffn_k_tile=_tile(ffp, FFN_DOWN_K_TILE, LANE),
        g_post_ffn=g_post_ffn[0],
    )

    caches = tuple(c[0] for c in (cache_k_g1, cache_v_g1, cache_k_g2, cache_v_g2, cache_k_g3, cache_v_g3))

    yp, zp, vp_state, ap_tail, w16 = _layer(x_prompt[0], p)
    ys, zs, vs_state, as_all, _ = _layer(x_sample.reshape(t_s, d), dict(p, w16=w16), s_len=s_len,
                                         caches=caches, conv_hist=state_conv[0])

    d_qkv = 3 * gw
    prompt_kv, sample_kv = [], []
    for gi, (win, _) in enumerate(DILATION_GROUPS):
        keep = min(win, seq)
        k_slab = (2 * d_a + d_qkv + gi * gw) // LANE
        v_slab = (2 * d_a + 2 * d_qkv + gi * gw) // LANE
        prompt_kv += [s.reshape(1, 1, keep, hpg, HEAD_DIM)
                      for s in _kv_state_rows(zp, k_slab, v_slab, hpg, seq - keep, keep)]
        sample_kv += [s.reshape(1, n_seq, s_len, hpg, HEAD_DIM)
                      for s in _kv_state_rows(zs, k_slab, v_slab, hpg, 0, t_s)]
    p_conv = ap_tail[SUBLANE - (cw - 1):].reshape(1, 1, cw - 1, d_ff)
    s_conv = as_all.reshape(n_seq, s_len, d_ff)[:, s_len - (cw - 1):].reshape(1, n_seq, cw - 1, d_ff)
    return (yp.reshape(1, seq, d), ys.reshape(n_seq, s_len, d),
            *prompt_kv, vp_state.reshape(1, 1, CHUNK, d_a), p_conv,
            *sample_kv, vs_state.reshape(1, n_seq, s_len, d_a), s_conv)
```
